```python
import math
import jax, jax.numpy as jnp
from jax import lax
import numpy as np

D_MODEL = 1024
BATCH = 2
SEQ = 8192
DEPTH = 2

CHUNK = 64
PLE_DIM = 256
POOL_WIDTH = D_MODEL // 4
POOL_GROUPS = 4
POOL_GROUP_DIM = POOL_WIDTH // POOL_GROUPS
POOL_WINDOWS = (2, 4, 8, 16)
ATTN_HEAD_DIM = 64
ATTN_HEADS = (3 * D_MODEL // 8) // ATTN_HEAD_DIM
ATTN_WIDTH = ATTN_HEADS * ATTN_HEAD_DIM
N_PREV_CHUNKS = 8
BAND_CHUNKS = N_PREV_CHUNKS + 1
REL_CLIP = 128
SSM_WIDTH = D_MODEL - POOL_WIDTH - ATTN_WIDTH
SSM_GROUP_DIM = 16
SSM_GROUPS = SSM_WIDTH // SSM_GROUP_DIM
SSM_STATE = 64
MIX_WIDTH = POOL_WIDTH + ATTN_WIDTH + SSM_WIDTH
IN_PROJ_WIDTH = POOL_WIDTH + 3 * ATTN_WIDTH + SSM_WIDTH
N_EXPERTS = 16
N_EXPERT_GROUPS = 4
EXPERTS_PER_GROUP = N_EXPERTS // N_EXPERT_GROUPS
TOP_K = 2
D_EXPERT = D_MODEL // 4
DN_ALPHA = (2 * DEPTH) ** 0.25
DN_BETA = (8 * DEPTH) ** -0.25
NORM_EPS = 1e-5

kernel_name = 'hybrid_pool_chunkattn_s5_grouped_moe'


def layer_norm(x, g, b):
    xf = x.astype(jnp.float32)
    mu = jnp.mean(xf, axis=-1, keepdims=True)
    xc = xf - mu
    var = jnp.mean(xc * xc, axis=-1, keepdims=True)
    y = xc * lax.rsqrt(var + NORM_EPS) * g.astype(jnp.float32) + b.astype(jnp.float32)
    return y.astype(x.dtype)


def rms_norm(x, g):
    xf = x.astype(jnp.float32)
    y = xf * lax.rsqrt(jnp.mean(xf * xf, axis=-1, keepdims=True) + NORM_EPS) * g.astype(jnp.float32)
    return y.astype(x.dtype)


def pool_mixer(u, w_pool, pool_scale):
    b, l, _ = u.shape
    ug = u.reshape(b, l, POOL_GROUPS, POOL_GROUP_DIM).astype(jnp.float32)
    cs = jnp.concatenate([jnp.zeros_like(ug[:, :1]), jnp.cumsum(ug, axis=1)], axis=1)
    pos = jnp.arange(l)
    outs = []
    for g, w in enumerate(POOL_WINDOWS):
        lo = jnp.maximum(pos + 1 - w, 0)
        cnt = jnp.minimum(pos + 1, w).astype(jnp.float32)
        mean = (cs[:, 1:, g] - cs[:, lo, g]) / cnt[None, :, None]
        outs.append(mean - ug[:, :, g])
    d = jnp.stack(outs, axis=2).astype(u.dtype)
    y = jnp.einsum('blgc,gce->blge', d, w_pool)
    return y.reshape(b, l, POOL_WIDTH) * pool_scale


def rel_bias_matrix(rel_bias):
    qi = jnp.arange(CHUNK)[:, None]
    kk = jnp.arange(BAND_CHUNKS * CHUNK)[None, :]
    rel = N_PREV_CHUNKS * CHUNK + qi - kk
    idx = jnp.clip(rel, -REL_CLIP, REL_CLIP) + REL_CLIP
    return rel_bias[:, idx]


def chunk_band(t, nc):
    tp = jnp.pad(t, ((0, 0), (N_PREV_CHUNKS, 0), (0, 0), (0, 0), (0, 0)))
    return jnp.concatenate([tp[:, j:j + nc] for j in range(BAND_CHUNKS)], axis=2)


def chunk_attention(q, k, v, rel_bias):
    b, l, _ = q.shape
    nc = l // CHUNK
    shp = (b, nc, CHUNK, ATTN_HEADS, ATTN_HEAD_DIM)
    q = q.reshape(shp)
    kb = chunk_band(k.reshape(shp), nc)
    vb = chunk_band(v.reshape(shp), nc)
    s = jnp.einsum('bnqhd,bnkhd->bnhqk', q, kb).astype(jnp.float32) * (ATTN_HEAD_DIM ** -0.5)
    s = s + rel_bias_matrix(rel_bias).astype(jnp.float32)[None, None]
    key_chunk = jnp.arange(nc)[:, None] - N_PREV_CHUNKS + jnp.arange(BAND_CHUNKS * CHUNK)[None, :] // CHUNK
    s = jnp.where((key_chunk >= 0)[None, :, None, None, :], s, -jnp.inf)
    pr = jax.nn.softmax(s, axis=-1).astype(vb.dtype)
    o = jnp.einsum('bnhqk,bnkhd->bnqhd', pr, vb)
    return o.reshape(b, l, ATTN_WIDTH)


def _complex_affine_combine(e1, e2):
    a1r, a1i, b1r, b1i = e1
    a2r, a2i, b2r, b2i = e2
    return (a2r * a1r - a2i * a1i,
            a2r * a1i + a2i * a1r,
            a2r * b1r - a2i * b1i + b2r,
            a2r * b1i + a2i * b1r + b2i)


def s5_mixer(u, a_re, a_im, log_dt, b_re, b_im, c_re, c_im, d_skip, w_glu, b_glu):
    f32 = jnp.float32
    bsz, l, _ = u.shape
    ug = u.reshape(bsz, l, SSM_GROUPS, SSM_GROUP_DIM).astype(f32)
    dt = jnp.exp(log_dt.astype(f32))[:, None]
    ar = a_re.astype(f32)
    ai = a_im.astype(f32)
    mag = jnp.exp(ar * dt)
    abar_re = mag * jnp.cos(ai * dt)
    abar_im = mag * jnp.sin(ai * dt)
    den = ar * ar + ai * ai
    nr = abar_re - 1.0
    ni = abar_im
    coef_re = ((nr * ar + ni * ai) / den)[..., None]
    coef_im = ((ni * ar - nr * ai) / den)[..., None]
    br = b_re.astype(f32)
    bi = b_im.astype(f32)
    bbar_re = coef_re * br - coef_im * bi
    bbar_im = coef_re * bi + coef_im * br
    bu_re = jnp.einsum('blgh,gph->blgp', ug, bbar_re)
    bu_im = jnp.einsum('blgh,gph->blgp', ug, bbar_im)
    shape = bu_re.shape
    elems = (jnp.broadcast_to(abar_re, shape), jnp.broadcast_to(abar_im, shape), bu_re, bu_im)
    _, _, s_re, s_im = lax.associative_scan(_complex_affine_combine, elems, axis=1)
    y = (jnp.einsum('blgp,ghp->blgh', s_re, c_re.astype(f32))
         - jnp.einsum('blgp,ghp->blgh', s_im, c_im.astype(f32)))
    y = y + d_skip.astype(f32).reshape(SSM_GROUPS, SSM_GROUP_DIM) * ug
    y = jax.nn.gelu(y.reshape(bsz, l, SSM_WIDTH)).astype(u.dtype)
    return y * jax.nn.sigmoid(y @ w_glu + b_glu)


def route(h, w_router, router_bias):
    scores = jax.nn.sigmoid((h @ w_router).astype(jnp.float32))
    biased = scores + router_bias.astype(jnp.float32)
    grouped = biased.reshape(*biased.shape[:-1], N_EXPERT_GROUPS, EXPERTS_PER_GROUP)
    group_score = lax.top_k(grouped, TOP_K)[0].sum(-1)
    top_group = jnp.argmax(group_score, axis=-1)
    expert_group = jnp.arange(N_EXPERTS) // EXPERTS_PER_GROUP
    in_group = expert_group == top_group[..., None]
    masked = jnp.where(in_group, biased, -jnp.inf)
    _, idx = lax.top_k(masked, TOP_K)
    w = jnp.take_along_axis(scores, idx, axis=-1)
    w = w / jnp.sum(w, axis=-1, keepdims=True)
    combine = jnp.sum(jax.nn.one_hot(idx, N_EXPERTS, dtype=jnp.float32) * w[..., None], axis=-2)
    return combine


def moe(h, combine, w_gate, w_up, w_down):
    out = jnp.zeros_like(h)
    for e in range(N_EXPERTS):
        a = jax.nn.silu(h @ w_gate[e]) * (h @ w_up[e])
        out = out + combine[..., e:e + 1].astype(h.dtype) * (a @ w_down[e])
    return out


def setup_inputs(seed: int = 0) -> dict:
    key = jax.random.key(seed)
    ks = jax.random.split(key, 32)
    f32 = jnp.float32
    nrm = lambda k, shape, s: jax.random.normal(k, shape, f32) * s
    L = DEPTH
    return {
        'x': nrm(ks[0], (BATCH, SEQ, D_MODEL), 1.0),
        'p': nrm(ks[1], (DEPTH, BATCH, SEQ, PLE_DIM), 1.0),
        'w_in': nrm(ks[2], (L, D_MODEL, IN_PROJ_WIDTH), D_MODEL ** -0.5),
        'w_out': nrm(ks[3], (L, MIX_WIDTH, D_MODEL), DN_BETA * MIX_WIDTH ** -0.5),
        'w_pool': nrm(ks[4], (L, POOL_GROUPS, POOL_GROUP_DIM, POOL_GROUP_DIM), POOL_GROUP_DIM ** -0.5),
        'pool_scale': 1.0 + nrm(ks[5], (L, POOL_WIDTH), 0.1),
        'rel_bias': nrm(ks[6], (L, ATTN_HEADS, 2 * REL_CLIP + 1), 0.1),
        'ssm_a_re': -0.5 + nrm(ks[7], (L, SSM_GROUPS, SSM_STATE), 0.01),
        'ssm_a_im': math.pi * jnp.arange(SSM_STATE, dtype=f32)[None, None, :] + nrm(ks[8], (L, SSM_GROUPS, SSM_STATE), 0.01),
        'ssm_log_dt': jax.random.uniform(ks[9], (L, SSM_GROUPS), f32, math.log(1e-3), math.log(1e-1)),
        'ssm_b_re': nrm(ks[10], (L, SSM_GROUPS, SSM_STATE, SSM_GROUP_DIM), (2 * SSM_GROUP_DIM) ** -0.5),
        'ssm_b_im': nrm(ks[11], (L, SSM_GROUPS, SSM_STATE, SSM_GROUP_DIM), (2 * SSM_GROUP_DIM) ** -0.5),
        'ssm_c_re': nrm(ks[12], (L, SSM_GROUPS, SSM_GROUP_DIM, SSM_STATE), (2 * SSM_STATE) ** -0.5),
        'ssm_c_im': nrm(ks[13], (L, SSM_GROUPS, SSM_GROUP_DIM, SSM_STATE), (2 * SSM_STATE) ** -0.5),
        'ssm_d': nrm(ks[14], (L, SSM_WIDTH), 1.0),
        'w_glu': nrm(ks[15], (L, SSM_WIDTH, SSM_WIDTH), SSM_WIDTH ** -0.5),
        'b_glu': nrm(ks[16], (L, SSM_WIDTH), 0.02),
        'g_pool': 1.0 + nrm(ks[17], (L, POOL_WIDTH), 0.05),
        'g_attn': 1.0 + nrm(ks[18], (L, ATTN_WIDTH), 0.05),
        'g_ssm': 1.0 + nrm(ks[19], (L, SSM_WIDTH), 0.05),
        'ln1_g': 1.0 + nrm(ks[20], (L, D_MODEL), 0.05),
        'ln1_b': nrm(ks[21], (L, D_MODEL), 0.02),
        'ln2_g': 1.0 + nrm(ks[22], (L, D_MODEL), 0.05),
        'ln2_b': nrm(ks[23], (L, D_MODEL), 0.02),
        'w_router': nrm(ks[24], (D_MODEL, N_EXPERTS), D_MODEL ** -0.5),
        'router_bias': nrm(ks[25], (N_EXPERTS,), 0.01),
        'w_exp_gate': nrm(ks[26], (L, N_EXPERTS, D_MODEL, D_EXPERT), D_MODEL ** -0.5),
        'w_exp_up': nrm(ks[27], (L, N_EXPERTS, D_MODEL, D_EXPERT), D_MODEL ** -0.5),
        'w_exp_down': nrm(ks[28], (L, N_EXPERTS, D_EXPERT, D_MODEL), DN_BETA * D_EXPERT ** -0.5),
        'w_ple_gate': nrm(ks[29], (L, D_MODEL, D_MODEL), D_MODEL ** -0.5),
        'w_ple_proj': nrm(ks[30], (L, PLE_DIM, D_MODEL), DN_BETA * PLE_DIM ** -0.5),
    }


def reference(x, p, w_in, w_out, w_pool, pool_scale, rel_bias, ssm_a_re, ssm_a_im, ssm_log_dt,
              ssm_b_re, ssm_b_im, ssm_c_re, ssm_c_im, ssm_d, w_glu, b_glu, g_pool, g_attn, g_ssm,
              ln1_g, ln1_b, ln2_g, ln2_b, w_router, router_bias, w_exp_gate, w_exp_up, w_exp_down,
              w_ple_gate, w_ple_proj):
    splits = [POOL_WIDTH, POOL_WIDTH + ATTN_WIDTH, POOL_WIDTH + 2 * ATTN_WIDTH, POOL_WIDTH + 3 * ATTN_WIDTH]
    for i in range(DEPTH):
        z = x @ w_in[i]
        u_pool, q, k, v, u_ssm = jnp.split(z, splits, axis=-1)
        y_pool = pool_mixer(u_pool, w_pool[i], pool_scale[i])
        y_attn = chunk_attention(q, k, v, rel_bias[i])
        y_ssm = s5_mixer(u_ssm, ssm_a_re[i], ssm_a_im[i], ssm_log_dt[i], ssm_b_re[i], ssm_b_im[i],
                         ssm_c_re[i], ssm_c_im[i], ssm_d[i], w_glu[i], b_glu[i])
        heads = jnp.concatenate([rms_norm(y_pool, g_pool[i]), rms_norm(y_attn, g_attn[i]),
                                 rms_norm(y_ssm, g_ssm[i])], axis=-1)
        h = layer_norm(DN_ALPHA * x + heads @ w_out[i], ln1_g[i], ln1_b[i])
        combine = route(h, w_router, router_bias)
        ffn = moe(h, combine, w_exp_gate[i], w_exp_up[i], w_exp_down[i])
        ple = jax.nn.sigmoid(h @ w_ple_gate[i]) * (p[i] @ w_ple_proj[i])
        x = layer_norm(DN_ALPHA * h + ffn + ple, ln2_g[i], ln2_b[i])
    return x
```

```python
import functools
import math

import jax
import jax.numpy as jnp
from jax import lax
from jax.experimental import pallas as pl
from jax.experimental.pallas import tpu as pltpu

F32 = jnp.float32
BF16 = jnp.bfloat16

D_MODEL = 1024
DEPTH = 2
CHUNK = 64
PLE_DIM = 256
POOL_WIDTH = 256
POOL_GROUP_DIM = 64
POOL_WINDOWS = (2, 4, 8, 16)
POOL_HALO = 16
ATTN_HEAD_DIM = 64
ATTN_HEADS = 6
ATTN_WIDTH = 384
N_PREV_CHUNKS = 8
REL_CLIP = 128
SSM_WIDTH = 384
SSM_GROUP_DIM = 16
SSM_GROUPS = 24
SSM_STATE = 64
N_EXPERTS = 16
EXPERTS_PER_GROUP = 4
D_EXPERT = 256
DN_ALPHA = (2 * DEPTH) ** 0.25
NORM_EPS = 1e-5

LANES = 128
VMEM_LIMIT_BYTES = 56 * 1024 * 1024

INPROJ_TOKENS = 512
POOL_TOKENS = 512
ATTN_Q_CHUNKS = 4
ATTN_Q_TOKENS = ATTN_Q_CHUNKS * CHUNK
ATTN_BAND_TOKENS = 3 * ATTN_Q_TOKENS
SSM_CHUNK = 32
SSM_CHUNK_WIDTH = SSM_CHUNK * SSM_GROUP_DIM
OUT_TOKENS = 512
MOE_TOKENS = 512
MOE_EXPERTS_PER_STEP = 2


def _cparams(*sem):
    return pltpu.CompilerParams(dimension_semantics=sem, vmem_limit_bytes=VMEM_LIMIT_BYTES)


def _dot(a, b):
    return jnp.dot(a, b, preferred_element_type=F32)


def _layer_norm(v, g, b):
    mu = jnp.mean(v, axis=-1, keepdims=True)
    vc = v - mu
    var = jnp.mean(vc * vc, axis=-1, keepdims=True)
    return vc * lax.rsqrt(var + NORM_EPS) * g + b


_Z_POOL = (0, 256)
_Z_Q = (256, 640)
_Z_K = (640, 1024)
_Z_V = (1024, 1408)
_Z_SSM = (1408, 1792)


def _inproj_body(x_ref, w_ref, up_ref, q_ref, k_ref, v_ref, us_ref):
    xb = x_ref[...].astype(BF16)

    def seg(lo_hi):
        return _dot(xb, w_ref[:, lo_hi[0]:lo_hi[1]])

    up_ref[...] = seg(_Z_POOL)
    q_ref[...] = (seg(_Z_Q) * (ATTN_HEAD_DIM ** -0.5)).astype(BF16)
    k_ref[...] = seg(_Z_K).astype(BF16)
    v_ref[...] = seg(_Z_V).astype(BF16)
    us_ref[...] = seg(_Z_SSM)


def _inproj(x, w_bf16):
    t = x.shape[0]
    tm = INPROJ_TOKENS
    row = lambda width: pl.BlockSpec((tm, width), lambda i: (i, 0))
    return pl.pallas_call(
        _inproj_body,
        grid=(t // tm,),
        in_specs=[row(D_MODEL), pl.BlockSpec(w_bf16.shape, lambda i: (0, 0))],
        out_specs=[row(POOL_WIDTH), row(ATTN_WIDTH), row(ATTN_WIDTH), row(ATTN_WIDTH), row(SSM_WIDTH)],
        out_shape=[jax.ShapeDtypeStruct((t, POOL_WIDTH), F32),
                   jax.ShapeDtypeStruct((t, ATTN_WIDTH), BF16),
                   jax.ShapeDtypeStruct((t, ATTN_WIDTH), BF16),
                   jax.ShapeDtypeStruct((t, ATTN_WIDTH), BF16),
                   jax.ShapeDtypeStruct((t, SSM_WIDTH), F32)],
        compiler_params=_cparams("parallel"),
        name="inproj",
    )(x, w_bf16)


def _pool_body(u_ref, halo_ref, w_ref, scale_ref, g_ref, o_ref, buf):
    i = pl.program_id(1)
    tp = u_ref.shape[0]
    x0 = u_ref[...]
    buf[0:POOL_HALO, :] = jnp.where(i == 0, 0.0, halo_ref[...])
    buf[POOL_HALO:, :] = x0
    pos = i * tp + lax.broadcasted_iota(jnp.int32, (tp, 1), 0)
    group = lax.broadcasted_iota(jnp.int32, (1, POOL_WIDTH), 1) // POOL_GROUP_DIM
    s = x0
    mean = jnp.zeros_like(x0)
    k = 1
    for gi, w in enumerate(POOL_WINDOWS):
        while k < w:
            s = s + buf[pl.ds(POOL_HALO - k, tp), :]
            k += 1
        cnt = jnp.minimum(pos + 1, w).astype(F32)
        mean = jnp.where(group == gi, s / cnt, mean)
    d = (mean - x0).astype(BF16)
    y = _dot(d, w_ref[...]) * scale_ref[...]
    r = lax.rsqrt(jnp.mean(y * y, axis=-1, keepdims=True) + NORM_EPS)
    o_ref[...] = (y * r * g_ref[...]).astype(BF16)


def _pool(u_pool, w_blockdiag_bf16, scale, gain, batch):
    t = u_pool.shape[0]
    tp = POOL_TOKENS
    nt = t // batch // tp
    halo_blocks = tp // POOL_HALO
    vec = pl.BlockSpec((1, POOL_WIDTH), lambda b, i: (0, 0))
    return pl.pallas_call(
        _pool_body,
        grid=(batch, nt),
        in_specs=[pl.BlockSpec((tp, POOL_WIDTH), lambda b, i: (b * nt + i, 0)),
                  pl.BlockSpec((POOL_HALO, POOL_WIDTH),
                               lambda b, i: (jnp.maximum((b * nt + i) * halo_blocks - 1, 0), 0)),
                  pl.BlockSpec((POOL_WIDTH, POOL_WIDTH), lambda b, i: (0, 0)),
                  vec, vec],
        out_specs=pl.BlockSpec((tp, POOL_WIDTH), lambda b, i: (b * nt + i, 0)),
        out_shape=jax.ShapeDtypeStruct((t, POOL_WIDTH), BF16),
        scratch_shapes=[pltpu.VMEM((POOL_HALO + tp, POOL_WIDTH), F32)],
        compiler_params=_cparams("parallel", "parallel"),
        name="pool",
    )(u_pool, u_pool, w_blockdiag_bf16, scale, gain)


def _attn_bias_table(rel_bias):
    r = jnp.arange(ATTN_Q_TOKENS)[:, None]
    c = jnp.arange(ATTN_BAND_TOKENS)[None, :]
    rel = N_PREV_CHUNKS * CHUNK + r - c
    idx = jnp.clip(rel, -REL_CLIP, REL_CLIP) + REL_CLIP
    qc = r // CHUNK
    kc = c // CHUNK
    in_band = (kc >= qc) & (kc <= qc + N_PREV_CHUNKS)
    return jnp.where(in_band[None], rel_bias.astype(F32)[:, idx], -jnp.inf)


def _attn_body(q_ref, k0_ref, k1_ref, k2_ref, v0_ref, v1_ref, v2_ref, bias_ref, g_ref, o_ref):
    i = pl.program_id(1)
    tq = ATTN_Q_TOKENS
    col = lax.broadcasted_iota(jnp.int32, (1, ATTN_BAND_TOKENS), 1)
    before_start = ((col < tq) & (i < 2)) | ((col < 2 * tq) & (i < 1))
    upper_half = lax.broadcasted_iota(jnp.int32, (1, LANES), 1) >= ATTN_HEAD_DIM
    outs = []
    for pair in range(ATTN_WIDTH // LANES):
        sl = slice(pair * LANES, (pair + 1) * LANES)
        qp = q_ref[:, sl]
        kp = jnp.concatenate([k0_ref[:, sl], k1_ref[:, sl], k2_ref[:, sl]], axis=0)
        vp = jnp.concatenate([v0_ref[:, sl], v1_ref[:, sl], v2_ref[:, sl]], axis=0)
        o_pair = None
        for half in range(2):
            head = 2 * pair + half
            qm = jnp.where(upper_half == bool(half), qp, jnp.zeros_like(qp))
            s = lax.dot_general(qm, kp, (((1,), (1,)), ((), ())), preferred_element_type=F32)
            s = s + bias_ref[head]
            s = jnp.where(before_start, -jnp.inf, s)
            m = jnp.max(s, axis=-1, keepdims=True)
            p = jnp.exp(s - m)
            l = jnp.sum(p, axis=-1, keepdims=True)
            o = _dot(p.astype(BF16), vp) / l
            o_pair = o if half == 0 else jnp.where(upper_half, o, o_pair)
        outs.append(o_pair)
    ss = sum(jnp.sum(o * o, axis=-1, keepdims=True) for o in outs)
    r = lax.rsqrt(ss / ATTN_WIDTH + NORM_EPS)
    for pair, o in enumerate(outs):
        sl = slice(pair * LANES, (pair + 1) * LANES)
        o_ref[:, sl] = (o * r * g_ref[:, sl]).astype(BF16)


def _attention(q, k, v, bias_table, gain, batch):
    t = q.shape[0]
    tq = ATTN_Q_TOKENS
    nq = t // batch // tq

    def blk(back):
        return pl.BlockSpec((tq, ATTN_WIDTH), lambda b, i: (b * nq + jnp.maximum(i - back, 0), 0))

    return pl.pallas_call(
        _attn_body,
        grid=(batch, nq),
        in_specs=[blk(0), blk(2), blk(1), blk(0), blk(2), blk(1), blk(0),
                  pl.BlockSpec(bias_table.shape, lambda b, i: (0, 0, 0)),
                  pl.BlockSpec((1, ATTN_WIDTH), lambda b, i: (0, 0))],
        out_specs=blk(0),
        out_shape=jax.ShapeDtypeStruct((t, ATTN_WIDTH), BF16),
        compiler_params=_cparams("parallel", "parallel"),
        name="attention",
    )(q, k, k, k, v, v, v, bias_table, gain)


def _s5_tables(a_re, a_im, log_dt, b_re, b_im, c_re, c_im, d_skip):
    hi = lax.Precision.HIGHEST
    tc = SSM_CHUNK
    g, p_dim = a_re.shape
    dt = jnp.exp(log_dt.astype(F32))[:, None]
    ar = a_re.astype(F32)
    ai = a_im.astype(F32)
    mag = jnp.exp(ar * dt)
    abar_re = mag * jnp.cos(ai * dt)
    abar_im = mag * jnp.sin(ai * dt)
    den = ar * ar + ai * ai
    nr = abar_re - 1.0
    ni = abar_im
    coef_re = ((nr * ar + ni * ai) / den)[..., None]
    coef_im = ((ni * ar - nr * ai) / den)[..., None]
    br = b_re.astype(F32)
    bi = b_im.astype(F32)
    bbar_re = coef_re * br - coef_im * bi
    bbar_im = coef_re * bi + coef_im * br
    n = jnp.arange(tc + 1, dtype=F32)[:, None, None]
    pmag = jnp.exp(n * (ar * dt))
    pw_re = pmag * jnp.cos(n * (ai * dt))
    pw_im = pmag * jnp.sin(n * (ai * dt))
    x_re = pw_re[..., None] * bbar_re[None] - pw_im[..., None] * bbar_im[None]
    x_im = pw_re[..., None] * bbar_im[None] + pw_im[..., None] * bbar_re[None]
    cr = c_re.astype(F32)
    ci = c_im.astype(F32)
    kern = (jnp.einsum('ghp,ngpk->nghk', cr, x_re[:tc], precision=hi)
            - jnp.einsum('ghp,ngpk->nghk', ci, x_im[:tc], precision=hi))
    j = jnp.arange(tc)[:, None]
    t = jnp.arange(tc)[None, :]
    lag = t - j
    toep = jnp.where((lag >= 0)[:, :, None, None, None], kern[jnp.maximum(lag, 0)], 0.0)
    toep = toep.transpose(2, 0, 4, 1, 3).reshape(g, tc * SSM_GROUP_DIM, tc * SSM_GROUP_DIM)
    rev = tc - 1 - jnp.arange(tc)
    est_re = x_re[rev].transpose(1, 0, 3, 2).reshape(g, tc * SSM_GROUP_DIM, p_dim)
    est_im = x_im[rev].transpose(1, 0, 3, 2).reshape(g, tc * SSM_GROUP_DIM, p_dim)
    pr = pw_re[1:, :, None, :]
    pi = pw_im[1:, :, None, :]
    int_re = (cr[None] * pr - ci[None] * pi).transpose(1, 3, 0, 2).reshape(g, p_dim, tc * SSM_GROUP_DIM)
    int_im = (-(cr[None] * pi + ci[None] * pr)).transpose(1, 3, 0, 2).reshape(g, p_dim, tc * SSM_GROUP_DIM)

    odd = (jnp.arange(g) % 2 == 1)[:, None, None]
    zero_e = jnp.zeros_like(est_re)
    lane_pad = lambda m: jnp.where(odd, jnp.concatenate([zero_e, m], -1), jnp.concatenate([m, zero_e], -1))
    zero_i = jnp.zeros_like(int_re)
    row_pad = lambda m: jnp.where(odd, jnp.concatenate([zero_i, m], 1), jnp.concatenate([m, zero_i], 1))
    pair = lambda m: m.reshape(g // 2, 1, 2 * p_dim)
    d_exp = jnp.tile(d_skip.astype(F32).reshape(g, 1, SSM_GROUP_DIM), (1, tc, 1)).reshape(g, 1, tc * SSM_GROUP_DIM)
    return dict(
        toep=toep.astype(BF16),
        est_re=lane_pad(est_re).astype(BF16), est_im=lane_pad(est_im).astype(BF16),
        int_re=row_pad(int_re).astype(BF16), int_im=row_pad(int_im).astype(BF16),
        apow_re=pair(pw_re[tc]), apow_im=pair(pw_im[tc]),
        d_exp=d_exp)


def _s5_body(u_ref, toep_ref, estre_ref, estim_ref, intre_ref, intim_ref, apre_ref, apim_ref, dexp_ref,
             y_ref, ere_s, eim_s, spre_s, spim_s, *, batch):
    nch = u_ref.shape[1]
    per_seq = nch // batch
    ub = [u_ref[gi].astype(BF16) for gi in range(2)]
    ere_s[...] = _dot(ub[0], estre_ref[0]) + _dot(ub[1], estre_ref[1])
    eim_s[...] = _dot(ub[0], estim_ref[0]) + _dot(ub[1], estim_ref[1])
    a_re = apre_ref[0]
    a_im = apim_ref[0]

    def step(c, carry):
        new = []
        for b in range(batch):
            s_re, s_im = carry[2 * b], carry[2 * b + 1]
            row = b * per_seq + c
            spre_s[pl.ds(row, 1), :] = s_re
            spim_s[pl.ds(row, 1), :] = s_im
            e_re = ere_s[pl.ds(row, 1), :]
            e_im = eim_s[pl.ds(row, 1), :]
            new.append(a_re * s_re - a_im * s_im + e_re)
            new.append(a_re * s_im + a_im * s_re + e_im)
        return tuple(new)

    zero = jnp.zeros((1, 2 * SSM_STATE), F32)
    lax.fori_loop(0, per_seq, step, (zero,) * (2 * batch))
    sp_re = spre_s[...].astype(BF16)
    sp_im = spim_s[...].astype(BF16)
    for gi in range(2):
        y = (_dot(ub[gi], toep_ref[gi]) + _dot(sp_re, intre_ref[gi]) + _dot(sp_im, intim_ref[gi])
             + dexp_ref[gi] * u_ref[gi])
        y_ref[gi] = jax.nn.gelu(y)


def _s5(u_grouped, tab, batch):
    g, nch, cw = u_grouped.shape
    two = lambda *rest: pl.BlockSpec((2,) + rest, lambda i: (i,) + (0,) * len(rest))
    one = lambda *rest: pl.BlockSpec((1,) + rest, lambda i: (i,) + (0,) * len(rest))
    state = pltpu.VMEM((nch, 2 * SSM_STATE), F32)
    return pl.pallas_call(
        functools.partial(_s5_body, batch=batch),
        grid=(g // 2,),
        in_specs=[two(nch, cw), two(cw, cw), two(cw, 2 * SSM_STATE), two(cw, 2 * SSM_STATE),
                  two(2 * SSM_STATE, cw), two(2 * SSM_STATE, cw),
                  one(1, 2 * SSM_STATE), one(1, 2 * SSM_STATE), two(1, cw)],
        out_specs=two(nch, cw),
        out_shape=jax.ShapeDtypeStruct((g, nch, cw), F32),
        scratch_shapes=[state, state, state, state],
        compiler_params=_cparams("parallel"),
        name="s5",
    )(u_grouped, tab['toep'], tab['est_re'], tab['est_im'], tab['int_re'], tab['int_im'],
      tab['apow_re'], tab['apow_im'], tab['d_exp'])


def _route_rows(scores, biased):
    ng = N_EXPERTS // EXPERTS_PER_GROUP
    group_score = []
    for gi in range(ng):
        a, b, c, d = biased[gi * EXPERTS_PER_GROUP:(gi + 1) * EXPERTS_PER_GROUP]
        hi1, lo1 = jnp.maximum(a, b), jnp.minimum(a, b)
        hi2, lo2 = jnp.maximum(c, d), jnp.minimum(c, d)
        top1 = jnp.maximum(hi1, hi2)
        top2 = jnp.maximum(jnp.minimum(hi1, hi2), jnp.maximum(lo1, lo2))
        group_score.append(top1 + top2)
    best = group_score[0]
    best_idx = jnp.zeros_like(best, dtype=jnp.int32)
    for gi in range(1, ng):
        better = group_score[gi] > best
        best = jnp.where(better, group_score[gi], best)
        best_idx = jnp.where(better, gi, best_idx)
    picked = []
    for e in range(N_EXPERTS):
        gi = e // EXPERTS_PER_GROUP
        rank = jnp.zeros_like(best_idx)
        for o in range(gi * EXPERTS_PER_GROUP, (gi + 1) * EXPERTS_PER_GROUP):
            if o == e:
                continue
            ahead = (biased[o] > biased[e]) | ((biased[o] == biased[e]) & (o < e))
            rank = rank + ahead.astype(jnp.int32)
        picked.append((best_idx == gi) & (rank < 2))
    wsum = sum(jnp.where(picked[e], scores[e], 0.0) for e in range(N_EXPERTS))
    return [jnp.where(picked[e], scores[e] / wsum, 0.0) for e in range(N_EXPERTS)]


def _outproj_body(x_ref, yp_ref, ya_ref, ys_ref, wglu_ref, bglu_ref, gssm_ref, wout_ref, g_ref, b_ref,
                  wrt_ref, rb_ref, h_ref, comb_ref):
    ys = ys_ref[...]
    gate = jax.nn.sigmoid(_dot(ys.astype(BF16), wglu_ref[...]) + bglu_ref[...])
    ys = ys * gate
    r = lax.rsqrt(jnp.mean(ys * ys, axis=-1, keepdims=True) + NORM_EPS)
    ysn = (ys * r * gssm_ref[...]).astype(BF16)
    mix = (_dot(yp_ref[...], wout_ref[0:POOL_WIDTH, :])
           + _dot(ya_ref[...], wout_ref[POOL_WIDTH:POOL_WIDTH + ATTN_WIDTH, :])
           + _dot(ysn, wout_ref[POOL_WIDTH + ATTN_WIDTH:, :]))
    h = _layer_norm(DN_ALPHA * x_ref[...] + mix, g_ref[...], b_ref[...])
    h_ref[...] = h
    logits = lax.dot_general(wrt_ref[...], h, (((1,), (1,)), ((), ())),
                             precision=lax.Precision.HIGHEST, preferred_element_type=F32)
    sc = jax.nn.sigmoid(logits)
    bs = sc + rb_ref[...]
    scores = [sc[e:e + 1, :] for e in range(N_EXPERTS)]
    biased = [bs[e:e + 1, :] for e in range(N_EXPERTS)]
    comb_ref[...] = jnp.concatenate(_route_rows(scores, biased), axis=0)


def _outproj(x, y_pool, y_attn, y_ssm, w_glu_bf16, b_glu, g_ssm, w_out_bf16, ln_g, ln_b, w_router_t, router_bias):
    t = x.shape[0]
    tm = OUT_TOKENS
    row = lambda width: pl.BlockSpec((tm, width), lambda i: (i, 0))
    full = lambda a: pl.BlockSpec(a.shape, lambda i: (0,) * a.ndim)
    return pl.pallas_call(
        _outproj_body,
        grid=(t // tm,),
        in_specs=[row(D_MODEL), row(POOL_WIDTH), row(ATTN_WIDTH), row(SSM_WIDTH),
                  full(w_glu_bf16), full(b_glu), full(g_ssm), full(w_out_bf16), full(ln_g), full(ln_b),
                  full(w_router_t), full(router_bias)],
        out_specs=[row(D_MODEL), pl.BlockSpec((N_EXPERTS, tm), lambda i: (0, i))],
        out_shape=[jax.ShapeDtypeStruct((t, D_MODEL), F32), jax.ShapeDtypeStruct((N_EXPERTS, t), F32)],
        compiler_params=_cparams("parallel"),
        name="outproj",
    )(x, y_pool, y_attn, y_ssm, w_glu_bf16, b_glu, g_ssm, w_out_bf16, ln_g, ln_b, w_router_t, router_bias)


def _moe_body(h_ref, comb_ref, p_ref, wg_ref, wu_ref, wd_ref, wpg_ref, wpp_ref, g_ref, b_ref, o_ref,
              hb_s, acc_s):
    j = pl.program_id(1)

    @pl.when(j == 0)
    def _():
        hb_s[...] = h_ref[...].astype(BF16)
        acc_s[...] = jnp.zeros_like(acc_s)

    hb = hb_s[...]
    total = None
    for e in range(MOE_EXPERTS_PER_STEP):
        gate = _dot(hb, wg_ref[e])
        up = _dot(hb, wu_ref[e])
        a = (jax.nn.silu(gate) * up * comb_ref[0, :, e:e + 1]).astype(BF16)
        d = _dot(a, wd_ref[e])
        total = d if total is None else total + d
    acc_s[...] += total

    @pl.when(j == pl.num_programs(1) - 1)
    def _():
        ple = jax.nn.sigmoid(_dot(hb, wpg_ref[...])) * _dot(p_ref[...].astype(BF16), wpp_ref[...])
        o_ref[...] = _layer_norm(DN_ALPHA * h_ref[...] + acc_s[...] + ple, g_ref[...], b_ref[...])


def _moe(h, comb_steps, p, wg_bf16, wu_bf16, wd_bf16, wpg_bf16, wpp_bf16, ln_g, ln_b):
    t = h.shape[0]
    tm = MOE_TOKENS
    eps = MOE_EXPERTS_PER_STEP
    full = lambda a: pl.BlockSpec(a.shape, lambda i, j: (0,) * a.ndim)
    return pl.pallas_call(
        _moe_body,
        grid=(t // tm, N_EXPERTS // eps),
        in_specs=[pl.BlockSpec((tm, D_MODEL), lambda i, j: (i, 0)),
                  pl.BlockSpec((1, tm, eps), lambda i, j: (j, i, 0)),
                  pl.BlockSpec((tm, PLE_DIM), lambda i, j: (i, 0)),
                  pl.BlockSpec((eps, D_MODEL, D_EXPERT), lambda i, j: (j, 0, 0)),
                  pl.BlockSpec((eps, D_MODEL, D_EXPERT), lambda i, j: (j, 0, 0)),
                  pl.BlockSpec((eps, D_EXPERT, D_MODEL), lambda i, j: (j, 0, 0)),
                  full(wpg_bf16), full(wpp_bf16), full(ln_g), full(ln_b)],
        out_specs=pl.BlockSpec((tm, D_MODEL), lambda i, j: (i, 0)),
        out_shape=jax.ShapeDtypeStruct((t, D_MODEL), F32),
        scratch_shapes=[pltpu.VMEM((tm, D_MODEL), BF16), pltpu.VMEM((tm, D_MODEL), F32)],
        compiler_params=_cparams("parallel", "arbitrary"),
        name="moe",
    )(h, comb_steps, p, wg_bf16, wu_bf16, wd_bf16, wpg_bf16, wpp_bf16, ln_g, ln_b)


def _block_diag(w):
    g, n, m = w.shape
    eye = jnp.eye(g, dtype=w.dtype)
    return (eye[:, None, :, None] * w[:, :, None, :]).reshape(g * n, g * m)


def _layer(x, p, batch, w_in, w_out, w_pool, pool_scale, rel_bias, ssm, w_glu, b_glu, g_pool, g_attn, g_ssm,
           ln1_g, ln1_b, ln2_g, ln2_b, w_router, router_bias, w_exp_gate, w_exp_up, w_exp_down,
           w_ple_gate, w_ple_proj):
    t = x.shape[0]
    vec = lambda a: a.astype(F32).reshape(1, -1)
    u_pool, q, k, v, u_ssm = _inproj(x, w_in.astype(BF16))
    y_pool = _pool(u_pool, _block_diag(w_pool).astype(BF16), vec(pool_scale), vec(g_pool), batch)
    y_attn = _attention(q, k, v, _attn_bias_table(rel_bias), vec(g_attn), batch)
    nch = t // SSM_CHUNK
    u_grouped = (u_ssm.reshape(nch, SSM_CHUNK, SSM_GROUPS, SSM_GROUP_DIM)
                 .transpose(2, 0, 1, 3).reshape(SSM_GROUPS, nch, SSM_CHUNK_WIDTH))
    y_grouped = _s5(u_grouped, _s5_tables(*ssm), batch)
    y_ssm = (y_grouped.reshape(SSM_GROUPS, nch, SSM_CHUNK, SSM_GROUP_DIM)
             .transpose(1, 2, 0, 3).reshape(t, SSM_WIDTH))
    h, comb_t = _outproj(x, y_pool, y_attn, y_ssm, w_glu.astype(BF16), vec(b_glu), vec(g_ssm),
                         w_out.astype(BF16), vec(ln1_g), vec(ln1_b),
                         w_router.astype(F32).T, router_bias.astype(F32).reshape(N_EXPERTS, 1))
    steps = N_EXPERTS // MOE_EXPERTS_PER_STEP
    comb_steps = comb_t.reshape(steps, MOE_EXPERTS_PER_STEP, t).transpose(0, 2, 1)
    return _moe(h, comb_steps, p, w_exp_gate.astype(BF16), w_exp_up.astype(BF16), w_exp_down.astype(BF16),
                w_ple_gate.astype(BF16), w_ple_proj.astype(BF16), vec(ln2_g), vec(ln2_b))


def kernel(x, p, w_in, w_out, w_pool, pool_scale, rel_bias, ssm_a_re, ssm_a_im, ssm_log_dt, ssm_b_re, ssm_b_im,
           ssm_c_re, ssm_c_im, ssm_d, w_glu, b_glu, g_pool, g_attn, g_ssm, ln1_g, ln1_b, ln2_g, ln2_b,
           w_router, router_bias, w_exp_gate, w_exp_up, w_exp_down, w_ple_gate, w_ple_proj):
    batch, seq, d = x.shape
    t = batch * seq
    xt = x.reshape(t, d)
    for i in range(DEPTH):
        ssm = (ssm_a_re[i], ssm_a_im[i], ssm_log_dt[i], ssm_b_re[i], ssm_b_im[i], ssm_c_re[i], ssm_c_im[i],
               ssm_d[i])
        xt = _layer(xt, p[i].reshape(t, PLE_DIM), batch, w_in[i], w_out[i], w_pool[i], pool_scale[i], rel_bias[i],
                    ssm, w_glu[i], b_glu[i], g_pool[i], g_attn[i], g_ssm[i], ln1_g[i], ln1_b[i], ln2_g[i],
                    ln2_b[i], w_router, router_bias, w_exp_gate[i], w_exp_up[i], w_exp_down[i],
                    w_ple_gate[i], w_ple_proj[i])
    return xt.reshape(batch, seq, d)
```

```python
import functools

import numpy as np
import jax
import jax.numpy as jnp
from jax import lax
from jax.experimental import pallas as pl
from jax.experimental.pallas import tpu as pltpu

F32 = jnp.float32
BF16 = jnp.bfloat16

D_MODEL = 1024
DEPTH = 2
CHUNK = 64
PLE_DIM = 256
POOL_WIDTH = 256
POOL_GROUP_DIM = 64
POOL_WINDOWS = (2, 4, 8, 16)
POOL_HALO = 16
ATTN_HEAD_DIM = 64
ATTN_HEADS = 6
ATTN_WIDTH = 384
N_PREV_CHUNKS = 8
REL_CLIP = 128
SSM_WIDTH = 384
SSM_GROUP_DIM = 16
SSM_GROUPS = 24
SSM_STATE = 64
N_EXPERTS = 16
EXPERTS_PER_GROUP = 4
D_EXPERT = 256
DN_ALPHA = (2 * DEPTH) ** 0.25
NORM_EPS = 1e-5

LANES = 128
VMEM_LIMIT_BYTES = 56 * 1024 * 1024

INPROJ_TOKENS = 512
POOL_TOKENS = 512
ATTN_Q_CHUNKS = 4
ATTN_Q_TOKENS = ATTN_Q_CHUNKS * CHUNK
ATTN_BAND_TOKENS = 3 * ATTN_Q_TOKENS
ATTN_BIAS_ROW = 1024
SSM_CHUNK = 16
SSM_CHUNK_WIDTH = SSM_CHUNK * SSM_GROUP_DIM
SSM_ROWS = 128
SSM_SLOTS = LANES // SSM_GROUP_DIM
SSM_LANE_BLOCKS = SSM_WIDTH // LANES
SSM_TIME_BLOCKS = SSM_CHUNK // SSM_SLOTS
SSM_PAIR_WIDTH = 2 * SSM_STATE
SSM_STATE_LANES = SSM_GROUPS * SSM_STATE
OUT_TOKENS = 512
MOE_TOKENS = 512
MOE_EXPERTS_PER_STEP = 2


def _cparams(*sem):
    return pltpu.CompilerParams(dimension_semantics=sem, vmem_limit_bytes=VMEM_LIMIT_BYTES)


def _dot(a, b):
    return jnp.dot(a, b, preferred_element_type=F32)


def _layer_norm(v, g, b):
    mu = jnp.mean(v, axis=-1, keepdims=True)
    vc = v - mu
    var = jnp.mean(vc * vc, axis=-1, keepdims=True)
    return vc * lax.rsqrt(var + NORM_EPS) * g + b


_Z_POOL = (0, 256)
_Z_Q = (256, 640)
_Z_K = (640, 1024)
_Z_V = (1024, 1408)
_Z_SSM = (1408, 1792)


def _inproj_body(x_ref, w_ref, up_ref, q_ref, k_ref, v_ref, us_ref):
    xb = x_ref[...].astype(BF16)

    def seg(lo_hi):
        return _dot(xb, w_ref[:, lo_hi[0]:lo_hi[1]])

    up_ref[...] = seg(_Z_POOL)
    q_ref[...] = (seg(_Z_Q) * (ATTN_HEAD_DIM ** -0.5)).astype(BF16)
    k_ref[...] = seg(_Z_K).astype(BF16)
    v_ref[...] = seg(_Z_V).astype(BF16)
    us_ref[...] = seg(_Z_SSM)


def _inproj(x, w_bf16):
    t = x.shape[0]
    tm = INPROJ_TOKENS
    row = lambda width: pl.BlockSpec((tm, width), lambda i: (i, 0))
    return pl.pallas_call(
        _inproj_body,
        grid=(t // tm,),
        in_specs=[row(D_MODEL), pl.BlockSpec(w_bf16.shape, lambda i: (0, 0))],
        out_specs=[row(POOL_WIDTH), row(ATTN_WIDTH), row(ATTN_WIDTH), row(ATTN_WIDTH), row(SSM_WIDTH)],
        out_shape=[jax.ShapeDtypeStruct((t, POOL_WIDTH), F32),
                   jax.ShapeDtypeStruct((t, ATTN_WIDTH), BF16),
                   jax.ShapeDtypeStruct((t, ATTN_WIDTH), BF16),
                   jax.ShapeDtypeStruct((t, ATTN_WIDTH), BF16),
                   jax.ShapeDtypeStruct((t, SSM_WIDTH), F32)],
        compiler_params=_cparams("parallel"),
        name="inproj",
    )(x, w_bf16)


def _pool_body(u_ref, halo_ref, w_ref, scale_ref, g_ref, o_ref, buf):
    i = pl.program_id(1)
    tp = u_ref.shape[0]
    x0 = u_ref[...]
    buf[0:POOL_HALO, :] = jnp.where(i == 0, 0.0, halo_ref[...])
    buf[POOL_HALO:, :] = x0
    pos = i * tp + lax.broadcasted_iota(jnp.int32, (tp, 1), 0)
    group = lax.broadcasted_iota(jnp.int32, (1, POOL_WIDTH), 1) // POOL_GROUP_DIM
    s = x0
    mean = jnp.zeros_like(x0)
    k = 1
    for gi, w in enumerate(POOL_WINDOWS):
        while k < w:
            s = s + buf[pl.ds(POOL_HALO - k, tp), :]
            k += 1
        cnt = jnp.minimum(pos + 1, w).astype(F32)
        mean = jnp.where(group == gi, s / cnt, mean)
    d = (mean - x0).astype(BF16)
    y = _dot(d, w_ref[...]) * scale_ref[...]
    r = lax.rsqrt(jnp.mean(y * y, axis=-1, keepdims=True) + NORM_EPS)
    o_ref[...] = (y * r * g_ref[...]).astype(BF16)


def _pool(u_pool, w_blockdiag_bf16, scale, gain, batch):
    t = u_pool.shape[0]
    tp = POOL_TOKENS
    nt = t // batch // tp
    halo_blocks = tp // POOL_HALO
    vec = pl.BlockSpec((1, POOL_WIDTH), lambda b, i: (0, 0))
    return pl.pallas_call(
        _pool_body,
        grid=(batch, nt),
        in_specs=[pl.BlockSpec((tp, POOL_WIDTH), lambda b, i: (b * nt + i, 0)),
                  pl.BlockSpec((POOL_HALO, POOL_WIDTH),
                               lambda b, i: (jnp.maximum((b * nt + i) * halo_blocks - 1, 0), 0)),
                  pl.BlockSpec((POOL_WIDTH, POOL_WIDTH), lambda b, i: (0, 0)),
                  vec, vec],
        out_specs=pl.BlockSpec((tp, POOL_WIDTH), lambda b, i: (b * nt + i, 0)),
        out_shape=jax.ShapeDtypeStruct((t, POOL_WIDTH), BF16),
        scratch_shapes=[pltpu.VMEM((POOL_HALO + tp, POOL_WIDTH), F32)],
        compiler_params=_cparams("parallel", "parallel"),
        name="pool",
    )(u_pool, u_pool, w_blockdiag_bf16, scale, gain)


def _attn_bias_rows(rel_bias):
    x = np.arange(ATTN_BIAS_ROW)
    x = np.where(x < ATTN_BAND_TOKENS, x, x - ATTN_BIAS_ROW)
    idx = np.clip(N_PREV_CHUNKS * CHUNK - x, -REL_CLIP, REL_CLIP) + REL_CLIP
    return rel_bias.astype(F32)[:, idx]


def _attn_body(q_ref, k0_ref, k1_ref, k2_ref, v0_ref, v1_ref, v2_ref, rows_ref, g_ref, o_ref, bias_s):
    b = pl.program_id(0)
    i = pl.program_id(1)
    tq = ATTN_Q_TOKENS

    @pl.when((b == 0) & (i == 0))
    def _():
        qc = lax.broadcasted_iota(jnp.int32, (tq, ATTN_BAND_TOKENS), 0) // CHUNK
        kc = lax.broadcasted_iota(jnp.int32, (tq, ATTN_BAND_TOKENS), 1) // CHUNK
        in_band = (kc >= qc) & (kc <= qc + N_PREV_CHUNKS)
        for head in range(ATTN_HEADS):
            full = jnp.broadcast_to(rows_ref[head:head + 1, :], (tq, ATTN_BIAS_ROW))
            shifted = pltpu.roll(full, 0, 1, stride=1, stride_axis=0)
            bias_s[head] = jnp.where(in_band, shifted[:, :ATTN_BAND_TOKENS], -jnp.inf)

    col = lax.broadcasted_iota(jnp.int32, (1, ATTN_BAND_TOKENS), 1)
    before_start = ((col < tq) & (i < 2)) | ((col < 2 * tq) & (i < 1))
    upper_half = lax.broadcasted_iota(jnp.int32, (1, LANES), 1) >= ATTN_HEAD_DIM
    outs = []
    for pair in range(ATTN_WIDTH // LANES):
        sl = slice(pair * LANES, (pair + 1) * LANES)
        qp = q_ref[:, sl]
        kp = jnp.concatenate([k0_ref[:, sl], k1_ref[:, sl], k2_ref[:, sl]], axis=0)
        vp = jnp.concatenate([v0_ref[:, sl], v1_ref[:, sl], v2_ref[:, sl]], axis=0)
        o_pair = None
        for half in range(2):
            head = 2 * pair + half
            qm = jnp.where(upper_half == bool(half), qp, jnp.zeros_like(qp))
            s = lax.dot_general(qm, kp, (((1,), (1,)), ((), ())), preferred_element_type=F32)
            s = s + bias_s[head]
            s = jnp.where(before_start, -jnp.inf, s)
            m = jnp.max(s, axis=-1, keepdims=True)
            p = jnp.exp(s - m)
            l = jnp.sum(p, axis=-1, keepdims=True)
            o = _dot(p.astype(BF16), vp) / l
            o_pair = o if half == 0 else jnp.where(upper_half, o, o_pair)
        outs.append(o_pair)
    ss = sum(jnp.sum(o * o, axis=-1, keepdims=True) for o in outs)
    r = lax.rsqrt(ss / ATTN_WIDTH + NORM_EPS)
    for pair, o in enumerate(outs):
        sl = slice(pair * LANES, (pair + 1) * LANES)
        o_ref[:, sl] = (o * r * g_ref[:, sl]).astype(BF16)


def _attention(q, k, v, bias_rows, gain, batch):
    t = q.shape[0]
    tq = ATTN_Q_TOKENS
    nq = t // batch // tq

    def blk(back):
        return pl.BlockSpec((tq, ATTN_WIDTH), lambda b, i: (b * nq + jnp.maximum(i - back, 0), 0))

    return pl.pallas_call(
        _attn_body,
        grid=(batch, nq),
        in_specs=[blk(0), blk(2), blk(1), blk(0), blk(2), blk(1), blk(0),
                  pl.BlockSpec(bias_rows.shape, lambda b, i: (0, 0)),
                  pl.BlockSpec((1, ATTN_WIDTH), lambda b, i: (0, 0))],
        out_specs=blk(0),
        out_shape=jax.ShapeDtypeStruct((t, ATTN_WIDTH), BF16),
        scratch_shapes=[pltpu.VMEM((ATTN_HEADS, tq, ATTN_BAND_TOKENS), F32)],
        compiler_params=_cparams("arbitrary", "arbitrary"),
        name="attention",
    )(q, k, k, k, v, v, v, bias_rows, gain)


def _s5_time_of_position():
    g = np.arange(SSM_GROUPS)[:, None]
    pos = np.arange(SSM_CHUNK)[None, :]
    return SSM_SLOTS * (pos // SSM_SLOTS) + (pos % SSM_SLOTS - g) % SSM_SLOTS


def _s5_tables(a_re, a_im, log_dt, b_re, b_im, c_re, c_im):
    hi = lax.Precision.HIGHEST
    tc = SSM_CHUNK
    g, p_dim = a_re.shape
    dt = jnp.exp(log_dt.astype(F32))[:, None]
    ar = a_re.astype(F32)
    ai = a_im.astype(F32)
    mag = jnp.exp(ar * dt)
    abar_re = mag * jnp.cos(ai * dt)
    abar_im = mag * jnp.sin(ai * dt)
    den = ar * ar + ai * ai
    nr = abar_re - 1.0
    ni = abar_im
    coef_re = ((nr * ar + ni * ai) / den)[..., None]
    coef_im = ((ni * ar - nr * ai) / den)[..., None]
    br = b_re.astype(F32)
    bi = b_im.astype(F32)
    bbar_re = coef_re * br - coef_im * bi
    bbar_im = coef_re * bi + coef_im * br
    n = jnp.arange(tc + 1, dtype=F32)[:, None, None]
    pmag = jnp.exp(n * (ar * dt))
    pw_re = pmag * jnp.cos(n * (ai * dt))
    pw_im = pmag * jnp.sin(n * (ai * dt))
    x_re = pw_re[..., None] * bbar_re[None] - pw_im[..., None] * bbar_im[None]
    x_im = pw_re[..., None] * bbar_im[None] + pw_im[..., None] * bbar_re[None]
    cr = c_re.astype(F32)
    ci = c_im.astype(F32)
    kern = (jnp.einsum('ghp,ngpk->nghk', cr, x_re[:tc], precision=hi)
            - jnp.einsum('ghp,ngpk->nghk', ci, x_im[:tc], precision=hi))

    time = _s5_time_of_position()
    lag = time[:, None, :] - time[:, :, None]
    sel_lag = jnp.asarray(lag[..., None] == np.arange(tc), F32)
    sel_end = jnp.asarray((tc - 1 - time)[..., None] == np.arange(tc), F32)
    sel_next = jnp.asarray((time + 1)[..., None] == np.arange(tc + 1), F32)
    cw = tc * SSM_GROUP_DIM
    toep = jnp.einsum('gjtl,lghk->gjkth', sel_lag, kern, precision=hi).reshape(g, cw, cw)
    est_re = jnp.einsum('gjn,ngpk->gjkp', sel_end, x_re[:tc], precision=hi).reshape(g, cw, p_dim)
    est_im = jnp.einsum('gjn,ngpk->gjkp', sel_end, x_im[:tc], precision=hi).reshape(g, cw, p_dim)
    nx_re = jnp.einsum('gtn,ngp->gtp', sel_next, pw_re, precision=hi)
    nx_im = jnp.einsum('gtn,ngp->gtp', sel_next, pw_im, precision=hi)
    int_re = (cr[:, None] * nx_re[:, :, None] - ci[:, None] * nx_im[:, :, None])
    int_im = -(cr[:, None] * nx_im[:, :, None] + ci[:, None] * nx_re[:, :, None])
    int_re = int_re.transpose(0, 3, 1, 2).reshape(g, p_dim, cw)
    int_im = int_im.transpose(0, 3, 1, 2).reshape(g, p_dim, cw)

    odd = (jnp.arange(g) % 2 == 1)[:, None, None]
    zero_e = jnp.zeros_like(est_re)
    lane_pad = lambda m: jnp.where(odd, jnp.concatenate([zero_e, m], -1), jnp.concatenate([m, zero_e], -1))
    zero_i = jnp.zeros_like(int_re)
    row_pad = lambda m: jnp.where(odd, jnp.concatenate([zero_i, m], 1), jnp.concatenate([m, zero_i], 1))
    return dict(
        toep=toep.astype(BF16),
        est=jnp.concatenate([lane_pad(est_re), lane_pad(est_im)], -1).astype(BF16),
        inter=jnp.concatenate([row_pad(int_re), row_pad(int_im)], 1).astype(BF16),
        apow_re=pw_re[tc].reshape(1, g * p_dim), apow_im=pw_im[tc].reshape(1, g * p_dim))


def _s5_body(u_ref, toep_ref, est_ref, int_ref, apre_ref, apim_ref, d_ref, y_ref,
             ub_s, ere_s, eim_s, spre_s, spim_s, sre_s, sim_s):
    rows = u_ref.shape[0]

    @pl.when(pl.program_id(1) == 0)
    def _():
        sre_s[...] = jnp.zeros_like(sre_s)
        sim_s[...] = jnp.zeros_like(sim_s)

    slot = lax.broadcasted_iota(jnp.int32, (1, LANES), 1) // SSM_GROUP_DIM

    def piece(t, v):
        lo = SSM_WIDTH * t + LANES * v
        return slice(lo, lo + LANES)

    for v in range(SSM_LANE_BLOCKS):
        for m in range(SSM_TIME_BLOCKS):
            rot = []
            for j in range(SSM_SLOTS):
                a = u_ref[:, piece(SSM_SLOTS * m + j, v)]
                rot.append(a if j == 0 else pltpu.roll(a, SSM_GROUP_DIM * j, 1))
            for gam in range(SSM_SLOTS):
                o = rot[0]
                for j in range(1, SSM_SLOTS):
                    o = jnp.where(slot == (j + gam) % SSM_SLOTS, rot[j], o)
                ub_s[SSM_SLOTS * v + gam, :, m * LANES:(m + 1) * LANES] = o.astype(BF16)

    for q in range(SSM_GROUPS // 2):
        e = _dot(ub_s[2 * q], est_ref[2 * q]) + _dot(ub_s[2 * q + 1], est_ref[2 * q + 1])
        ere_s[:, q * LANES:(q + 1) * LANES] = e[:, :SSM_PAIR_WIDTH]
        eim_s[:, q * LANES:(q + 1) * LANES] = e[:, SSM_PAIR_WIDTH:]

    a_re = apre_ref[...]
    a_im = apim_ref[...]

    def carry_step(r, carry):
        s_re, s_im = carry
        spre_s[pl.ds(r, 1), :] = s_re
        spim_s[pl.ds(r, 1), :] = s_im
        e_re = ere_s[pl.ds(r, 1), :]
        e_im = eim_s[pl.ds(r, 1), :]
        return (a_re * s_re - a_im * s_im + e_re, a_re * s_im + a_im * s_re + e_im)

    s_re, s_im = lax.fori_loop(0, rows, carry_step, (sre_s[...], sim_s[...]))
    sre_s[...] = s_re
    sim_s[...] = s_im

    for v in range(SSM_LANE_BLOCKS):
        yg = []
        for gam in range(SSM_SLOTS):
            g = SSM_SLOTS * v + gam
            q = g // 2
            sp = jnp.concatenate([spre_s[:, q * LANES:(q + 1) * LANES], spim_s[:, q * LANES:(q + 1) * LANES]],
                                 axis=1).astype(BF16)
            yg.append(_dot(ub_s[g], toep_ref[g]) + _dot(sp, int_ref[g]))
        d = d_ref[:, v * LANES:(v + 1) * LANES]
        for m in range(SSM_TIME_BLOCKS):
            for j in range(SSM_SLOTS):
                o = yg[0][:, m * LANES:(m + 1) * LANES]
                for gam in range(1, SSM_SLOTS):
                    o = jnp.where(slot == (j + gam) % SSM_SLOTS, yg[gam][:, m * LANES:(m + 1) * LANES], o)
                if j:
                    o = pltpu.roll(o, LANES - SSM_GROUP_DIM * j, 1)
                sl = piece(SSM_SLOTS * m + j, v)
                y_ref[:, sl] = jax.nn.gelu(o + d * u_ref[:, sl])


def _s5(u_ssm, tab, d_skip, batch):
    t = u_ssm.shape[0]
    nch = t // SSM_CHUNK
    rows = SSM_ROWS
    steps = nch // batch // rows
    width = SSM_CHUNK * SSM_WIDTH
    u_rows = u_ssm.reshape(nch, width)
    full = lambda a: pl.BlockSpec(a.shape, lambda b, i: (0,) * a.ndim)
    blk = pl.BlockSpec((rows, width), lambda b, i: (b * steps + i, 0))
    state = pltpu.VMEM((rows, SSM_STATE_LANES), F32)
    carry = pltpu.VMEM((1, SSM_STATE_LANES), F32)
    y_rows = pl.pallas_call(
        _s5_body,
        grid=(batch, steps),
        in_specs=[blk, full(tab['toep']), full(tab['est']), full(tab['inter']),
                  full(tab['apow_re']), full(tab['apow_im']), full(d_skip)],
        out_specs=blk,
        out_shape=jax.ShapeDtypeStruct((nch, width), F32),
        scratch_shapes=[pltpu.VMEM((SSM_GROUPS, rows, SSM_CHUNK_WIDTH), BF16), state, state, state, state,
                        carry, carry],
        compiler_params=_cparams("arbitrary", "arbitrary"),
        name="s5",
    )(u_rows, tab['toep'], tab['est'], tab['inter'], tab['apow_re'], tab['apow_im'], d_skip)
    return y_rows.reshape(t, SSM_WIDTH)


def _route_rows(scores, biased):
    ng = N_EXPERTS // EXPERTS_PER_GROUP
    group_score = []
    for gi in range(ng):
        a, b, c, d = biased[gi * EXPERTS_PER_GROUP:(gi + 1) * EXPERTS_PER_GROUP]
        hi1, lo1 = jnp.maximum(a, b), jnp.minimum(a, b)
        hi2, lo2 = jnp.maximum(c, d), jnp.minimum(c, d)
        top1 = jnp.maximum(hi1, hi2)
        top2 = jnp.maximum(jnp.minimum(hi1, hi2), jnp.maximum(lo1, lo2))
        group_score.append(top1 + top2)
    best = group_score[0]
    best_idx = jnp.zeros_like(best, dtype=jnp.int32)
    for gi in range(1, ng):
        better = group_score[gi] > best
        best = jnp.where(better, group_score[gi], best)
        best_idx = jnp.where(better, gi, best_idx)
    picked = []
    for e in range(N_EXPERTS):
        gi = e // EXPERTS_PER_GROUP
        rank = jnp.zeros_like(best_idx)
        for o in range(gi * EXPERTS_PER_GROUP, (gi + 1) * EXPERTS_PER_GROUP):
            if o == e:
                continue
            ahead = (biased[o] > biased[e]) | ((biased[o] == biased[e]) & (o < e))
            rank = rank + ahead.astype(jnp.int32)
        picked.append((best_idx == gi) & (rank < 2))
    wsum = sum(jnp.where(picked[e], scores[e], 0.0) for e in range(N_EXPERTS))
    return [jnp.where(picked[e], scores[e] / wsum, 0.0) for e in range(N_EXPERTS)]


def _outproj_body(x_ref, yp_ref, ya_ref, ys_ref, wglu_ref, bglu_ref, gssm_ref, wout_ref, g_ref, b_ref,
                  wrt_ref, rb_ref, h_ref, comb_ref):
    ys = ys_ref[...]
    gate = jax.nn.sigmoid(_dot(ys.astype(BF16), wglu_ref[...]) + bglu_ref[...])
    ys = ys * gate
    r = lax.rsqrt(jnp.mean(ys * ys, axis=-1, keepdims=True) + NORM_EPS)
    ysn = (ys * r * gssm_ref[...]).astype(BF16)
    mix = (_dot(yp_ref[...], wout_ref[0:POOL_WIDTH, :])
           + _dot(ya_ref[...], wout_ref[POOL_WIDTH:POOL_WIDTH + ATTN_WIDTH, :])
           + _dot(ysn, wout_ref[POOL_WIDTH + ATTN_WIDTH:, :]))
    h = _layer_norm(DN_ALPHA * x_ref[...] + mix, g_ref[...], b_ref[...])
    h_ref[...] = h
    logits = lax.dot_general(wrt_ref[...], h, (((1,), (1,)), ((), ())),
                             precision=lax.Precision.HIGHEST, preferred_element_type=F32)
    sc = jax.nn.sigmoid(logits)
    bs = sc + rb_ref[...]
    scores = [sc[e:e + 1, :] for e in range(N_EXPERTS)]
    biased = [bs[e:e + 1, :] for e in range(N_EXPERTS)]
    comb_t = jnp.concatenate(_route_rows(scores, biased)
                             + [jnp.zeros((LANES - N_EXPERTS, sc.shape[1]), F32)], axis=0)
    comb_ref[...] = comb_t.T


def _outproj(x, y_pool, y_attn, y_ssm, w_glu_bf16, b_glu, g_ssm, w_out_bf16, ln_g, ln_b, w_router_t, router_bias):
    t = x.shape[0]
    tm = OUT_TOKENS
    row = lambda width: pl.BlockSpec((tm, width), lambda i: (i, 0))
    full = lambda a: pl.BlockSpec(a.shape, lambda i: (0,) * a.ndim)
    return pl.pallas_call(
        _outproj_body,
        grid=(t // tm,),
        in_specs=[row(D_MODEL), row(POOL_WIDTH), row(ATTN_WIDTH), row(SSM_WIDTH),
                  full(w_glu_bf16), full(b_glu), full(g_ssm), full(w_out_bf16), full(ln_g), full(ln_b),
                  full(w_router_t), full(router_bias)],
        out_specs=[row(D_MODEL), row(LANES)],
        out_shape=[jax.ShapeDtypeStruct((t, D_MODEL), F32), jax.ShapeDtypeStruct((t, LANES), F32)],
        compiler_params=_cparams("parallel"),
        name="outproj",
    )(x, y_pool, y_attn, y_ssm, w_glu_bf16, b_glu, g_ssm, w_out_bf16, ln_g, ln_b, w_router_t, router_bias)


def _moe_body(h_ref, comb_ref, p_ref, wg_ref, wu_ref, wd_ref, wpg_ref, wpp_ref, g_ref, b_ref, o_ref,
              hb_s, acc_s):
    j = pl.program_id(1)

    @pl.when(j == 0)
    def _():
        hb_s[...] = h_ref[...].astype(BF16)
        acc_s[...] = jnp.zeros_like(acc_s)

    hb = hb_s[...]
    comb = comb_ref[...]
    lane = lax.broadcasted_iota(jnp.int32, (1, LANES), 1)
    total = None
    for e in range(MOE_EXPERTS_PER_STEP):
        gate = _dot(hb, wg_ref[e])
        up = _dot(hb, wu_ref[e])
        c = jnp.sum(jnp.where(lane == j * MOE_EXPERTS_PER_STEP + e, comb, 0.0), axis=1, keepdims=True)
        a = (jax.nn.silu(gate) * up * c).astype(BF16)
        d = _dot(a, wd_ref[e])
        total = d if total is None else total + d
    acc_s[...] += total

    @pl.when(j == pl.num_programs(1) - 1)
    def _():
        ple = jax.nn.sigmoid(_dot(hb, wpg_ref[...])) * _dot(p_ref[...].astype(BF16), wpp_ref[...])
        o_ref[...] = _layer_norm(DN_ALPHA * h_ref[...] + acc_s[...] + ple, g_ref[...], b_ref[...])


def _moe(h, comb, p_all, layer, wg_bf16, wu_bf16, wd_bf16, wpg_bf16, wpp_bf16, ln_g, ln_b):
    t = h.shape[0]
    tm = MOE_TOKENS
    nt = t // tm
    eps = MOE_EXPERTS_PER_STEP
    full = lambda a: pl.BlockSpec(a.shape, lambda i, j: (0,) * a.ndim)
    return pl.pallas_call(
        _moe_body,
        grid=(nt, N_EXPERTS // eps),
        in_specs=[pl.BlockSpec((tm, D_MODEL), lambda i, j: (i, 0)),
                  pl.BlockSpec((tm, LANES), lambda i, j: (i, 0)),
                  pl.BlockSpec((tm, PLE_DIM), lambda i, j: (layer * nt + i, 0)),
                  pl.BlockSpec((eps, D_MODEL, D_EXPERT), lambda i, j: (j, 0, 0)),
                  pl.BlockSpec((eps, D_MODEL, D_EXPERT), lambda i, j: (j, 0, 0)),
                  pl.BlockSpec((eps, D_EXPERT, D_MODEL), lambda i, j: (j, 0, 0)),
                  full(wpg_bf16), full(wpp_bf16), full(ln_g), full(ln_b)],
        out_specs=pl.BlockSpec((tm, D_MODEL), lambda i, j: (i, 0)),
        out_shape=jax.ShapeDtypeStruct((t, D_MODEL), F32),
        scratch_shapes=[pltpu.VMEM((tm, D_MODEL), BF16), pltpu.VMEM((tm, D_MODEL), F32)],
        compiler_params=_cparams("parallel", "arbitrary"),
        name="moe",
    )(h, comb, p_all, wg_bf16, wu_bf16, wd_bf16, wpg_bf16, wpp_bf16, ln_g, ln_b)


def _block_diag(w):
    g, n, m = w.shape
    eye = jnp.eye(g, dtype=w.dtype)
    return (eye[:, None, :, None] * w[:, :, None, :]).reshape(g * n, g * m)


def _layer(x, p_all, layer, batch, w_in, w_out, w_pool, pool_scale, rel_bias, ssm, ssm_d, w_glu, b_glu,
           g_pool, g_attn, g_ssm, ln1_g, ln1_b, ln2_g, ln2_b, w_router, router_bias,
           w_exp_gate, w_exp_up, w_exp_down, w_ple_gate, w_ple_proj):
    vec = lambda a: a.astype(F32).reshape(1, -1)
    u_pool, q, k, v, u_ssm = _inproj(x, w_in.astype(BF16))
    y_pool = _pool(u_pool, _block_diag(w_pool).astype(BF16), vec(pool_scale), vec(g_pool), batch)
    y_attn = _attention(q, k, v, _attn_bias_rows(rel_bias), vec(g_attn), batch)
    y_ssm = _s5(u_ssm, _s5_tables(*ssm), vec(ssm_d), batch)
    h, comb = _outproj(x, y_pool, y_attn, y_ssm, w_glu.astype(BF16), vec(b_glu), vec(g_ssm),
                       w_out.astype(BF16), vec(ln1_g), vec(ln1_b),
                       w_router.astype(F32).T, router_bias.astype(F32).reshape(N_EXPERTS, 1))
    return _moe(h, comb, p_all, layer, w_exp_gate.astype(BF16), w_exp_up.astype(BF16), w_exp_down.astype(BF16),
                w_ple_gate.astype(BF16), w_ple_proj.astype(BF16), vec(ln2_g), vec(ln2_b))


def kernel(x, p, w_in, w_out, w_pool, pool_scale, rel_bias, ssm_a_re, ssm_a_im, ssm_log_dt, ssm_b_re, ssm_b_im,
           ssm_c_re, ssm_c_im, ssm_d, w_glu, b_glu, g_pool, g_attn, g_ssm, ln1_g, ln1_b, ln2_g, ln2_b,
           w_router, router_bias, w_exp_gate, w_exp_up, w_exp_down, w_ple_gate, w_ple_proj):
    batch, seq, d = x.shape
    t = batch * seq
    xt = x.reshape(t, d)
    p_all = p.reshape(DEPTH * t, PLE_DIM)
    for i in range(DEPTH):
        ssm = (ssm_a_re[i], ssm_a_im[i], ssm_log_dt[i], ssm_b_re[i], ssm_b_im[i], ssm_c_re[i], ssm_c_im[i])
        xt = _layer(xt, p_all, i, batch, w_in[i], w_out[i], w_pool[i], pool_scale[i], rel_bias[i],
                    ssm, ssm_d[i], w_glu[i], b_glu[i], g_pool[i], g_attn[i], g_ssm[i], ln1_g[i], ln1_b[i],
                    ln2_g[i], ln2_b[i], w_router, router_bias, w_exp_gate[i], w_exp_up[i], w_exp_down[i],
                    w_ple_gate[i], w_ple_proj[i])
    return xt.reshape(batch, seq, d)
```

```python
import functools

import numpy as np
import jax
import jax.numpy as jnp
from jax import lax
from jax.experimental import pallas as pl
from jax.experimental.pallas import tpu as pltpu

F32 = jnp.float32
BF16 = jnp.bfloat16

D_MODEL = 1024
DEPTH = 2
CHUNK = 64
PLE_DIM = 256
POOL_WIDTH = 256
POOL_GROUP_DIM = 64
POOL_WINDOWS = (2, 4, 8, 16)
POOL_HALO = 16
ATTN_HEAD_DIM = 64
ATTN_HEADS = 6
ATTN_WIDTH = 384
N_PREV_CHUNKS = 8
REL_CLIP = 128
SSM_WIDTH = 384
SSM_GROUP_DIM = 16
SSM_GROUPS = 24
SSM_STATE = 64
N_EXPERTS = 16
EXPERTS_PER_GROUP = 4
D_EXPERT = 256
DN_ALPHA = (2 * DEPTH) ** 0.25
NORM_EPS = 1e-5

LANES = 128
VMEM_LIMIT_BYTES = 56 * 1024 * 1024

INPROJ_TOKENS = 1024
POOL_TOKENS = 1024
ATTN_Q_CHUNKS = 4
ATTN_Q_TOKENS = ATTN_Q_CHUNKS * CHUNK
ATTN_BAND_TOKENS = 3 * ATTN_Q_TOKENS
ATTN_BIAS_ROW = 1024
SSM_CHUNK = 16
SSM_CHUNK_WIDTH = SSM_CHUNK * SSM_GROUP_DIM
SSM_ROWS = 128
SSM_SLOTS = LANES // SSM_GROUP_DIM
SSM_LANE_BLOCKS = SSM_WIDTH // LANES
SSM_TIME_BLOCKS = SSM_CHUNK // SSM_SLOTS
SSM_PAIR_WIDTH = 2 * SSM_STATE
SSM_STATE_LANES = SSM_GROUPS * SSM_STATE
OUT_TOKENS = 1024
MOE_TOKENS = 512
MOE_EXPERTS_PER_STEP = 2


def _cparams(*sem):
    return pltpu.CompilerParams(dimension_semantics=sem, vmem_limit_bytes=VMEM_LIMIT_BYTES)


def _dot(a, b):
    return jnp.dot(a, b, preferred_element_type=F32)


def _layer_norm(v, g, b):
    mu = jnp.mean(v, axis=-1, keepdims=True)
    vc = v - mu
    var = jnp.mean(vc * vc, axis=-1, keepdims=True)
    return vc * lax.rsqrt(var + NORM_EPS) * g + b


def _ssm_piece(t, v):
    lo = SSM_WIDTH * t + LANES * v
    return slice(lo, lo + LANES)


def _inproj_body(x_ref, w_ref, up_ref, q_ref, k_ref, v_ref, us_ref, zs):
    xb = x_ref[...].astype(BF16)

    def cols(lo, hi):
        return _dot(xb, w_ref[:, lo:hi])

    up_ref[...] = cols(0, 256)
    qk = cols(256, 768)
    q_ref[...] = (qk[:, :ATTN_WIDTH] * (ATTN_HEAD_DIM ** -0.5)).astype(BF16)
    k_ref[:, :LANES] = qk[:, ATTN_WIDTH:].astype(BF16)
    k_ref[:, LANES:] = cols(768, 1024).astype(BF16)
    vs = cols(1024, 1536)
    v_ref[...] = vs[:, :ATTN_WIDTH].astype(BF16)
    zs[0] = vs[:, ATTN_WIDTH:]
    s_rest = cols(1536, 1792)
    zs[1] = s_rest[:, :LANES]
    zs[2] = s_rest[:, LANES:]
    chunks = us_ref.shape[0]
    for t in range(SSM_CHUNK):
        for v in range(SSM_LANE_BLOCKS):
            us_ref[:, _ssm_piece(t, v)] = zs[v, pl.ds(t, chunks, stride=SSM_CHUNK), :]


def _inproj(x, w_bf16):
    t = x.shape[0]
    tm = INPROJ_TOKENS
    row = lambda width: pl.BlockSpec((tm, width), lambda i: (i, 0))
    chunk_rows = pl.BlockSpec((tm // SSM_CHUNK, SSM_CHUNK * SSM_WIDTH), lambda i: (i, 0))
    return pl.pallas_call(
        _inproj_body,
        grid=(t // tm,),
        in_specs=[row(D_MODEL), pl.BlockSpec(w_bf16.shape, lambda i: (0, 0))],
        out_specs=[row(POOL_WIDTH), row(ATTN_WIDTH), row(ATTN_WIDTH), row(ATTN_WIDTH), chunk_rows],
        out_shape=[jax.ShapeDtypeStruct((t, POOL_WIDTH), F32),
                   jax.ShapeDtypeStruct((t, ATTN_WIDTH), BF16),
                   jax.ShapeDtypeStruct((t, ATTN_WIDTH), BF16),
                   jax.ShapeDtypeStruct((t, ATTN_WIDTH), BF16),
                   jax.ShapeDtypeStruct((t // SSM_CHUNK, SSM_CHUNK * SSM_WIDTH), F32)],
        scratch_shapes=[pltpu.VMEM((SSM_LANE_BLOCKS, tm, LANES), F32)],
        compiler_params=_cparams("parallel"),
        name="inproj",
    )(x, w_bf16)


def _pool_body(u_ref, halo_ref, w_ref, scale_ref, g_ref, o_ref, buf):
    i = pl.program_id(1)
    tp = u_ref.shape[0]
    x0 = u_ref[...]
    buf[0:POOL_HALO, :] = jnp.where(i == 0, 0.0, halo_ref[...])
    buf[POOL_HALO:, :] = x0
    pos = i * tp + lax.broadcasted_iota(jnp.int32, (tp, 1), 0)
    group = lax.broadcasted_iota(jnp.int32, (1, POOL_WIDTH), 1) // POOL_GROUP_DIM
    s = x0
    mean = jnp.zeros_like(x0)
    k = 1
    for gi, w in enumerate(POOL_WINDOWS):
        while k < w:
            s = s + buf[pl.ds(POOL_HALO - k, tp), :]
            k += 1
        cnt = jnp.minimum(pos + 1, w).astype(F32)
        mean = jnp.where(group == gi, s / cnt, mean)
    d = (mean - x0).astype(BF16)
    y = _dot(d, w_ref[...]) * scale_ref[...]
    r = lax.rsqrt(jnp.mean(y * y, axis=-1, keepdims=True) + NORM_EPS)
    o_ref[...] = (y * r * g_ref[...]).astype(BF16)


def _pool(u_pool, w_blockdiag_bf16, scale, gain, batch):
    t = u_pool.shape[0]
    tp = POOL_TOKENS
    nt = t // batch // tp
    halo_blocks = tp // POOL_HALO
    vec = pl.BlockSpec((1, POOL_WIDTH), lambda b, i: (0, 0))
    return pl.pallas_call(
        _pool_body,
        grid=(batch, nt),
        in_specs=[pl.BlockSpec((tp, POOL_WIDTH), lambda b, i: (b * nt + i, 0)),
                  pl.BlockSpec((POOL_HALO, POOL_WIDTH),
                               lambda b, i: (jnp.maximum((b * nt + i) * halo_blocks - 1, 0), 0)),
                  pl.BlockSpec((POOL_WIDTH, POOL_WIDTH), lambda b, i: (0, 0)),
                  vec, vec],
        out_specs=pl.BlockSpec((tp, POOL_WIDTH), lambda b, i: (b * nt + i, 0)),
        out_shape=jax.ShapeDtypeStruct((t, POOL_WIDTH), BF16),
        scratch_shapes=[pltpu.VMEM((POOL_HALO + tp, POOL_WIDTH), F32)],
        compiler_params=_cparams("parallel", "parallel"),
        name="pool",
    )(u_pool, u_pool, w_blockdiag_bf16, scale, gain)


def _attn_bias_rows(rel_bias):
    x = np.arange(ATTN_BIAS_ROW)
    x = np.where(x < ATTN_BAND_TOKENS, x, x - ATTN_BIAS_ROW)
    idx = np.clip(N_PREV_CHUNKS * CHUNK - x, -REL_CLIP, REL_CLIP) + REL_CLIP
    return rel_bias.astype(F32)[:, idx]


def _attn_body(q_ref, k0_ref, k1_ref, k2_ref, v0_ref, v1_ref, v2_ref, rows_ref, g_ref, o_ref, bias_s):
    b = pl.program_id(0)
    i = pl.program_id(1)
    tq = ATTN_Q_TOKENS

    @pl.when((b == 0) & (i == 0))
    def _():
        qc = lax.broadcasted_iota(jnp.int32, (tq, ATTN_BAND_TOKENS), 0) // CHUNK
        kc = lax.broadcasted_iota(jnp.int32, (tq, ATTN_BAND_TOKENS), 1) // CHUNK
        in_band = (kc >= qc) & (kc <= qc + N_PREV_CHUNKS)
        for head in range(ATTN_HEADS):
            full = jnp.broadcast_to(rows_ref[head:head + 1, :], (tq, ATTN_BIAS_ROW))
            shifted = pltpu.roll(full, 0, 1, stride=1, stride_axis=0)
            bias_s[head] = jnp.where(in_band, shifted[:, :ATTN_BAND_TOKENS], -jnp.inf)

    col = lax.broadcasted_iota(jnp.int32, (1, ATTN_BAND_TOKENS), 1)
    before_start = ((col < tq) & (i < 2)) | ((col < 2 * tq) & (i < 1))
    upper_half = lax.broadcasted_iota(jnp.int32, (1, LANES), 1) >= ATTN_HEAD_DIM
    outs = []
    for pair in range(ATTN_WIDTH // LANES):
        sl = slice(pair * LANES, (pair + 1) * LANES)
        qp = q_ref[:, sl]
        kp = jnp.concatenate([k0_ref[:, sl], k1_ref[:, sl], k2_ref[:, sl]], axis=0)
        vp = jnp.concatenate([v0_ref[:, sl], v1_ref[:, sl], v2_ref[:, sl]], axis=0)
        o_pair = None
        for half in range(2):
            head = 2 * pair + half
            qm = jnp.where(upper_half == bool(half), qp, jnp.zeros_like(qp))
            s = lax.dot_general(qm, kp, (((1,), (1,)), ((), ())), preferred_element_type=F32)
            s = s + bias_s[head]
            s = jnp.where(before_start, -jnp.inf, s)
            m = jnp.max(s, axis=-1, keepdims=True)
            p = jnp.exp(s - m)
            l = jnp.sum(p, axis=-1, keepdims=True)
            o = _dot(p.astype(BF16), vp) / l
            o_pair = o if half == 0 else jnp.where(upper_half, o, o_pair)
        outs.append(o_pair)
    ss = sum(jnp.sum(o * o, axis=-1, keepdims=True) for o in outs)
    r = lax.rsqrt(ss / ATTN_WIDTH + NORM_EPS)
    for pair, o in enumerate(outs):
        sl = slice(pair * LANES, (pair + 1) * LANES)
        o_ref[:, sl] = (o * r * g_ref[:, sl]).astype(BF16)


def _attention(q, k, v, bias_rows, gain, batch):
    t = q.shape[0]
    tq = ATTN_Q_TOKENS
    nq = t // batch // tq

    def blk(back):
        return pl.BlockSpec((tq, ATTN_WIDTH), lambda b, i: (b * nq + jnp.maximum(i - back, 0), 0))

    return pl.pallas_call(
        _attn_body,
        grid=(batch, nq),
        in_specs=[blk(0), blk(2), blk(1), blk(0), blk(2), blk(1), blk(0),
                  pl.BlockSpec(bias_rows.shape, lambda b, i: (0, 0)),
                  pl.BlockSpec((1, ATTN_WIDTH), lambda b, i: (0, 0))],
        out_specs=blk(0),
        out_shape=jax.ShapeDtypeStruct((t, ATTN_WIDTH), BF16),
        scratch_shapes=[pltpu.VMEM((ATTN_HEADS, tq, ATTN_BAND_TOKENS), F32)],
        compiler_params=_cparams("arbitrary", "arbitrary"),
        name="attention",
    )(q, k, k, k, v, v, v, bias_rows, gain)


def _s5_position_of_time(g, time):
    return SSM_SLOTS * (time // SSM_SLOTS) + (time % SSM_SLOTS + g) % SSM_SLOTS


def _s5_tables(a_re, a_im, log_dt, b_re, b_im, c_re, c_im):
    hi = lax.Precision.HIGHEST
    tc = SSM_CHUNK
    g, p_dim = a_re.shape
    dt = jnp.exp(log_dt.astype(F32))[:, None]
    ar = a_re.astype(F32)
    ai = a_im.astype(F32)
    mag = jnp.exp(ar * dt)
    abar_re = mag * jnp.cos(ai * dt)
    abar_im = mag * jnp.sin(ai * dt)
    den = ar * ar + ai * ai
    nr = abar_re - 1.0
    ni = abar_im
    coef_re = ((nr * ar + ni * ai) / den)[..., None]
    coef_im = ((ni * ar - nr * ai) / den)[..., None]
    br = b_re.astype(F32)
    bi = b_im.astype(F32)
    bbar_re = coef_re * br - coef_im * bi
    bbar_im = coef_re * bi + coef_im * br
    n = jnp.arange(tc + 1, dtype=F32)[:, None, None]
    pmag = jnp.exp(n * (ar * dt))
    pw_re = pmag * jnp.cos(n * (ai * dt))
    pw_im = pmag * jnp.sin(n * (ai * dt))
    x_re = pw_re[..., None] * bbar_re[None] - pw_im[..., None] * bbar_im[None]
    x_im = pw_re[..., None] * bbar_im[None] + pw_im[..., None] * bbar_re[None]
    cr = c_re.astype(F32)
    ci = c_im.astype(F32)
    cw = tc * SSM_GROUP_DIM
    kern = (jnp.einsum('ghp,ngpk->gknh', cr, x_re[:tc], precision=hi)
            - jnp.einsum('ghp,ngpk->gknh', ci, x_im[:tc], precision=hi)).reshape(g, SSM_GROUP_DIM, cw)
    est_re = x_re[:tc].transpose(1, 0, 3, 2).reshape(g, cw, p_dim)
    est_im = x_im[:tc].transpose(1, 0, 3, 2).reshape(g, cw, p_dim)
    crt = cr.transpose(0, 2, 1)[:, :, None, :]
    cit = ci.transpose(0, 2, 1)[:, :, None, :]
    nx_re = pw_re[1:].transpose(1, 2, 0)[..., None]
    nx_im = pw_im[1:].transpose(1, 2, 0)[..., None]
    int_re = (crt * nx_re - cit * nx_im).reshape(g, p_dim, cw)
    int_im = (-(crt * nx_im + cit * nx_re)).reshape(g, p_dim, cw)

    odd = (jnp.arange(g) % 2 == 1)[:, None, None]
    zero_e = jnp.zeros_like(est_re)
    lane_pad = lambda m: jnp.where(odd, jnp.concatenate([zero_e, m], -1), jnp.concatenate([m, zero_e], -1))
    zero_i = jnp.zeros_like(int_re)
    row_pad = lambda m: jnp.where(odd, jnp.concatenate([zero_i, m], 1), jnp.concatenate([m, zero_i], 1))
    return dict(
        kern=kern,
        est=jnp.concatenate([lane_pad(est_re), lane_pad(est_im)], -1).astype(BF16),
        inter=jnp.concatenate([row_pad(int_re), row_pad(int_im)], 1),
        apow_re=pw_re[tc].reshape(1, g * p_dim), apow_im=pw_im[tc].reshape(1, g * p_dim))


def _s5_prepare(kern_ref, est_ref, int_ref, toep_s, est_s, int_s):
    lane = lax.broadcasted_iota(jnp.int32, (1, LANES), 1)
    zero = jnp.zeros((SSM_GROUP_DIM, LANES), F32)
    for g in range(SSM_GROUPS):
        turn = SSM_GROUP_DIM * (g % SSM_SLOTS)
        spin = lambda a: pltpu.roll(a, turn, 1) if turn else a
        k0 = kern_ref[g, :, 0:LANES]
        k1 = kern_ref[g, :, LANES:2 * LANES]
        for time in range(SSM_CHUNK):
            shift = SSM_GROUP_DIM * (time % SSM_SLOTS)
            r0 = pltpu.roll(k0, shift, 1) if shift else k0
            if time < SSM_SLOTS:
                r1 = pltpu.roll(k1, shift, 1) if shift else k1
                h0 = jnp.where(lane >= shift, r0, zero)
                h1 = jnp.where(lane >= shift, r1, r0)
            else:
                h0 = zero
                h1 = jnp.where(lane >= shift, r0, zero)
            rows = pl.ds(SSM_GROUP_DIM * _s5_position_of_time(g, time), SSM_GROUP_DIM)
            toep_s[g, rows, 0:LANES] = spin(h0).astype(BF16)
            toep_s[g, rows, LANES:2 * LANES] = spin(h1).astype(BF16)
            est_s[g, rows, :] = est_ref[g, pl.ds(SSM_GROUP_DIM * (SSM_CHUNK - 1 - time), SSM_GROUP_DIM), :]
        for m in range(SSM_TIME_BLOCKS):
            int_s[g, :, m * LANES:(m + 1) * LANES] = spin(int_ref[g, :, m * LANES:(m + 1) * LANES]).astype(BF16)


def _s5_body(u_ref, kern_ref, estin_ref, intin_ref, apre_ref, apim_ref, d_ref, y_ref,
             toep_ref, est_ref, int_ref, ub_s, ere_s, eim_s, spre_s, spim_s, sre_s, sim_s):
    rows = u_ref.shape[0]

    @pl.when((pl.program_id(0) == 0) & (pl.program_id(1) == 0))
    def _():
        _s5_prepare(kern_ref, estin_ref, intin_ref, toep_ref, est_ref, int_ref)

    @pl.when(pl.program_id(1) == 0)
    def _():
        sre_s[...] = jnp.zeros_like(sre_s)
        sim_s[...] = jnp.zeros_like(sim_s)

    slot = lax.broadcasted_iota(jnp.int32, (1, LANES), 1) // SSM_GROUP_DIM
    piece = _ssm_piece

    for v in range(SSM_LANE_BLOCKS):
        for m in range(SSM_TIME_BLOCKS):
            rot = []
            for j in range(SSM_SLOTS):
                a = u_ref[:, piece(SSM_SLOTS * m + j, v)]
                rot.append(a if j == 0 else pltpu.roll(a, SSM_GROUP_DIM * j, 1))
            for gam in range(SSM_SLOTS):
                o = rot[0]
                for j in range(1, SSM_SLOTS):
                    o = jnp.where(slot == (j + gam) % SSM_SLOTS, rot[j], o)
                ub_s[SSM_SLOTS * v + gam, :, m * LANES:(m + 1) * LANES] = o.astype(BF16)

    for q in range(SSM_GROUPS // 2):
        e = _dot(ub_s[2 * q], est_ref[2 * q]) + _dot(ub_s[2 * q + 1], est_ref[2 * q + 1])
        ere_s[:, q * LANES:(q + 1) * LANES] = e[:, :SSM_PAIR_WIDTH]
        eim_s[:, q * LANES:(q + 1) * LANES] = e[:, SSM_PAIR_WIDTH:]

    a_re = apre_ref[...]
    a_im = apim_ref[...]

    def carry_step(r, carry):
        s_re, s_im = carry
        spre_s[pl.ds(r, 1), :] = s_re
        spim_s[pl.ds(r, 1), :] = s_im
        e_re = ere_s[pl.ds(r, 1), :]
        e_im = eim_s[pl.ds(r, 1), :]
        return (a_re * s_re - a_im * s_im + e_re, a_re * s_im + a_im * s_re + e_im)

    s_re, s_im = lax.fori_loop(0, rows, carry_step, (sre_s[...], sim_s[...]))
    sre_s[...] = s_re
    sim_s[...] = s_im

    for v in range(SSM_LANE_BLOCKS):
        yg = []
        for gam in range(SSM_SLOTS):
            g = SSM_SLOTS * v + gam
            q = g // 2
            sp = jnp.concatenate([spre_s[:, q * LANES:(q + 1) * LANES], spim_s[:, q * LANES:(q + 1) * LANES]],
                                 axis=1).astype(BF16)
            yg.append(_dot(ub_s[g], toep_ref[g]) + _dot(sp, int_ref[g]))
        d = d_ref[:, v * LANES:(v + 1) * LANES]
        for m in range(SSM_TIME_BLOCKS):
            for j in range(SSM_SLOTS):
                o = yg[0][:, m * LANES:(m + 1) * LANES]
                for gam in range(1, SSM_SLOTS):
                    o = jnp.where(slot == (j + gam) % SSM_SLOTS, yg[gam][:, m * LANES:(m + 1) * LANES], o)
                if j:
                    o = pltpu.roll(o, LANES - SSM_GROUP_DIM * j, 1)
                sl = piece(SSM_SLOTS * m + j, v)
                y_ref[:, sl] = jax.nn.gelu(o + d * u_ref[:, sl])


def _s5(u_rows, tab, d_skip, batch):
    nch, width = u_rows.shape
    rows = SSM_ROWS
    steps = nch // batch // rows
    once = lambda a: pl.BlockSpec(a.shape, lambda b, i: (0,) * a.ndim, pipeline_mode=pl.Buffered(1))
    blk = pl.BlockSpec((rows, width), lambda b, i: (b * steps + i, 0))
    table = pltpu.VMEM((SSM_GROUPS, SSM_CHUNK_WIDTH, SSM_CHUNK_WIDTH), BF16)
    state = pltpu.VMEM((rows, SSM_STATE_LANES), F32)
    carry = pltpu.VMEM((1, SSM_STATE_LANES), F32)
    return pl.pallas_call(
        _s5_body,
        grid=(batch, steps),
        in_specs=[blk, once(tab['kern']), once(tab['est']), once(tab['inter']),
                  once(tab['apow_re']), once(tab['apow_im']), once(d_skip)],
        out_specs=blk,
        out_shape=jax.ShapeDtypeStruct((nch, width), F32),
        scratch_shapes=[table, table, table,
                        pltpu.VMEM((SSM_GROUPS, rows, SSM_CHUNK_WIDTH), BF16), state, state, state, state,
                        carry, carry],
        compiler_params=_cparams("arbitrary", "arbitrary"),
        name="s5",
    )(u_rows, tab['kern'], tab['est'], tab['inter'], tab['apow_re'], tab['apow_im'], d_skip)


def _route_rows(scores, biased):
    ng = N_EXPERTS // EXPERTS_PER_GROUP
    group_score = []
    for gi in range(ng):
        a, b, c, d = biased[gi * EXPERTS_PER_GROUP:(gi + 1) * EXPERTS_PER_GROUP]
        hi1, lo1 = jnp.maximum(a, b), jnp.minimum(a, b)
        hi2, lo2 = jnp.maximum(c, d), jnp.minimum(c, d)
        top1 = jnp.maximum(hi1, hi2)
        top2 = jnp.maximum(jnp.minimum(hi1, hi2), jnp.maximum(lo1, lo2))
        group_score.append(top1 + top2)
    best = group_score[0]
    best_idx = jnp.zeros_like(best, dtype=jnp.int32)
    for gi in range(1, ng):
        better = group_score[gi] > best
        best = jnp.where(better, group_score[gi], best)
        best_idx = jnp.where(better, gi, best_idx)
    picked = []
    for e in range(N_EXPERTS):
        gi = e // EXPERTS_PER_GROUP
        rank = jnp.zeros_like(best_idx)
        for o in range(gi * EXPERTS_PER_GROUP, (gi + 1) * EXPERTS_PER_GROUP):
            if o == e:
                continue
            ahead = (biased[o] > biased[e]) | ((biased[o] == biased[e]) & (o < e))
            rank = rank + ahead.astype(jnp.int32)
        picked.append((best_idx == gi) & (rank < 2))
    wsum = sum(jnp.where(picked[e], scores[e], 0.0) for e in range(N_EXPERTS))
    return [jnp.where(picked[e], scores[e] / wsum, 0.0) for e in range(N_EXPERTS)]


def _outproj_body(x_ref, yp_ref, ya_ref, ys_ref, wglu_ref, bglu_ref, gssm_ref, wout_ref, g_ref, b_ref,
                  wrhi_ref, wrlo_ref, rb_ref, h_ref, comb_ref, ys_s):
    chunks = ys_ref.shape[0]
    for t in range(SSM_CHUNK):
        for v in range(SSM_LANE_BLOCKS):
            ys_s[v, pl.ds(t, chunks, stride=SSM_CHUNK), :] = ys_ref[:, _ssm_piece(t, v)]
    ys = jnp.concatenate([ys_s[v] for v in range(SSM_LANE_BLOCKS)], axis=1)
    gate = jax.nn.sigmoid(_dot(ys.astype(BF16), wglu_ref[...]) + bglu_ref[...])
    ys = ys * gate
    r = lax.rsqrt(jnp.mean(ys * ys, axis=-1, keepdims=True) + NORM_EPS)
    ysn = (ys * r * gssm_ref[...]).astype(BF16)
    mix = _dot(jnp.concatenate([yp_ref[...], ya_ref[...], ysn], axis=1), wout_ref[...])
    h = _layer_norm(DN_ALPHA * x_ref[...] + mix, g_ref[...], b_ref[...])
    h_ref[...] = h
    h_hi = h.astype(BF16)
    h_lo = (h - h_hi.astype(F32)).astype(BF16)
    logits = _dot(h_hi, wrhi_ref[...]) + _dot(h_lo, wrhi_ref[...]) + _dot(h_hi, wrlo_ref[...])
    sc = jax.nn.sigmoid(logits.T[:N_EXPERTS, :])
    bs = sc + rb_ref[...]
    scores = [sc[e:e + 1, :] for e in range(N_EXPERTS)]
    biased = [bs[e:e + 1, :] for e in range(N_EXPERTS)]
    comb_t = jnp.concatenate(_route_rows(scores, biased)
                             + [jnp.zeros((LANES - N_EXPERTS, sc.shape[1]), F32)], axis=0)
    comb_ref[...] = comb_t.T


def _outproj(x, y_pool, y_attn, y_ssm_rows, w_glu_bf16, b_glu, g_ssm, w_out_bf16, ln_g, ln_b,
             w_router_hi, w_router_lo, router_bias):
    t = x.shape[0]
    tm = OUT_TOKENS
    row = lambda width: pl.BlockSpec((tm, width), lambda i: (i, 0))
    full = lambda a: pl.BlockSpec(a.shape, lambda i: (0,) * a.ndim)
    return pl.pallas_call(
        _outproj_body,
        grid=(t // tm,),
        in_specs=[row(D_MODEL), row(POOL_WIDTH), row(ATTN_WIDTH),
                  pl.BlockSpec((tm // SSM_CHUNK, SSM_CHUNK * SSM_WIDTH), lambda i: (i, 0)),
                  full(w_glu_bf16), full(b_glu), full(g_ssm), full(w_out_bf16), full(ln_g), full(ln_b),
                  full(w_router_hi), full(w_router_lo), full(router_bias)],
        out_specs=[row(D_MODEL), row(LANES)],
        out_shape=[jax.ShapeDtypeStruct((t, D_MODEL), F32), jax.ShapeDtypeStruct((t, LANES), F32)],
        scratch_shapes=[pltpu.VMEM((SSM_LANE_BLOCKS, tm, LANES), F32)],
        compiler_params=_cparams("parallel"),
        name="outproj",
    )(x, y_pool, y_attn, y_ssm_rows, w_glu_bf16, b_glu, g_ssm, w_out_bf16, ln_g, ln_b,
      w_router_hi, w_router_lo, router_bias)


def _moe_body(h_ref, comb_ref, p_ref, wg_ref, wu_ref, wd_ref, wpg_ref, wpp_ref, g_ref, b_ref, o_ref,
              hb_s, acc_s):
    j = pl.program_id(1)

    @pl.when(j == 0)
    def _():
        hb_s[...] = h_ref[...].astype(BF16)
        acc_s[...] = jnp.zeros_like(acc_s)

    hb = hb_s[...]
    comb = comb_ref[...]
    lane = lax.broadcasted_iota(jnp.int32, (1, LANES), 1)
    total = None
    for e in range(MOE_EXPERTS_PER_STEP):
        gate = _dot(hb, wg_ref[e])
        up = _dot(hb, wu_ref[e])
        c = jnp.sum(jnp.where(lane == j * MOE_EXPERTS_PER_STEP + e, comb, 0.0), axis=1, keepdims=True)
        a = (jax.nn.silu(gate) * up * c).astype(BF16)
        d = _dot(a, wd_ref[e])
        total = d if total is None else total + d
    acc_s[...] += total

    @pl.when(j == pl.num_programs(1) - 1)
    def _():
        ple = jax.nn.sigmoid(_dot(hb, wpg_ref[...])) * _dot(p_ref[...].astype(BF16), wpp_ref[...])
        o_ref[...] = _layer_norm(DN_ALPHA * h_ref[...] + acc_s[...] + ple, g_ref[...], b_ref[...])


def _moe(h, comb, p_all, layer, wg_bf16, wu_bf16, wd_bf16, wpg_bf16, wpp_bf16, ln_g, ln_b):
    t = h.shape[0]
    tm = MOE_TOKENS
    nt = t // tm
    eps = MOE_EXPERTS_PER_STEP
    full = lambda a: pl.BlockSpec(a.shape, lambda i, j: (0,) * a.ndim)
    return pl.pallas_call(
        _moe_body,
        grid=(nt, N_EXPERTS // eps),
        in_specs=[pl.BlockSpec((tm, D_MODEL), lambda i, j: (i, 0)),
                  pl.BlockSpec((tm, LANES), lambda i, j: (i, 0)),
                  pl.BlockSpec((tm, PLE_DIM), lambda i, j: (layer * nt + i, 0)),
                  pl.BlockSpec((eps, D_MODEL, D_EXPERT), lambda i, j: (j, 0, 0)),
                  pl.BlockSpec((eps, D_MODEL, D_EXPERT), lambda i, j: (j, 0, 0)),
                  pl.BlockSpec((eps, D_EXPERT, D_MODEL), lambda i, j: (j, 0, 0)),
                  full(wpg_bf16), full(wpp_bf16), full(ln_g), full(ln_b)],
        out_specs=pl.BlockSpec((tm, D_MODEL), lambda i, j: (i, 0)),
        out_shape=jax.ShapeDtypeStruct((t, D_MODEL), F32),
        scratch_shapes=[pltpu.VMEM((tm, D_MODEL), BF16), pltpu.VMEM((tm, D_MODEL), F32)],
        compiler_params=_cparams("parallel", "arbitrary"),
        name="moe",
    )(h, comb, p_all, wg_bf16, wu_bf16, wd_bf16, wpg_bf16, wpp_bf16, ln_g, ln_b)


def _block_diag(w):
    g, n, m = w.shape
    eye = jnp.eye(g, dtype=w.dtype)
    return (eye[:, None, :, None] * w[:, :, None, :]).reshape(g * n, g * m)


def _layer(x, p_all, layer, batch, w_in, w_out, w_pool, pool_scale, rel_bias, ssm, ssm_d, w_glu, b_glu,
           g_pool, g_attn, g_ssm, ln1_g, ln1_b, ln2_g, ln2_b, w_router, router_bias,
           w_exp_gate, w_exp_up, w_exp_down, w_ple_gate, w_ple_proj):
    vec = lambda a: a.astype(F32).reshape(1, -1)
    u_pool, q, k, v, u_ssm = _inproj(x, w_in.astype(BF16))
    y_pool = _pool(u_pool, _block_diag(w_pool).astype(BF16), vec(pool_scale), vec(g_pool), batch)
    y_attn = _attention(q, k, v, _attn_bias_rows(rel_bias), vec(g_attn), batch)
    y_ssm = _s5(u_ssm, _s5_tables(*ssm), vec(ssm_d), batch)
    wr = jnp.pad(w_router.astype(F32), ((0, 0), (0, LANES - N_EXPERTS)))
    wr_hi = wr.astype(BF16)
    wr_lo = (wr - wr_hi.astype(F32)).astype(BF16)
    h, comb = _outproj(x, y_pool, y_attn, y_ssm, w_glu.astype(BF16), vec(b_glu), vec(g_ssm),
                       w_out.astype(BF16), vec(ln1_g), vec(ln1_b),
                       wr_hi, wr_lo, router_bias.astype(F32).reshape(N_EXPERTS, 1))
    return _moe(h, comb, p_all, layer, w_exp_gate.astype(BF16), w_exp_up.astype(BF16), w_exp_down.astype(BF16),
                w_ple_gate.astype(BF16), w_ple_proj.astype(BF16), vec(ln2_g), vec(ln2_b))


def kernel(x, p, w_in, w_out, w_pool, pool_scale, rel_bias, ssm_a_re, ssm_a_im, ssm_log_dt, ssm_b_re, ssm_b_im,
           ssm_c_re, ssm_c_im, ssm_d, w_glu, b_glu, g_pool, g_attn, g_ssm, ln1_g, ln1_b, ln2_g, ln2_b,
           w_router, router_bias, w_exp_gate, w_exp_up, w_exp_down, w_ple_gate, w_ple_proj):
    batch, seq, d = x.shape
    t = batch * seq
    xt = x.reshape(t, d)
    p_all = p.reshape(DEPTH * t, PLE_DIM)
    for i in range(DEPTH):
        ssm = (ssm_a_re[i], ssm_a_im[i], ssm_log_dt[i], ssm_b_re[i], ssm_b_im[i], ssm_c_re[i], ssm_c_im[i])
        xt = _layer(xt, p_all, i, batch, w_in[i], w_out[i], w_pool[i], pool_scale[i], rel_bias[i],
                    ssm, ssm_d[i], w_glu[i], b_glu[i], g_pool[i], g_attn[i], g_ssm[i], ln1_g[i], ln1_b[i],
                    ln2_g[i], ln2_b[i], w_router, router_bias, w_exp_gate[i], w_exp_up[i], w_exp_down[i],
                    w_ple_gate[i], w_ple_proj[i])
    return xt.reshape(batch, seq, d)
```

```python
import functools

import numpy as np
import jax
import jax.numpy as jnp
from jax import lax
from jax.experimental import pallas as pl
from jax.experimental.pallas import tpu as pltpu

F32 = jnp.float32
BF16 = jnp.bfloat16

D_MODEL = 1024
DEPTH = 2
CHUNK = 64
PLE_DIM = 256
POOL_WIDTH = 256
POOL_GROUP_DIM = 64
POOL_WINDOWS = (2, 4, 8, 16)
POOL_HALO = 16
ATTN_HEAD_DIM = 64
ATTN_HEADS = 6
ATTN_WIDTH = 384
N_PREV_CHUNKS = 8
REL_CLIP = 128
SSM_WIDTH = 384
SSM_GROUP_DIM = 16
SSM_GROUPS = 24
SSM_STATE = 64
N_EXPERTS = 16
EXPERTS_PER_GROUP = 4
D_EXPERT = 256
DN_ALPHA = (2 * DEPTH) ** 0.25
NORM_EPS = 1e-5

LANES = 128
VMEM_LIMIT_BYTES = 56 * 1024 * 1024

INPROJ_TOKENS = 1024
POOL_TOKENS = 1024
ATTN_Q_CHUNKS = 4
ATTN_Q_TOKENS = ATTN_Q_CHUNKS * CHUNK
ATTN_BAND_TOKENS = 3 * ATTN_Q_TOKENS
ATTN_BIAS_ROW = 1024
SSM_CHUNK = 16
SSM_CHUNK_WIDTH = SSM_CHUNK * SSM_GROUP_DIM
SSM_ROWS = 128
SSM_SLOTS = LANES // SSM_GROUP_DIM
SSM_LANE_BLOCKS = SSM_WIDTH // LANES
SSM_TIME_BLOCKS = SSM_CHUNK // SSM_SLOTS
SSM_PAIR_WIDTH = 2 * SSM_STATE
SSM_STATE_LANES = SSM_GROUPS * SSM_STATE
MOE_TOKENS = 1024
MOE_BLOCK_ROWS = 256
SORT_POS_LANE = N_EXPERTS


def _cparams(*sem):
    return pltpu.CompilerParams(dimension_semantics=sem, vmem_limit_bytes=VMEM_LIMIT_BYTES)


def _dot(a, b):
    return jnp.dot(a, b, preferred_element_type=F32)


def _layer_norm(v, g, b):
    mu = jnp.mean(v, axis=-1, keepdims=True)
    vc = v - mu
    var = jnp.mean(vc * vc, axis=-1, keepdims=True)
    return vc * lax.rsqrt(var + NORM_EPS) * g + b


def _ssm_piece(t, v):
    lo = SSM_WIDTH * t + LANES * v
    return slice(lo, lo + LANES)


def _inproj_body(x_ref, w_ref, up_ref, q_ref, k_ref, v_ref, us_ref, zs):
    xb = x_ref[...].astype(BF16)

    def cols(lo, hi):
        return _dot(xb, w_ref[:, lo:hi])

    up_ref[...] = cols(0, 256)
    qk = cols(256, 768)
    q_ref[...] = (qk[:, :ATTN_WIDTH] * (ATTN_HEAD_DIM ** -0.5)).astype(BF16)
    k_ref[:, :LANES] = qk[:, ATTN_WIDTH:].astype(BF16)
    k_ref[:, LANES:] = cols(768, 1024).astype(BF16)
    vs = cols(1024, 1536)
    v_ref[...] = vs[:, :ATTN_WIDTH].astype(BF16)
    zs[0] = vs[:, ATTN_WIDTH:]
    s_rest = cols(1536, 1792)
    zs[1] = s_rest[:, :LANES]
    zs[2] = s_rest[:, LANES:]
    chunks = us_ref.shape[0]
    for t in range(SSM_CHUNK):
        for v in range(SSM_LANE_BLOCKS):
            us_ref[:, _ssm_piece(t, v)] = zs[v, pl.ds(t, chunks, stride=SSM_CHUNK), :]


def _inproj(x, w_bf16):
    t = x.shape[0]
    tm = INPROJ_TOKENS
    row = lambda width: pl.BlockSpec((tm, width), lambda i: (i, 0))
    chunk_rows = pl.BlockSpec((tm // SSM_CHUNK, SSM_CHUNK * SSM_WIDTH), lambda i: (i, 0))
    return pl.pallas_call(
        _inproj_body,
        grid=(t // tm,),
        in_specs=[row(D_MODEL), pl.BlockSpec(w_bf16.shape, lambda i: (0, 0))],
        out_specs=[row(POOL_WIDTH), row(ATTN_WIDTH), row(ATTN_WIDTH), row(ATTN_WIDTH), chunk_rows],
        out_shape=[jax.ShapeDtypeStruct((t, POOL_WIDTH), F32),
                   jax.ShapeDtypeStruct((t, ATTN_WIDTH), BF16),
                   jax.ShapeDtypeStruct((t, ATTN_WIDTH), BF16),
                   jax.ShapeDtypeStruct((t, ATTN_WIDTH), BF16),
                   jax.ShapeDtypeStruct((t // SSM_CHUNK, SSM_CHUNK * SSM_WIDTH), F32)],
        scratch_shapes=[pltpu.VMEM((SSM_LANE_BLOCKS, tm, LANES), F32)],
        compiler_params=_cparams("parallel"),
        name="inproj",
    )(x, w_bf16)


def _pool_body(u_ref, halo_ref, w_ref, scale_ref, g_ref, o_ref, buf):
    i = pl.program_id(1)
    tp = u_ref.shape[0]
    x0 = u_ref[...]
    buf[0:POOL_HALO, :] = jnp.where(i == 0, 0.0, halo_ref[...])
    buf[POOL_HALO:, :] = x0
    pos = i * tp + lax.broadcasted_iota(jnp.int32, (tp, 1), 0)
    group = lax.broadcasted_iota(jnp.int32, (1, POOL_WIDTH), 1) // POOL_GROUP_DIM
    s = x0
    mean = jnp.zeros_like(x0)
    k = 1
    for gi, w in enumerate(POOL_WINDOWS):
        while k < w:
            s = s + buf[pl.ds(POOL_HALO - k, tp), :]
            k += 1
        cnt = jnp.minimum(pos + 1, w).astype(F32)
        mean = jnp.where(group == gi, s / cnt, mean)
    d = (mean - x0).astype(BF16)
    y = _dot(d, w_ref[...]) * scale_ref[...]
    r = lax.rsqrt(jnp.mean(y * y, axis=-1, keepdims=True) + NORM_EPS)
    o_ref[...] = (y * r * g_ref[...]).astype(BF16)


def _pool(u_pool, w_blockdiag_bf16, scale, gain, batch):
    t = u_pool.shape[0]
    tp = POOL_TOKENS
    nt = t // batch // tp
    halo_blocks = tp // POOL_HALO
    vec = pl.BlockSpec((1, POOL_WIDTH), lambda b, i: (0, 0))
    return pl.pallas_call(
        _pool_body,
        grid=(batch, nt),
        in_specs=[pl.BlockSpec((tp, POOL_WIDTH), lambda b, i: (b * nt + i, 0)),
                  pl.BlockSpec((POOL_HALO, POOL_WIDTH),
                               lambda b, i: (jnp.maximum((b * nt + i) * halo_blocks - 1, 0), 0)),
                  pl.BlockSpec((POOL_WIDTH, POOL_WIDTH), lambda b, i: (0, 0)),
                  vec, vec],
        out_specs=pl.BlockSpec((tp, POOL_WIDTH), lambda b, i: (b * nt + i, 0)),
        out_shape=jax.ShapeDtypeStruct((t, POOL_WIDTH), BF16),
        scratch_shapes=[pltpu.VMEM((POOL_HALO + tp, POOL_WIDTH), F32)],
        compiler_params=_cparams("parallel", "parallel"),
        name="pool",
    )(u_pool, u_pool, w_blockdiag_bf16, scale, gain)


def _attn_bias_rows(rel_bias):
    x = np.arange(ATTN_BIAS_ROW)
    x = np.where(x < ATTN_BAND_TOKENS, x, x - ATTN_BIAS_ROW)
    idx = np.clip(N_PREV_CHUNKS * CHUNK - x, -REL_CLIP, REL_CLIP) + REL_CLIP
    return rel_bias.astype(F32)[:, idx]


def _attn_body(q_ref, k0_ref, k1_ref, k2_ref, v0_ref, v1_ref, v2_ref, rows_ref, g_ref, o_ref, bias_s):
    b = pl.program_id(0)
    i = pl.program_id(1)
    tq = ATTN_Q_TOKENS

    @pl.when((b == 0) & (i == 0))
    def _():
        qc = lax.broadcasted_iota(jnp.int32, (tq, ATTN_BAND_TOKENS), 0) // CHUNK
        kc = lax.broadcasted_iota(jnp.int32, (tq, ATTN_BAND_TOKENS), 1) // CHUNK
        in_band = (kc >= qc) & (kc <= qc + N_PREV_CHUNKS)
        for head in range(ATTN_HEADS):
            full = jnp.broadcast_to(rows_ref[head:head + 1, :], (tq, ATTN_BIAS_ROW))
            shifted = pltpu.roll(full, 0, 1, stride=1, stride_axis=0)
            bias_s[head] = jnp.where(in_band, shifted[:, :ATTN_BAND_TOKENS], -jnp.inf)

    col = lax.broadcasted_iota(jnp.int32, (1, ATTN_BAND_TOKENS), 1)
    before_start = ((col < tq) & (i < 2)) | ((col < 2 * tq) & (i < 1))
    upper_half = lax.broadcasted_iota(jnp.int32, (1, LANES), 1) >= ATTN_HEAD_DIM
    outs = []
    for pair in range(ATTN_WIDTH // LANES):
        sl = slice(pair * LANES, (pair + 1) * LANES)
        qp = q_ref[:, sl]
        kp = jnp.concatenate([k0_ref[:, sl], k1_ref[:, sl], k2_ref[:, sl]], axis=0)
        vp = jnp.concatenate([v0_ref[:, sl], v1_ref[:, sl], v2_ref[:, sl]], axis=0)
        o_pair = None
        for half in range(2):
            head = 2 * pair + half
            qm = jnp.where(upper_half == bool(half), qp, jnp.zeros_like(qp))
            s = lax.dot_general(qm, kp, (((1,), (1,)), ((), ())), preferred_element_type=F32)
            s = s + bias_s[head]
            s = jnp.where(before_start, -jnp.inf, s)
            m = jnp.max(s, axis=-1, keepdims=True)
            p = jnp.exp(s - m)
            l = jnp.sum(p, axis=-1, keepdims=True)
            o = _dot(p.astype(BF16), vp) / l
            o_pair = o if half == 0 else jnp.where(upper_half, o, o_pair)
        outs.append(o_pair)
    ss = sum(jnp.sum(o * o, axis=-1, keepdims=True) for o in outs)
    r = lax.rsqrt(ss / ATTN_WIDTH + NORM_EPS)
    for pair, o in enumerate(outs):
        sl = slice(pair * LANES, (pair + 1) * LANES)
        o_ref[:, sl] = (o * r * g_ref[:, sl]).astype(BF16)


def _attention(q, k, v, bias_rows, gain, batch):
    t = q.shape[0]
    tq = ATTN_Q_TOKENS
    nq = t // batch // tq

    def blk(back):
        return pl.BlockSpec((tq, ATTN_WIDTH), lambda b, i: (b * nq + jnp.maximum(i - back, 0), 0))

    return pl.pallas_call(
        _attn_body,
        grid=(batch, nq),
        in_specs=[blk(0), blk(2), blk(1), blk(0), blk(2), blk(1), blk(0),
                  pl.BlockSpec(bias_rows.shape, lambda b, i: (0, 0)),
                  pl.BlockSpec((1, ATTN_WIDTH), lambda b, i: (0, 0))],
        out_specs=blk(0),
        out_shape=jax.ShapeDtypeStruct((t, ATTN_WIDTH), BF16),
        scratch_shapes=[pltpu.VMEM((ATTN_HEADS, tq, ATTN_BAND_TOKENS), F32)],
        compiler_params=_cparams("arbitrary", "arbitrary"),
        name="attention",
    )(q, k, k, k, v, v, v, bias_rows, gain)


def _s5_position_of_time(g, time):
    return SSM_SLOTS * (time // SSM_SLOTS) + (time % SSM_SLOTS + g) % SSM_SLOTS


def _s5_tables(a_re, a_im, log_dt, b_re, b_im, c_re, c_im):
    hi = lax.Precision.HIGHEST
    tc = SSM_CHUNK
    g, p_dim = a_re.shape
    dt = jnp.exp(log_dt.astype(F32))[:, None]
    ar = a_re.astype(F32)
    ai = a_im.astype(F32)
    mag = jnp.exp(ar * dt)
    abar_re = mag * jnp.cos(ai * dt)
    abar_im = mag * jnp.sin(ai * dt)
    den = ar * ar + ai * ai
    nr = abar_re - 1.0
    ni = abar_im
    coef_re = ((nr * ar + ni * ai) / den)[..., None]
    coef_im = ((ni * ar - nr * ai) / den)[..., None]
    br = b_re.astype(F32)
    bi = b_im.astype(F32)
    bbar_re = coef_re * br - coef_im * bi
    bbar_im = coef_re * bi + coef_im * br
    n = jnp.arange(tc + 1, dtype=F32)[:, None, None]
    pmag = jnp.exp(n * (ar * dt))
    pw_re = pmag * jnp.cos(n * (ai * dt))
    pw_im = pmag * jnp.sin(n * (ai * dt))
    x_re = pw_re[..., None] * bbar_re[None] - pw_im[..., None] * bbar_im[None]
    x_im = pw_re[..., None] * bbar_im[None] + pw_im[..., None] * bbar_re[None]
    cr = c_re.astype(F32)
    ci = c_im.astype(F32)
    cw = tc * SSM_GROUP_DIM
    kern = (jnp.einsum('ghp,ngpk->gknh', cr, x_re[:tc], precision=hi)
            - jnp.einsum('ghp,ngpk->gknh', ci, x_im[:tc], precision=hi)).reshape(g, SSM_GROUP_DIM, cw)
    est_re = x_re[:tc].transpose(1, 0, 3, 2).reshape(g, cw, p_dim)
    est_im = x_im[:tc].transpose(1, 0, 3, 2).reshape(g, cw, p_dim)
    crt = cr.transpose(0, 2, 1)[:, :, None, :]
    cit = ci.transpose(0, 2, 1)[:, :, None, :]
    nx_re = pw_re[1:].transpose(1, 2, 0)[..., None]
    nx_im = pw_im[1:].transpose(1, 2, 0)[..., None]
    int_re = (crt * nx_re - cit * nx_im).reshape(g, p_dim, cw)
    int_im = (-(crt * nx_im + cit * nx_re)).reshape(g, p_dim, cw)

    odd = (jnp.arange(g) % 2 == 1)[:, None, None]
    zero_e = jnp.zeros_like(est_re)
    lane_pad = lambda m: jnp.where(odd, jnp.concatenate([zero_e, m], -1), jnp.concatenate([m, zero_e], -1))
    zero_i = jnp.zeros_like(int_re)
    row_pad = lambda m: jnp.where(odd, jnp.concatenate([zero_i, m], 1), jnp.concatenate([m, zero_i], 1))
    return dict(
        kern=kern,
        est=jnp.concatenate([lane_pad(est_re), lane_pad(est_im)], -1).astype(BF16),
        inter=jnp.concatenate([row_pad(int_re), row_pad(int_im)], 1),
        apow_re=pw_re[tc].reshape(1, g * p_dim), apow_im=pw_im[tc].reshape(1, g * p_dim))


def _s5_prepare(kern_ref, est_ref, int_ref, toep_s, est_s, int_s):
    lane = lax.broadcasted_iota(jnp.int32, (1, LANES), 1)
    zero = jnp.zeros((SSM_GROUP_DIM, LANES), F32)
    for g in range(SSM_GROUPS):
        turn = SSM_GROUP_DIM * (g % SSM_SLOTS)
        spin = lambda a: pltpu.roll(a, turn, 1) if turn else a
        k0 = kern_ref[g, :, 0:LANES]
        k1 = kern_ref[g, :, LANES:2 * LANES]
        for time in range(SSM_CHUNK):
            shift = SSM_GROUP_DIM * (time % SSM_SLOTS)
            r0 = pltpu.roll(k0, shift, 1) if shift else k0
            if time < SSM_SLOTS:
                r1 = pltpu.roll(k1, shift, 1) if shift else k1
                h0 = jnp.where(lane >= shift, r0, zero)
                h1 = jnp.where(lane >= shift, r1, r0)
            else:
                h0 = zero
                h1 = jnp.where(lane >= shift, r0, zero)
            rows = pl.ds(SSM_GROUP_DIM * _s5_position_of_time(g, time), SSM_GROUP_DIM)
            toep_s[g, rows, 0:LANES] = spin(h0).astype(BF16)
            toep_s[g, rows, LANES:2 * LANES] = spin(h1).astype(BF16)
            est_s[g, rows, :] = est_ref[g, pl.ds(SSM_GROUP_DIM * (SSM_CHUNK - 1 - time), SSM_GROUP_DIM), :]
        for m in range(SSM_TIME_BLOCKS):
            int_s[g, :, m * LANES:(m + 1) * LANES] = spin(int_ref[g, :, m * LANES:(m + 1) * LANES]).astype(BF16)


def _s5_body(u_ref, kern_ref, estin_ref, intin_ref, apre_ref, apim_ref, d_ref, y_ref,
             toep_ref, est_ref, int_ref, ub_s, ere_s, eim_s, spre_s, spim_s, sre_s, sim_s):
    rows = u_ref.shape[0]

    @pl.when((pl.program_id(0) == 0) & (pl.program_id(1) == 0))
    def _():
        _s5_prepare(kern_ref, estin_ref, intin_ref, toep_ref, est_ref, int_ref)

    @pl.when(pl.program_id(1) == 0)
    def _():
        sre_s[...] = jnp.zeros_like(sre_s)
        sim_s[...] = jnp.zeros_like(sim_s)

    slot = lax.broadcasted_iota(jnp.int32, (1, LANES), 1) // SSM_GROUP_DIM
    piece = _ssm_piece

    for v in range(SSM_LANE_BLOCKS):
        for m in range(SSM_TIME_BLOCKS):
            rot = []
            for j in range(SSM_SLOTS):
                a = u_ref[:, piece(SSM_SLOTS * m + j, v)]
                rot.append(a if j == 0 else pltpu.roll(a, SSM_GROUP_DIM * j, 1))
            for gam in range(SSM_SLOTS):
                o = rot[0]
                for j in range(1, SSM_SLOTS):
                    o = jnp.where(slot == (j + gam) % SSM_SLOTS, rot[j], o)
                ub_s[SSM_SLOTS * v + gam, :, m * LANES:(m + 1) * LANES] = o.astype(BF16)

    for q in range(SSM_GROUPS // 2):
        e = _dot(ub_s[2 * q], est_ref[2 * q]) + _dot(ub_s[2 * q + 1], est_ref[2 * q + 1])
        ere_s[:, q * LANES:(q + 1) * LANES] = e[:, :SSM_PAIR_WIDTH]
        eim_s[:, q * LANES:(q + 1) * LANES] = e[:, SSM_PAIR_WIDTH:]

    a_re = apre_ref[...]
    a_im = apim_ref[...]

    def carry_step(r, carry):
        s_re, s_im = carry
        spre_s[pl.ds(r, 1), :] = s_re
        spim_s[pl.ds(r, 1), :] = s_im
        e_re = ere_s[pl.ds(r, 1), :]
        e_im = eim_s[pl.ds(r, 1), :]
        return (a_re * s_re - a_im * s_im + e_re, a_re * s_im + a_im * s_re + e_im)

    s_re, s_im = lax.fori_loop(0, rows, carry_step, (sre_s[...], sim_s[...]))
    sre_s[...] = s_re
    sim_s[...] = s_im

    for v in range(SSM_LANE_BLOCKS):
        yg = []
        for gam in range(SSM_SLOTS):
            g = SSM_SLOTS * v + gam
            q = g // 2
            sp = jnp.concatenate([spre_s[:, q * LANES:(q + 1) * LANES], spim_s[:, q * LANES:(q + 1) * LANES]],
                                 axis=1).astype(BF16)
            yg.append(_dot(ub_s[g], toep_ref[g]) + _dot(sp, int_ref[g]))
        d = d_ref[:, v * LANES:(v + 1) * LANES]
        for m in range(SSM_TIME_BLOCKS):
            for j in range(SSM_SLOTS):
                o = yg[0][:, m * LANES:(m + 1) * LANES]
                for gam in range(1, SSM_SLOTS):
                    o = jnp.where(slot == (j + gam) % SSM_SLOTS, yg[gam][:, m * LANES:(m + 1) * LANES], o)
                if j:
                    o = pltpu.roll(o, LANES - SSM_GROUP_DIM * j, 1)
                sl = piece(SSM_SLOTS * m + j, v)
                y_ref[:, sl] = jax.nn.gelu(o + d * u_ref[:, sl])


def _s5(u_rows, tab, d_skip, batch):
    nch, width = u_rows.shape
    rows = SSM_ROWS
    steps = nch // batch // rows
    once = lambda a: pl.BlockSpec(a.shape, lambda b, i: (0,) * a.ndim, pipeline_mode=pl.Buffered(1))
    blk = pl.BlockSpec((rows, width), lambda b, i: (b * steps + i, 0))
    table = pltpu.VMEM((SSM_GROUPS, SSM_CHUNK_WIDTH, SSM_CHUNK_WIDTH), BF16)
    state = pltpu.VMEM((rows, SSM_STATE_LANES), F32)
    carry = pltpu.VMEM((1, SSM_STATE_LANES), F32)
    return pl.pallas_call(
        _s5_body,
        grid=(batch, steps),
        in_specs=[blk, once(tab['kern']), once(tab['est']), once(tab['inter']),
                  once(tab['apow_re']), once(tab['apow_im']), once(d_skip)],
        out_specs=blk,
        out_shape=jax.ShapeDtypeStruct((nch, width), F32),
        scratch_shapes=[table, table, table,
                        pltpu.VMEM((SSM_GROUPS, rows, SSM_CHUNK_WIDTH), BF16), state, state, state, state,
                        carry, carry],
        compiler_params=_cparams("arbitrary", "arbitrary"),
        name="s5",
    )(u_rows, tab['kern'], tab['est'], tab['inter'], tab['apow_re'], tab['apow_im'], d_skip)


def _route_rows(scores, biased):
    ng = N_EXPERTS // EXPERTS_PER_GROUP
    group_score = []
    for gi in range(ng):
        a, b, c, d = biased[gi * EXPERTS_PER_GROUP:(gi + 1) * EXPERTS_PER_GROUP]
        hi1, lo1 = jnp.maximum(a, b), jnp.minimum(a, b)
        hi2, lo2 = jnp.maximum(c, d), jnp.minimum(c, d)
        top1 = jnp.maximum(hi1, hi2)
        top2 = jnp.maximum(jnp.minimum(hi1, hi2), jnp.maximum(lo1, lo2))
        group_score.append(top1 + top2)
    best = group_score[0]
    best_idx = jnp.zeros_like(best, dtype=jnp.int32)
    for gi in range(1, ng):
        better = group_score[gi] > best
        best = jnp.where(better, group_score[gi], best)
        best_idx = jnp.where(better, gi, best_idx)
    picked = []
    for e in range(N_EXPERTS):
        gi = e // EXPERTS_PER_GROUP
        rank = jnp.zeros_like(best_idx)
        for o in range(gi * EXPERTS_PER_GROUP, (gi + 1) * EXPERTS_PER_GROUP):
            if o == e:
                continue
            ahead = (biased[o] > biased[e]) | ((biased[o] == biased[e]) & (o < e))
            rank = rank + ahead.astype(jnp.int32)
        picked.append((best_idx == gi) & (rank < 2))
    wsum = sum(jnp.where(picked[e], scores[e], 0.0) for e in range(N_EXPERTS))
    return [jnp.where(picked[e], scores[e] / wsum, 0.0) for e in range(N_EXPERTS)], best_idx


def _group_sort_positions(best_idx, before_ref):
    ng = N_EXPERTS // EXPERTS_PER_GROUP
    member = [(best_idx == gi).astype(F32) for gi in range(ng)]
    pad = jnp.zeros((8 - ng, best_idx.shape[1]), F32)
    earlier = _dot(jnp.concatenate(member + [pad], axis=0).astype(BF16), before_ref[...])
    counts = [jnp.sum(m, axis=1, keepdims=True) for m in member]
    pos = jnp.zeros_like(member[0])
    start = jnp.zeros_like(counts[0])
    for gi in range(ng):
        pos = pos + member[gi] * (start + earlier[gi:gi + 1, :])
        start = start + counts[gi]
    return pos, counts


def _outproj_body(x_ref, yp_ref, ya_ref, ys_ref, wglu_ref, bglu_ref, gssm_ref, wout_ref, g_ref, b_ref,
                  wrhi_ref, wrlo_ref, rb_ref, before_ref, h_ref, comb_ref, pos_ref, cnt_ref, ys_s):
    chunks = ys_ref.shape[0]
    for t in range(SSM_CHUNK):
        for v in range(SSM_LANE_BLOCKS):
            ys_s[v, pl.ds(t, chunks, stride=SSM_CHUNK), :] = ys_ref[:, _ssm_piece(t, v)]
    ys = jnp.concatenate([ys_s[v] for v in range(SSM_LANE_BLOCKS)], axis=1)
    gate = jax.nn.sigmoid(_dot(ys.astype(BF16), wglu_ref[...]) + bglu_ref[...])
    ys = ys * gate
    r = lax.rsqrt(jnp.mean(ys * ys, axis=-1, keepdims=True) + NORM_EPS)
    ysn = (ys * r * gssm_ref[...]).astype(BF16)
    mix = _dot(jnp.concatenate([yp_ref[...], ya_ref[...], ysn], axis=1), wout_ref[...])
    h = _layer_norm(DN_ALPHA * x_ref[...] + mix, g_ref[...], b_ref[...])
    h_ref[...] = h
    h_hi = h.astype(BF16)
    h_lo = (h - h_hi.astype(F32)).astype(BF16)
    logits = _dot(h_hi, wrhi_ref[...]) + _dot(h_lo, wrhi_ref[...]) + _dot(h_hi, wrlo_ref[...])
    sc = jax.nn.sigmoid(logits.T[:N_EXPERTS, :])
    bs = sc + rb_ref[...]
    scores = [sc[e:e + 1, :] for e in range(N_EXPERTS)]
    biased = [bs[e:e + 1, :] for e in range(N_EXPERTS)]
    comb_rows, best_idx = _route_rows(scores, biased)
    pos, counts = _group_sort_positions(best_idx, before_ref)
    tokens = sc.shape[1]
    comb_t = jnp.concatenate(comb_rows + [pos, jnp.zeros((LANES - N_EXPERTS - 1, tokens), F32)], axis=0)
    comb_ref[...] = comb_t.T
    pos_ref[...] = jnp.concatenate([pos, jnp.zeros((7, tokens), F32)], axis=0)
    cnt_ref[...] = jnp.concatenate(
        [jnp.broadcast_to(c, (1, LANES)) for c in counts]
        + [jnp.zeros((8 - len(counts), LANES), F32)], axis=0).astype(jnp.int32)


def _outproj(x, y_pool, y_attn, y_ssm_rows, w_glu_bf16, b_glu, g_ssm, w_out_bf16, ln_g, ln_b,
             w_router_hi, w_router_lo, router_bias):
    t = x.shape[0]
    tm = MOE_TOKENS
    nt = t // tm
    row = lambda width: pl.BlockSpec((tm, width), lambda i: (i, 0))
    full = lambda a: pl.BlockSpec(a.shape, lambda i: (0,) * a.ndim)
    token = np.arange(tm)
    before = jnp.asarray(token[:, None] < token[None, :], BF16)
    return pl.pallas_call(
        _outproj_body,
        grid=(nt,),
        in_specs=[row(D_MODEL), row(POOL_WIDTH), row(ATTN_WIDTH),
                  pl.BlockSpec((tm // SSM_CHUNK, SSM_CHUNK * SSM_WIDTH), lambda i: (i, 0)),
                  full(w_glu_bf16), full(b_glu), full(g_ssm), full(w_out_bf16), full(ln_g), full(ln_b),
                  full(w_router_hi), full(w_router_lo), full(router_bias), full(before)],
        out_specs=[row(D_MODEL), row(LANES), pl.BlockSpec((8, tm), lambda i: (0, i)),
                   pl.BlockSpec((8, LANES), lambda i: (i, 0))],
        out_shape=[jax.ShapeDtypeStruct((t, D_MODEL), F32), jax.ShapeDtypeStruct((t, LANES), F32),
                   jax.ShapeDtypeStruct((8, t), F32), jax.ShapeDtypeStruct((8 * nt, LANES), jnp.int32)],
        scratch_shapes=[pltpu.VMEM((SSM_LANE_BLOCKS, tm, LANES), F32)],
        compiler_params=_cparams("parallel"),
        name="outproj",
    )(x, y_pool, y_attn, y_ssm_rows, w_glu_bf16, b_glu, g_ssm, w_out_bf16, ln_g, ln_b,
      w_router_hi, w_router_lo, router_bias, before)


def _moe_body(cnt_ref, h_ref, comb_ref, pos_ref, p_ref, wg_ref, wu_ref, wd_ref, wpg_ref, wpp_ref, g_ref, b_ref,
              o_ref, hs_s, cs_s, acc_s):
    i = pl.program_id(0)
    group = pl.program_id(1)
    ng = pl.num_programs(1)
    tm = h_ref.shape[0]

    @pl.when(group == 0)
    def _():
        row = lax.broadcasted_iota(jnp.int32, (tm, tm), 0).astype(F32)
        perm = jnp.where(pos_ref[0:1, :] == row, 1.0, 0.0).astype(BF16)
        hs_s[...] = _dot(perm, h_ref[...].astype(BF16)).astype(BF16)
        comb = comb_ref[...]
        comb_hi = comb.astype(BF16)
        comb_lo = (comb - comb_hi.astype(F32)).astype(BF16)
        cs_s[...] = _dot(perm, comb_hi) + _dot(perm, comb_lo)
        acc_s[...] = jnp.zeros_like(acc_s)

    count = cnt_ref[i * ng + group]
    start = jnp.int32(0)
    for gi in range(N_EXPERTS // EXPERTS_PER_GROUP - 1):
        start = start + jnp.where(group > gi, cnt_ref[i * ng + gi], 0)
    lane = lax.broadcasted_iota(jnp.int32, (1, LANES), 1)
    for blk in range(tm // MOE_BLOCK_ROWS):
        lo = blk * MOE_BLOCK_ROWS
        rows = pl.ds(lo, MOE_BLOCK_ROWS)

        @pl.when((start < lo + MOE_BLOCK_ROWS) & (start + count > lo))
        def _():
            x = hs_s[rows, :]
            cs = cs_s[rows, :]
            total = None
            for e in range(EXPERTS_PER_GROUP):
                gate = _dot(x, wg_ref[e])
                up = _dot(x, wu_ref[e])
                c = jnp.sum(jnp.where(lane == group * EXPERTS_PER_GROUP + e, cs, 0.0), axis=1, keepdims=True)
                a = (jax.nn.silu(gate) * up * c).astype(BF16)
                d = _dot(a, wd_ref[e])
                total = d if total is None else total + d
            acc_s[rows, :] += total

    @pl.when(group == ng - 1)
    def _():
        col = lax.broadcasted_iota(jnp.int32, (tm, tm), 1).astype(F32)
        unperm = jnp.where(comb_ref[:, SORT_POS_LANE:SORT_POS_LANE + 1] == col, 1.0, 0.0).astype(BF16)
        ffn = _dot(unperm, acc_s[...].astype(BF16))
        hb = h_ref[...].astype(BF16)
        ple = jax.nn.sigmoid(_dot(hb, wpg_ref[...])) * _dot(p_ref[...].astype(BF16), wpp_ref[...])
        o_ref[...] = _layer_norm(DN_ALPHA * h_ref[...] + ffn + ple, g_ref[...], b_ref[...])


def _moe(h, comb, pos_rows, counts, p_all, layer, wg_bf16, wu_bf16, wd_bf16, wpg_bf16, wpp_bf16, ln_g, ln_b):
    t = h.shape[0]
    tm = MOE_TOKENS
    nt = t // tm
    ng = N_EXPERTS // EXPERTS_PER_GROUP
    full = lambda a: pl.BlockSpec(a.shape, lambda i, g, cnt: (0,) * a.ndim)
    grid_spec = pltpu.PrefetchScalarGridSpec(
        num_scalar_prefetch=1,
        grid=(nt, ng),
        in_specs=[pl.BlockSpec((tm, D_MODEL), lambda i, g, cnt: (i, 0)),
                  pl.BlockSpec((tm, LANES), lambda i, g, cnt: (i, 0)),
                  pl.BlockSpec((8, tm), lambda i, g, cnt: (0, i)),
                  pl.BlockSpec((tm, PLE_DIM), lambda i, g, cnt: (layer * nt + i, 0)),
                  pl.BlockSpec((EXPERTS_PER_GROUP, D_MODEL, D_EXPERT), lambda i, g, cnt: (g, 0, 0)),
                  pl.BlockSpec((EXPERTS_PER_GROUP, D_MODEL, D_EXPERT), lambda i, g, cnt: (g, 0, 0)),
                  pl.BlockSpec((EXPERTS_PER_GROUP, D_EXPERT, D_MODEL), lambda i, g, cnt: (g, 0, 0)),
                  full(wpg_bf16), full(wpp_bf16), full(ln_g), full(ln_b)],
        out_specs=pl.BlockSpec((tm, D_MODEL), lambda i, g, cnt: (i, 0)),
        scratch_shapes=[pltpu.VMEM((tm, D_MODEL), BF16), pltpu.VMEM((tm, LANES), F32),
                        pltpu.VMEM((tm, D_MODEL), F32)])
    return pl.pallas_call(
        _moe_body,
        grid_spec=grid_spec,
        out_shape=jax.ShapeDtypeStruct((t, D_MODEL), F32),
        compiler_params=_cparams("parallel", "arbitrary"),
        name="moe",
    )(counts, h, comb, pos_rows, p_all, wg_bf16, wu_bf16, wd_bf16, wpg_bf16, wpp_bf16, ln_g, ln_b)


def _block_diag(w):
    g, n, m = w.shape
    eye = jnp.eye(g, dtype=w.dtype)
    return (eye[:, None, :, None] * w[:, :, None, :]).reshape(g * n, g * m)


def _layer(x, p_all, layer, batch, w_in, w_out, w_pool, pool_scale, rel_bias, ssm, ssm_d, w_glu, b_glu,
           g_pool, g_attn, g_ssm, ln1_g, ln1_b, ln2_g, ln2_b, w_router, router_bias,
           w_exp_gate, w_exp_up, w_exp_down, w_ple_gate, w_ple_proj):
    vec = lambda a: a.astype(F32).reshape(1, -1)
    u_pool, q, k, v, u_ssm = _inproj(x, w_in.astype(BF16))
    y_pool = _pool(u_pool, _block_diag(w_pool).astype(BF16), vec(pool_scale), vec(g_pool), batch)
    y_attn = _attention(q, k, v, _attn_bias_rows(rel_bias), vec(g_attn), batch)
    y_ssm = _s5(u_ssm, _s5_tables(*ssm), vec(ssm_d), batch)
    wr = jnp.pad(w_router.astype(F32), ((0, 0), (0, LANES - N_EXPERTS)))
    wr_hi = wr.astype(BF16)
    wr_lo = (wr - wr_hi.astype(F32)).astype(BF16)
    h, comb, pos_rows, cnt = _outproj(x, y_pool, y_attn, y_ssm, w_glu.astype(BF16), vec(b_glu), vec(g_ssm),
                                      w_out.astype(BF16), vec(ln1_g), vec(ln1_b),
                                      wr_hi, wr_lo, router_bias.astype(F32).reshape(N_EXPERTS, 1))
    ng = N_EXPERTS // EXPERTS_PER_GROUP
    counts = cnt[:, 0].reshape(-1, 8)[:, :ng].reshape(-1)
    return _moe(h, comb, pos_rows, counts, p_all, layer, w_exp_gate.astype(BF16), w_exp_up.astype(BF16),
                w_exp_down.astype(BF16), w_ple_gate.astype(BF16), w_ple_proj.astype(BF16), vec(ln2_g), vec(ln2_b))


def kernel(x, p, w_in, w_out, w_pool, pool_scale, rel_bias, ssm_a_re, ssm_a_im, ssm_log_dt, ssm_b_re, ssm_b_im,
           ssm_c_re, ssm_c_im, ssm_d, w_glu, b_glu, g_pool, g_attn, g_ssm, ln1_g, ln1_b, ln2_g, ln2_b,
           w_router, router_bias, w_exp_gate, w_exp_up, w_exp_down, w_ple_gate, w_ple_proj):
    batch, seq, d = x.shape
    t = batch * seq
    xt = x.reshape(t, d)
    p_all = p.reshape(DEPTH * t, PLE_DIM)
    for i in range(DEPTH):
        ssm = (ssm_a_re[i], ssm_a_im[i], ssm_log_dt[i], ssm_b_re[i], ssm_b_im[i], ssm_c_re[i], ssm_c_im[i])
        xt = _layer(xt, p_all, i, batch, w_in[i], w_out[i], w_pool[i], pool_scale[i], rel_bias[i],
                    ssm, ssm_d[i], w_glu[i], b_glu[i], g_pool[i], g_attn[i], g_ssm[i], ln1_g[i], ln1_b[i],
                    ln2_g[i], ln2_b[i], w_router, router_bias, w_exp_gate[i], w_exp_up[i], w_exp_down[i],
                    w_ple_gate[i], w_ple_proj[i])
    return xt.reshape(batch, seq, d)
```

```python
import functools
import math

import numpy as np
import jax
import jax.numpy as jnp
from jax import lax
from jax.experimental import pallas as pl
from jax.experimental.pallas import tpu as pltpu

F32 = jnp.float32
BF16 = jnp.bfloat16

D_MODEL = 1024
DEPTH = 2
CHUNK = 64
PLE_DIM = 256
POOL_WIDTH = 256
POOL_GROUP_DIM = 64
POOL_WINDOWS = (2, 4, 8, 16)
POOL_HALO = 32
ATTN_HEAD_DIM = 64
ATTN_HEADS = 6
ATTN_WIDTH = 384
N_PREV_CHUNKS = 8
REL_CLIP = 128
SSM_WIDTH = 384
SSM_GROUP_DIM = 16
SSM_GROUPS = 24
SSM_STATE = 64
N_EXPERTS = 16
EXPERTS_PER_GROUP = 4
D_EXPERT = 256
DN_ALPHA = (2 * DEPTH) ** 0.25
NORM_EPS = 1e-5
LOG2_E = math.log2(math.e)

LANES = 128
VMEM_LIMIT_BYTES = 56 * 1024 * 1024

INPROJ_TOKENS = 1024
ATTN_Q_CHUNKS = 4
ATTN_Q_TOKENS = ATTN_Q_CHUNKS * CHUNK
ATTN_BAND_TOKENS = 3 * ATTN_Q_TOKENS
ATTN_BIAS_ROW = 1024
SSM_CHUNK = 16
SSM_CHUNK_WIDTH = SSM_CHUNK * SSM_GROUP_DIM
SSM_ROWS = 128
SSM_SLOTS = LANES // SSM_GROUP_DIM
SSM_LANE_BLOCKS = SSM_WIDTH // LANES
SSM_TIME_BLOCKS = SSM_CHUNK // SSM_SLOTS
SSM_PAIR_WIDTH = 2 * SSM_STATE
SSM_STATE_LANES = SSM_GROUPS * SSM_STATE
MOE_TOKENS = 1024
MOE_WINDOW_ROWS = 320
BF16_ROW_PACK = 16
SORT_POS_LANE = N_EXPERTS
COMB_LO_LANE = 32


def _cparams(*sem):
    return pltpu.CompilerParams(dimension_semantics=sem, vmem_limit_bytes=VMEM_LIMIT_BYTES)


def _dot(a, b):
    return jnp.dot(a, b, preferred_element_type=F32)


def _layer_block(a, layer, **kwargs):
    return pl.BlockSpec((None,) + a.shape[1:], lambda *_: (layer,) + (0,) * (a.ndim - 1), **kwargs)


def _layer_norm(v, g, b):
    mu = jnp.mean(v, axis=-1, keepdims=True)
    vc = v - mu
    var = jnp.mean(vc * vc, axis=-1, keepdims=True)
    return vc * lax.rsqrt(var + NORM_EPS) * g + b


def _ssm_piece(t, v):
    lo = SSM_WIDTH * t + LANES * v
    return slice(lo, lo + LANES)


def _pool_mix(x0, buf, lvl_a, lvl_b, pos, w_ref, scale_ref, g_ref):
    n = x0.shape[0] + POOL_HALO
    group = lax.broadcasted_iota(jnp.int32, (1, POOL_WIDTH), 1) // POOL_GROUP_DIM
    mean = jnp.zeros_like(x0)
    src, dst = buf, lvl_a
    for gi, w in enumerate(POOL_WINDOWS):
        lo = 8 * (gi + 1)
        dst[lo:n, :] = src[lo:n, :] + src[lo - w // 2:n - w // 2, :]
        inv_cnt = 1.0 / jnp.minimum(pos + 1, w).astype(F32)
        mean = jnp.where(group == gi, dst[POOL_HALO:n, :] * inv_cnt, mean)
        src, dst = dst, (lvl_b if dst is lvl_a else lvl_a)
    d = (mean - x0).astype(BF16)
    y = _dot(d, w_ref[...]) * scale_ref[...]
    r = lax.rsqrt(jnp.mean(y * y, axis=-1, keepdims=True) + NORM_EPS)
    return (y * r * g_ref[...]).astype(BF16)


def _inproj_body(x_ref, w_ref, wpool_ref, pscale_ref, pgain_ref, yp_ref, q_ref, k_ref, v_ref, us_ref,
                 zs, buf, lvl_a, lvl_b, *, tiles_per_seq):
    xb = x_ref[...].astype(BF16)
    tm = x_ref.shape[0]
    tile_in_seq = pl.program_id(0) % tiles_per_seq

    def cols(lo, hi):
        return _dot(xb, w_ref[:, lo:hi])

    u_pool = cols(0, 256)

    @pl.when(tile_in_seq == 0)
    def _():
        buf[0:POOL_HALO, :] = jnp.zeros((POOL_HALO, POOL_WIDTH), F32)

    buf[POOL_HALO:, :] = u_pool
    pos = tile_in_seq * tm + lax.broadcasted_iota(jnp.int32, (tm, 1), 0)
    yp_ref[...] = _pool_mix(u_pool, buf, lvl_a, lvl_b, pos, wpool_ref, pscale_ref, pgain_ref)
    buf[0:POOL_HALO, :] = u_pool[tm - POOL_HALO:, :]

    qk = cols(256, 768)
    q_ref[...] = (qk[:, :ATTN_WIDTH] * (ATTN_HEAD_DIM ** -0.5 * LOG2_E)).astype(BF16)
    k_ref[:, :LANES] = qk[:, ATTN_WIDTH:].astype(BF16)
    k_ref[:, LANES:] = cols(768, 1024).astype(BF16)
    vs = cols(1024, 1536)
    v_ref[...] = vs[:, :ATTN_WIDTH].astype(BF16)
    zs[0] = vs[:, ATTN_WIDTH:]
    s_rest = cols(1536, 1792)
    zs[1] = s_rest[:, :LANES]
    zs[2] = s_rest[:, LANES:]
    chunks = us_ref.shape[0]
    for t in range(SSM_CHUNK):
        for v in range(SSM_LANE_BLOCKS):
            us_ref[:, _ssm_piece(t, v)] = zs[v, pl.ds(t, chunks, stride=SSM_CHUNK), :]


def _inproj(x, w_bf16, w_pool_blockdiag_bf16, pool_scale, pool_gain, layer, batch):
    t = x.shape[0]
    tm = INPROJ_TOKENS
    row = lambda width: pl.BlockSpec((tm, width), lambda i: (i, 0))
    full = lambda a: _layer_block(a, layer)
    chunk_rows = pl.BlockSpec((tm // SSM_CHUNK, SSM_CHUNK * SSM_WIDTH), lambda i: (i, 0))
    return pl.pallas_call(
        functools.partial(_inproj_body, tiles_per_seq=t // batch // tm),
        grid=(t // tm,),
        in_specs=[row(D_MODEL), full(w_bf16), full(w_pool_blockdiag_bf16), full(pool_scale), full(pool_gain)],
        out_specs=[row(POOL_WIDTH), row(ATTN_WIDTH), row(ATTN_WIDTH), row(ATTN_WIDTH), chunk_rows],
        out_shape=[jax.ShapeDtypeStruct((t, POOL_WIDTH), BF16),
                   jax.ShapeDtypeStruct((t, ATTN_WIDTH), BF16),
                   jax.ShapeDtypeStruct((t, ATTN_WIDTH), BF16),
                   jax.ShapeDtypeStruct((t, ATTN_WIDTH), BF16),
                   jax.ShapeDtypeStruct((t // SSM_CHUNK, SSM_CHUNK * SSM_WIDTH), F32)],
        scratch_shapes=[pltpu.VMEM((SSM_LANE_BLOCKS, tm, LANES), F32)]
        + [pltpu.VMEM((POOL_HALO + tm, POOL_WIDTH), F32)] * 3,
        compiler_params=_cparams("arbitrary"),
        name="inproj",
    )(x, w_bf16, w_pool_blockdiag_bf16, pool_scale, pool_gain)


def _attn_bias_rows(rel_bias):
    x = np.arange(ATTN_BIAS_ROW)
    x = np.where(x < ATTN_BAND_TOKENS, x, x - ATTN_BIAS_ROW)
    idx = np.clip(N_PREV_CHUNKS * CHUNK - x, -REL_CLIP, REL_CLIP) + REL_CLIP
    return rel_bias.astype(F32)[..., idx] * LOG2_E


def _attn_body(q_ref, k0_ref, k1_ref, k2_ref, v0_ref, v1_ref, v2_ref, rows_ref, g_ref, o_ref, bias_s):
    b = pl.program_id(0)
    i = pl.program_id(1)
    tq = ATTN_Q_TOKENS

    @pl.when((b == 0) & (i == 0))
    def _():
        qc = lax.broadcasted_iota(jnp.int32, (tq, ATTN_BAND_TOKENS), 0) // CHUNK
        kc = lax.broadcasted_iota(jnp.int32, (tq, ATTN_BAND_TOKENS), 1) // CHUNK
        in_band = (kc >= qc) & (kc <= qc + N_PREV_CHUNKS)
        for head in range(ATTN_HEADS):
            full = jnp.broadcast_to(rows_ref[head:head + 1, :], (tq, ATTN_BIAS_ROW))
            shifted = pltpu.roll(full, 0, 1, stride=1, stride_axis=0)
            bias_s[head // 2, (head % 2) * tq:(head % 2 + 1) * tq, :] = jnp.where(
                in_band, shifted[:, :ATTN_BAND_TOKENS], -jnp.inf)

    upper_half = lax.broadcasted_iota(jnp.int32, (1, LANES), 1) >= ATTN_HEAD_DIM

    def heads(masked):
        outs = []
        for pair in range(ATTN_WIDTH // LANES):
            sl = slice(pair * LANES, (pair + 1) * LANES)
            qp = q_ref[:, sl]
            kp = jnp.concatenate([k0_ref[:, sl], k1_ref[:, sl], k2_ref[:, sl]], axis=0)
            vp = jnp.concatenate([v0_ref[:, sl], v1_ref[:, sl], v2_ref[:, sl]], axis=0)
            zero = jnp.zeros_like(qp)
            q2 = jnp.concatenate([jnp.where(upper_half, zero, qp), jnp.where(upper_half, qp, zero)], axis=0)
            s = lax.dot_general(q2, kp, (((1,), (1,)), ((), ())), preferred_element_type=F32)
            s = s + bias_s[pair]
            if masked:
                key = lax.broadcasted_iota(jnp.int32, (1, ATTN_BAND_TOKENS), 1)
                s = jnp.where(((key < tq) & (i < 2)) | ((key < 2 * tq) & (i < 1)), -jnp.inf, s)
            m = jnp.max(s, axis=-1, keepdims=True)
            p = jnp.exp2(s - m)
            l = jnp.sum(p, axis=-1, keepdims=True)
            o = _dot(p.astype(BF16), vp) * (1.0 / l)
            outs.append(jnp.where(upper_half, o[tq:, :], o[:tq, :]))
        ss = sum(jnp.sum(o * o, axis=-1, keepdims=True) for o in outs)
        r = lax.rsqrt(ss / ATTN_WIDTH + NORM_EPS)
        for pair, o in enumerate(outs):
            sl = slice(pair * LANES, (pair + 1) * LANES)
            o_ref[:, sl] = (o * r * g_ref[:, sl]).astype(BF16)

    @pl.when(i < 2)
    def _():
        heads(True)

    @pl.when(i >= 2)
    def _():
        heads(False)


def _attention(q, k, v, bias_rows, gain, layer, batch):
    t = q.shape[0]
    tq = ATTN_Q_TOKENS
    nq = t // batch // tq

    def blk(back):
        return pl.BlockSpec((tq, ATTN_WIDTH), lambda b, i: (b * nq + jnp.maximum(i - back, 0), 0))

    return pl.pallas_call(
        _attn_body,
        grid=(batch, nq),
        in_specs=[blk(0), blk(2), blk(1), blk(0), blk(2), blk(1), blk(0),
                  _layer_block(bias_rows, layer), _layer_block(gain, layer)],
        out_specs=blk(0),
        out_shape=jax.ShapeDtypeStruct((t, ATTN_WIDTH), BF16),
        scratch_shapes=[pltpu.VMEM((ATTN_HEADS // 2, 2 * tq, ATTN_BAND_TOKENS), F32)],
        compiler_params=_cparams("arbitrary", "arbitrary"),
        name="attention",
    )(q, k, k, k, v, v, v, bias_rows, gain)


def _s5_position_of_time(g, time):
    return SSM_SLOTS * (time // SSM_SLOTS) + (time % SSM_SLOTS + g) % SSM_SLOTS


def _s5_tables(a_re, a_im, log_dt, b_re, b_im, c_re, c_im):
    hi = lax.Precision.HIGHEST
    tc = SSM_CHUNK
    g, p_dim = a_re.shape
    dt = jnp.exp(log_dt.astype(F32))[:, None]
    ar = a_re.astype(F32)
    ai = a_im.astype(F32)
    mag = jnp.exp(ar * dt)
    abar_re = mag * jnp.cos(ai * dt)
    abar_im = mag * jnp.sin(ai * dt)
    den = ar * ar + ai * ai
    nr = abar_re - 1.0
    ni = abar_im
    coef_re = ((nr * ar + ni * ai) / den)[..., None]
    coef_im = ((ni * ar - nr * ai) / den)[..., None]
    br = b_re.astype(F32)
    bi = b_im.astype(F32)
    bbar_re = coef_re * br - coef_im * bi
    bbar_im = coef_re * bi + coef_im * br
    n = jnp.arange(tc + 1, dtype=F32)[:, None, None]
    pmag = jnp.exp(n * (ar * dt))
    pw_re = pmag * jnp.cos(n * (ai * dt))
    pw_im = pmag * jnp.sin(n * (ai * dt))
    x_re = pw_re[..., None] * bbar_re[None] - pw_im[..., None] * bbar_im[None]
    x_im = pw_re[..., None] * bbar_im[None] + pw_im[..., None] * bbar_re[None]
    cr = c_re.astype(F32)
    ci = c_im.astype(F32)
    cw = tc * SSM_GROUP_DIM
    kern = (jnp.einsum('ghp,ngpk->gknh', cr, x_re[:tc], precision=hi)
            - jnp.einsum('ghp,ngpk->gknh', ci, x_im[:tc], precision=hi)).reshape(g, SSM_GROUP_DIM, cw)
    est_re = x_re[:tc].transpose(1, 0, 3, 2).reshape(g, cw, p_dim)
    est_im = x_im[:tc].transpose(1, 0, 3, 2).reshape(g, cw, p_dim)
    crt = cr.transpose(0, 2, 1)[:, :, None, :]
    cit = ci.transpose(0, 2, 1)[:, :, None, :]
    nx_re = pw_re[1:].transpose(1, 2, 0)[..., None]
    nx_im = pw_im[1:].transpose(1, 2, 0)[..., None]
    int_re = (crt * nx_re - cit * nx_im).reshape(g, p_dim, cw)
    int_im = (-(crt * nx_im + cit * nx_re)).reshape(g, p_dim, cw)

    odd = (jnp.arange(g) % 2 == 1)[:, None, None]
    zero_e = jnp.zeros_like(est_re)
    lane_pad = lambda m: jnp.where(odd, jnp.concatenate([zero_e, m], -1), jnp.concatenate([m, zero_e], -1))
    zero_i = jnp.zeros_like(int_re)
    row_pad = lambda m: jnp.where(odd, jnp.concatenate([zero_i, m], 1), jnp.concatenate([m, zero_i], 1))
    return dict(
        kern=kern,
        est=jnp.concatenate([lane_pad(est_re), lane_pad(est_im)], -1).astype(BF16),
        inter=jnp.concatenate([row_pad(int_re), row_pad(int_im)], 1),
        apow_re=pw_re[tc].reshape(1, g * p_dim), apow_im=pw_im[tc].reshape(1, g * p_dim))


def _s5_prepare(kern_ref, est_ref, int_ref, toep_s, est_s, int_s):
    lane = lax.broadcasted_iota(jnp.int32, (1, LANES), 1)
    zero = jnp.zeros((SSM_GROUP_DIM, LANES), F32)
    for g in range(SSM_GROUPS):
        turn = SSM_GROUP_DIM * (g % SSM_SLOTS)
        spin = lambda a: pltpu.roll(a, turn, 1) if turn else a
        k0 = kern_ref[g, :, 0:LANES]
        k1 = kern_ref[g, :, LANES:2 * LANES]
        for time in range(SSM_CHUNK):
            shift = SSM_GROUP_DIM * (time % SSM_SLOTS)
            r0 = pltpu.roll(k0, shift, 1) if shift else k0
            if time < SSM_SLOTS:
                r1 = pltpu.roll(k1, shift, 1) if shift else k1
                h0 = jnp.where(lane >= shift, r0, zero)
                h1 = jnp.where(lane >= shift, r1, r0)
            else:
                h0 = zero
                h1 = jnp.where(lane >= shift, r0, zero)
            rows = pl.ds(SSM_GROUP_DIM * _s5_position_of_time(g, time), SSM_GROUP_DIM)
            toep_s[g, rows, 0:LANES] = spin(h0).astype(BF16)
            toep_s[g, rows, LANES:2 * LANES] = spin(h1).astype(BF16)
            est_s[g, rows, :] = est_ref[g, pl.ds(SSM_GROUP_DIM * (SSM_CHUNK - 1 - time), SSM_GROUP_DIM), :]
        for m in range(SSM_TIME_BLOCKS):
            int_s[g, :, m * LANES:(m + 1) * LANES] = spin(int_ref[g, :, m * LANES:(m + 1) * LANES]).astype(BF16)


def _s5_body(u_ref, kern_ref, estin_ref, intin_ref, apre_ref, apim_ref, d_ref, y_ref,
             toep_ref, est_ref, int_ref, ub_s, ere_s, eim_s, spre_s, spim_s, sre_s, sim_s):
    rows = u_ref.shape[0]

    @pl.when((pl.program_id(0) == 0) & (pl.program_id(1) == 0))
    def _():
        _s5_prepare(kern_ref, estin_ref, intin_ref, toep_ref, est_ref, int_ref)

    @pl.when(pl.program_id(1) == 0)
    def _():
        sre_s[...] = jnp.zeros_like(sre_s)
        sim_s[...] = jnp.zeros_like(sim_s)

    slot = lax.broadcasted_iota(jnp.int32, (1, LANES), 1) // SSM_GROUP_DIM
    piece = _ssm_piece

    for v in range(SSM_LANE_BLOCKS):
        for m in range(SSM_TIME_BLOCKS):
            rot = []
            for j in range(SSM_SLOTS):
                a = u_ref[:, piece(SSM_SLOTS * m + j, v)]
                rot.append(a if j == 0 else pltpu.roll(a, SSM_GROUP_DIM * j, 1))
            for gam in range(SSM_SLOTS):
                o = rot[0]
                for j in range(1, SSM_SLOTS):
                    o = jnp.where(slot == (j + gam) % SSM_SLOTS, rot[j], o)
                ub_s[SSM_SLOTS * v + gam, :, m * LANES:(m + 1) * LANES] = o.astype(BF16)

    for q in range(SSM_GROUPS // 2):
        e = _dot(ub_s[2 * q], est_ref[2 * q]) + _dot(ub_s[2 * q + 1], est_ref[2 * q + 1])
        ere_s[:, q * LANES:(q + 1) * LANES] = e[:, :SSM_PAIR_WIDTH]
        eim_s[:, q * LANES:(q + 1) * LANES] = e[:, SSM_PAIR_WIDTH:]

    a_re = apre_ref[...]
    a_im = apim_ref[...]

    def carry_step(r, carry):
        s_re, s_im = carry
        spre_s[pl.ds(r, 1), :] = s_re
        spim_s[pl.ds(r, 1), :] = s_im
        e_re = ere_s[pl.ds(r, 1), :]
        e_im = eim_s[pl.ds(r, 1), :]
        return (a_re * s_re - a_im * s_im + e_re, a_re * s_im + a_im * s_re + e_im)

    s_re, s_im = lax.fori_loop(0, rows, carry_step, (sre_s[...], sim_s[...]))
    sre_s[...] = s_re
    sim_s[...] = s_im

    for v in range(SSM_LANE_BLOCKS):
        yg = []
        for gam in range(SSM_SLOTS):
            g = SSM_SLOTS * v + gam
            q = g // 2
            sp = jnp.concatenate([spre_s[:, q * LANES:(q + 1) * LANES], spim_s[:, q * LANES:(q + 1) * LANES]],
                                 axis=1).astype(BF16)
            yg.append(_dot(ub_s[g], toep_ref[g]) + _dot(sp, int_ref[g]))
        d = d_ref[:, v * LANES:(v + 1) * LANES]
        for m in range(SSM_TIME_BLOCKS):
            for j in range(SSM_SLOTS):
                o = yg[0][:, m * LANES:(m + 1) * LANES]
                for gam in range(1, SSM_SLOTS):
                    o = jnp.where(slot == (j + gam) % SSM_SLOTS, yg[gam][:, m * LANES:(m + 1) * LANES], o)
                if j:
                    o = pltpu.roll(o, LANES - SSM_GROUP_DIM * j, 1)
                sl = piece(SSM_SLOTS * m + j, v)
                y_ref[:, sl] = jax.nn.gelu(o + d * u_ref[:, sl])


def _s5(u_rows, tab, d_skip, layer, batch):
    nch, width = u_rows.shape
    rows = SSM_ROWS
    steps = nch // batch // rows
    once = lambda a: _layer_block(a, layer, pipeline_mode=pl.Buffered(1))
    blk = pl.BlockSpec((rows, width), lambda b, i: (b * steps + i, 0))
    table = pltpu.VMEM((SSM_GROUPS, SSM_CHUNK_WIDTH, SSM_CHUNK_WIDTH), BF16)
    state = pltpu.VMEM((rows, SSM_STATE_LANES), F32)
    carry = pltpu.VMEM((1, SSM_STATE_LANES), F32)
    return pl.pallas_call(
        _s5_body,
        grid=(batch, steps),
        in_specs=[blk, once(tab['kern']), once(tab['est']), once(tab['inter']),
                  once(tab['apow_re']), once(tab['apow_im']), once(d_skip)],
        out_specs=blk,
        out_shape=jax.ShapeDtypeStruct((nch, width), F32),
        scratch_shapes=[table, table, table,
                        pltpu.VMEM((SSM_GROUPS, rows, SSM_CHUNK_WIDTH), BF16), state, state, state, state,
                        carry, carry],
        compiler_params=_cparams("arbitrary", "arbitrary"),
        name="s5",
    )(u_rows, tab['kern'], tab['est'], tab['inter'], tab['apow_re'], tab['apow_im'], d_skip)


def _route_rows(scores, biased):
    ng = N_EXPERTS // EXPERTS_PER_GROUP
    group_score = []
    for gi in range(ng):
        a, b, c, d = biased[gi * EXPERTS_PER_GROUP:(gi + 1) * EXPERTS_PER_GROUP]
        hi1, lo1 = jnp.maximum(a, b), jnp.minimum(a, b)
        hi2, lo2 = jnp.maximum(c, d), jnp.minimum(c, d)
        top1 = jnp.maximum(hi1, hi2)
        top2 = jnp.maximum(jnp.minimum(hi1, hi2), jnp.maximum(lo1, lo2))
        group_score.append(top1 + top2)
    best = group_score[0]
    best_idx = jnp.zeros_like(best, dtype=jnp.int32)
    for gi in range(1, ng):
        better = group_score[gi] > best
        best = jnp.where(better, group_score[gi], best)
        best_idx = jnp.where(better, gi, best_idx)
    picked = []
    for e in range(N_EXPERTS):
        gi = e // EXPERTS_PER_GROUP
        rank = jnp.zeros_like(best_idx)
        for o in range(gi * EXPERTS_PER_GROUP, (gi + 1) * EXPERTS_PER_GROUP):
            if o == e:
                continue
            ahead = (biased[o] > biased[e]) | ((biased[o] == biased[e]) & (o < e))
            rank = rank + ahead.astype(jnp.int32)
        picked.append((best_idx == gi) & (rank < 2))
    wsum = sum(jnp.where(picked[e], scores[e], 0.0) for e in range(N_EXPERTS))
    return [jnp.where(picked[e], scores[e] / wsum, 0.0) for e in range(N_EXPERTS)], best_idx


def _group_sort_positions(best_idx, before_ref):
    ng = N_EXPERTS // EXPERTS_PER_GROUP
    tokens = best_idx.shape[1]
    member = [(best_idx == gi).astype(F32) for gi in range(ng)]
    stacked = jnp.concatenate(member + [jnp.zeros((8 - ng, tokens), F32)], axis=0)
    parts = []
    run = jnp.zeros((8, 1), F32)
    for blk in range(tokens // LANES):
        piece = stacked[:, blk * LANES:(blk + 1) * LANES]
        parts.append(_dot(piece.astype(BF16), before_ref[...]) + run)
        run = run + jnp.sum(piece, axis=1, keepdims=True)
    earlier = jnp.concatenate(parts, axis=1)
    counts = [run[gi:gi + 1, :] for gi in range(ng)]
    pos = jnp.zeros_like(member[0])
    start = jnp.zeros_like(counts[0])
    for gi in range(ng):
        pos = pos + member[gi] * (start + earlier[gi:gi + 1, :])
        start = start + counts[gi]
    return pos, counts


def _outproj_body(x_ref, yp_ref, ya_ref, ys_ref, wglu_ref, bglu_ref, gssm_ref, wout_ref, g_ref, b_ref,
                  wr_ref, rb_ref, before_ref, h_ref, comb_ref, pos_ref, cnt_ref, ys_s):
    chunks = ys_ref.shape[0]
    for t in range(SSM_CHUNK):
        for v in range(SSM_LANE_BLOCKS):
            ys_s[v, pl.ds(t, chunks, stride=SSM_CHUNK), :] = ys_ref[:, _ssm_piece(t, v)]
    ys = jnp.concatenate([ys_s[v] for v in range(SSM_LANE_BLOCKS)], axis=1)
    gate = jax.nn.sigmoid(_dot(ys.astype(BF16), wglu_ref[...]) + bglu_ref[...])
    ys = ys * gate
    r = lax.rsqrt(jnp.mean(ys * ys, axis=-1, keepdims=True) + NORM_EPS)
    ysn = (ys * r * gssm_ref[...]).astype(BF16)
    mix = _dot(jnp.concatenate([yp_ref[...], ya_ref[...], ysn], axis=1), wout_ref[...])
    h = _layer_norm(DN_ALPHA * x_ref[...] + mix, g_ref[...], b_ref[...])
    h_ref[...] = h
    h_hi = h.astype(BF16)
    h_lo = (h - h_hi.astype(F32)).astype(BF16)
    parts_t = (_dot(h_hi, wr_ref[...]) + _dot(h_lo, wr_ref[...])).T
    sc = jax.nn.sigmoid(parts_t[:N_EXPERTS, :] + parts_t[N_EXPERTS:2 * N_EXPERTS, :])
    bs = sc + rb_ref[...]
    scores = [sc[e:e + 1, :] for e in range(N_EXPERTS)]
    biased = [bs[e:e + 1, :] for e in range(N_EXPERTS)]
    comb_rows, best_idx = _route_rows(scores, biased)
    pos, counts = _group_sort_positions(best_idx, before_ref)
    tokens = sc.shape[1]
    comb_t = jnp.concatenate(comb_rows + [pos, jnp.zeros((LANES - N_EXPERTS - 1, tokens), F32)], axis=0)
    comb_ref[...] = comb_t.T
    pos_ref[...] = jnp.concatenate([pos, jnp.zeros((7, tokens), F32)], axis=0)
    cnt_ref[...] = jnp.concatenate(
        [jnp.broadcast_to(c, (1, LANES)) for c in counts]
        + [jnp.zeros((8 - len(counts), LANES), F32)], axis=0).astype(jnp.int32)


def _outproj(x, y_pool, y_attn, y_ssm_rows, w_glu_bf16, b_glu, g_ssm, w_out_bf16, ln_g, ln_b,
             w_router_split, router_bias, layer):
    t = x.shape[0]
    tm = MOE_TOKENS
    nt = t // tm
    row = lambda width: pl.BlockSpec((tm, width), lambda i: (i, 0))
    full = lambda a: pl.BlockSpec(a.shape, lambda i: (0,) * a.ndim)
    per_layer = lambda a: _layer_block(a, layer)
    token = np.arange(LANES)
    before = jnp.asarray(token[:, None] < token[None, :], BF16)
    return pl.pallas_call(
        _outproj_body,
        grid=(nt,),
        in_specs=[row(D_MODEL), row(POOL_WIDTH), row(ATTN_WIDTH),
                  pl.BlockSpec((tm // SSM_CHUNK, SSM_CHUNK * SSM_WIDTH), lambda i: (i, 0)),
                  per_layer(w_glu_bf16), per_layer(b_glu), per_layer(g_ssm), per_layer(w_out_bf16),
                  per_layer(ln_g), per_layer(ln_b), full(w_router_split), full(router_bias), full(before)],
        out_specs=[row(D_MODEL), row(LANES), pl.BlockSpec((8, tm), lambda i: (0, i)),
                   pl.BlockSpec((8, LANES), lambda i: (i, 0))],
        out_shape=[jax.ShapeDtypeStruct((t, D_MODEL), F32), jax.ShapeDtypeStruct((t, LANES), F32),
                   jax.ShapeDtypeStruct((8, t), F32), jax.ShapeDtypeStruct((8 * nt, LANES), jnp.int32)],
        scratch_shapes=[pltpu.VMEM((SSM_LANE_BLOCKS, tm, LANES), F32)],
        compiler_params=_cparams("parallel"),
        name="outproj",
    )(x, y_pool, y_attn, y_ssm_rows, w_glu_bf16, b_glu, g_ssm, w_out_bf16, ln_g, ln_b,
      w_router_split, router_bias, before)


def _moe_body(cnt_ref, h_ref, comb_ref, pos_ref, p_ref, wg_ref, wu_ref, wd_ref, wpg_ref, wpp_ref, g_ref, b_ref,
              o_ref, hs_s, cs_s, acc_s):
    i = pl.program_id(0)
    group = pl.program_id(1)
    ng = pl.num_programs(1)
    tm = h_ref.shape[0]

    @pl.when(group == 0)
    def _():
        row = lax.broadcasted_iota(jnp.int32, (tm, tm), 0).astype(F32)
        perm = jnp.where(pos_ref[0:1, :] == row, 1.0, 0.0).astype(BF16)
        hs_s[...] = _dot(perm, h_ref[...].astype(BF16)).astype(BF16)
        comb = comb_ref[...]
        comb_lo = comb - comb.astype(BF16).astype(F32)
        low_lanes = lax.broadcasted_iota(jnp.int32, (1, LANES), 1) < COMB_LO_LANE
        both = _dot(perm, jnp.where(low_lanes, comb, pltpu.roll(comb_lo, COMB_LO_LANE, 1)).astype(BF16))
        cs_s[...] = both + pltpu.roll(both, LANES - COMB_LO_LANE, 1)
        acc_s[...] = jnp.zeros_like(acc_s)

    count = cnt_ref[i * ng + group]
    start = jnp.int32(0)
    for gi in range(N_EXPERTS // EXPERTS_PER_GROUP - 1):
        start = start + jnp.where(group > gi, cnt_ref[i * ng + gi], 0)
    lane = lax.broadcasted_iota(jnp.int32, (1, LANES), 1)
    first = (start // BF16_ROW_PACK) * BF16_ROW_PACK
    windows = (start - first + count + MOE_WINDOW_ROWS - 1) // MOE_WINDOW_ROWS

    def window(w, carry):
        wanted = first + w * MOE_WINDOW_ROWS
        lo = pl.multiple_of(jnp.minimum(wanted, tm - MOE_WINDOW_ROWS), BF16_ROW_PACK)
        rows = pl.ds(lo, MOE_WINDOW_ROWS)
        x = hs_s[rows, :]
        fresh = lo + lax.broadcasted_iota(jnp.int32, (MOE_WINDOW_ROWS, 1), 0) >= wanted
        cs = jnp.where(fresh, cs_s[rows, :], 0.0)
        total = None
        for e in range(EXPERTS_PER_GROUP):
            gate = _dot(x, wg_ref[e])
            up = _dot(x, wu_ref[e])
            c = jnp.sum(jnp.where(lane == group * EXPERTS_PER_GROUP + e, cs, 0.0), axis=1, keepdims=True)
            a = (jax.nn.silu(gate) * up * c).astype(BF16)
            d = _dot(a, wd_ref[e])
            total = d if total is None else total + d
        acc_s[rows, :] += total
        return carry

    lax.fori_loop(0, windows, window, 0)

    @pl.when(group == ng - 1)
    def _():
        col = lax.broadcasted_iota(jnp.int32, (tm, tm), 1).astype(F32)
        unperm = jnp.where(comb_ref[:, SORT_POS_LANE:SORT_POS_LANE + 1] == col, 1.0, 0.0).astype(BF16)
        ffn = _dot(unperm, acc_s[...].astype(BF16))
        hb = h_ref[...].astype(BF16)
        ple = jax.nn.sigmoid(_dot(hb, wpg_ref[...])) * _dot(p_ref[...].astype(BF16), wpp_ref[...])
        o_ref[...] = _layer_norm(DN_ALPHA * h_ref[...] + ffn + ple, g_ref[...], b_ref[...])


def _moe(h, comb, pos_rows, counts, p_all, layer, wg_bf16, wu_bf16, wd_bf16, wpg_bf16, wpp_bf16, ln_g, ln_b):
    t = h.shape[0]
    tm = MOE_TOKENS
    nt = t // tm
    ng = N_EXPERTS // EXPERTS_PER_GROUP
    full = lambda a: _layer_block(a, layer)
    experts = lambda rows, cols: pl.BlockSpec((EXPERTS_PER_GROUP, rows, cols),
                                              lambda i, g, cnt: (layer * ng + g, 0, 0))
    grid_spec = pltpu.PrefetchScalarGridSpec(
        num_scalar_prefetch=1,
        grid=(nt, ng),
        in_specs=[pl.BlockSpec((tm, D_MODEL), lambda i, g, cnt: (i, 0)),
                  pl.BlockSpec((tm, LANES), lambda i, g, cnt: (i, 0)),
                  pl.BlockSpec((8, tm), lambda i, g, cnt: (0, i)),
                  pl.BlockSpec((tm, PLE_DIM), lambda i, g, cnt: (layer * nt + i, 0)),
                  experts(D_MODEL, D_EXPERT), experts(D_MODEL, D_EXPERT), experts(D_EXPERT, D_MODEL),
                  full(wpg_bf16), full(wpp_bf16), full(ln_g), full(ln_b)],
        out_specs=pl.BlockSpec((tm, D_MODEL), lambda i, g, cnt: (i, 0)),
        scratch_shapes=[pltpu.VMEM((tm, D_MODEL), BF16), pltpu.VMEM((tm, LANES), F32),
                        pltpu.VMEM((tm, D_MODEL), F32)])
    return pl.pallas_call(
        _moe_body,
        grid_spec=grid_spec,
        out_shape=jax.ShapeDtypeStruct((t, D_MODEL), F32),
        compiler_params=_cparams("parallel", "arbitrary"),
        name="moe",
    )(counts, h, comb, pos_rows, p_all, wg_bf16, wu_bf16, wd_bf16, wpg_bf16, wpp_bf16, ln_g, ln_b)


def _block_diag(w):
    g, n, m = w.shape
    eye = jnp.eye(g, dtype=w.dtype)
    return (eye[:, None, :, None] * w[:, :, None, :]).reshape(g * n, g * m)


def kernel(x, p, w_in, w_out, w_pool, pool_scale, rel_bias, ssm_a_re, ssm_a_im, ssm_log_dt, ssm_b_re, ssm_b_im,
           ssm_c_re, ssm_c_im, ssm_d, w_glu, b_glu, g_pool, g_attn, g_ssm, ln1_g, ln1_b, ln2_g, ln2_b,
           w_router, router_bias, w_exp_gate, w_exp_up, w_exp_down, w_ple_gate, w_ple_proj):
    batch, seq, d = x.shape
    t = batch * seq
    xt = x.reshape(t, d)
    p_all = p.reshape(DEPTH * t, PLE_DIM)

    vec = lambda a: a.astype(F32).reshape(DEPTH, 1, -1)
    w_in_b = w_in.astype(BF16)
    w_pool_b = jax.vmap(_block_diag)(w_pool).astype(BF16)
    bias_rows = _attn_bias_rows(rel_bias)
    tables = jax.vmap(_s5_tables)(ssm_a_re, ssm_a_im, ssm_log_dt, ssm_b_re, ssm_b_im, ssm_c_re, ssm_c_im)
    w_glu_b = w_glu.astype(BF16)
    w_out_b = w_out.astype(BF16)
    wr = w_router.astype(F32)
    wr_hi = wr.astype(BF16)
    wr_lo = (wr - wr_hi.astype(F32)).astype(BF16)
    wr_split = jnp.pad(jnp.concatenate([wr_hi, wr_lo], axis=1), ((0, 0), (0, LANES - 2 * N_EXPERTS)))
    r_bias = router_bias.astype(F32).reshape(N_EXPERTS, 1)
    stack_experts = lambda w: w.astype(BF16).reshape((DEPTH * N_EXPERTS,) + w.shape[2:])
    wg_b, wu_b, wd_b = stack_experts(w_exp_gate), stack_experts(w_exp_up), stack_experts(w_exp_down)
    wpg_b = w_ple_gate.astype(BF16)
    wpp_b = w_ple_proj.astype(BF16)
    ng = N_EXPERTS // EXPERTS_PER_GROUP

    for layer in range(DEPTH):
        y_pool, q, k, v, u_ssm = _inproj(xt, w_in_b, w_pool_b, vec(pool_scale), vec(g_pool), layer, batch)
        y_attn = _attention(q, k, v, bias_rows, vec(g_attn), layer, batch)
        y_ssm = _s5(u_ssm, tables, vec(ssm_d), layer, batch)
        h, comb, pos_rows, cnt = _outproj(xt, y_pool, y_attn, y_ssm, w_glu_b, vec(b_glu), vec(g_ssm), w_out_b,
                                          vec(ln1_g), vec(ln1_b), wr_split, r_bias, layer)
        counts = cnt[:, 0].reshape(-1, 8)[:, :ng].reshape(-1)
        xt = _moe(h, comb, pos_rows, counts, p_all, layer, wg_b, wu_b, wd_b, wpg_b, wpp_b, vec(ln2_g), vec(ln2_b))
    return xt.reshape(batch, seq, d)
```

```python
import functools
import math

import numpy as np
import jax
import jax.numpy as jnp
from jax import lax
from jax.experimental import pallas as pl
from jax.experimental.pallas import tpu as pltpu

F32 = jnp.float32
BF16 = jnp.bfloat16

D_MODEL = 1024
DEPTH = 2
CHUNK = 64
PLE_DIM = 256
POOL_WIDTH = 256
POOL_GROUP_DIM = 64
POOL_WINDOWS = (2, 4, 8, 16)
POOL_HALO = 32
ATTN_HEAD_DIM = 64
ATTN_HEADS = 6
ATTN_WIDTH = 384
N_PREV_CHUNKS = 8
REL_CLIP = 128
SSM_WIDTH = 384
SSM_GROUP_DIM = 16
SSM_GROUPS = 24
SSM_STATE = 64
N_EXPERTS = 16
EXPERTS_PER_GROUP = 4
D_EXPERT = 256
DN_ALPHA = (2 * DEPTH) ** 0.25
NORM_EPS = 1e-5
LOG2_E = math.log2(math.e)

LANES = 128
VMEM_LIMIT_BYTES = 56 * 1024 * 1024

INPROJ_TOKENS = 1024
INPROJ_ROWS = 512
ATTN_Q_CHUNKS = 4
ATTN_Q_TOKENS = ATTN_Q_CHUNKS * CHUNK
ATTN_BAND_TOKENS = 3 * ATTN_Q_TOKENS
ATTN_BIAS_ROW = 1024
SSM_CHUNK = 16
SSM_CHUNK_WIDTH = SSM_CHUNK * SSM_GROUP_DIM
SSM_ROWS = 128
SSM_SLOTS = LANES // SSM_GROUP_DIM
SSM_LANE_BLOCKS = SSM_WIDTH // LANES
SSM_TIME_BLOCKS = SSM_CHUNK // SSM_SLOTS
SSM_PAIR_WIDTH = 2 * SSM_STATE
SSM_STATE_LANES = SSM_GROUPS * SSM_STATE
OUT_ROWS = 256
MOE_SIDE_ROWS = 256
MOE_TOKENS = 1024
MOE_WINDOW_ROWS = 320
BF16_ROW_PACK = 16
SORT_POS_LANE = N_EXPERTS
COMB_LO_LANE = 32


def _cparams(*sem):
    return pltpu.CompilerParams(dimension_semantics=sem, vmem_limit_bytes=VMEM_LIMIT_BYTES)


def _dot(a, b):
    return jnp.dot(a, b, preferred_element_type=F32)


def _layer_block(a, layer, **kwargs):
    return pl.BlockSpec((None,) + a.shape[1:], lambda *_: (layer,) + (0,) * (a.ndim - 1), **kwargs)


def _layer_norm(v, g, b):
    mu = jnp.mean(v, axis=-1, keepdims=True)
    vc = v - mu
    var = jnp.mean(vc * vc, axis=-1, keepdims=True)
    return vc * lax.rsqrt(var + NORM_EPS) * g + b


def _ssm_piece(t, v):
    lo = SSM_WIDTH * t + LANES * v
    return slice(lo, lo + LANES)


def _pool_mix(x0, buf, lvl_a, lvl_b, pos, w_ref, scale_ref, g_ref):
    n = x0.shape[0] + POOL_HALO
    group = lax.broadcasted_iota(jnp.int32, (1, POOL_WIDTH), 1) // POOL_GROUP_DIM
    mean = jnp.zeros_like(x0)
    src, dst = buf, lvl_a
    for gi, w in enumerate(POOL_WINDOWS):
        lo = 8 * (gi + 1)
        dst[lo:n, :] = src[lo:n, :] + src[lo - w // 2:n - w // 2, :]
        inv_cnt = 1.0 / jnp.minimum(pos + 1, w).astype(F32)
        mean = jnp.where(group == gi, dst[POOL_HALO:n, :] * inv_cnt, mean)
        src, dst = dst, (lvl_b if dst is lvl_a else lvl_a)
    d = (mean - x0).astype(BF16)
    y = _dot(d, w_ref[...]) * scale_ref[...]
    r = lax.rsqrt(jnp.mean(y * y, axis=-1, keepdims=True) + NORM_EPS)
    return (y * r * g_ref[...]).astype(BF16)


def _inproj_body(x_ref, w_ref, wpool_ref, pscale_ref, pgain_ref, yp_ref, q_ref, k_ref, v_ref, us_ref,
                 zs, buf, lvl_a, lvl_b, *, tiles_per_seq):
    tm = x_ref.shape[0]
    tile_in_seq = pl.program_id(0) % tiles_per_seq

    @pl.when(tile_in_seq == 0)
    def _():
        buf[0:POOL_HALO, :] = jnp.zeros((POOL_HALO, POOL_WIDTH), F32)

    for r0 in range(0, tm, INPROJ_ROWS):
        sl = pl.ds(r0, INPROJ_ROWS)
        xb = x_ref[sl, :].astype(BF16)

        def cols(lo, hi):
            return _dot(xb, w_ref[:, lo:hi])

        buf[pl.ds(POOL_HALO + r0, INPROJ_ROWS), :] = cols(0, 256)
        qk = cols(256, 768)
        q_ref[sl, :] = (qk[:, :ATTN_WIDTH] * (ATTN_HEAD_DIM ** -0.5 * LOG2_E)).astype(BF16)
        k_ref[sl, :LANES] = qk[:, ATTN_WIDTH:].astype(BF16)
        k_ref[sl, LANES:] = cols(768, 1024).astype(BF16)
        vs = cols(1024, 1536)
        v_ref[sl, :] = vs[:, :ATTN_WIDTH].astype(BF16)
        zs[0, sl, :] = vs[:, ATTN_WIDTH:]
        s_rest = cols(1536, 1792)
        zs[1, sl, :] = s_rest[:, :LANES]
        zs[2, sl, :] = s_rest[:, LANES:]

    u_pool = buf[POOL_HALO:, :]
    pos = tile_in_seq * tm + lax.broadcasted_iota(jnp.int32, (tm, 1), 0)
    yp_ref[...] = _pool_mix(u_pool, buf, lvl_a, lvl_b, pos, wpool_ref, pscale_ref, pgain_ref)
    buf[0:POOL_HALO, :] = buf[tm:, :]

    chunks = us_ref.shape[0]
    for t in range(SSM_CHUNK):
        for v in range(SSM_LANE_BLOCKS):
            us_ref[:, _ssm_piece(t, v)] = zs[v, pl.ds(t, chunks, stride=SSM_CHUNK), :]


def _inproj(x, w_bf16, w_pool_blockdiag_bf16, pool_scale, pool_gain, layer, batch):
    t = x.shape[0]
    tm = INPROJ_TOKENS
    row = lambda width: pl.BlockSpec((tm, width), lambda i: (i, 0))
    full = lambda a: _layer_block(a, layer)
    chunk_rows = pl.BlockSpec((tm // SSM_CHUNK, SSM_CHUNK * SSM_WIDTH), lambda i: (i, 0))
    return pl.pallas_call(
        functools.partial(_inproj_body, tiles_per_seq=t // batch // tm),
        grid=(t // tm,),
        in_specs=[row(D_MODEL), full(w_bf16), full(w_pool_blockdiag_bf16), full(pool_scale), full(pool_gain)],
        out_specs=[row(POOL_WIDTH), row(ATTN_WIDTH), row(ATTN_WIDTH), row(ATTN_WIDTH), chunk_rows],
        out_shape=[jax.ShapeDtypeStruct((t, POOL_WIDTH), BF16),
                   jax.ShapeDtypeStruct((t, ATTN_WIDTH), BF16),
                   jax.ShapeDtypeStruct((t, ATTN_WIDTH), BF16),
                   jax.ShapeDtypeStruct((t, ATTN_WIDTH), BF16),
                   jax.ShapeDtypeStruct((t // SSM_CHUNK, SSM_CHUNK * SSM_WIDTH), F32)],
        scratch_shapes=[pltpu.VMEM((SSM_LANE_BLOCKS, tm, LANES), F32)]
        + [pltpu.VMEM((POOL_HALO + tm, POOL_WIDTH), F32)] * 3,
        compiler_params=_cparams("arbitrary"),
        name="inproj",
    )(x, w_bf16, w_pool_blockdiag_bf16, pool_scale, pool_gain)


def _attn_bias_rows(rel_bias):
    x = np.arange(ATTN_BIAS_ROW)
    x = np.where(x < ATTN_BAND_TOKENS, x, x - ATTN_BIAS_ROW)
    idx = np.clip(N_PREV_CHUNKS * CHUNK - x, -REL_CLIP, REL_CLIP) + REL_CLIP
    return rel_bias.astype(F32)[..., idx] * LOG2_E


def _attn_body(q_ref, k0_ref, k1_ref, k2_ref, v0_ref, v1_ref, v2_ref, rows_ref, g_ref, o_ref, bias_s):
    b = pl.program_id(0)
    i = pl.program_id(1)
    tq = ATTN_Q_TOKENS

    @pl.when((b == 0) & (i == 0))
    def _():
        qc = lax.broadcasted_iota(jnp.int32, (tq, ATTN_BAND_TOKENS), 0) // CHUNK
        kc = lax.broadcasted_iota(jnp.int32, (tq, ATTN_BAND_TOKENS), 1) // CHUNK
        in_band = (kc >= qc) & (kc <= qc + N_PREV_CHUNKS)
        for head in range(ATTN_HEADS):
            full = jnp.broadcast_to(rows_ref[head:head + 1, :], (tq, ATTN_BIAS_ROW))
            shifted = pltpu.roll(full, 0, 1, stride=1, stride_axis=0)
            bias_s[head // 2, (head % 2) * tq:(head % 2 + 1) * tq, :] = jnp.where(
                in_band, shifted[:, :ATTN_BAND_TOKENS], -jnp.inf)

    upper_half = lax.broadcasted_iota(jnp.int32, (1, LANES), 1) >= ATTN_HEAD_DIM

    def heads(masked):
        outs = []
        for pair in range(ATTN_WIDTH // LANES):
            sl = slice(pair * LANES, (pair + 1) * LANES)
            qp = q_ref[:, sl]
            kp = jnp.concatenate([k0_ref[:, sl], k1_ref[:, sl], k2_ref[:, sl]], axis=0)
            vp = jnp.concatenate([v0_ref[:, sl], v1_ref[:, sl], v2_ref[:, sl]], axis=0)
            zero = jnp.zeros_like(qp)
            q2 = jnp.concatenate([jnp.where(upper_half, zero, qp), jnp.where(upper_half, qp, zero)], axis=0)
            s = lax.dot_general(q2, kp, (((1,), (1,)), ((), ())), preferred_element_type=F32)
            s = s + bias_s[pair]
            if masked:
                key = lax.broadcasted_iota(jnp.int32, (1, ATTN_BAND_TOKENS), 1)
                s = jnp.where(((key < tq) & (i < 2)) | ((key < 2 * tq) & (i < 1)), -jnp.inf, s)
            m = jnp.max(s, axis=-1, keepdims=True)
            p = jnp.exp2(s - m)
            l = jnp.sum(p, axis=-1, keepdims=True)
            o = _dot(p.astype(BF16), vp) * (1.0 / l)
            outs.append(jnp.where(upper_half, o[tq:, :], o[:tq, :]))
        ss = sum(jnp.sum(o * o, axis=-1, keepdims=True) for o in outs)
        r = lax.rsqrt(ss / ATTN_WIDTH + NORM_EPS)
        for pair, o in enumerate(outs):
            sl = slice(pair * LANES, (pair + 1) * LANES)
            o_ref[:, sl] = (o * r * g_ref[:, sl]).astype(BF16)

    @pl.when(i < 2)
    def _():
        heads(True)

    @pl.when(i >= 2)
    def _():
        heads(False)


def _attention(q, k, v, bias_rows, gain, layer, batch):
    t = q.shape[0]
    tq = ATTN_Q_TOKENS
    nq = t // batch // tq

    def blk(back):
        return pl.BlockSpec((tq, ATTN_WIDTH), lambda b, i: (b * nq + jnp.maximum(i - back, 0), 0))

    return pl.pallas_call(
        _attn_body,
        grid=(batch, nq),
        in_specs=[blk(0), blk(2), blk(1), blk(0), blk(2), blk(1), blk(0),
                  _layer_block(bias_rows, layer), _layer_block(gain, layer)],
        out_specs=blk(0),
        out_shape=jax.ShapeDtypeStruct((t, ATTN_WIDTH), BF16),
        scratch_shapes=[pltpu.VMEM((ATTN_HEADS // 2, 2 * tq, ATTN_BAND_TOKENS), F32)],
        compiler_params=_cparams("arbitrary", "arbitrary"),
        name="attention",
    )(q, k, k, k, v, v, v, bias_rows, gain)


def _s5_position_of_time(g, time):
    return SSM_SLOTS * (time // SSM_SLOTS) + (time % SSM_SLOTS + g) % SSM_SLOTS


def _s5_tables(a_re, a_im, log_dt, b_re, b_im, c_re, c_im):
    hi = lax.Precision.HIGHEST
    tc = SSM_CHUNK
    g, p_dim = a_re.shape
    dt = jnp.exp(log_dt.astype(F32))[:, None]
    ar = a_re.astype(F32)
    ai = a_im.astype(F32)
    mag = jnp.exp(ar * dt)
    abar_re = mag * jnp.cos(ai * dt)
    abar_im = mag * jnp.sin(ai * dt)
    den = ar * ar + ai * ai
    nr = abar_re - 1.0
    ni = abar_im
    coef_re = ((nr * ar + ni * ai) / den)[..., None]
    coef_im = ((ni * ar - nr * ai) / den)[..., None]
    br = b_re.astype(F32)
    bi = b_im.astype(F32)
    bbar_re = coef_re * br - coef_im * bi
    bbar_im = coef_re * bi + coef_im * br
    n = jnp.arange(tc + 1, dtype=F32)[:, None, None]
    pmag = jnp.exp(n * (ar * dt))
    pw_re = pmag * jnp.cos(n * (ai * dt))
    pw_im = pmag * jnp.sin(n * (ai * dt))
    x_re = pw_re[..., None] * bbar_re[None] - pw_im[..., None] * bbar_im[None]
    x_im = pw_re[..., None] * bbar_im[None] + pw_im[..., None] * bbar_re[None]
    cr = c_re.astype(F32)
    ci = c_im.astype(F32)
    cw = tc * SSM_GROUP_DIM
    kern = (jnp.einsum('ghp,ngpk->gknh', cr, x_re[:tc], precision=hi)
            - jnp.einsum('ghp,ngpk->gknh', ci, x_im[:tc], precision=hi)).reshape(g, SSM_GROUP_DIM, cw)
    est_re = x_re[:tc].transpose(1, 0, 3, 2).reshape(g, cw, p_dim)
    est_im = x_im[:tc].transpose(1, 0, 3, 2).reshape(g, cw, p_dim)
    crt = cr.transpose(0, 2, 1)[:, :, None, :]
    cit = ci.transpose(0, 2, 1)[:, :, None, :]
    nx_re = pw_re[1:].transpose(1, 2, 0)[..., None]
    nx_im = pw_im[1:].transpose(1, 2, 0)[..., None]
    int_re = (crt * nx_re - cit * nx_im).reshape(g, p_dim, cw)
    int_im = (-(crt * nx_im + cit * nx_re)).reshape(g, p_dim, cw)

    odd = (jnp.arange(g) % 2 == 1)[:, None, None]
    zero_e = jnp.zeros_like(est_re)
    lane_pad = lambda m: jnp.where(odd, jnp.concatenate([zero_e, m], -1), jnp.concatenate([m, zero_e], -1))
    zero_i = jnp.zeros_like(int_re)
    row_pad = lambda m: jnp.where(odd, jnp.concatenate([zero_i, m], 1), jnp.concatenate([m, zero_i], 1))
    return dict(
        kern=kern,
        est=jnp.concatenate([lane_pad(est_re), lane_pad(est_im)], -1).astype(BF16),
        inter=jnp.concatenate([row_pad(int_re), row_pad(int_im)], 1),
        apow_re=pw_re[tc].reshape(1, g * p_dim), apow_im=pw_im[tc].reshape(1, g * p_dim))


def _s5_prepare(kern_ref, est_ref, int_ref, toep_s, est_s, int_s):
    lane = lax.broadcasted_iota(jnp.int32, (1, LANES), 1)
    zero = jnp.zeros((SSM_GROUP_DIM, LANES), F32)
    for g in range(SSM_GROUPS):
        turn = SSM_GROUP_DIM * (g % SSM_SLOTS)
        spin = lambda a: pltpu.roll(a, turn, 1) if turn else a
        k0 = kern_ref[g, :, 0:LANES]
        k1 = kern_ref[g, :, LANES:2 * LANES]
        for time in range(SSM_CHUNK):
            shift = SSM_GROUP_DIM * (time % SSM_SLOTS)
            r0 = pltpu.roll(k0, shift, 1) if shift else k0
            if time < SSM_SLOTS:
                r1 = pltpu.roll(k1, shift, 1) if shift else k1
                h0 = jnp.where(lane >= shift, r0, zero)
                h1 = jnp.where(lane >= shift, r1, r0)
            else:
                h0 = zero
                h1 = jnp.where(lane >= shift, r0, zero)
            rows = pl.ds(SSM_GROUP_DIM * _s5_position_of_time(g, time), SSM_GROUP_DIM)
            toep_s[g, rows, 0:LANES] = spin(h0).astype(BF16)
            toep_s[g, rows, LANES:2 * LANES] = spin(h1).astype(BF16)
            est_s[g, rows, :] = est_ref[g, pl.ds(SSM_GROUP_DIM * (SSM_CHUNK - 1 - time), SSM_GROUP_DIM), :]
        for m in range(SSM_TIME_BLOCKS):
            int_s[g, :, m * LANES:(m + 1) * LANES] = spin(int_ref[g, :, m * LANES:(m + 1) * LANES]).astype(BF16)


def _s5_body(u_ref, kern_ref, estin_ref, intin_ref, apre_ref, apim_ref, d_ref, y_ref,
             toep_ref, est_ref, int_ref, ub_s, ere_s, eim_s, spre_s, spim_s, sre_s, sim_s):
    rows = u_ref.shape[0]

    @pl.when((pl.program_id(0) == 0) & (pl.program_id(1) == 0))
    def _():
        _s5_prepare(kern_ref, estin_ref, intin_ref, toep_ref, est_ref, int_ref)

    @pl.when(pl.program_id(1) == 0)
    def _():
        sre_s[...] = jnp.zeros_like(sre_s)
        sim_s[...] = jnp.zeros_like(sim_s)

    slot = lax.broadcasted_iota(jnp.int32, (1, LANES), 1) // SSM_GROUP_DIM
    piece = _ssm_piece

    for v in range(SSM_LANE_BLOCKS):
        for m in range(SSM_TIME_BLOCKS):
            rot = []
            for j in range(SSM_SLOTS):
                a = u_ref[:, piece(SSM_SLOTS * m + j, v)]
                rot.append(a if j == 0 else pltpu.roll(a, SSM_GROUP_DIM * j, 1))
            for gam in range(SSM_SLOTS):
                o = rot[0]
                for j in range(1, SSM_SLOTS):
                    o = jnp.where(slot == (j + gam) % SSM_SLOTS, rot[j], o)
                ub_s[SSM_SLOTS * v + gam, :, m * LANES:(m + 1) * LANES] = o.astype(BF16)

    for q in range(SSM_GROUPS // 2):
        e = _dot(ub_s[2 * q], est_ref[2 * q]) + _dot(ub_s[2 * q + 1], est_ref[2 * q + 1])
        ere_s[:, q * LANES:(q + 1) * LANES] = e[:, :SSM_PAIR_WIDTH]
        eim_s[:, q * LANES:(q + 1) * LANES] = e[:, SSM_PAIR_WIDTH:]

    a_re = apre_ref[...]
    a_im = apim_ref[...]

    def carry_step(r, carry):
        s_re, s_im = carry
        spre_s[pl.ds(r, 1), :] = s_re
        spim_s[pl.ds(r, 1), :] = s_im
        e_re = ere_s[pl.ds(r, 1), :]
        e_im = eim_s[pl.ds(r, 1), :]
        return (a_re * s_re - a_im * s_im + e_re, a_re * s_im + a_im * s_re + e_im)

    s_re, s_im = lax.fori_loop(0, rows, carry_step, (sre_s[...], sim_s[...]))
    sre_s[...] = s_re
    sim_s[...] = s_im

    for v in range(SSM_LANE_BLOCKS):
        yg = []
        for gam in range(SSM_SLOTS):
            g = SSM_SLOTS * v + gam
            q = g // 2
            sp = jnp.concatenate([spre_s[:, q * LANES:(q + 1) * LANES], spim_s[:, q * LANES:(q + 1) * LANES]],
                                 axis=1).astype(BF16)
            yg.append(_dot(ub_s[g], toep_ref[g]) + _dot(sp, int_ref[g]))
        d = d_ref[:, v * LANES:(v + 1) * LANES]
        for m in range(SSM_TIME_BLOCKS):
            for j in range(SSM_SLOTS):
                o = yg[0][:, m * LANES:(m + 1) * LANES]
                for gam in range(1, SSM_SLOTS):
                    o = jnp.where(slot == (j + gam) % SSM_SLOTS, yg[gam][:, m * LANES:(m + 1) * LANES], o)
                if j:
                    o = pltpu.roll(o, LANES - SSM_GROUP_DIM * j, 1)
                sl = piece(SSM_SLOTS * m + j, v)
                y_ref[:, sl] = jax.nn.gelu(o + d * u_ref[:, sl])


def _s5(u_rows, tab, d_skip, layer, batch):
    nch, width = u_rows.shape
    rows = SSM_ROWS
    steps = nch // batch // rows
    once = lambda a: _layer_block(a, layer, pipeline_mode=pl.Buffered(1))
    blk = pl.BlockSpec((rows, width), lambda b, i: (b * steps + i, 0))
    table = pltpu.VMEM((SSM_GROUPS, SSM_CHUNK_WIDTH, SSM_CHUNK_WIDTH), BF16)
    state = pltpu.VMEM((rows, SSM_STATE_LANES), F32)
    carry = pltpu.VMEM((1, SSM_STATE_LANES), F32)
    return pl.pallas_call(
        _s5_body,
        grid=(batch, steps),
        in_specs=[blk, once(tab['kern']), once(tab['est']), once(tab['inter']),
                  once(tab['apow_re']), once(tab['apow_im']), once(d_skip)],
        out_specs=blk,
        out_shape=jax.ShapeDtypeStruct((nch, width), F32),
        scratch_shapes=[table, table, table,
                        pltpu.VMEM((SSM_GROUPS, rows, SSM_CHUNK_WIDTH), BF16), state, state, state, state,
                        carry, carry],
        compiler_params=_cparams("arbitrary", "arbitrary"),
        name="s5",
    )(u_rows, tab['kern'], tab['est'], tab['inter'], tab['apow_re'], tab['apow_im'], d_skip)


def _route_rows(scores, biased):
    ng = N_EXPERTS // EXPERTS_PER_GROUP
    group_score = []
    for gi in range(ng):
        a, b, c, d = biased[gi * EXPERTS_PER_GROUP:(gi + 1) * EXPERTS_PER_GROUP]
        hi1, lo1 = jnp.maximum(a, b), jnp.minimum(a, b)
        hi2, lo2 = jnp.maximum(c, d), jnp.minimum(c, d)
        top1 = jnp.maximum(hi1, hi2)
        top2 = jnp.maximum(jnp.minimum(hi1, hi2), jnp.maximum(lo1, lo2))
        group_score.append(top1 + top2)
    best = group_score[0]
    best_idx = jnp.zeros_like(best, dtype=jnp.int32)
    for gi in range(1, ng):
        better = group_score[gi] > best
        best = jnp.where(better, group_score[gi], best)
        best_idx = jnp.where(better, gi, best_idx)
    picked = []
    for e in range(N_EXPERTS):
        gi = e // EXPERTS_PER_GROUP
        rank = jnp.zeros_like(best_idx)
        for o in range(gi * EXPERTS_PER_GROUP, (gi + 1) * EXPERTS_PER_GROUP):
            if o == e:
                continue
            ahead = (biased[o] > biased[e]) | ((biased[o] == biased[e]) & (o < e))
            rank = rank + ahead.astype(jnp.int32)
        picked.append((best_idx == gi) & (rank < 2))
    wsum = sum(jnp.where(picked[e], scores[e], 0.0) for e in range(N_EXPERTS))
    return [jnp.where(picked[e], scores[e] / wsum, 0.0) for e in range(N_EXPERTS)], best_idx


def _group_sort_positions(best_idx, before_ref):
    ng = N_EXPERTS // EXPERTS_PER_GROUP
    tokens = best_idx.shape[1]
    member = [(best_idx == gi).astype(F32) for gi in range(ng)]
    stacked = jnp.concatenate(member + [jnp.zeros((8 - ng, tokens), F32)], axis=0)
    parts = []
    run = jnp.zeros((8, 1), F32)
    for blk in range(tokens // LANES):
        piece = stacked[:, blk * LANES:(blk + 1) * LANES]
        parts.append(_dot(piece.astype(BF16), before_ref[...]) + run)
        run = run + jnp.sum(piece, axis=1, keepdims=True)
    earlier = jnp.concatenate(parts, axis=1)
    counts = [run[gi:gi + 1, :] for gi in range(ng)]
    pos = jnp.zeros_like(member[0])
    start = jnp.zeros_like(counts[0])
    for gi in range(ng):
        pos = pos + member[gi] * (start + earlier[gi:gi + 1, :])
        start = start + counts[gi]
    return pos, counts


def _outproj_body(x_ref, yp_ref, ya_ref, ys_ref, wglu_ref, bglu_ref, gssm_ref, wout_ref, g_ref, b_ref,
                  wr_ref, rb_ref, before_ref, h_ref, comb_ref, pos_ref, cnt_ref, ys_s):
    chunks = ys_ref.shape[0]
    for t in range(SSM_CHUNK):
        for v in range(SSM_LANE_BLOCKS):
            ys_s[v, pl.ds(t, chunks, stride=SSM_CHUNK), :] = ys_ref[:, _ssm_piece(t, v)]
    def rows_block(sl):
        ys = jnp.concatenate([ys_s[v, sl, :] for v in range(SSM_LANE_BLOCKS)], axis=1)
        gate = jax.nn.sigmoid(_dot(ys.astype(BF16), wglu_ref[...]) + bglu_ref[...])
        ys = ys * gate
        r = lax.rsqrt(jnp.mean(ys * ys, axis=-1, keepdims=True) + NORM_EPS)
        ysn = (ys * r * gssm_ref[...]).astype(BF16)
        mix = _dot(jnp.concatenate([yp_ref[sl, :], ya_ref[sl, :], ysn], axis=1), wout_ref[...])
        h = _layer_norm(DN_ALPHA * x_ref[sl, :] + mix, g_ref[...], b_ref[...])
        h_ref[sl, :] = h
        h_hi = h.astype(BF16)
        h_lo = (h - h_hi.astype(F32)).astype(BF16)
        return _dot(h_hi, wr_ref[...]) + _dot(h_lo, wr_ref[...])

    tokens = x_ref.shape[0]
    parts = jnp.concatenate([rows_block(pl.ds(r0, OUT_ROWS)) for r0 in range(0, tokens, OUT_ROWS)], axis=0)
    parts_t = parts.T
    sc = jax.nn.sigmoid(parts_t[:N_EXPERTS, :] + parts_t[N_EXPERTS:2 * N_EXPERTS, :])
    bs = sc + rb_ref[...]
    scores = [sc[e:e + 1, :] for e in range(N_EXPERTS)]
    biased = [bs[e:e + 1, :] for e in range(N_EXPERTS)]
    comb_rows, best_idx = _route_rows(scores, biased)
    pos, counts = _group_sort_positions(best_idx, before_ref)
    comb_t = jnp.concatenate(comb_rows + [pos, jnp.zeros((LANES - N_EXPERTS - 1, tokens), F32)], axis=0)
    comb_ref[...] = comb_t.T
    pos_ref[...] = jnp.concatenate([pos, jnp.zeros((7, tokens), F32)], axis=0)
    cnt_ref[...] = jnp.concatenate(
        [jnp.broadcast_to(c, (1, LANES)) for c in counts]
        + [jnp.zeros((8 - len(counts), LANES), F32)], axis=0).astype(jnp.int32)


def _outproj(x, y_pool, y_attn, y_ssm_rows, w_glu_bf16, b_glu, g_ssm, w_out_bf16, ln_g, ln_b,
             w_router_split, router_bias, layer):
    t = x.shape[0]
    tm = MOE_TOKENS
    nt = t // tm
    row = lambda width: pl.BlockSpec((tm, width), lambda i: (i, 0))
    full = lambda a: pl.BlockSpec(a.shape, lambda i: (0,) * a.ndim)
    per_layer = lambda a: _layer_block(a, layer)
    token = np.arange(LANES)
    before = jnp.asarray(token[:, None] < token[None, :], BF16)
    return pl.pallas_call(
        _outproj_body,
        grid=(nt,),
        in_specs=[row(D_MODEL), row(POOL_WIDTH), row(ATTN_WIDTH),
                  pl.BlockSpec((tm // SSM_CHUNK, SSM_CHUNK * SSM_WIDTH), lambda i: (i, 0)),
                  per_layer(w_glu_bf16), per_layer(b_glu), per_layer(g_ssm), per_layer(w_out_bf16),
                  per_layer(ln_g), per_layer(ln_b), full(w_router_split), full(router_bias), full(before)],
        out_specs=[row(D_MODEL), row(LANES), pl.BlockSpec((8, tm), lambda i: (0, i)),
                   pl.BlockSpec((8, LANES), lambda i: (i, 0))],
        out_shape=[jax.ShapeDtypeStruct((t, D_MODEL), F32), jax.ShapeDtypeStruct((t, LANES), F32),
                   jax.ShapeDtypeStruct((8, t), F32), jax.ShapeDtypeStruct((8 * nt, LANES), jnp.int32)],
        scratch_shapes=[pltpu.VMEM((SSM_LANE_BLOCKS, tm, LANES), F32)],
        compiler_params=_cparams("parallel"),
        name="outproj",
    )(x, y_pool, y_attn, y_ssm_rows, w_glu_bf16, b_glu, g_ssm, w_out_bf16, ln_g, ln_b,
      w_router_split, router_bias, before)


def _moe_body(cnt_ref, h_ref, comb_ref, pos_ref, p_ref, wg_ref, wu_ref, wd_ref, wpg_ref, wpp_ref, g_ref, b_ref,
              o_ref, hs_s, cs_s, acc_s):
    i = pl.program_id(0)
    group = pl.program_id(1)
    ng = pl.num_programs(1)
    tm = h_ref.shape[0]

    @pl.when(group == 0)
    def _():
        hb = h_ref[...].astype(BF16)
        comb = comb_ref[...]
        comb_lo = comb - comb.astype(BF16).astype(F32)
        low_lanes = lax.broadcasted_iota(jnp.int32, (1, LANES), 1) < COMB_LO_LANE
        comb_b = jnp.where(low_lanes, comb, pltpu.roll(comb_lo, COMB_LO_LANE, 1)).astype(BF16)
        for r0 in range(0, tm, MOE_SIDE_ROWS):
            sl = pl.ds(r0, MOE_SIDE_ROWS)
            row = r0 + lax.broadcasted_iota(jnp.int32, (MOE_SIDE_ROWS, tm), 0)
            perm = jnp.where(pos_ref[0:1, :] == row.astype(F32), 1.0, 0.0).astype(BF16)
            hs_s[sl, :] = _dot(perm, hb).astype(BF16)
            both = _dot(perm, comb_b)
            cs_s[sl, :] = both + pltpu.roll(both, LANES - COMB_LO_LANE, 1)
        acc_s[...] = jnp.zeros_like(acc_s)

    count = cnt_ref[i * ng + group]
    start = jnp.int32(0)
    for gi in range(N_EXPERTS // EXPERTS_PER_GROUP - 1):
        start = start + jnp.where(group > gi, cnt_ref[i * ng + gi], 0)
    lane = lax.broadcasted_iota(jnp.int32, (1, LANES), 1)
    first = (start // BF16_ROW_PACK) * BF16_ROW_PACK
    windows = (start - first + count + MOE_WINDOW_ROWS - 1) // MOE_WINDOW_ROWS

    def window(w, carry):
        wanted = first + w * MOE_WINDOW_ROWS
        lo = pl.multiple_of(jnp.minimum(wanted, tm - MOE_WINDOW_ROWS), BF16_ROW_PACK)
        rows = pl.ds(lo, MOE_WINDOW_ROWS)
        x = hs_s[rows, :]
        fresh = lo + lax.broadcasted_iota(jnp.int32, (MOE_WINDOW_ROWS, 1), 0) >= wanted
        cs = jnp.where(fresh, cs_s[rows, :], 0.0)
        total = None
        for e in range(EXPERTS_PER_GROUP):
            gate = _dot(x, wg_ref[e])
            up = _dot(x, wu_ref[e])
            c = jnp.sum(jnp.where(lane == group * EXPERTS_PER_GROUP + e, cs, 0.0), axis=1, keepdims=True)
            a = (jax.nn.silu(gate) * up * c).astype(BF16)
            d = _dot(a, wd_ref[e])
            total = d if total is None else total + d
        acc_s[rows, :] += total
        return carry

    lax.fori_loop(0, windows, window, 0)

    @pl.when(group == ng - 1)
    def _():
        sorted_out = acc_s[...].astype(BF16)
        col = lax.broadcasted_iota(jnp.int32, (MOE_SIDE_ROWS, tm), 1).astype(F32)
        for r0 in range(0, tm, MOE_SIDE_ROWS):
            sl = pl.ds(r0, MOE_SIDE_ROWS)
            unperm = jnp.where(comb_ref[sl, SORT_POS_LANE:SORT_POS_LANE + 1] == col, 1.0, 0.0).astype(BF16)
            ffn = _dot(unperm, sorted_out)
            h = h_ref[sl, :]
            ple = jax.nn.sigmoid(_dot(h.astype(BF16), wpg_ref[...])) * _dot(p_ref[sl, :].astype(BF16), wpp_ref[...])
            o_ref[sl, :] = _layer_norm(DN_ALPHA * h + ffn + ple, g_ref[...], b_ref[...])


def _moe(h, comb, pos_rows, counts, p_all, layer, wg_bf16, wu_bf16, wd_bf16, wpg_bf16, wpp_bf16, ln_g, ln_b):
    t = h.shape[0]
    tm = MOE_TOKENS
    nt = t // tm
    ng = N_EXPERTS // EXPERTS_PER_GROUP
    full = lambda a: _layer_block(a, layer)
    experts = lambda rows, cols: pl.BlockSpec((EXPERTS_PER_GROUP, rows, cols),
                                              lambda i, g, cnt: (layer * ng + g, 0, 0))
    grid_spec = pltpu.PrefetchScalarGridSpec(
        num_scalar_prefetch=1,
        grid=(nt, ng),
        in_specs=[pl.BlockSpec((tm, D_MODEL), lambda i, g, cnt: (i, 0)),
                  pl.BlockSpec((tm, LANES), lambda i, g, cnt: (i, 0)),
                  pl.BlockSpec((8, tm), lambda i, g, cnt: (0, i)),
                  pl.BlockSpec((tm, PLE_DIM), lambda i, g, cnt: (layer * nt + i, 0)),
                  experts(D_MODEL, D_EXPERT), experts(D_MODEL, D_EXPERT), experts(D_EXPERT, D_MODEL),
                  full(wpg_bf16), full(wpp_bf16), full(ln_g), full(ln_b)],
        out_specs=pl.BlockSpec((tm, D_MODEL), lambda i, g, cnt: (i, 0)),
        scratch_shapes=[pltpu.VMEM((tm, D_MODEL), BF16), pltpu.VMEM((tm, LANES), F32),
                        pltpu.VMEM((tm, D_MODEL), F32)])
    return pl.pallas_call(
        _moe_body,
        grid_spec=grid_spec,
        out_shape=jax.ShapeDtypeStruct((t, D_MODEL), F32),
        compiler_params=_cparams("parallel", "arbitrary"),
        name="moe",
    )(counts, h, comb, pos_rows, p_all, wg_bf16, wu_bf16, wd_bf16, wpg_bf16, wpp_bf16, ln_g, ln_b)


def _block_diag(w):
    g, n, m = w.shape
    eye = jnp.eye(g, dtype=w.dtype)
    return (eye[:, None, :, None] * w[:, :, None, :]).reshape(g * n, g * m)


def kernel(x, p, w_in, w_out, w_pool, pool_scale, rel_bias, ssm_a_re, ssm_a_im, ssm_log_dt, ssm_b_re, ssm_b_im,
           ssm_c_re, ssm_c_im, ssm_d, w_glu, b_glu, g_pool, g_attn, g_ssm, ln1_g, ln1_b, ln2_g, ln2_b,
           w_router, router_bias, w_exp_gate, w_exp_up, w_exp_down, w_ple_gate, w_ple_proj):
    batch, seq, d = x.shape
    t = batch * seq
    xt = x.reshape(t, d)
    p_all = p.reshape(DEPTH * t, PLE_DIM)

    vec = lambda a: a.astype(F32).reshape(DEPTH, 1, -1)
    w_in_b = w_in.astype(BF16)
    w_pool_b = jax.vmap(_block_diag)(w_pool).astype(BF16)
    bias_rows = _attn_bias_rows(rel_bias)
    tables = jax.vmap(_s5_tables)(ssm_a_re, ssm_a_im, ssm_log_dt, ssm_b_re, ssm_b_im, ssm_c_re, ssm_c_im)
    w_glu_b = w_glu.astype(BF16)
    w_out_b = w_out.astype(BF16)
    wr = w_router.astype(F32)
    wr_hi = wr.astype(BF16)
    wr_lo = (wr - wr_hi.astype(F32)).astype(BF16)
    wr_split = jnp.pad(jnp.concatenate([wr_hi, wr_lo], axis=1), ((0, 0), (0, LANES - 2 * N_EXPERTS)))
    r_bias = router_bias.astype(F32).reshape(N_EXPERTS, 1)
    stack_experts = lambda w: w.astype(BF16).reshape((DEPTH * N_EXPERTS,) + w.shape[2:])
    wg_b, wu_b, wd_b = stack_experts(w_exp_gate), stack_experts(w_exp_up), stack_experts(w_exp_down)
    wpg_b = w_ple_gate.astype(BF16)
    wpp_b = w_ple_proj.astype(BF16)
    ng = N_EXPERTS // EXPERTS_PER_GROUP

    for layer in range(DEPTH):
        y_pool, q, k, v, u_ssm = _inproj(xt, w_in_b, w_pool_b, vec(pool_scale), vec(g_pool), layer, batch)
        y_attn = _attention(q, k, v, bias_rows, vec(g_attn), layer, batch)
        y_ssm = _s5(u_ssm, tables, vec(ssm_d), layer, batch)
        h, comb, pos_rows, cnt = _outproj(xt, y_pool, y_attn, y_ssm, w_glu_b, vec(b_glu), vec(g_ssm), w_out_b,
                                          vec(ln1_g), vec(ln1_b), wr_split, r_bias, layer)
        counts = cnt[:, 0].reshape(-1, 8)[:, :ng].reshape(-1)
        xt = _moe(h, comb, pos_rows, counts, p_all, layer, wg_b, wu_b, wd_b, wpg_b, wpp_b, vec(ln2_g), vec(ln2_b))
    return xt.reshape(batch, seq, d)
```

```python
import functools
import math

import numpy as np
import jax
import jax.numpy as jnp
from jax import lax
from jax.experimental import pallas as pl
from jax.experimental.pallas import tpu as pltpu

F32 = jnp.float32
BF16 = jnp.bfloat16

D_MODEL = 1024
DEPTH = 2
CHUNK = 64
PLE_DIM = 256
POOL_WIDTH = 256
POOL_GROUP_DIM = 64
POOL_WINDOWS = (2, 4, 8, 16)
POOL_HALO = 32
ATTN_HEAD_DIM = 64
ATTN_HEADS = 6
ATTN_WIDTH = 384
N_PREV_CHUNKS = 8
REL_CLIP = 128
SSM_WIDTH = 384
SSM_GROUP_DIM = 16
SSM_GROUPS = 24
SSM_STATE = 64
N_EXPERTS = 16
EXPERTS_PER_GROUP = 4
D_EXPERT = 256
DN_ALPHA = (2 * DEPTH) ** 0.25
NORM_EPS = 1e-5
LOG2_E = math.log2(math.e)

LANES = 128
VMEM_LIMIT_BYTES = 56 * 1024 * 1024

INPROJ_TOKENS = 1024
INPROJ_ROWS = 512
ATTN_Q_CHUNKS = 4
ATTN_Q_TOKENS = ATTN_Q_CHUNKS * CHUNK
ATTN_BAND_TOKENS = 3 * ATTN_Q_TOKENS
ATTN_BIAS_ROW = 1024
SSM_CHUNK = 16
SSM_CHUNK_WIDTH = SSM_CHUNK * SSM_GROUP_DIM
SSM_ROWS = 128
SSM_SLOTS = LANES // SSM_GROUP_DIM
SSM_LANE_BLOCKS = SSM_WIDTH // LANES
SSM_TIME_BLOCKS = SSM_CHUNK // SSM_SLOTS
SSM_PAIR_WIDTH = 2 * SSM_STATE
SSM_STATE_LANES = SSM_GROUPS * SSM_STATE
OUT_ROWS = 256
MOE_SIDE_ROWS = 256
MOE_TOKENS = 1024
MOE_WINDOW_ROWS = 320
BF16_ROW_PACK = 16
SORT_POS_LANE = N_EXPERTS
COMB_LO_LANE = 32


def _cparams(*sem):
    return pltpu.CompilerParams(dimension_semantics=sem, vmem_limit_bytes=VMEM_LIMIT_BYTES)


def _dot(a, b):
    return jnp.dot(a, b, preferred_element_type=F32)


def _layer_block(a, layer, **kwargs):
    return pl.BlockSpec((None,) + a.shape[1:], lambda *_: (layer,) + (0,) * (a.ndim - 1), **kwargs)


def _layer_norm(v, g, b):
    mu = jnp.mean(v, axis=-1, keepdims=True)
    vc = v - mu
    var = jnp.mean(vc * vc, axis=-1, keepdims=True)
    return vc * lax.rsqrt(var + NORM_EPS) * g + b


def _ssm_piece(t, v):
    lo = SSM_WIDTH * t + LANES * v
    return slice(lo, lo + LANES)


def _pool_mix(x0, buf, lvl_a, lvl_b, pos, w_ref, scale_ref, g_ref):
    n = x0.shape[0] + POOL_HALO
    group = lax.broadcasted_iota(jnp.int32, (1, POOL_WIDTH), 1) // POOL_GROUP_DIM
    mean = jnp.zeros_like(x0)
    src, dst = buf, lvl_a
    for gi, w in enumerate(POOL_WINDOWS):
        lo = 8 * (gi + 1)
        dst[lo:n, :] = src[lo:n, :] + src[lo - w // 2:n - w // 2, :]
        inv_cnt = 1.0 / jnp.minimum(pos + 1, w).astype(F32)
        mean = jnp.where(group == gi, dst[POOL_HALO:n, :] * inv_cnt, mean)
        src, dst = dst, (lvl_b if dst is lvl_a else lvl_a)
    d = (mean - x0).astype(BF16)
    y = _dot(d, w_ref[...]) * scale_ref[...]
    r = lax.rsqrt(jnp.mean(y * y, axis=-1, keepdims=True) + NORM_EPS)
    return (y * r * g_ref[...]).astype(BF16)


def _inproj_body(x_ref, w_ref, wpool_ref, pscale_ref, pgain_ref, yp_ref, q_ref, k_ref, v_ref, us_ref,
                 zs, buf, lvl_a, lvl_b, *, tiles_per_seq):
    tm = x_ref.shape[0]
    tile_in_seq = pl.program_id(0) % tiles_per_seq

    @pl.when(tile_in_seq == 0)
    def _():
        buf[0:POOL_HALO, :] = jnp.zeros((POOL_HALO, POOL_WIDTH), F32)

    for r0 in range(0, tm, INPROJ_ROWS):
        sl = pl.ds(r0, INPROJ_ROWS)
        xb = x_ref[sl, :].astype(BF16)

        def cols(lo, hi):
            return _dot(xb, w_ref[:, lo:hi])

        buf[pl.ds(POOL_HALO + r0, INPROJ_ROWS), :] = cols(0, 256)
        qk = cols(256, 768)
        q_ref[sl, :] = (qk[:, :ATTN_WIDTH] * (ATTN_HEAD_DIM ** -0.5 * LOG2_E)).astype(BF16)
        k_ref[sl, :LANES] = qk[:, ATTN_WIDTH:].astype(BF16)
        k_ref[sl, LANES:] = cols(768, 1024).astype(BF16)
        vs = cols(1024, 1536)
        v_ref[sl, :] = vs[:, :ATTN_WIDTH].astype(BF16)
        zs[0, sl, :] = vs[:, ATTN_WIDTH:]
        s_rest = cols(1536, 1792)
        zs[1, sl, :] = s_rest[:, :LANES]
        zs[2, sl, :] = s_rest[:, LANES:]

    u_pool = buf[POOL_HALO:, :]
    pos = tile_in_seq * tm + lax.broadcasted_iota(jnp.int32, (tm, 1), 0)
    yp_ref[...] = _pool_mix(u_pool, buf, lvl_a, lvl_b, pos, wpool_ref, pscale_ref, pgain_ref)
    buf[0:POOL_HALO, :] = buf[tm:, :]

    chunks = us_ref.shape[0]
    for t in range(SSM_CHUNK):
        for v in range(SSM_LANE_BLOCKS):
            us_ref[:, _ssm_piece(t, v)] = zs[v, pl.ds(t, chunks, stride=SSM_CHUNK), :]


def _inproj(x, w_bf16, w_pool_blockdiag_bf16, pool_scale, pool_gain, layer, batch):
    t = x.shape[0]
    tm = INPROJ_TOKENS
    row = lambda width: pl.BlockSpec((tm, width), lambda i: (i, 0))
    full = lambda a: _layer_block(a, layer)
    chunk_rows = pl.BlockSpec((tm // SSM_CHUNK, SSM_CHUNK * SSM_WIDTH), lambda i: (i, 0))
    return pl.pallas_call(
        functools.partial(_inproj_body, tiles_per_seq=t // batch // tm),
        grid=(t // tm,),
        in_specs=[row(D_MODEL), full(w_bf16), full(w_pool_blockdiag_bf16), full(pool_scale), full(pool_gain)],
        out_specs=[row(POOL_WIDTH), row(ATTN_WIDTH), row(ATTN_WIDTH), row(ATTN_WIDTH), chunk_rows],
        out_shape=[jax.ShapeDtypeStruct((t, POOL_WIDTH), BF16),
                   jax.ShapeDtypeStruct((t, ATTN_WIDTH), BF16),
                   jax.ShapeDtypeStruct((t, ATTN_WIDTH), BF16),
                   jax.ShapeDtypeStruct((t, ATTN_WIDTH), BF16),
                   jax.ShapeDtypeStruct((t // SSM_CHUNK, SSM_CHUNK * SSM_WIDTH), F32)],
        scratch_shapes=[pltpu.VMEM((SSM_LANE_BLOCKS, tm, LANES), F32)]
        + [pltpu.VMEM((POOL_HALO + tm, POOL_WIDTH), F32)] * 3,
        compiler_params=_cparams("arbitrary"),
        name="inproj",
    )(x, w_bf16, w_pool_blockdiag_bf16, pool_scale, pool_gain)


def _attn_bias_rows(rel_bias):
    x = np.arange(ATTN_BIAS_ROW)
    x = np.where(x < ATTN_BAND_TOKENS, x, x - ATTN_BIAS_ROW)
    idx = np.clip(N_PREV_CHUNKS * CHUNK - x, -REL_CLIP, REL_CLIP) + REL_CLIP
    return rel_bias.astype(F32)[..., idx] * LOG2_E


def _attn_body(q_ref, k0_ref, k1_ref, k2_ref, v0_ref, v1_ref, v2_ref, rows_ref, g_ref, o_ref, bias_s):
    b = pl.program_id(0)
    i = pl.program_id(1)
    tq = ATTN_Q_TOKENS

    @pl.when((b == 0) & (i == 0))
    def _():
        qc = lax.broadcasted_iota(jnp.int32, (tq, ATTN_BAND_TOKENS), 0) // CHUNK
        kc = lax.broadcasted_iota(jnp.int32, (tq, ATTN_BAND_TOKENS), 1) // CHUNK
        in_band = (kc >= qc) & (kc <= qc + N_PREV_CHUNKS)
        for head in range(ATTN_HEADS):
            full = jnp.broadcast_to(rows_ref[head:head + 1, :], (tq, ATTN_BIAS_ROW))
            shifted = pltpu.roll(full, 0, 1, stride=1, stride_axis=0)
            bias_s[head // 2, (head % 2) * tq:(head % 2 + 1) * tq, :] = jnp.where(
                in_band, shifted[:, :ATTN_BAND_TOKENS], -jnp.inf)

    upper_half = lax.broadcasted_iota(jnp.int32, (1, LANES), 1) >= ATTN_HEAD_DIM

    def heads(masked):
        outs = []
        for pair in range(ATTN_WIDTH // LANES):
            sl = slice(pair * LANES, (pair + 1) * LANES)
            qp = q_ref[:, sl]
            kp = jnp.concatenate([k0_ref[:, sl], k1_ref[:, sl], k2_ref[:, sl]], axis=0)
            vp = jnp.concatenate([v0_ref[:, sl], v1_ref[:, sl], v2_ref[:, sl]], axis=0)
            zero = jnp.zeros_like(qp)
            q2 = jnp.concatenate([jnp.where(upper_half, zero, qp), jnp.where(upper_half, qp, zero)], axis=0)
            s = lax.dot_general(q2, kp, (((1,), (1,)), ((), ())), preferred_element_type=F32)
            s = s + bias_s[pair]
            if masked:
                key = lax.broadcasted_iota(jnp.int32, (1, ATTN_BAND_TOKENS), 1)
                s = jnp.where(((key < tq) & (i < 2)) | ((key < 2 * tq) & (i < 1)), -jnp.inf, s)
            m = jnp.max(s, axis=-1, keepdims=True)
            p = jnp.exp2(s - m)
            l = jnp.sum(p, axis=-1, keepdims=True)
            o = _dot(p.astype(BF16), vp) * (1.0 / l)
            outs.append(jnp.where(upper_half, o[tq:, :], o[:tq, :]))
        ss = sum(jnp.sum(o * o, axis=-1, keepdims=True) for o in outs)
        r = lax.rsqrt(ss / ATTN_WIDTH + NORM_EPS)
        for pair, o in enumerate(outs):
            sl = slice(pair * LANES, (pair + 1) * LANES)
            o_ref[:, sl] = (o * r * g_ref[:, sl]).astype(BF16)

    @pl.when(i < 2)
    def _():
        heads(True)

    @pl.when(i >= 2)
    def _():
        heads(False)


def _attention(q, k, v, bias_rows, gain, layer, batch):
    t = q.shape[0]
    tq = ATTN_Q_TOKENS
    nq = t // batch // tq

    def blk(back):
        return pl.BlockSpec((tq, ATTN_WIDTH), lambda b, i: (b * nq + jnp.maximum(i - back, 0), 0))

    return pl.pallas_call(
        _attn_body,
        grid=(batch, nq),
        in_specs=[blk(0), blk(2), blk(1), blk(0), blk(2), blk(1), blk(0),
                  _layer_block(bias_rows, layer), _layer_block(gain, layer)],
        out_specs=blk(0),
        out_shape=jax.ShapeDtypeStruct((t, ATTN_WIDTH), BF16),
        scratch_shapes=[pltpu.VMEM((ATTN_HEADS // 2, 2 * tq, ATTN_BAND_TOKENS), F32)],
        compiler_params=_cparams("arbitrary", "arbitrary"),
        name="attention",
    )(q, k, k, k, v, v, v, bias_rows, gain)


def _s5_position_of_time(g, time):
    return SSM_SLOTS * (time // SSM_SLOTS) + (time % SSM_SLOTS + g) % SSM_SLOTS


def _s5_tables(a_re, a_im, log_dt, b_re, b_im, c_re, c_im):
    hi = lax.Precision.HIGHEST
    tc = SSM_CHUNK
    g, p_dim = a_re.shape
    dt = jnp.exp(log_dt.astype(F32))[:, None]
    ar = a_re.astype(F32)
    ai = a_im.astype(F32)
    mag = jnp.exp(ar * dt)
    abar_re = mag * jnp.cos(ai * dt)
    abar_im = mag * jnp.sin(ai * dt)
    den = ar * ar + ai * ai
    nr = abar_re - 1.0
    ni = abar_im
    coef_re = ((nr * ar + ni * ai) / den)[..., None]
    coef_im = ((ni * ar - nr * ai) / den)[..., None]
    br = b_re.astype(F32)
    bi = b_im.astype(F32)
    bbar_re = coef_re * br - coef_im * bi
    bbar_im = coef_re * bi + coef_im * br
    n = jnp.arange(tc + 1, dtype=F32)[:, None, None]
    pmag = jnp.exp(n * (ar * dt))
    pw_re = pmag * jnp.cos(n * (ai * dt))
    pw_im = pmag * jnp.sin(n * (ai * dt))
    x_re = pw_re[..., None] * bbar_re[None] - pw_im[..., None] * bbar_im[None]
    x_im = pw_re[..., None] * bbar_im[None] + pw_im[..., None] * bbar_re[None]
    cr = c_re.astype(F32)
    ci = c_im.astype(F32)
    cw = tc * SSM_GROUP_DIM
    kern = (jnp.einsum('ghp,ngpk->gknh', cr, x_re[:tc], precision=hi)
            - jnp.einsum('ghp,ngpk->gknh', ci, x_im[:tc], precision=hi)).reshape(g, SSM_GROUP_DIM, cw)
    est_re = x_re[:tc].transpose(1, 0, 3, 2).reshape(g, cw, p_dim)
    est_im = x_im[:tc].transpose(1, 0, 3, 2).reshape(g, cw, p_dim)
    crt = cr.transpose(0, 2, 1)[:, :, None, :]
    cit = ci.transpose(0, 2, 1)[:, :, None, :]
    nx_re = pw_re[1:].transpose(1, 2, 0)[..., None]
    nx_im = pw_im[1:].transpose(1, 2, 0)[..., None]
    int_re = (crt * nx_re - cit * nx_im).reshape(g, p_dim, cw)
    int_im = (-(crt * nx_im + cit * nx_re)).reshape(g, p_dim, cw)

    return dict(
        kern=kern,
        est=jnp.concatenate([est_re, est_im], -1),
        inter=jnp.concatenate([int_re, int_im], 1),
        apow_re=pw_re[tc].reshape(1, g * p_dim), apow_im=pw_im[tc].reshape(1, g * p_dim))


def _s5_prepare(kern_ref, est_ref, int_ref, toep_s, est_s, int_s):
    lane = lax.broadcasted_iota(jnp.int32, (1, LANES), 1)
    zero = jnp.zeros((SSM_GROUP_DIM, LANES), F32)
    zero_rows = jnp.zeros((SSM_STATE, SSM_CHUNK_WIDTH), BF16)
    for g in range(SSM_GROUPS):
        mine = (lane >= SSM_STATE) if g % 2 else (lane < SSM_STATE)
        turn = SSM_GROUP_DIM * (g % SSM_SLOTS)
        spin = lambda a: pltpu.roll(a, turn, 1) if turn else a
        k0 = kern_ref[g, :, 0:LANES]
        k1 = kern_ref[g, :, LANES:2 * LANES]
        for time in range(SSM_CHUNK):
            shift = SSM_GROUP_DIM * (time % SSM_SLOTS)
            r0 = pltpu.roll(k0, shift, 1) if shift else k0
            if time < SSM_SLOTS:
                r1 = pltpu.roll(k1, shift, 1) if shift else k1
                h0 = jnp.where(lane >= shift, r0, zero)
                h1 = jnp.where(lane >= shift, r1, r0)
            else:
                h0 = zero
                h1 = jnp.where(lane >= shift, r0, zero)
            rows = pl.ds(SSM_GROUP_DIM * _s5_position_of_time(g, time), SSM_GROUP_DIM)
            toep_s[g, rows, 0:LANES] = spin(h0).astype(BF16)
            toep_s[g, rows, LANES:2 * LANES] = spin(h1).astype(BF16)
            e = est_ref[g, pl.ds(SSM_GROUP_DIM * (SSM_CHUNK - 1 - time), SSM_GROUP_DIM), :]
            e_swapped = pltpu.roll(e, SSM_STATE, 1)
            e_re, e_im = (e_swapped, e) if g % 2 else (e, e_swapped)
            est_s[g, rows, 0:LANES] = jnp.where(mine, e_re, zero).astype(BF16)
            est_s[g, rows, LANES:2 * LANES] = jnp.where(mine, e_im, zero).astype(BF16)
        for part in range(2):
            src = pl.ds(part * SSM_STATE, SSM_STATE)
            base = part * SSM_PAIR_WIDTH
            own = pl.ds(base + (g % 2) * SSM_STATE, SSM_STATE)
            other = pl.ds(base + (1 - g % 2) * SSM_STATE, SSM_STATE)
            int_s[g, other, :] = zero_rows
            for m in range(SSM_TIME_BLOCKS):
                cols = slice(m * LANES, (m + 1) * LANES)
                int_s[g, own, cols] = spin(int_ref[g, src, cols]).astype(BF16)


def _s5_body(u_ref, kern_ref, estin_ref, intin_ref, apre_ref, apim_ref, d_ref, y_ref,
             toep_ref, est_ref, int_ref, ub_s, ere_s, eim_s, spre_s, spim_s, sre_s, sim_s):
    rows = u_ref.shape[0]

    @pl.when((pl.program_id(0) == 0) & (pl.program_id(1) == 0))
    def _():
        _s5_prepare(kern_ref, estin_ref, intin_ref, toep_ref, est_ref, int_ref)

    @pl.when(pl.program_id(1) == 0)
    def _():
        sre_s[...] = jnp.zeros_like(sre_s)
        sim_s[...] = jnp.zeros_like(sim_s)

    slot = lax.broadcasted_iota(jnp.int32, (1, LANES), 1) // SSM_GROUP_DIM
    piece = _ssm_piece

    for v in range(SSM_LANE_BLOCKS):
        for m in range(SSM_TIME_BLOCKS):
            rot = []
            for j in range(SSM_SLOTS):
                a = u_ref[:, piece(SSM_SLOTS * m + j, v)]
                rot.append(a if j == 0 else pltpu.roll(a, SSM_GROUP_DIM * j, 1))
            for gam in range(SSM_SLOTS):
                o = rot[0]
                for j in range(1, SSM_SLOTS):
                    o = jnp.where(slot == (j + gam) % SSM_SLOTS, rot[j], o)
                ub_s[SSM_SLOTS * v + gam, :, m * LANES:(m + 1) * LANES] = o.astype(BF16)

    for q in range(SSM_GROUPS // 2):
        e = _dot(ub_s[2 * q], est_ref[2 * q]) + _dot(ub_s[2 * q + 1], est_ref[2 * q + 1])
        ere_s[:, q * LANES:(q + 1) * LANES] = e[:, :SSM_PAIR_WIDTH]
        eim_s[:, q * LANES:(q + 1) * LANES] = e[:, SSM_PAIR_WIDTH:]

    a_re = apre_ref[...]
    a_im = apim_ref[...]

    def carry_step(r, carry):
        s_re, s_im = carry
        spre_s[pl.ds(r, 1), :] = s_re
        spim_s[pl.ds(r, 1), :] = s_im
        e_re = ere_s[pl.ds(r, 1), :]
        e_im = eim_s[pl.ds(r, 1), :]
        return (a_re * s_re - a_im * s_im + e_re, a_re * s_im + a_im * s_re + e_im)

    s_re, s_im = lax.fori_loop(0, rows, carry_step, (sre_s[...], sim_s[...]))
    sre_s[...] = s_re
    sim_s[...] = s_im

    for v in range(SSM_LANE_BLOCKS):
        yg = []
        for gam in range(SSM_SLOTS):
            g = SSM_SLOTS * v + gam
            q = g // 2
            sp = jnp.concatenate([spre_s[:, q * LANES:(q + 1) * LANES], spim_s[:, q * LANES:(q + 1) * LANES]],
                                 axis=1).astype(BF16)
            yg.append(_dot(ub_s[g], toep_ref[g]) + _dot(sp, int_ref[g]))
        d = d_ref[:, v * LANES:(v + 1) * LANES]
        for m in range(SSM_TIME_BLOCKS):
            for j in range(SSM_SLOTS):
                o = yg[0][:, m * LANES:(m + 1) * LANES]
                for gam in range(1, SSM_SLOTS):
                    o = jnp.where(slot == (j + gam) % SSM_SLOTS, yg[gam][:, m * LANES:(m + 1) * LANES], o)
                if j:
                    o = pltpu.roll(o, LANES - SSM_GROUP_DIM * j, 1)
                sl = piece(SSM_SLOTS * m + j, v)
                y_ref[:, sl] = jax.nn.gelu(o + d * u_ref[:, sl])


def _s5(u_rows, tab, d_skip, layer, batch):
    nch, width = u_rows.shape
    rows = SSM_ROWS
    steps = nch // batch // rows
    once = lambda a: _layer_block(a, layer, pipeline_mode=pl.Buffered(1))
    blk = pl.BlockSpec((rows, width), lambda b, i: (b * steps + i, 0))
    table = pltpu.VMEM((SSM_GROUPS, SSM_CHUNK_WIDTH, SSM_CHUNK_WIDTH), BF16)
    state = pltpu.VMEM((rows, SSM_STATE_LANES), F32)
    carry = pltpu.VMEM((1, SSM_STATE_LANES), F32)
    return pl.pallas_call(
        _s5_body,
        grid=(batch, steps),
        in_specs=[blk, once(tab['kern']), once(tab['est']), once(tab['inter']),
                  once(tab['apow_re']), once(tab['apow_im']), once(d_skip)],
        out_specs=blk,
        out_shape=jax.ShapeDtypeStruct((nch, width), F32),
        scratch_shapes=[table, table, table,
                        pltpu.VMEM((SSM_GROUPS, rows, SSM_CHUNK_WIDTH), BF16), state, state, state, state,
                        carry, carry],
        compiler_params=_cparams("arbitrary", "arbitrary"),
        name="s5",
    )(u_rows, tab['kern'], tab['est'], tab['inter'], tab['apow_re'], tab['apow_im'], d_skip)


def _route_rows(scores, biased):
    ng = N_EXPERTS // EXPERTS_PER_GROUP
    group_score = []
    for gi in range(ng):
        a, b, c, d = biased[gi * EXPERTS_PER_GROUP:(gi + 1) * EXPERTS_PER_GROUP]
        hi1, lo1 = jnp.maximum(a, b), jnp.minimum(a, b)
        hi2, lo2 = jnp.maximum(c, d), jnp.minimum(c, d)
        top1 = jnp.maximum(hi1, hi2)
        top2 = jnp.maximum(jnp.minimum(hi1, hi2), jnp.maximum(lo1, lo2))
        group_score.append(top1 + top2)
    best = group_score[0]
    best_idx = jnp.zeros_like(best, dtype=jnp.int32)
    for gi in range(1, ng):
        better = group_score[gi] > best
        best = jnp.where(better, group_score[gi], best)
        best_idx = jnp.where(better, gi, best_idx)
    picked = []
    for e in range(N_EXPERTS):
        gi = e // EXPERTS_PER_GROUP
        rank = jnp.zeros_like(best_idx)
        for o in range(gi * EXPERTS_PER_GROUP, (gi + 1) * EXPERTS_PER_GROUP):
            if o == e:
                continue
            ahead = (biased[o] > biased[e]) | ((biased[o] == biased[e]) & (o < e))
            rank = rank + ahead.astype(jnp.int32)
        picked.append((best_idx == gi) & (rank < 2))
    wsum = sum(jnp.where(picked[e], scores[e], 0.0) for e in range(N_EXPERTS))
    return [jnp.where(picked[e], scores[e] / wsum, 0.0) for e in range(N_EXPERTS)], best_idx


def _group_sort_positions(best_idx, before_ref):
    ng = N_EXPERTS // EXPERTS_PER_GROUP
    tokens = best_idx.shape[1]
    member = [(best_idx == gi).astype(F32) for gi in range(ng)]
    stacked = jnp.concatenate(member + [jnp.zeros((8 - ng, tokens), F32)], axis=0)
    parts = []
    run = jnp.zeros((8, 1), F32)
    for blk in range(tokens // LANES):
        piece = stacked[:, blk * LANES:(blk + 1) * LANES]
        parts.append(_dot(piece.astype(BF16), before_ref[...]) + run)
        run = run + jnp.sum(piece, axis=1, keepdims=True)
    earlier = jnp.concatenate(parts, axis=1)
    counts = [run[gi:gi + 1, :] for gi in range(ng)]
    pos = jnp.zeros_like(member[0])
    start = jnp.zeros_like(counts[0])
    for gi in range(ng):
        pos = pos + member[gi] * (start + earlier[gi:gi + 1, :])
        start = start + counts[gi]
    return pos, counts


def _outproj_body(x_ref, yp_ref, ya_ref, ys_ref, wglu_ref, bglu_ref, gssm_ref, wout_ref, g_ref, b_ref,
                  wr_ref, rb_ref, before_ref, h_ref, comb_ref, pos_ref, cnt_ref, ys_s):
    chunks = ys_ref.shape[0]
    for t in range(SSM_CHUNK):
        for v in range(SSM_LANE_BLOCKS):
            ys_s[v, pl.ds(t, chunks, stride=SSM_CHUNK), :] = ys_ref[:, _ssm_piece(t, v)]
    def rows_block(sl):
        ys = jnp.concatenate([ys_s[v, sl, :] for v in range(SSM_LANE_BLOCKS)], axis=1)
        gate = jax.nn.sigmoid(_dot(ys.astype(BF16), wglu_ref[...]) + bglu_ref[...])
        ys = ys * gate
        r = lax.rsqrt(jnp.mean(ys * ys, axis=-1, keepdims=True) + NORM_EPS)
        ysn = (ys * r * gssm_ref[...]).astype(BF16)
        mix = _dot(jnp.concatenate([yp_ref[sl, :], ya_ref[sl, :], ysn], axis=1), wout_ref[...])
        h = _layer_norm(DN_ALPHA * x_ref[sl, :] + mix, g_ref[...], b_ref[...])
        h_ref[sl, :] = h
        h_hi = h.astype(BF16)
        h_lo = (h - h_hi.astype(F32)).astype(BF16)
        return _dot(h_hi, wr_ref[...]) + _dot(h_lo, wr_ref[...])

    tokens = x_ref.shape[0]
    parts = jnp.concatenate([rows_block(pl.ds(r0, OUT_ROWS)) for r0 in range(0, tokens, OUT_ROWS)], axis=0)
    parts_t = parts.T
    sc = jax.nn.sigmoid(parts_t[:N_EXPERTS, :] + parts_t[N_EXPERTS:2 * N_EXPERTS, :])
    bs = sc + rb_ref[...]
    scores = [sc[e:e + 1, :] for e in range(N_EXPERTS)]
    biased = [bs[e:e + 1, :] for e in range(N_EXPERTS)]
    comb_rows, best_idx = _route_rows(scores, biased)
    pos, counts = _group_sort_positions(best_idx, before_ref)
    comb_t = jnp.concatenate(comb_rows + [pos, jnp.zeros((LANES - N_EXPERTS - 1, tokens), F32)], axis=0)
    comb_ref[...] = comb_t.T
    pos_ref[...] = jnp.concatenate([pos, jnp.zeros((7, tokens), F32)], axis=0)
    cnt_ref[...] = jnp.concatenate(
        [jnp.broadcast_to(c, (1, LANES)) for c in counts]
        + [jnp.zeros((8 - len(counts), LANES), F32)], axis=0).astype(jnp.int32)


def _outproj(x, y_pool, y_attn, y_ssm_rows, w_glu_bf16, b_glu, g_ssm, w_out_bf16, ln_g, ln_b,
             w_router_split, router_bias, layer):
    t = x.shape[0]
    tm = MOE_TOKENS
    nt = t // tm
    row = lambda width: pl.BlockSpec((tm, width), lambda i: (i, 0))
    full = lambda a: pl.BlockSpec(a.shape, lambda i: (0,) * a.ndim)
    per_layer = lambda a: _layer_block(a, layer)
    token = np.arange(LANES)
    before = jnp.asarray(token[:, None] < token[None, :], BF16)
    return pl.pallas_call(
        _outproj_body,
        grid=(nt,),
        in_specs=[row(D_MODEL), row(POOL_WIDTH), row(ATTN_WIDTH),
                  pl.BlockSpec((tm // SSM_CHUNK, SSM_CHUNK * SSM_WIDTH), lambda i: (i, 0)),
                  per_layer(w_glu_bf16), per_layer(b_glu), per_layer(g_ssm), per_layer(w_out_bf16),
                  per_layer(ln_g), per_layer(ln_b), full(w_router_split), full(router_bias), full(before)],
        out_specs=[row(D_MODEL), row(LANES), pl.BlockSpec((8, tm), lambda i: (0, i)),
                   pl.BlockSpec((8, LANES), lambda i: (i, 0))],
        out_shape=[jax.ShapeDtypeStruct((t, D_MODEL), F32), jax.ShapeDtypeStruct((t, LANES), F32),
                   jax.ShapeDtypeStruct((8, t), F32), jax.ShapeDtypeStruct((8 * nt, LANES), jnp.int32)],
        scratch_shapes=[pltpu.VMEM((SSM_LANE_BLOCKS, tm, LANES), F32)],
        compiler_params=_cparams("parallel"),
        name="outproj",
    )(x, y_pool, y_attn, y_ssm_rows, w_glu_bf16, b_glu, g_ssm, w_out_bf16, ln_g, ln_b,
      w_router_split, router_bias, before)


def _moe_body(cnt_ref, h_ref, comb_ref, pos_ref, p_ref, wg_ref, wu_ref, wd_ref, wpg_ref, wpp_ref, g_ref, b_ref,
              o_ref, hs_s, cs_s, acc_s, ple_s):
    i = pl.program_id(0)
    group = pl.program_id(1)
    ng = pl.num_programs(1)
    tm = h_ref.shape[0]

    @pl.when(group == 0)
    def _():
        hb = h_ref[...].astype(BF16)
        comb = comb_ref[...]
        comb_lo = comb - comb.astype(BF16).astype(F32)
        low_lanes = lax.broadcasted_iota(jnp.int32, (1, LANES), 1) < COMB_LO_LANE
        comb_b = jnp.where(low_lanes, comb, pltpu.roll(comb_lo, COMB_LO_LANE, 1)).astype(BF16)
        for r0 in range(0, tm, MOE_SIDE_ROWS):
            sl = pl.ds(r0, MOE_SIDE_ROWS)
            row = r0 + lax.broadcasted_iota(jnp.int32, (MOE_SIDE_ROWS, tm), 0)
            perm = jnp.where(pos_ref[0:1, :] == row.astype(F32), 1.0, 0.0).astype(BF16)
            hs_s[sl, :] = _dot(perm, hb).astype(BF16)
            both = _dot(perm, comb_b)
            cs_s[sl, :] = both + pltpu.roll(both, LANES - COMB_LO_LANE, 1)
        acc_s[...] = jnp.zeros_like(acc_s)

    count = cnt_ref[i * ng + group]
    start = jnp.int32(0)
    for gi in range(N_EXPERTS // EXPERTS_PER_GROUP - 1):
        start = start + jnp.where(group > gi, cnt_ref[i * ng + gi], 0)
    lane = lax.broadcasted_iota(jnp.int32, (1, LANES), 1)
    first = (start // BF16_ROW_PACK) * BF16_ROW_PACK
    windows = (start - first + count + MOE_WINDOW_ROWS - 1) // MOE_WINDOW_ROWS

    def window(w, carry):
        wanted = first + w * MOE_WINDOW_ROWS
        lo = pl.multiple_of(jnp.minimum(wanted, tm - MOE_WINDOW_ROWS), BF16_ROW_PACK)
        rows = pl.ds(lo, MOE_WINDOW_ROWS)
        x = hs_s[rows, :]
        fresh = lo + lax.broadcasted_iota(jnp.int32, (MOE_WINDOW_ROWS, 1), 0) >= wanted
        cs = jnp.where(fresh, cs_s[rows, :], 0.0)
        total = None
        for e in range(EXPERTS_PER_GROUP):
            gate = _dot(x, wg_ref[e])
            up = _dot(x, wu_ref[e])
            c = jnp.sum(jnp.where(lane == group * EXPERTS_PER_GROUP + e, cs, 0.0), axis=1, keepdims=True)
            a = (jax.nn.silu(gate) * up * c).astype(BF16)
            d = _dot(a, wd_ref[e])
            total = d if total is None else total + d
        acc_s[rows, :] += total
        return carry

    lax.fori_loop(0, windows, window, 0)

    half = tm // 2
    for mid in (1, 2):
        @pl.when(group == mid)
        def _():
            for r0 in range((mid - 1) * half, mid * half, MOE_SIDE_ROWS):
                sl = pl.ds(r0, MOE_SIDE_ROWS)
                hb = h_ref[sl, :].astype(BF16)
                ple_s[sl, :] = (jax.nn.sigmoid(_dot(hb, wpg_ref[...]))
                                * _dot(p_ref[sl, :].astype(BF16), wpp_ref[...]))

    @pl.when(group == ng - 1)
    def _():
        sorted_out = acc_s[...].astype(BF16)
        col = lax.broadcasted_iota(jnp.int32, (MOE_SIDE_ROWS, tm), 1).astype(F32)
        for r0 in range(0, tm, MOE_SIDE_ROWS):
            sl = pl.ds(r0, MOE_SIDE_ROWS)
            unperm = jnp.where(comb_ref[sl, SORT_POS_LANE:SORT_POS_LANE + 1] == col, 1.0, 0.0).astype(BF16)
            ffn = _dot(unperm, sorted_out)
            o_ref[sl, :] = _layer_norm(DN_ALPHA * h_ref[sl, :] + ffn + ple_s[sl, :], g_ref[...], b_ref[...])


def _moe(h, comb, pos_rows, counts, p_all, layer, wg_bf16, wu_bf16, wd_bf16, wpg_bf16, wpp_bf16, ln_g, ln_b):
    t = h.shape[0]
    tm = MOE_TOKENS
    nt = t // tm
    ng = N_EXPERTS // EXPERTS_PER_GROUP
    full = lambda a: _layer_block(a, layer)
    experts = lambda rows, cols: pl.BlockSpec((EXPERTS_PER_GROUP, rows, cols),
                                              lambda i, g, cnt: (layer * ng + g, 0, 0))
    grid_spec = pltpu.PrefetchScalarGridSpec(
        num_scalar_prefetch=1,
        grid=(nt, ng),
        in_specs=[pl.BlockSpec((tm, D_MODEL), lambda i, g, cnt: (i, 0)),
                  pl.BlockSpec((tm, LANES), lambda i, g, cnt: (i, 0)),
                  pl.BlockSpec((8, tm), lambda i, g, cnt: (0, i)),
                  pl.BlockSpec((tm, PLE_DIM), lambda i, g, cnt: (layer * nt + i, 0)),
                  experts(D_MODEL, D_EXPERT), experts(D_MODEL, D_EXPERT), experts(D_EXPERT, D_MODEL),
                  full(wpg_bf16), full(wpp_bf16), full(ln_g), full(ln_b)],
        out_specs=pl.BlockSpec((tm, D_MODEL), lambda i, g, cnt: (i, 0)),
        scratch_shapes=[pltpu.VMEM((tm, D_MODEL), BF16), pltpu.VMEM((tm, LANES), F32),
                        pltpu.VMEM((tm, D_MODEL), F32), pltpu.VMEM((tm, D_MODEL), F32)])
    return pl.pallas_call(
        _moe_body,
        grid_spec=grid_spec,
        out_shape=jax.ShapeDtypeStruct((t, D_MODEL), F32),
        compiler_params=_cparams("parallel", "arbitrary"),
        name="moe",
    )(counts, h, comb, pos_rows, p_all, wg_bf16, wu_bf16, wd_bf16, wpg_bf16, wpp_bf16, ln_g, ln_b)


def _block_diag(w):
    g, n, m = w.shape
    eye = jnp.eye(g, dtype=w.dtype)
    return (eye[:, None, :, None] * w[:, :, None, :]).reshape(g * n, g * m)


def kernel(x, p, w_in, w_out, w_pool, pool_scale, rel_bias, ssm_a_re, ssm_a_im, ssm_log_dt, ssm_b_re, ssm_b_im,
           ssm_c_re, ssm_c_im, ssm_d, w_glu, b_glu, g_pool, g_attn, g_ssm, ln1_g, ln1_b, ln2_g, ln2_b,
           w_router, router_bias, w_exp_gate, w_exp_up, w_exp_down, w_ple_gate, w_ple_proj):
    batch, seq, d = x.shape
    t = batch * seq
    xt = x.reshape(t, d)
    p_all = p.reshape(DEPTH * t, PLE_DIM)

    vec = lambda a: a.astype(F32).reshape(DEPTH, 1, -1)
    w_in_b = w_in.astype(BF16)
    w_pool_b = jax.vmap(_block_diag)(w_pool).astype(BF16)
    bias_rows = _attn_bias_rows(rel_bias)
    tables = jax.vmap(_s5_tables)(ssm_a_re, ssm_a_im, ssm_log_dt, ssm_b_re, ssm_b_im, ssm_c_re, ssm_c_im)
    w_glu_b = w_glu.astype(BF16)
    w_out_b = w_out.astype(BF16)
    wr = w_router.astype(F32)
    wr_hi = wr.astype(BF16)
    wr_lo = (wr - wr_hi.astype(F32)).astype(BF16)
    wr_split = jnp.pad(jnp.concatenate([wr_hi, wr_lo], axis=1), ((0, 0), (0, LANES - 2 * N_EXPERTS)))
    r_bias = router_bias.astype(F32).reshape(N_EXPERTS, 1)
    stack_experts = lambda w: w.astype(BF16).reshape((DEPTH * N_EXPERTS,) + w.shape[2:])
    wg_b, wu_b, wd_b = stack_experts(w_exp_gate), stack_experts(w_exp_up), stack_experts(w_exp_down)
    wpg_b = w_ple_gate.astype(BF16)
    wpp_b = w_ple_proj.astype(BF16)
    ng = N_EXPERTS // EXPERTS_PER_GROUP

    for layer in range(DEPTH):
        y_pool, q, k, v, u_ssm = _inproj(xt, w_in_b, w_pool_b, vec(pool_scale), vec(g_pool), layer, batch)
        y_attn = _attention(q, k, v, bias_rows, vec(g_attn), layer, batch)
        y_ssm = _s5(u_ssm, tables, vec(ssm_d), layer, batch)
        h, comb, pos_rows, cnt = _outproj(xt, y_pool, y_attn, y_ssm, w_glu_b, vec(b_glu), vec(g_ssm), w_out_b,
                                          vec(ln1_g), vec(ln1_b), wr_split, r_bias, layer)
        counts = cnt[:, 0].reshape(-1, 8)[:, :ng].reshape(-1)
        xt = _moe(h, comb, pos_rows, counts, p_all, layer, wg_b, wu_b, wd_b, wpg_b, wpp_b, vec(ln2_g), vec(ln2_b))
    return xt.reshape(batch, seq, d)
```

```python
import functools
import math

import numpy as np
import jax
import jax.numpy as jnp
from jax import lax
from jax.experimental import pallas as pl
from jax.experimental.pallas import tpu as pltpu

F32 = jnp.float32
BF16 = jnp.bfloat16

D_MODEL = 1024
DEPTH = 2
CHUNK = 64
PLE_DIM = 256
POOL_WIDTH = 256
POOL_GROUP_DIM = 64
POOL_WINDOWS = (2, 4, 8, 16)
POOL_HALO = 32
ATTN_HEAD_DIM = 64
ATTN_HEADS = 6
ATTN_WIDTH = 384
N_PREV_CHUNKS = 8
REL_CLIP = 128
SSM_WIDTH = 384
SSM_GROUP_DIM = 16
SSM_GROUPS = 24
SSM_STATE = 64
N_EXPERTS = 16
EXPERTS_PER_GROUP = 4
D_EXPERT = 256
DN_ALPHA = (2 * DEPTH) ** 0.25
NORM_EPS = 1e-5
LOG2_E = math.log2(math.e)

LANES = 128
VMEM_LIMIT_BYTES = 56 * 1024 * 1024

INPROJ_TOKENS = 1024
INPROJ_ROWS = 512
ATTN_Q_CHUNKS = 4
ATTN_Q_TOKENS = ATTN_Q_CHUNKS * CHUNK
ATTN_BAND_TOKENS = 3 * ATTN_Q_TOKENS
ATTN_BIAS_ROW = 1024
SSM_CHUNK = 16
SSM_CHUNK_WIDTH = SSM_CHUNK * SSM_GROUP_DIM
SSM_ROWS = 128
SSM_SLOTS = LANES // SSM_GROUP_DIM
SSM_LANE_BLOCKS = SSM_WIDTH // LANES
SSM_TIME_BLOCKS = SSM_CHUNK // SSM_SLOTS
SSM_PAIR_WIDTH = 2 * SSM_STATE
SSM_STATE_LANES = SSM_GROUPS * SSM_STATE
OUT_ROWS = 256
MOE_SIDE_ROWS = 256
MOE_TOKENS = 1024
MOE_WINDOW_ROWS = 320
BF16_ROW_PACK = 16
SORT_POS_LANE = N_EXPERTS
COMB_LO_LANE = 32


def _cparams(*sem):
    return pltpu.CompilerParams(dimension_semantics=sem, vmem_limit_bytes=VMEM_LIMIT_BYTES)


def _dot(a, b):
    return jnp.dot(a, b, preferred_element_type=F32)


def _layer_block(a, layer, **kwargs):
    return pl.BlockSpec((None,) + a.shape[1:], lambda *_: (layer,) + (0,) * (a.ndim - 1), **kwargs)


def _layer_norm(v, g, b):
    mu = jnp.mean(v, axis=-1, keepdims=True)
    vc = v - mu
    var = jnp.mean(vc * vc, axis=-1, keepdims=True)
    return vc * lax.rsqrt(var + NORM_EPS) * g + b


def _ssm_piece(t, v):
    lo = SSM_WIDTH * t + LANES * v
    return slice(lo, lo + LANES)


def _pool_mix(x0, buf, lvl_a, lvl_b, pos, w_ref, scale_ref, g_ref):
    n = x0.shape[0] + POOL_HALO
    group = lax.broadcasted_iota(jnp.int32, (1, POOL_WIDTH), 1) // POOL_GROUP_DIM
    mean = jnp.zeros_like(x0)
    src, dst = buf, lvl_a
    for gi, w in enumerate(POOL_WINDOWS):
        lo = 8 * (gi + 1)
        dst[lo:n, :] = src[lo:n, :] + src[lo - w // 2:n - w // 2, :]
        inv_cnt = 1.0 / jnp.minimum(pos + 1, w).astype(F32)
        mean = jnp.where(group == gi, dst[POOL_HALO:n, :] * inv_cnt, mean)
        src, dst = dst, (lvl_b if dst is lvl_a else lvl_a)
    d = (mean - x0).astype(BF16)
    y = _dot(d, w_ref[...]) * scale_ref[...]
    r = lax.rsqrt(jnp.mean(y * y, axis=-1, keepdims=True) + NORM_EPS)
    return (y * r * g_ref[...]).astype(BF16)


def _inproj_body(x_ref, wf_ref, wpool_ref, pscale_ref, pgain_ref, yp_ref, q_ref, k_ref, v_ref, us_ref,
                 w_ref, zs, buf, lvl_a, lvl_b, *, tiles_per_seq):
    tm = x_ref.shape[0]
    tile_in_seq = pl.program_id(0) % tiles_per_seq

    @pl.when(pl.program_id(0) == 0)
    def _():
        w_ref[...] = wf_ref[...].astype(BF16)

    @pl.when(tile_in_seq == 0)
    def _():
        buf[0:POOL_HALO, :] = jnp.zeros((POOL_HALO, POOL_WIDTH), F32)

    for r0 in range(0, tm, INPROJ_ROWS):
        sl = pl.ds(r0, INPROJ_ROWS)
        xb = x_ref[sl, :].astype(BF16)

        def cols(lo, hi):
            return _dot(xb, w_ref[:, lo:hi])

        buf[pl.ds(POOL_HALO + r0, INPROJ_ROWS), :] = cols(0, 256)
        qk = cols(256, 768)
        q_ref[sl, :] = (qk[:, :ATTN_WIDTH] * (ATTN_HEAD_DIM ** -0.5 * LOG2_E)).astype(BF16)
        k_ref[sl, :LANES] = qk[:, ATTN_WIDTH:].astype(BF16)
        k_ref[sl, LANES:] = cols(768, 1024).astype(BF16)
        vs = cols(1024, 1536)
        v_ref[sl, :] = vs[:, :ATTN_WIDTH].astype(BF16)
        zs[0, sl, :] = vs[:, ATTN_WIDTH:]
        s_rest = cols(1536, 1792)
        zs[1, sl, :] = s_rest[:, :LANES]
        zs[2, sl, :] = s_rest[:, LANES:]

    u_pool = buf[POOL_HALO:, :]
    pos = tile_in_seq * tm + lax.broadcasted_iota(jnp.int32, (tm, 1), 0)
    yp_ref[...] = _pool_mix(u_pool, buf, lvl_a, lvl_b, pos, wpool_ref, pscale_ref, pgain_ref)
    buf[0:POOL_HALO, :] = buf[tm:, :]

    chunks = us_ref.shape[0]
    for t in range(SSM_CHUNK):
        for v in range(SSM_LANE_BLOCKS):
            us_ref[:, _ssm_piece(t, v)] = zs[v, pl.ds(t, chunks, stride=SSM_CHUNK), :]


def _inproj(x, w_in, w_pool_blockdiag_bf16, pool_scale, pool_gain, layer, batch):
    t = x.shape[0]
    tm = INPROJ_TOKENS
    row = lambda width: pl.BlockSpec((tm, width), lambda i: (i, 0))
    full = lambda a: _layer_block(a, layer)
    chunk_rows = pl.BlockSpec((tm // SSM_CHUNK, SSM_CHUNK * SSM_WIDTH), lambda i: (i, 0))
    return pl.pallas_call(
        functools.partial(_inproj_body, tiles_per_seq=t // batch // tm),
        grid=(t // tm,),
        in_specs=[row(D_MODEL), _layer_block(w_in, layer, pipeline_mode=pl.Buffered(1)),
                  full(w_pool_blockdiag_bf16), full(pool_scale), full(pool_gain)],
        out_specs=[row(POOL_WIDTH), row(ATTN_WIDTH), row(ATTN_WIDTH), row(ATTN_WIDTH), chunk_rows],
        out_shape=[jax.ShapeDtypeStruct((t, POOL_WIDTH), BF16),
                   jax.ShapeDtypeStruct((t, ATTN_WIDTH), BF16),
                   jax.ShapeDtypeStruct((t, ATTN_WIDTH), BF16),
                   jax.ShapeDtypeStruct((t, ATTN_WIDTH), BF16),
                   jax.ShapeDtypeStruct((t // SSM_CHUNK, SSM_CHUNK * SSM_WIDTH), F32)],
        scratch_shapes=[pltpu.VMEM(w_in.shape[1:], BF16), pltpu.VMEM((SSM_LANE_BLOCKS, tm, LANES), F32)]
        + [pltpu.VMEM((POOL_HALO + tm, POOL_WIDTH), F32)] * 3,
        compiler_params=_cparams("arbitrary"),
        name="inproj",
    )(x, w_in, w_pool_blockdiag_bf16, pool_scale, pool_gain)


def _attn_bias_rows(rel_bias):
    x = np.arange(ATTN_BIAS_ROW)
    x = np.where(x < ATTN_BAND_TOKENS, x, x - ATTN_BIAS_ROW)
    idx = np.clip(N_PREV_CHUNKS * CHUNK - x, -REL_CLIP, REL_CLIP) + REL_CLIP
    return rel_bias.astype(F32)[..., idx] * LOG2_E


def _attn_body(q_ref, k0_ref, k1_ref, k2_ref, v0_ref, v1_ref, v2_ref, rows_ref, g_ref, o_ref, bias_s):
    b = pl.program_id(0)
    i = pl.program_id(1)
    tq = ATTN_Q_TOKENS

    @pl.when((b == 0) & (i == 0))
    def _():
        qc = lax.broadcasted_iota(jnp.int32, (tq, ATTN_BAND_TOKENS), 0) // CHUNK
        kc = lax.broadcasted_iota(jnp.int32, (tq, ATTN_BAND_TOKENS), 1) // CHUNK
        in_band = (kc >= qc) & (kc <= qc + N_PREV_CHUNKS)
        for head in range(ATTN_HEADS):
            full = jnp.broadcast_to(rows_ref[head:head + 1, :], (tq, ATTN_BIAS_ROW))
            shifted = pltpu.roll(full, 0, 1, stride=1, stride_axis=0)
            bias_s[head // 2, (head % 2) * tq:(head % 2 + 1) * tq, :] = jnp.where(
                in_band, shifted[:, :ATTN_BAND_TOKENS], -jnp.inf)

    upper_half = lax.broadcasted_iota(jnp.int32, (1, LANES), 1) >= ATTN_HEAD_DIM

    def heads(masked):
        outs = []
        for pair in range(ATTN_WIDTH // LANES):
            sl = slice(pair * LANES, (pair + 1) * LANES)
            qp = q_ref[:, sl]
            kp = jnp.concatenate([k0_ref[:, sl], k1_ref[:, sl], k2_ref[:, sl]], axis=0)
            vp = jnp.concatenate([v0_ref[:, sl], v1_ref[:, sl], v2_ref[:, sl]], axis=0)
            zero = jnp.zeros_like(qp)
            q2 = jnp.concatenate([jnp.where(upper_half, zero, qp), jnp.where(upper_half, qp, zero)], axis=0)
            s = lax.dot_general(q2, kp, (((1,), (1,)), ((), ())), preferred_element_type=F32)
            s = s + bias_s[pair]
            if masked:
                key = lax.broadcasted_iota(jnp.int32, (1, ATTN_BAND_TOKENS), 1)
                s = jnp.where(((key < tq) & (i < 2)) | ((key < 2 * tq) & (i < 1)), -jnp.inf, s)
            m = jnp.max(s, axis=-1, keepdims=True)
            p = jnp.exp2(s - m)
            l = jnp.sum(p, axis=-1, keepdims=True)
            o = _dot(p.astype(BF16), vp) * (1.0 / l)
            outs.append(jnp.where(upper_half, o[tq:, :], o[:tq, :]))
        ss = sum(jnp.sum(o * o, axis=-1, keepdims=True) for o in outs)
        r = lax.rsqrt(ss / ATTN_WIDTH + NORM_EPS)
        for pair, o in enumerate(outs):
            sl = slice(pair * LANES, (pair + 1) * LANES)
            o_ref[:, sl] = (o * r * g_ref[:, sl]).astype(BF16)

    @pl.when(i < 2)
    def _():
        heads(True)

    @pl.when(i >= 2)
    def _():
        heads(False)


def _attention(q, k, v, bias_rows, gain, layer, batch):
    t = q.shape[0]
    tq = ATTN_Q_TOKENS
    nq = t // batch // tq

    def blk(back):
        return pl.BlockSpec((tq, ATTN_WIDTH), lambda b, i: (b * nq + jnp.maximum(i - back, 0), 0))

    return pl.pallas_call(
        _attn_body,
        grid=(batch, nq),
        in_specs=[blk(0), blk(2), blk(1), blk(0), blk(2), blk(1), blk(0),
                  _layer_block(bias_rows, layer), _layer_block(gain, layer)],
        out_specs=blk(0),
        out_shape=jax.ShapeDtypeStruct((t, ATTN_WIDTH), BF16),
        scratch_shapes=[pltpu.VMEM((ATTN_HEADS // 2, 2 * tq, ATTN_BAND_TOKENS), F32)],
        compiler_params=_cparams("arbitrary", "arbitrary"),
        name="attention",
    )(q, k, k, k, v, v, v, bias_rows, gain)


def _s5_position_of_time(g, time):
    return SSM_SLOTS * (time // SSM_SLOTS) + (time % SSM_SLOTS + g) % SSM_SLOTS


def _s5_tables(a_re, a_im, log_dt, b_re, b_im, c_re, c_im):
    hi = lax.Precision.HIGHEST
    tc = SSM_CHUNK
    g, p_dim = a_re.shape
    dt = jnp.exp(log_dt.astype(F32))[:, None]
    ar = a_re.astype(F32)
    ai = a_im.astype(F32)
    mag = jnp.exp(ar * dt)
    abar_re = mag * jnp.cos(ai * dt)
    abar_im = mag * jnp.sin(ai * dt)
    den = ar * ar + ai * ai
    nr = abar_re - 1.0
    ni = abar_im
    coef_re = ((nr * ar + ni * ai) / den)[..., None]
    coef_im = ((ni * ar - nr * ai) / den)[..., None]
    br = b_re.astype(F32)
    bi = b_im.astype(F32)
    bbar_re = coef_re * br - coef_im * bi
    bbar_im = coef_re * bi + coef_im * br
    n = jnp.arange(tc + 1, dtype=F32)
    pmag = jnp.exp((ar * dt)[..., None] * n)
    pw_re = pmag * jnp.cos((ai * dt)[..., None] * n)
    pw_im = pmag * jnp.sin((ai * dt)[..., None] * n)
    cw = tc * SSM_GROUP_DIM
    lag_rep = jnp.asarray(np.kron(np.eye(tc), np.ones((1, SSM_GROUP_DIM))), F32)
    ch_rep = jnp.asarray(np.kron(np.ones((1, tc)), np.eye(SSM_GROUP_DIM)), F32)
    expand = lambda a, rep: jnp.einsum('gpn,nx->gpx', a, rep, precision=hi)
    c_re_rep = expand(c_re.astype(F32).transpose(0, 2, 1), ch_rep)
    c_im_rep = expand(c_im.astype(F32).transpose(0, 2, 1), ch_rep)

    def output_coefficients(first_power):
        p_re = expand(pw_re[..., first_power:first_power + tc], lag_rep)
        p_im = expand(pw_im[..., first_power:first_power + tc], lag_rep)
        return c_re_rep * p_re - c_im_rep * p_im, -(c_re_rep * p_im + c_im_rep * p_re)

    on_re, on_im = output_coefficients(0)
    kern = (jnp.einsum('gpk,gpx->gkx', bbar_re, on_re, precision=hi)
            + jnp.einsum('gpk,gpx->gkx', bbar_im, on_im, precision=hi))
    inter = jnp.concatenate(output_coefficients(1), axis=1)
    pt_re = pw_re[..., :tc].transpose(0, 2, 1)[:, :, None, :]
    pt_im = pw_im[..., :tc].transpose(0, 2, 1)[:, :, None, :]
    bt_re = bbar_re.transpose(0, 2, 1)[:, None]
    bt_im = bbar_im.transpose(0, 2, 1)[:, None]
    est_re = (pt_re * bt_re - pt_im * bt_im).reshape(g, cw, p_dim)
    est_im = (pt_re * bt_im + pt_im * bt_re).reshape(g, cw, p_dim)
    return dict(
        kern=kern,
        est=jnp.concatenate([est_re, est_im], -1),
        inter=inter,
        apow_re=pw_re[..., tc].reshape(1, g * p_dim), apow_im=pw_im[..., tc].reshape(1, g * p_dim))


def _s5_prepare(kern_ref, est_ref, int_ref, toep_s, est_s, int_s):
    lane = lax.broadcasted_iota(jnp.int32, (1, LANES), 1)
    zero = jnp.zeros((SSM_GROUP_DIM, LANES), F32)
    zero_rows = jnp.zeros((SSM_STATE, SSM_CHUNK_WIDTH), BF16)
    for g in range(SSM_GROUPS):
        mine = (lane >= SSM_STATE) if g % 2 else (lane < SSM_STATE)
        turn = SSM_GROUP_DIM * (g % SSM_SLOTS)
        spin = lambda a: pltpu.roll(a, turn, 1) if turn else a
        k0 = kern_ref[g, :, 0:LANES]
        k1 = kern_ref[g, :, LANES:2 * LANES]
        for time in range(SSM_CHUNK):
            shift = SSM_GROUP_DIM * (time % SSM_SLOTS)
            r0 = pltpu.roll(k0, shift, 1) if shift else k0
            if time < SSM_SLOTS:
                r1 = pltpu.roll(k1, shift, 1) if shift else k1
                h0 = jnp.where(lane >= shift, r0, zero)
                h1 = jnp.where(lane >= shift, r1, r0)
            else:
                h0 = zero
                h1 = jnp.where(lane >= shift, r0, zero)
            rows = pl.ds(SSM_GROUP_DIM * _s5_position_of_time(g, time), SSM_GROUP_DIM)
            toep_s[g, rows, 0:LANES] = spin(h0).astype(BF16)
            toep_s[g, rows, LANES:2 * LANES] = spin(h1).astype(BF16)
            e = est_ref[g, pl.ds(SSM_GROUP_DIM * (SSM_CHUNK - 1 - time), SSM_GROUP_DIM), :]
            e_swapped = pltpu.roll(e, SSM_STATE, 1)
            e_re, e_im = (e_swapped, e) if g % 2 else (e, e_swapped)
            est_s[g, rows, 0:LANES] = jnp.where(mine, e_re, zero).astype(BF16)
            est_s[g, rows, LANES:2 * LANES] = jnp.where(mine, e_im, zero).astype(BF16)
        for part in range(2):
            src = pl.ds(part * SSM_STATE, SSM_STATE)
            base = part * SSM_PAIR_WIDTH
            own = pl.ds(base + (g % 2) * SSM_STATE, SSM_STATE)
            other = pl.ds(base + (1 - g % 2) * SSM_STATE, SSM_STATE)
            int_s[g, other, :] = zero_rows
            for m in range(SSM_TIME_BLOCKS):
                cols = slice(m * LANES, (m + 1) * LANES)
                int_s[g, own, cols] = spin(int_ref[g, src, cols]).astype(BF16)


def _s5_body(u_ref, kern_ref, estin_ref, intin_ref, apre_ref, apim_ref, d_ref, y_ref,
             toep_ref, est_ref, int_ref, ub_s, ere_s, eim_s, spre_s, spim_s, sre_s, sim_s):
    rows = u_ref.shape[0]

    @pl.when((pl.program_id(0) == 0) & (pl.program_id(1) == 0))
    def _():
        _s5_prepare(kern_ref, estin_ref, intin_ref, toep_ref, est_ref, int_ref)

    @pl.when(pl.program_id(1) == 0)
    def _():
        sre_s[...] = jnp.zeros_like(sre_s)
        sim_s[...] = jnp.zeros_like(sim_s)

    slot = lax.broadcasted_iota(jnp.int32, (1, LANES), 1) // SSM_GROUP_DIM
    piece = _ssm_piece

    for v in range(SSM_LANE_BLOCKS):
        for m in range(SSM_TIME_BLOCKS):
            rot = []
            for j in range(SSM_SLOTS):
                a = u_ref[:, piece(SSM_SLOTS * m + j, v)]
                rot.append(a if j == 0 else pltpu.roll(a, SSM_GROUP_DIM * j, 1))
            for gam in range(SSM_SLOTS):
                o = rot[0]
                for j in range(1, SSM_SLOTS):
                    o = jnp.where(slot == (j + gam) % SSM_SLOTS, rot[j], o)
                ub_s[SSM_SLOTS * v + gam, :, m * LANES:(m + 1) * LANES] = o.astype(BF16)

    for q in range(SSM_GROUPS // 2):
        e = _dot(ub_s[2 * q], est_ref[2 * q]) + _dot(ub_s[2 * q + 1], est_ref[2 * q + 1])
        ere_s[:, q * LANES:(q + 1) * LANES] = e[:, :SSM_PAIR_WIDTH]
        eim_s[:, q * LANES:(q + 1) * LANES] = e[:, SSM_PAIR_WIDTH:]

    a_re = apre_ref[...]
    a_im = apim_ref[...]

    def carry_step(r, carry):
        s_re, s_im = carry
        spre_s[pl.ds(r, 1), :] = s_re
        spim_s[pl.ds(r, 1), :] = s_im
        e_re = ere_s[pl.ds(r, 1), :]
        e_im = eim_s[pl.ds(r, 1), :]
        return (a_re * s_re - a_im * s_im + e_re, a_re * s_im + a_im * s_re + e_im)

    s_re, s_im = lax.fori_loop(0, rows, carry_step, (sre_s[...], sim_s[...]))
    sre_s[...] = s_re
    sim_s[...] = s_im

    for v in range(SSM_LANE_BLOCKS):
        yg = []
        for gam in range(SSM_SLOTS):
            g = SSM_SLOTS * v + gam
            q = g // 2
            sp = jnp.concatenate([spre_s[:, q * LANES:(q + 1) * LANES], spim_s[:, q * LANES:(q + 1) * LANES]],
                                 axis=1).astype(BF16)
            yg.append(_dot(ub_s[g], toep_ref[g]) + _dot(sp, int_ref[g]))
        d = d_ref[:, v * LANES:(v + 1) * LANES]
        for m in range(SSM_TIME_BLOCKS):
            for j in range(SSM_SLOTS):
                o = yg[0][:, m * LANES:(m + 1) * LANES]
                for gam in range(1, SSM_SLOTS):
                    o = jnp.where(slot == (j + gam) % SSM_SLOTS, yg[gam][:, m * LANES:(m + 1) * LANES], o)
                if j:
                    o = pltpu.roll(o, LANES - SSM_GROUP_DIM * j, 1)
                sl = piece(SSM_SLOTS * m + j, v)
                y_ref[:, sl] = jax.nn.gelu(o + d * u_ref[:, sl])


def _s5(u_rows, tab, d_skip, layer, batch):
    nch, width = u_rows.shape
    rows = SSM_ROWS
    steps = nch // batch // rows
    once = lambda a: _layer_block(a, layer, pipeline_mode=pl.Buffered(1))
    blk = pl.BlockSpec((rows, width), lambda b, i: (b * steps + i, 0))
    table = pltpu.VMEM((SSM_GROUPS, SSM_CHUNK_WIDTH, SSM_CHUNK_WIDTH), BF16)
    state = pltpu.VMEM((rows, SSM_STATE_LANES), F32)
    carry = pltpu.VMEM((1, SSM_STATE_LANES), F32)
    return pl.pallas_call(
        _s5_body,
        grid=(batch, steps),
        in_specs=[blk, once(tab['kern']), once(tab['est']), once(tab['inter']),
                  once(tab['apow_re']), once(tab['apow_im']), once(d_skip)],
        out_specs=blk,
        out_shape=jax.ShapeDtypeStruct((nch, width), F32),
        scratch_shapes=[table, table, table,
                        pltpu.VMEM((SSM_GROUPS, rows, SSM_CHUNK_WIDTH), BF16), state, state, state, state,
                        carry, carry],
        compiler_params=_cparams("arbitrary", "arbitrary"),
        name="s5",
    )(u_rows, tab['kern'], tab['est'], tab['inter'], tab['apow_re'], tab['apow_im'], d_skip)


def _route_rows(scores, biased):
    ng = N_EXPERTS // EXPERTS_PER_GROUP
    group_score = []
    for gi in range(ng):
        a, b, c, d = biased[gi * EXPERTS_PER_GROUP:(gi + 1) * EXPERTS_PER_GROUP]
        hi1, lo1 = jnp.maximum(a, b), jnp.minimum(a, b)
        hi2, lo2 = jnp.maximum(c, d), jnp.minimum(c, d)
        top1 = jnp.maximum(hi1, hi2)
        top2 = jnp.maximum(jnp.minimum(hi1, hi2), jnp.maximum(lo1, lo2))
        group_score.append(top1 + top2)
    best = group_score[0]
    best_idx = jnp.zeros_like(best, dtype=jnp.int32)
    for gi in range(1, ng):
        better = group_score[gi] > best
        best = jnp.where(better, group_score[gi], best)
        best_idx = jnp.where(better, gi, best_idx)
    picked = []
    for e in range(N_EXPERTS):
        gi = e // EXPERTS_PER_GROUP
        rank = jnp.zeros_like(best_idx)
        for o in range(gi * EXPERTS_PER_GROUP, (gi + 1) * EXPERTS_PER_GROUP):
            if o == e:
                continue
            ahead = (biased[o] > biased[e]) | ((biased[o] == biased[e]) & (o < e))
            rank = rank + ahead.astype(jnp.int32)
        picked.append((best_idx == gi) & (rank < 2))
    wsum = sum(jnp.where(picked[e], scores[e], 0.0) for e in range(N_EXPERTS))
    return [jnp.where(picked[e], scores[e] / wsum, 0.0) for e in range(N_EXPERTS)], best_idx


def _group_sort_positions(best_idx, before_ref):
    ng = N_EXPERTS // EXPERTS_PER_GROUP
    tokens = best_idx.shape[1]
    member = [(best_idx == gi).astype(F32) for gi in range(ng)]
    stacked = jnp.concatenate(member + [jnp.zeros((8 - ng, tokens), F32)], axis=0)
    parts = []
    run = jnp.zeros((8, 1), F32)
    for blk in range(tokens // LANES):
        piece = stacked[:, blk * LANES:(blk + 1) * LANES]
        parts.append(_dot(piece.astype(BF16), before_ref[...]) + run)
        run = run + jnp.sum(piece, axis=1, keepdims=True)
    earlier = jnp.concatenate(parts, axis=1)
    counts = [run[gi:gi + 1, :] for gi in range(ng)]
    pos = jnp.zeros_like(member[0])
    start = jnp.zeros_like(counts[0])
    for gi in range(ng):
        pos = pos + member[gi] * (start + earlier[gi:gi + 1, :])
        start = start + counts[gi]
    return pos, counts


def _outproj_body(x_ref, yp_ref, ya_ref, ys_ref, wgluf_ref, bglu_ref, gssm_ref, woutf_ref, g_ref, b_ref,
                  wr_ref, rb_ref, before_ref, h_ref, comb_ref, pos_ref, cnt_ref, ys_s, wglu_ref, wout_ref):
    @pl.when(pl.program_id(0) == 0)
    def _():
        wglu_ref[...] = wgluf_ref[...].astype(BF16)
        wout_ref[...] = woutf_ref[...].astype(BF16)

    chunks = ys_ref.shape[0]
    for t in range(SSM_CHUNK):
        for v in range(SSM_LANE_BLOCKS):
            ys_s[v, pl.ds(t, chunks, stride=SSM_CHUNK), :] = ys_ref[:, _ssm_piece(t, v)]
    def rows_block(sl):
        ys = jnp.concatenate([ys_s[v, sl, :] for v in range(SSM_LANE_BLOCKS)], axis=1)
        gate = jax.nn.sigmoid(_dot(ys.astype(BF16), wglu_ref[...]) + bglu_ref[...])
        ys = ys * gate
        r = lax.rsqrt(jnp.mean(ys * ys, axis=-1, keepdims=True) + NORM_EPS)
        ysn = (ys * r * gssm_ref[...]).astype(BF16)
        mix = _dot(jnp.concatenate([yp_ref[sl, :], ya_ref[sl, :], ysn], axis=1), wout_ref[...])
        h = _layer_norm(DN_ALPHA * x_ref[sl, :] + mix, g_ref[...], b_ref[...])
        h_ref[sl, :] = h
        h_hi = h.astype(BF16)
        h_lo = (h - h_hi.astype(F32)).astype(BF16)
        return _dot(h_hi, wr_ref[...]) + _dot(h_lo, wr_ref[...])

    tokens = x_ref.shape[0]
    parts = jnp.concatenate([rows_block(pl.ds(r0, OUT_ROWS)) for r0 in range(0, tokens, OUT_ROWS)], axis=0)
    parts_t = parts.T
    sc = jax.nn.sigmoid(parts_t[:N_EXPERTS, :] + parts_t[N_EXPERTS:2 * N_EXPERTS, :])
    bs = sc + rb_ref[...]
    scores = [sc[e:e + 1, :] for e in range(N_EXPERTS)]
    biased = [bs[e:e + 1, :] for e in range(N_EXPERTS)]
    comb_rows, best_idx = _route_rows(scores, biased)
    pos, counts = _group_sort_positions(best_idx, before_ref)
    comb_t = jnp.concatenate(comb_rows + [pos, jnp.zeros((LANES - N_EXPERTS - 1, tokens), F32)], axis=0)
    comb_ref[...] = comb_t.T
    pos_ref[...] = jnp.concatenate([pos, jnp.zeros((7, tokens), F32)], axis=0)
    cnt_ref[...] = jnp.concatenate(
        [jnp.broadcast_to(c, (1, LANES)) for c in counts]
        + [jnp.zeros((8 - len(counts), LANES), F32)], axis=0).astype(jnp.int32)


def _outproj(x, y_pool, y_attn, y_ssm_rows, w_glu, b_glu, g_ssm, w_out, ln_g, ln_b,
             w_router_split, router_bias, layer):
    t = x.shape[0]
    tm = MOE_TOKENS
    nt = t // tm
    row = lambda width: pl.BlockSpec((tm, width), lambda i: (i, 0))
    full = lambda a: pl.BlockSpec(a.shape, lambda i: (0,) * a.ndim)
    per_layer = lambda a: _layer_block(a, layer)
    once = lambda a: _layer_block(a, layer, pipeline_mode=pl.Buffered(1))
    token = np.arange(LANES)
    before = jnp.asarray(token[:, None] < token[None, :], BF16)
    return pl.pallas_call(
        _outproj_body,
        grid=(nt,),
        in_specs=[row(D_MODEL), row(POOL_WIDTH), row(ATTN_WIDTH),
                  pl.BlockSpec((tm // SSM_CHUNK, SSM_CHUNK * SSM_WIDTH), lambda i: (i, 0)),
                  once(w_glu), per_layer(b_glu), per_layer(g_ssm), once(w_out),
                  per_layer(ln_g), per_layer(ln_b), full(w_router_split), full(router_bias), full(before)],
        out_specs=[row(D_MODEL), row(LANES), pl.BlockSpec((8, tm), lambda i: (0, i)),
                   pl.BlockSpec((8, LANES), lambda i: (i, 0))],
        out_shape=[jax.ShapeDtypeStruct((t, D_MODEL), F32), jax.ShapeDtypeStruct((t, LANES), F32),
                   jax.ShapeDtypeStruct((8, t), F32), jax.ShapeDtypeStruct((8 * nt, LANES), jnp.int32)],
        scratch_shapes=[pltpu.VMEM((SSM_LANE_BLOCKS, tm, LANES), F32),
                        pltpu.VMEM(w_glu.shape[1:], BF16), pltpu.VMEM(w_out.shape[1:], BF16)],
        compiler_params=_cparams("arbitrary"),
        name="outproj",
    )(x, y_pool, y_attn, y_ssm_rows, w_glu, b_glu, g_ssm, w_out, ln_g, ln_b,
      w_router_split, router_bias, before)


def _moe_body(cnt_ref, h_ref, comb_ref, pos_ref, p_ref, wg_ref, wu_ref, wd_ref, wpgf_ref, wppf_ref, g_ref, b_ref,
              o_ref, hs_s, cs_s, acc_s, ple_s, wpg_ref, wpp_ref):
    i = pl.program_id(0)
    group = pl.program_id(1)
    ng = pl.num_programs(1)
    tm = h_ref.shape[0]

    @pl.when((i == 0) & (group == 0))
    def _():
        wpg_ref[...] = wpgf_ref[...].astype(BF16)
        wpp_ref[...] = wppf_ref[...].astype(BF16)

    @pl.when(group == 0)
    def _():
        hb = h_ref[...].astype(BF16)
        comb = comb_ref[...]
        comb_lo = comb - comb.astype(BF16).astype(F32)
        low_lanes = lax.broadcasted_iota(jnp.int32, (1, LANES), 1) < COMB_LO_LANE
        comb_b = jnp.where(low_lanes, comb, pltpu.roll(comb_lo, COMB_LO_LANE, 1)).astype(BF16)
        for r0 in range(0, tm, MOE_SIDE_ROWS):
            sl = pl.ds(r0, MOE_SIDE_ROWS)
            row = r0 + lax.broadcasted_iota(jnp.int32, (MOE_SIDE_ROWS, tm), 0)
            perm = jnp.where(pos_ref[0:1, :] == row.astype(F32), 1.0, 0.0).astype(BF16)
            hs_s[sl, :] = _dot(perm, hb).astype(BF16)
            both = _dot(perm, comb_b)
            cs_s[sl, :] = both + pltpu.roll(both, LANES - COMB_LO_LANE, 1)
        acc_s[...] = jnp.zeros_like(acc_s)

    count = cnt_ref[i * ng + group]
    start = jnp.int32(0)
    for gi in range(N_EXPERTS // EXPERTS_PER_GROUP - 1):
        start = start + jnp.where(group > gi, cnt_ref[i * ng + gi], 0)
    lane = lax.broadcasted_iota(jnp.int32, (1, LANES), 1)
    first = (start // BF16_ROW_PACK) * BF16_ROW_PACK
    windows = (start - first + count + MOE_WINDOW_ROWS - 1) // MOE_WINDOW_ROWS

    def window(w, carry):
        wanted = first + w * MOE_WINDOW_ROWS
        lo = pl.multiple_of(jnp.minimum(wanted, tm - MOE_WINDOW_ROWS), BF16_ROW_PACK)
        rows = pl.ds(lo, MOE_WINDOW_ROWS)
        x = hs_s[rows, :]
        fresh = lo + lax.broadcasted_iota(jnp.int32, (MOE_WINDOW_ROWS, 1), 0) >= wanted
        cs = jnp.where(fresh, cs_s[rows, :], 0.0)
        total = None
        for e in range(EXPERTS_PER_GROUP):
            gate = _dot(x, wg_ref[e])
            up = _dot(x, wu_ref[e])
            c = jnp.sum(jnp.where(lane == group * EXPERTS_PER_GROUP + e, cs, 0.0), axis=1, keepdims=True)
            a = (jax.nn.silu(gate) * up * c).astype(BF16)
            d = _dot(a, wd_ref[e])
            total = d if total is None else total + d
        acc_s[rows, :] += total
        return carry

    lax.fori_loop(0, windows, window, 0)

    half = tm // 2
    for mid in (1, 2):
        @pl.when(group == mid)
        def _():
            for r0 in range((mid - 1) * half, mid * half, MOE_SIDE_ROWS):
                sl = pl.ds(r0, MOE_SIDE_ROWS)
                hb = h_ref[sl, :].astype(BF16)
                ple_s[sl, :] = (jax.nn.sigmoid(_dot(hb, wpg_ref[...]))
                                * _dot(p_ref[sl, :].astype(BF16), wpp_ref[...]))

    @pl.when(group == ng - 1)
    def _():
        sorted_out = acc_s[...].astype(BF16)
        col = lax.broadcasted_iota(jnp.int32, (MOE_SIDE_ROWS, tm), 1).astype(F32)
        for r0 in range(0, tm, MOE_SIDE_ROWS):
            sl = pl.ds(r0, MOE_SIDE_ROWS)
            unperm = jnp.where(comb_ref[sl, SORT_POS_LANE:SORT_POS_LANE + 1] == col, 1.0, 0.0).astype(BF16)
            ffn = _dot(unperm, sorted_out)
            o_ref[sl, :] = _layer_norm(DN_ALPHA * h_ref[sl, :] + ffn + ple_s[sl, :], g_ref[...], b_ref[...])


def _moe(h, comb, pos_rows, counts, p_all, layer, wg_bf16, wu_bf16, wd_bf16, w_ple_gate, w_ple_proj, ln_g, ln_b):
    t = h.shape[0]
    tm = MOE_TOKENS
    nt = t // tm
    ng = N_EXPERTS // EXPERTS_PER_GROUP
    full = lambda a: _layer_block(a, layer)
    once = lambda a: _layer_block(a, layer, pipeline_mode=pl.Buffered(1))
    experts = lambda rows, cols: pl.BlockSpec((EXPERTS_PER_GROUP, rows, cols),
                                              lambda i, g, cnt: (layer * ng + g, 0, 0))
    grid_spec = pltpu.PrefetchScalarGridSpec(
        num_scalar_prefetch=1,
        grid=(nt, ng),
        in_specs=[pl.BlockSpec((tm, D_MODEL), lambda i, g, cnt: (i, 0)),
                  pl.BlockSpec((tm, LANES), lambda i, g, cnt: (i, 0)),
                  pl.BlockSpec((8, tm), lambda i, g, cnt: (0, i)),
                  pl.BlockSpec((tm, PLE_DIM), lambda i, g, cnt: (layer * nt + i, 0)),
                  experts(D_MODEL, D_EXPERT), experts(D_MODEL, D_EXPERT), experts(D_EXPERT, D_MODEL),
                  once(w_ple_gate), once(w_ple_proj), full(ln_g), full(ln_b)],
        out_specs=pl.BlockSpec((tm, D_MODEL), lambda i, g, cnt: (i, 0)),
        scratch_shapes=[pltpu.VMEM((tm, D_MODEL), BF16), pltpu.VMEM((tm, LANES), F32),
                        pltpu.VMEM((tm, D_MODEL), F32), pltpu.VMEM((tm, D_MODEL), F32),
                        pltpu.VMEM(w_ple_gate.shape[1:], BF16), pltpu.VMEM(w_ple_proj.shape[1:], BF16)])
    return pl.pallas_call(
        _moe_body,
        grid_spec=grid_spec,
        out_shape=jax.ShapeDtypeStruct((t, D_MODEL), F32),
        compiler_params=_cparams("arbitrary", "arbitrary"),
        name="moe",
    )(counts, h, comb, pos_rows, p_all, wg_bf16, wu_bf16, wd_bf16, w_ple_gate, w_ple_proj, ln_g, ln_b)


def _block_diag(w):
    g, n, m = w.shape
    eye = jnp.eye(g, dtype=w.dtype)
    return (eye[:, None, :, None] * w[:, :, None, :]).reshape(g * n, g * m)


def kernel(x, p, w_in, w_out, w_pool, pool_scale, rel_bias, ssm_a_re, ssm_a_im, ssm_log_dt, ssm_b_re, ssm_b_im,
           ssm_c_re, ssm_c_im, ssm_d, w_glu, b_glu, g_pool, g_attn, g_ssm, ln1_g, ln1_b, ln2_g, ln2_b,
           w_router, router_bias, w_exp_gate, w_exp_up, w_exp_down, w_ple_gate, w_ple_proj):
    batch, seq, d = x.shape
    t = batch * seq
    xt = x.reshape(t, d)
    p_all = p.reshape(DEPTH * t, PLE_DIM)

    vec = lambda a: a.astype(F32).reshape(DEPTH, 1, -1)
    f32 = lambda a: a.astype(F32)
    w_pool_b = jax.vmap(_block_diag)(w_pool).astype(BF16)
    bias_rows = _attn_bias_rows(rel_bias)
    tables = jax.vmap(_s5_tables)(ssm_a_re, ssm_a_im, ssm_log_dt, ssm_b_re, ssm_b_im, ssm_c_re, ssm_c_im)
    wr = w_router.astype(F32)
    wr_hi = wr.astype(BF16)
    wr_lo = (wr - wr_hi.astype(F32)).astype(BF16)
    wr_split = jnp.pad(jnp.concatenate([wr_hi, wr_lo], axis=1), ((0, 0), (0, LANES - 2 * N_EXPERTS)))
    r_bias = router_bias.astype(F32).reshape(N_EXPERTS, 1)
    stack_experts = lambda w: w.astype(BF16).reshape((DEPTH * N_EXPERTS,) + w.shape[2:])
    wg_b, wu_b, wd_b = stack_experts(w_exp_gate), stack_experts(w_exp_up), stack_experts(w_exp_down)
    ng = N_EXPERTS // EXPERTS_PER_GROUP

    for layer in range(DEPTH):
        y_pool, q, k, v, u_ssm = _inproj(xt, f32(w_in), w_pool_b, vec(pool_scale), vec(g_pool), layer, batch)
        y_attn = _attention(q, k, v, bias_rows, vec(g_attn), layer, batch)
        y_ssm = _s5(u_ssm, tables, vec(ssm_d), layer, batch)
        h, comb, pos_rows, cnt = _outproj(xt, y_pool, y_attn, y_ssm, f32(w_glu), vec(b_glu), vec(g_ssm),
                                          f32(w_out), vec(ln1_g), vec(ln1_b), wr_split, r_bias, layer)
        counts = cnt[:, 0].reshape(-1, 8)[:, :ng].reshape(-1)
        xt = _moe(h, comb, pos_rows, counts, p_all, layer, wg_b, wu_b, wd_b, f32(w_ple_gate), f32(w_ple_proj),
                  vec(ln2_g), vec(ln2_b))
    return xt.reshape(batch, seq, d)
```

```python
import functools
import math

import numpy as np
import jax
import jax.numpy as jnp
from jax import lax
from jax.experimental import pallas as pl
from jax.experimental.pallas import tpu as pltpu

F32 = jnp.float32
BF16 = jnp.bfloat16

D_MODEL = 1024
DEPTH = 2
CHUNK = 64
PLE_DIM = 256
POOL_WIDTH = 256
POOL_GROUP_DIM = 64
POOL_WINDOWS = (2, 4, 8, 16)
POOL_HALO = 32
ATTN_HEAD_DIM = 64
ATTN_HEADS = 6
ATTN_WIDTH = 384
N_PREV_CHUNKS = 8
REL_CLIP = 128
SSM_WIDTH = 384
SSM_GROUP_DIM = 16
SSM_GROUPS = 24
SSM_STATE = 64
N_EXPERTS = 16
EXPERTS_PER_GROUP = 4
D_EXPERT = 256
DN_ALPHA = (2 * DEPTH) ** 0.25
NORM_EPS = 1e-5
LOG2_E = math.log2(math.e)

LANES = 128
VMEM_LIMIT_BYTES = 56 * 1024 * 1024

INPROJ_TOKENS = 1024
INPROJ_ROWS = 512
ATTN_Q_CHUNKS = 4
ATTN_Q_TOKENS = ATTN_Q_CHUNKS * CHUNK
ATTN_BAND_TOKENS = 3 * ATTN_Q_TOKENS
ATTN_BIAS_ROW = 1024
SSM_CHUNK = 16
SSM_CHUNK_WIDTH = SSM_CHUNK * SSM_GROUP_DIM
SSM_ROWS = 128
SSM_SLOTS = LANES // SSM_GROUP_DIM
SSM_LANE_BLOCKS = SSM_WIDTH // LANES
SSM_TIME_BLOCKS = SSM_CHUNK // SSM_SLOTS
SSM_PAIR_WIDTH = 2 * SSM_STATE
SSM_STATE_LANES = SSM_GROUPS * SSM_STATE
OUT_ROWS = 256
MOE_SIDE_ROWS = 256
MOE_TOKENS = 1024
MOE_WINDOW_ROWS = 320
BF16_ROW_PACK = 16
SORT_POS_LANE = N_EXPERTS
COMB_LO_LANE = 32


def _cparams(*sem):
    return pltpu.CompilerParams(dimension_semantics=sem, vmem_limit_bytes=VMEM_LIMIT_BYTES)


def _dot(a, b):
    return jnp.dot(a, b, preferred_element_type=F32)


def _layer_block(a, layer, **kwargs):
    return pl.BlockSpec((None,) + a.shape[1:], lambda *_: (layer,) + (0,) * (a.ndim - 1), **kwargs)


def _layer_norm(v, g, b):
    mu = jnp.mean(v, axis=-1, keepdims=True)
    vc = v - mu
    var = jnp.mean(vc * vc, axis=-1, keepdims=True)
    return vc * lax.rsqrt(var + NORM_EPS) * g + b


def _ssm_piece(t, v):
    lo = SSM_WIDTH * t + LANES * v
    return slice(lo, lo + LANES)


def _pool_mix(x0, buf, lvl_a, lvl_b, pos, w_ref, scale_ref, g_ref):
    n = x0.shape[0] + POOL_HALO
    group = lax.broadcasted_iota(jnp.int32, (1, POOL_WIDTH), 1) // POOL_GROUP_DIM
    mean = jnp.zeros_like(x0)
    src, dst = buf, lvl_a
    for gi, w in enumerate(POOL_WINDOWS):
        lo = 8 * (gi + 1)
        dst[lo:n, :] = src[lo:n, :] + src[lo - w // 2:n - w // 2, :]
        inv_cnt = 1.0 / jnp.minimum(pos + 1, w).astype(F32)
        mean = jnp.where(group == gi, dst[POOL_HALO:n, :] * inv_cnt, mean)
        src, dst = dst, (lvl_b if dst is lvl_a else lvl_a)
    d = (mean - x0).astype(BF16)
    y = _dot(d, w_ref[...]) * scale_ref[...]
    r = lax.rsqrt(jnp.mean(y * y, axis=-1, keepdims=True) + NORM_EPS)
    return (y * r * g_ref[...]).astype(BF16)


def _inproj_body(x_ref, wf_ref, wpool_ref, pscale_ref, pgain_ref, yp_ref, q_ref, k_ref, v_ref, us_ref,
                 w_ref, zs, buf, lvl_a, lvl_b, *, tiles_per_seq):
    tm = x_ref.shape[0]
    tile_in_seq = pl.program_id(0) % tiles_per_seq

    @pl.when(pl.program_id(0) == 0)
    def _():
        w_ref[...] = wf_ref[...].astype(BF16)

    @pl.when(tile_in_seq == 0)
    def _():
        buf[0:POOL_HALO, :] = jnp.zeros((POOL_HALO, POOL_WIDTH), F32)

    for r0 in range(0, tm, INPROJ_ROWS):
        sl = pl.ds(r0, INPROJ_ROWS)
        xb = x_ref[sl, :].astype(BF16)

        def cols(lo, hi):
            return _dot(xb, w_ref[:, lo:hi])

        buf[pl.ds(POOL_HALO + r0, INPROJ_ROWS), :] = cols(0, 256)
        qk = cols(256, 768)
        q_ref[sl, :] = (qk[:, :ATTN_WIDTH] * (ATTN_HEAD_DIM ** -0.5 * LOG2_E)).astype(BF16)
        k_ref[sl, :LANES] = qk[:, ATTN_WIDTH:].astype(BF16)
        k_ref[sl, LANES:] = cols(768, 1024).astype(BF16)
        vs = cols(1024, 1536)
        v_ref[sl, :] = vs[:, :ATTN_WIDTH].astype(BF16)
        zs[0, sl, :] = vs[:, ATTN_WIDTH:]
        s_rest = cols(1536, 1792)
        zs[1, sl, :] = s_rest[:, :LANES]
        zs[2, sl, :] = s_rest[:, LANES:]

    u_pool = buf[POOL_HALO:, :]
    pos = tile_in_seq * tm + lax.broadcasted_iota(jnp.int32, (tm, 1), 0)
    yp_ref[...] = _pool_mix(u_pool, buf, lvl_a, lvl_b, pos, wpool_ref, pscale_ref, pgain_ref)
    buf[0:POOL_HALO, :] = buf[tm:, :]

    chunks = us_ref.shape[0]
    for t in range(SSM_CHUNK):
        for v in range(SSM_LANE_BLOCKS):
            us_ref[:, _ssm_piece(t, v)] = zs[v, pl.ds(t, chunks, stride=SSM_CHUNK), :]


def _inproj(x, w_in, w_pool_blockdiag_bf16, pool_scale, pool_gain, layer, batch):
    t = x.shape[0]
    tm = INPROJ_TOKENS
    row = lambda width: pl.BlockSpec((tm, width), lambda i: (i, 0))
    full = lambda a: _layer_block(a, layer)
    chunk_rows = pl.BlockSpec((tm // SSM_CHUNK, SSM_CHUNK * SSM_WIDTH), lambda i: (i, 0))
    return pl.pallas_call(
        functools.partial(_inproj_body, tiles_per_seq=t // batch // tm),
        grid=(t // tm,),
        in_specs=[row(D_MODEL), _layer_block(w_in, layer, pipeline_mode=pl.Buffered(1)),
                  full(w_pool_blockdiag_bf16), full(pool_scale), full(pool_gain)],
        out_specs=[row(POOL_WIDTH), row(ATTN_WIDTH), row(ATTN_WIDTH), row(ATTN_WIDTH), chunk_rows],
        out_shape=[jax.ShapeDtypeStruct((t, POOL_WIDTH), BF16),
                   jax.ShapeDtypeStruct((t, ATTN_WIDTH), BF16),
                   jax.ShapeDtypeStruct((t, ATTN_WIDTH), BF16),
                   jax.ShapeDtypeStruct((t, ATTN_WIDTH), BF16),
                   jax.ShapeDtypeStruct((t // SSM_CHUNK, SSM_CHUNK * SSM_WIDTH), F32)],
        scratch_shapes=[pltpu.VMEM(w_in.shape[1:], BF16), pltpu.VMEM((SSM_LANE_BLOCKS, tm, LANES), F32)]
        + [pltpu.VMEM((POOL_HALO + tm, POOL_WIDTH), F32)] * 3,
        compiler_params=_cparams("arbitrary"),
        name="inproj",
    )(x, w_in, w_pool_blockdiag_bf16, pool_scale, pool_gain)


def _attn_bias_rows(rel_bias):
    x = np.arange(ATTN_BIAS_ROW)
    x = np.where(x < ATTN_BAND_TOKENS, x, x - ATTN_BIAS_ROW)
    idx = np.clip(N_PREV_CHUNKS * CHUNK - x, -REL_CLIP, REL_CLIP) + REL_CLIP
    return rel_bias.astype(F32)[..., idx] * LOG2_E


def _attn_body(q_ref, k0_ref, k1_ref, k2_ref, v0_ref, v1_ref, v2_ref, rows_ref, g_ref, o_ref, bias_s):
    b = pl.program_id(0)
    i = pl.program_id(1)
    tq = ATTN_Q_TOKENS

    @pl.when((b == 0) & (i == 0))
    def _():
        qc = lax.broadcasted_iota(jnp.int32, (tq, ATTN_BAND_TOKENS), 0) // CHUNK
        kc = lax.broadcasted_iota(jnp.int32, (tq, ATTN_BAND_TOKENS), 1) // CHUNK
        in_band = (kc >= qc) & (kc <= qc + N_PREV_CHUNKS)
        for head in range(ATTN_HEADS):
            full = jnp.broadcast_to(rows_ref[head:head + 1, :], (tq, ATTN_BIAS_ROW))
            shifted = pltpu.roll(full, 0, 1, stride=1, stride_axis=0)
            bias_s[head // 2, (head % 2) * tq:(head % 2 + 1) * tq, :] = jnp.where(
                in_band, shifted[:, :ATTN_BAND_TOKENS], -jnp.inf)

    upper_half = lax.broadcasted_iota(jnp.int32, (1, LANES), 1) >= ATTN_HEAD_DIM

    def heads(masked):
        outs = []
        for pair in range(ATTN_WIDTH // LANES):
            sl = slice(pair * LANES, (pair + 1) * LANES)
            qp = q_ref[:, sl]
            kp = jnp.concatenate([k0_ref[:, sl], k1_ref[:, sl], k2_ref[:, sl]], axis=0)
            vp = jnp.concatenate([v0_ref[:, sl], v1_ref[:, sl], v2_ref[:, sl]], axis=0)
            zero = jnp.zeros_like(qp)
            q2 = jnp.concatenate([jnp.where(upper_half, zero, qp), jnp.where(upper_half, qp, zero)], axis=0)
            s = lax.dot_general(q2, kp, (((1,), (1,)), ((), ())), preferred_element_type=F32)
            s = s + bias_s[pair]
            if masked:
                key = lax.broadcasted_iota(jnp.int32, (1, ATTN_BAND_TOKENS), 1)
                s = jnp.where(((key < tq) & (i < 2)) | ((key < 2 * tq) & (i < 1)), -jnp.inf, s)
            m = jnp.max(s, axis=-1, keepdims=True)
            p = jnp.exp2(s - m)
            l = jnp.sum(p, axis=-1, keepdims=True)
            o = _dot(p.astype(BF16), vp) * (1.0 / l)
            outs.append(jnp.where(upper_half, o[tq:, :], o[:tq, :]))
        ss = sum(jnp.sum(o * o, axis=-1, keepdims=True) for o in outs)
        r = lax.rsqrt(ss / ATTN_WIDTH + NORM_EPS)
        for pair, o in enumerate(outs):
            sl = slice(pair * LANES, (pair + 1) * LANES)
            o_ref[:, sl] = (o * r * g_ref[:, sl]).astype(BF16)

    @pl.when(i < 2)
    def _():
        heads(True)

    @pl.when(i >= 2)
    def _():
        heads(False)


def _attention(q, k, v, bias_rows, gain, layer, batch):
    t = q.shape[0]
    tq = ATTN_Q_TOKENS
    nq = t // batch // tq

    def blk(back):
        return pl.BlockSpec((tq, ATTN_WIDTH), lambda b, i: (b * nq + jnp.maximum(i - back, 0), 0))

    return pl.pallas_call(
        _attn_body,
        grid=(batch, nq),
        in_specs=[blk(0), blk(2), blk(1), blk(0), blk(2), blk(1), blk(0),
                  _layer_block(bias_rows, layer), _layer_block(gain, layer)],
        out_specs=blk(0),
        out_shape=jax.ShapeDtypeStruct((t, ATTN_WIDTH), BF16),
        scratch_shapes=[pltpu.VMEM((ATTN_HEADS // 2, 2 * tq, ATTN_BAND_TOKENS), F32)],
        compiler_params=_cparams("arbitrary", "arbitrary"),
        name="attention",
    )(q, k, k, k, v, v, v, bias_rows, gain)


def _s5_position_of_time(g, time):
    return SSM_SLOTS * (time // SSM_SLOTS) + (time % SSM_SLOTS + g) % SSM_SLOTS


def _s5_tables(a_re, a_im, log_dt, b_re, b_im, c_re, c_im):
    hi = lax.Precision.HIGHEST
    tc = SSM_CHUNK
    g, p_dim = a_re.shape
    dt = jnp.exp(log_dt.astype(F32))[:, None]
    ar = a_re.astype(F32)
    ai = a_im.astype(F32)
    mag = jnp.exp(ar * dt)
    abar_re = mag * jnp.cos(ai * dt)
    abar_im = mag * jnp.sin(ai * dt)
    den = ar * ar + ai * ai
    nr = abar_re - 1.0
    ni = abar_im
    coef_re = ((nr * ar + ni * ai) / den)[..., None]
    coef_im = ((ni * ar - nr * ai) / den)[..., None]
    br = b_re.astype(F32)
    bi = b_im.astype(F32)
    bbar_re = coef_re * br - coef_im * bi
    bbar_im = coef_re * bi + coef_im * br
    n = jnp.arange(tc + 1, dtype=F32)
    pmag = jnp.exp((ar * dt)[..., None] * n)
    pw_re = pmag * jnp.cos((ai * dt)[..., None] * n)
    pw_im = pmag * jnp.sin((ai * dt)[..., None] * n)
    cw = tc * SSM_GROUP_DIM
    lag_rep = jnp.asarray(np.kron(np.eye(tc), np.ones((1, SSM_GROUP_DIM))), F32)
    ch_rep = jnp.asarray(np.kron(np.ones((1, tc)), np.eye(SSM_GROUP_DIM)), F32)
    expand = lambda a, rep: jnp.einsum('gpn,nx->gpx', a, rep, precision=hi)
    c_re_rep = expand(c_re.astype(F32).transpose(0, 2, 1), ch_rep)
    c_im_rep = expand(c_im.astype(F32).transpose(0, 2, 1), ch_rep)

    def output_coefficients(first_power, rep, subscripts):
        p_re = jnp.einsum(subscripts, pw_re[..., first_power:first_power + tc], rep, precision=hi)
        p_im = jnp.einsum(subscripts, pw_im[..., first_power:first_power + tc], rep, precision=hi)
        return c_re_rep * p_re - c_im_rep * p_im, -(c_re_rep * p_im + c_im_rep * p_re)

    on_re, on_im = output_coefficients(0, lag_rep, 'gpn,nx->gpx')
    kern = (jnp.einsum('gpk,gpx->gkx', bbar_re, on_re, precision=hi)
            + jnp.einsum('gpk,gpx->gkx', bbar_im, on_im, precision=hi))
    position = np.arange(tc)
    time_at = (SSM_SLOTS * (position // SSM_SLOTS)
               + (position % SSM_SLOTS - np.arange(g)[:, None]) % SSM_SLOTS)
    slot_rep = np.repeat(time_at[:, None, :] == np.arange(tc)[None, :, None], SSM_GROUP_DIM, axis=2)
    inter = jnp.concatenate(output_coefficients(1, jnp.asarray(slot_rep, F32), 'gpn,gnx->gpx'), axis=1)
    pt_re = pw_re[..., :tc].transpose(0, 2, 1)[:, :, None, :]
    pt_im = pw_im[..., :tc].transpose(0, 2, 1)[:, :, None, :]
    bt_re = bbar_re.transpose(0, 2, 1)[:, None]
    bt_im = bbar_im.transpose(0, 2, 1)[:, None]
    est_re = (pt_re * bt_re - pt_im * bt_im).reshape(g, cw, p_dim)
    est_im = (pt_re * bt_im + pt_im * bt_re).reshape(g, cw, p_dim)
    return dict(
        kern=kern,
        est=jnp.concatenate([est_re, est_im], -1),
        est_swapped=jnp.concatenate([est_im, est_re], -1),
        inter=inter,
        apow_re=pw_re[..., tc].reshape(1, g * p_dim), apow_im=pw_im[..., tc].reshape(1, g * p_dim))


def _s5_prepare(kern_ref, est_ref, estsw_ref, int_ref, toep_s, est_s, int_s):
    lane = lax.broadcasted_iota(jnp.int32, (1, LANES), 1)
    slot = lane // SSM_GROUP_DIM
    zero = jnp.zeros((SSM_GROUP_DIM, LANES), F32)
    zero_rows = jnp.zeros((SSM_STATE, SSM_CHUNK_WIDTH), BF16)
    for g in range(SSM_GROUPS):
        mine = (lane >= SSM_STATE) if g % 2 else (lane < SSM_STATE)
        turn = g % SSM_SLOTS
        time_slot = (slot - turn) % SSM_SLOTS
        k0 = [kern_ref[g, :, 0:LANES]]
        k1 = [kern_ref[g, :, LANES:2 * LANES]]
        for r in range(1, SSM_SLOTS):
            k0.append(pltpu.roll(k0[0], SSM_GROUP_DIM * r, 1))
            k1.append(pltpu.roll(k1[0], SSM_GROUP_DIM * r, 1))
        for time in range(SSM_CHUNK):
            r = (time + turn) % SSM_SLOTS
            later = time_slot >= time % SSM_SLOTS
            if time < SSM_SLOTS:
                h0 = jnp.where(later, k0[r], zero)
                h1 = jnp.where(later, k1[r], k0[r])
            else:
                h0 = zero
                h1 = jnp.where(later, k0[r], zero)
            rows = pl.ds(SSM_GROUP_DIM * _s5_position_of_time(g, time), SSM_GROUP_DIM)
            toep_s[g, rows, 0:LANES] = h0.astype(BF16)
            toep_s[g, rows, LANES:2 * LANES] = h1.astype(BF16)
            src = pl.ds(SSM_GROUP_DIM * (SSM_CHUNK - 1 - time), SSM_GROUP_DIM)
            e, e_swapped = est_ref[g, src, :], estsw_ref[g, src, :]
            e_re, e_im = (e_swapped, e) if g % 2 else (e, e_swapped)
            est_s[g, rows, 0:LANES] = jnp.where(mine, e_re, zero).astype(BF16)
            est_s[g, rows, LANES:2 * LANES] = jnp.where(mine, e_im, zero).astype(BF16)
        for part in range(2):
            src = pl.ds(part * SSM_STATE, SSM_STATE)
            base = part * SSM_PAIR_WIDTH
            int_s[g, pl.ds(base + (1 - g % 2) * SSM_STATE, SSM_STATE), :] = zero_rows
            int_s[g, pl.ds(base + (g % 2) * SSM_STATE, SSM_STATE), :] = int_ref[g, src, :].astype(BF16)


def _s5_body(u_ref, kern_ref, estin_ref, estswin_ref, intin_ref, apre_ref, apim_ref, d_ref, y_ref,
             toep_ref, est_ref, int_ref, ub_s, ere_s, eim_s, spre_s, spim_s, sre_s, sim_s):
    rows = u_ref.shape[0]

    @pl.when((pl.program_id(0) == 0) & (pl.program_id(1) == 0))
    def _():
        _s5_prepare(kern_ref, estin_ref, estswin_ref, intin_ref, toep_ref, est_ref, int_ref)

    @pl.when(pl.program_id(1) == 0)
    def _():
        sre_s[...] = jnp.zeros_like(sre_s)
        sim_s[...] = jnp.zeros_like(sim_s)

    slot = lax.broadcasted_iota(jnp.int32, (1, LANES), 1) // SSM_GROUP_DIM
    slot_bits = [(b, (slot & b) != 0) for b in (1, 2, 4)]
    piece = _ssm_piece

    for v in range(SSM_LANE_BLOCKS):
        for m in range(SSM_TIME_BLOCKS):
            rot = []
            for j in range(SSM_SLOTS):
                a = u_ref[:, piece(SSM_SLOTS * m + j, v)]
                rot.append(a if j == 0 else pltpu.roll(a, SSM_GROUP_DIM * j, 1))
            for bit in slot_bits:
                rot = [jnp.where(bit[1], rot[(i + bit[0]) % SSM_SLOTS], rot[i]) for i in range(SSM_SLOTS)]
            for gam in range(SSM_SLOTS):
                ub_s[SSM_SLOTS * v + gam, :, m * LANES:(m + 1) * LANES] = rot[-gam % SSM_SLOTS].astype(BF16)

    for q in range(SSM_GROUPS // 2):
        e = _dot(ub_s[2 * q], est_ref[2 * q]) + _dot(ub_s[2 * q + 1], est_ref[2 * q + 1])
        ere_s[:, q * LANES:(q + 1) * LANES] = e[:, :SSM_PAIR_WIDTH]
        eim_s[:, q * LANES:(q + 1) * LANES] = e[:, SSM_PAIR_WIDTH:]

    a_re = apre_ref[...]
    a_im = apim_ref[...]

    def carry_step(r, carry):
        s_re, s_im = carry
        spre_s[pl.ds(r, 1), :] = s_re
        spim_s[pl.ds(r, 1), :] = s_im
        e_re = ere_s[pl.ds(r, 1), :]
        e_im = eim_s[pl.ds(r, 1), :]
        return (a_re * s_re - a_im * s_im + e_re, a_re * s_im + a_im * s_re + e_im)

    s_re, s_im = lax.fori_loop(0, rows, carry_step, (sre_s[...], sim_s[...]))
    sre_s[...] = s_re
    sim_s[...] = s_im

    for v in range(SSM_LANE_BLOCKS):
        yg = []
        for gam in range(SSM_SLOTS):
            g = SSM_SLOTS * v + gam
            q = g // 2
            sp = jnp.concatenate([spre_s[:, q * LANES:(q + 1) * LANES], spim_s[:, q * LANES:(q + 1) * LANES]],
                                 axis=1).astype(BF16)
            yg.append(_dot(ub_s[g], toep_ref[g]) + _dot(sp, int_ref[g]))
        d = d_ref[:, v * LANES:(v + 1) * LANES]
        for m in range(SSM_TIME_BLOCKS):
            back = [yg[-i % SSM_SLOTS][:, m * LANES:(m + 1) * LANES] for i in range(SSM_SLOTS)]
            for bit in slot_bits:
                back = [jnp.where(bit[1], back[(i - bit[0]) % SSM_SLOTS], back[i]) for i in range(SSM_SLOTS)]
            for j in range(SSM_SLOTS):
                o = back[j]
                if j:
                    o = pltpu.roll(o, LANES - SSM_GROUP_DIM * j, 1)
                sl = piece(SSM_SLOTS * m + j, v)
                y_ref[:, sl] = jax.nn.gelu(o + d * u_ref[:, sl])


def _s5(u_rows, tab, d_skip, layer, batch):
    nch, width = u_rows.shape
    rows = SSM_ROWS
    steps = nch // batch // rows
    once = lambda a: _layer_block(a, layer, pipeline_mode=pl.Buffered(1))
    blk = pl.BlockSpec((rows, width), lambda b, i: (b * steps + i, 0))
    table = pltpu.VMEM((SSM_GROUPS, SSM_CHUNK_WIDTH, SSM_CHUNK_WIDTH), BF16)
    state = pltpu.VMEM((rows, SSM_STATE_LANES), F32)
    carry = pltpu.VMEM((1, SSM_STATE_LANES), F32)
    return pl.pallas_call(
        _s5_body,
        grid=(batch, steps),
        in_specs=[blk, once(tab['kern']), once(tab['est']), once(tab['est_swapped']), once(tab['inter']),
                  once(tab['apow_re']), once(tab['apow_im']), once(d_skip)],
        out_specs=blk,
        out_shape=jax.ShapeDtypeStruct((nch, width), F32),
        scratch_shapes=[table, table, table,
                        pltpu.VMEM((SSM_GROUPS, rows, SSM_CHUNK_WIDTH), BF16), state, state, state, state,
                        carry, carry],
        compiler_params=_cparams("arbitrary", "arbitrary"),
        name="s5",
    )(u_rows, tab['kern'], tab['est'], tab['est_swapped'], tab['inter'], tab['apow_re'], tab['apow_im'], d_skip)


def _route_rows(scores, biased):
    ng = N_EXPERTS // EXPERTS_PER_GROUP
    group_score = []
    for gi in range(ng):
        a, b, c, d = biased[gi * EXPERTS_PER_GROUP:(gi + 1) * EXPERTS_PER_GROUP]
        hi1, lo1 = jnp.maximum(a, b), jnp.minimum(a, b)
        hi2, lo2 = jnp.maximum(c, d), jnp.minimum(c, d)
        top1 = jnp.maximum(hi1, hi2)
        top2 = jnp.maximum(jnp.minimum(hi1, hi2), jnp.maximum(lo1, lo2))
        group_score.append(top1 + top2)
    best = group_score[0]
    best_idx = jnp.zeros_like(best, dtype=jnp.int32)
    for gi in range(1, ng):
        better = group_score[gi] > best
        best = jnp.where(better, group_score[gi], best)
        best_idx = jnp.where(better, gi, best_idx)
    picked = []
    for e in range(N_EXPERTS):
        gi = e // EXPERTS_PER_GROUP
        rank = jnp.zeros_like(best_idx)
        for o in range(gi * EXPERTS_PER_GROUP, (gi + 1) * EXPERTS_PER_GROUP):
            if o == e:
                continue
            ahead = (biased[o] > biased[e]) | ((biased[o] == biased[e]) & (o < e))
            rank = rank + ahead.astype(jnp.int32)
        picked.append((best_idx == gi) & (rank < 2))
    wsum = sum(jnp.where(picked[e], scores[e], 0.0) for e in range(N_EXPERTS))
    return [jnp.where(picked[e], scores[e] / wsum, 0.0) for e in range(N_EXPERTS)], best_idx


def _group_sort_positions(best_idx, before_ref):
    ng = N_EXPERTS // EXPERTS_PER_GROUP
    tokens = best_idx.shape[1]
    member = [(best_idx == gi).astype(F32) for gi in range(ng)]
    stacked = jnp.concatenate(member + [jnp.zeros((8 - ng, tokens), F32)], axis=0)
    parts = []
    run = jnp.zeros((8, 1), F32)
    for blk in range(tokens // LANES):
        piece = stacked[:, blk * LANES:(blk + 1) * LANES]
        parts.append(_dot(piece.astype(BF16), before_ref[...]) + run)
        run = run + jnp.sum(piece, axis=1, keepdims=True)
    earlier = jnp.concatenate(parts, axis=1)
    counts = [run[gi:gi + 1, :] for gi in range(ng)]
    pos = jnp.zeros_like(member[0])
    start = jnp.zeros_like(counts[0])
    for gi in range(ng):
        pos = pos + member[gi] * (start + earlier[gi:gi + 1, :])
        start = start + counts[gi]
    return pos, counts


def _outproj_body(x_ref, yp_ref, ya_ref, ys_ref, wgluf_ref, bglu_ref, gssm_ref, woutf_ref, g_ref, b_ref,
                  wr_ref, rb_ref, before_ref, h_ref, comb_ref, pos_ref, cnt_ref, ys_s, wglu_ref, wout_ref):
    @pl.when(pl.program_id(0) == 0)
    def _():
        wglu_ref[...] = wgluf_ref[...].astype(BF16)
        wout_ref[...] = woutf_ref[...].astype(BF16)

    chunks = ys_ref.shape[0]
    for t in range(SSM_CHUNK):
        for v in range(SSM_LANE_BLOCKS):
            ys_s[v, pl.ds(t, chunks, stride=SSM_CHUNK), :] = ys_ref[:, _ssm_piece(t, v)]
    def rows_block(sl):
        ys = jnp.concatenate([ys_s[v, sl, :] for v in range(SSM_LANE_BLOCKS)], axis=1)
        gate = jax.nn.sigmoid(_dot(ys.astype(BF16), wglu_ref[...]) + bglu_ref[...])
        ys = ys * gate
        r = lax.rsqrt(jnp.mean(ys * ys, axis=-1, keepdims=True) + NORM_EPS)
        ysn = (ys * r * gssm_ref[...]).astype(BF16)
        mix = _dot(jnp.concatenate([yp_ref[sl, :], ya_ref[sl, :], ysn], axis=1), wout_ref[...])
        h = _layer_norm(DN_ALPHA * x_ref[sl, :] + mix, g_ref[...], b_ref[...])
        h_ref[sl, :] = h
        h_hi = h.astype(BF16)
        h_lo = (h - h_hi.astype(F32)).astype(BF16)
        return _dot(h_hi, wr_ref[...]) + _dot(h_lo, wr_ref[...])

    tokens = x_ref.shape[0]
    parts = jnp.concatenate([rows_block(pl.ds(r0, OUT_ROWS)) for r0 in range(0, tokens, OUT_ROWS)], axis=0)
    parts_t = parts.T
    sc = jax.nn.sigmoid(parts_t[:N_EXPERTS, :] + parts_t[N_EXPERTS:2 * N_EXPERTS, :])
    bs = sc + rb_ref[...]
    scores = [sc[e:e + 1, :] for e in range(N_EXPERTS)]
    biased = [bs[e:e + 1, :] for e in range(N_EXPERTS)]
    comb_rows, best_idx = _route_rows(scores, biased)
    pos, counts = _group_sort_positions(best_idx, before_ref)
    comb_t = jnp.concatenate(comb_rows + [pos, jnp.zeros((LANES - N_EXPERTS - 1, tokens), F32)], axis=0)
    comb_ref[...] = comb_t.T
    pos_ref[...] = jnp.concatenate([pos, jnp.zeros((7, tokens), F32)], axis=0)
    cnt_ref[...] = jnp.concatenate(
        [jnp.broadcast_to(c, (1, LANES)) for c in counts]
        + [jnp.zeros((8 - len(counts), LANES), F32)], axis=0).astype(jnp.int32)


def _outproj(x, y_pool, y_attn, y_ssm_rows, w_glu, b_glu, g_ssm, w_out, ln_g, ln_b,
             w_router_split, router_bias, layer):
    t = x.shape[0]
    tm = MOE_TOKENS
    nt = t // tm
    row = lambda width: pl.BlockSpec((tm, width), lambda i: (i, 0))
    full = lambda a: pl.BlockSpec(a.shape, lambda i: (0,) * a.ndim)
    per_layer = lambda a: _layer_block(a, layer)
    once = lambda a: _layer_block(a, layer, pipeline_mode=pl.Buffered(1))
    token = np.arange(LANES)
    before = jnp.asarray(token[:, None] < token[None, :], BF16)
    return pl.pallas_call(
        _outproj_body,
        grid=(nt,),
        in_specs=[row(D_MODEL), row(POOL_WIDTH), row(ATTN_WIDTH),
                  pl.BlockSpec((tm // SSM_CHUNK, SSM_CHUNK * SSM_WIDTH), lambda i: (i, 0)),
                  once(w_glu), per_layer(b_glu), per_layer(g_ssm), once(w_out),
                  per_layer(ln_g), per_layer(ln_b), full(w_router_split), full(router_bias), full(before)],
        out_specs=[row(D_MODEL), row(LANES), pl.BlockSpec((8, tm), lambda i: (0, i)),
                   pl.BlockSpec((8, LANES), lambda i: (i, 0))],
        out_shape=[jax.ShapeDtypeStruct((t, D_MODEL), F32), jax.ShapeDtypeStruct((t, LANES), F32),
                   jax.ShapeDtypeStruct((8, t), F32), jax.ShapeDtypeStruct((8 * nt, LANES), jnp.int32)],
        scratch_shapes=[pltpu.VMEM((SSM_LANE_BLOCKS, tm, LANES), F32),
                        pltpu.VMEM(w_glu.shape[1:], BF16), pltpu.VMEM(w_out.shape[1:], BF16)],
        compiler_params=_cparams("arbitrary"),
        name="outproj",
    )(x, y_pool, y_attn, y_ssm_rows, w_glu, b_glu, g_ssm, w_out, ln_g, ln_b,
      w_router_split, router_bias, before)


def _moe_body(cnt_ref, h_ref, comb_ref, pos_ref, p_ref, wg_ref, wu_ref, wd_ref, wpgf_ref, wppf_ref, g_ref, b_ref,
              o_ref, hs_s, cs_s, acc_s, ple_s, wpg_ref, wpp_ref):
    i = pl.program_id(0)
    group = pl.program_id(1)
    ng = pl.num_programs(1)
    tm = h_ref.shape[0]

    @pl.when((i == 0) & (group == 0))
    def _():
        wpg_ref[...] = wpgf_ref[...].astype(BF16)
        wpp_ref[...] = wppf_ref[...].astype(BF16)

    @pl.when(group == 0)
    def _():
        hb = h_ref[...].astype(BF16)
        comb = comb_ref[...]
        comb_lo = comb - comb.astype(BF16).astype(F32)
        low_lanes = lax.broadcasted_iota(jnp.int32, (1, LANES), 1) < COMB_LO_LANE
        comb_b = jnp.where(low_lanes, comb, pltpu.roll(comb_lo, COMB_LO_LANE, 1)).astype(BF16)
        for r0 in range(0, tm, MOE_SIDE_ROWS):
            sl = pl.ds(r0, MOE_SIDE_ROWS)
            row = r0 + lax.broadcasted_iota(jnp.int32, (MOE_SIDE_ROWS, tm), 0)
            perm = jnp.where(pos_ref[0:1, :] == row.astype(F32), 1.0, 0.0).astype(BF16)
            hs_s[sl, :] = _dot(perm, hb).astype(BF16)
            both = _dot(perm, comb_b)
            cs_s[sl, :] = both + pltpu.roll(both, LANES - COMB_LO_LANE, 1)
        acc_s[...] = jnp.zeros_like(acc_s)

    count = cnt_ref[i * ng + group]
    start = jnp.int32(0)
    for gi in range(N_EXPERTS // EXPERTS_PER_GROUP - 1):
        start = start + jnp.where(group > gi, cnt_ref[i * ng + gi], 0)
    lane = lax.broadcasted_iota(jnp.int32, (1, LANES), 1)
    first = (start // BF16_ROW_PACK) * BF16_ROW_PACK
    windows = (start - first + count + MOE_WINDOW_ROWS - 1) // MOE_WINDOW_ROWS

    def window(w, carry):
        wanted = first + w * MOE_WINDOW_ROWS
        lo = pl.multiple_of(jnp.minimum(wanted, tm - MOE_WINDOW_ROWS), BF16_ROW_PACK)
        rows = pl.ds(lo, MOE_WINDOW_ROWS)
        x = hs_s[rows, :]
        fresh = lo + lax.broadcasted_iota(jnp.int32, (MOE_WINDOW_ROWS, 1), 0) >= wanted
        cs = jnp.where(fresh, cs_s[rows, :], 0.0)
        total = None
        for e in range(EXPERTS_PER_GROUP):
            gate = _dot(x, wg_ref[e])
            up = _dot(x, wu_ref[e])
            c = jnp.sum(jnp.where(lane == group * EXPERTS_PER_GROUP + e, cs, 0.0), axis=1, keepdims=True)
            a = (jax.nn.silu(gate) * up * c).astype(BF16)
            d = _dot(a, wd_ref[e])
            total = d if total is None else total + d
        acc_s[rows, :] += total
        return carry

    lax.fori_loop(0, windows, window, 0)

    half = tm // 2
    for mid in (1, 2):
        @pl.when(group == mid)
        def _():
            for r0 in range((mid - 1) * half, mid * half, MOE_SIDE_ROWS):
                sl = pl.ds(r0, MOE_SIDE_ROWS)
                hb = h_ref[sl, :].astype(BF16)
                ple_s[sl, :] = (jax.nn.sigmoid(_dot(hb, wpg_ref[...]))
                                * _dot(p_ref[sl, :].astype(BF16), wpp_ref[...]))

    @pl.when(group == ng - 1)
    def _():
        sorted_out = acc_s[...].astype(BF16)
        col = lax.broadcasted_iota(jnp.int32, (MOE_SIDE_ROWS, tm), 1).astype(F32)
        for r0 in range(0, tm, MOE_SIDE_ROWS):
            sl = pl.ds(r0, MOE_SIDE_ROWS)
            unperm = jnp.where(comb_ref[sl, SORT_POS_LANE:SORT_POS_LANE + 1] == col, 1.0, 0.0).astype(BF16)
            ffn = _dot(unperm, sorted_out)
            o_ref[sl, :] = _layer_norm(DN_ALPHA * h_ref[sl, :] + ffn + ple_s[sl, :], g_ref[...], b_ref[...])


def _moe(h, comb, pos_rows, counts, p_all, layer, wg_bf16, wu_bf16, wd_bf16, w_ple_gate, w_ple_proj, ln_g, ln_b):
    t = h.shape[0]
    tm = MOE_TOKENS
    nt = t // tm
    ng = N_EXPERTS // EXPERTS_PER_GROUP
    full = lambda a: _layer_block(a, layer)
    once = lambda a: _layer_block(a, layer, pipeline_mode=pl.Buffered(1))
    experts = lambda rows, cols: pl.BlockSpec((EXPERTS_PER_GROUP, rows, cols),
                                              lambda i, g, cnt: (layer * ng + g, 0, 0))
    grid_spec = pltpu.PrefetchScalarGridSpec(
        num_scalar_prefetch=1,
        grid=(nt, ng),
        in_specs=[pl.BlockSpec((tm, D_MODEL), lambda i, g, cnt: (i, 0)),
                  pl.BlockSpec((tm, LANES), lambda i, g, cnt: (i, 0)),
                  pl.BlockSpec((8, tm), lambda i, g, cnt: (0, i)),
                  pl.BlockSpec((tm, PLE_DIM), lambda i, g, cnt: (layer * nt + i, 0)),
                  experts(D_MODEL, D_EXPERT), experts(D_MODEL, D_EXPERT), experts(D_EXPERT, D_MODEL),
                  once(w_ple_gate), once(w_ple_proj), full(ln_g), full(ln_b)],
        out_specs=pl.BlockSpec((tm, D_MODEL), lambda i, g, cnt: (i, 0)),
        scratch_shapes=[pltpu.VMEM((tm, D_MODEL), BF16), pltpu.VMEM((tm, LANES), F32),
                        pltpu.VMEM((tm, D_MODEL), F32), pltpu.VMEM((tm, D_MODEL), F32),
                        pltpu.VMEM(w_ple_gate.shape[1:], BF16), pltpu.VMEM(w_ple_proj.shape[1:], BF16)])
    return pl.pallas_call(
        _moe_body,
        grid_spec=grid_spec,
        out_shape=jax.ShapeDtypeStruct((t, D_MODEL), F32),
        compiler_params=_cparams("arbitrary", "arbitrary"),
        name="moe",
    )(counts, h, comb, pos_rows, p_all, wg_bf16, wu_bf16, wd_bf16, w_ple_gate, w_ple_proj, ln_g, ln_b)


def _block_diag(w):
    g, n, m = w.shape
    eye = jnp.eye(g, dtype=w.dtype)
    return (eye[:, None, :, None] * w[:, :, None, :]).reshape(g * n, g * m)


def kernel(x, p, w_in, w_out, w_pool, pool_scale, rel_bias, ssm_a_re, ssm_a_im, ssm_log_dt, ssm_b_re, ssm_b_im,
           ssm_c_re, ssm_c_im, ssm_d, w_glu, b_glu, g_pool, g_attn, g_ssm, ln1_g, ln1_b, ln2_g, ln2_b,
           w_router, router_bias, w_exp_gate, w_exp_up, w_exp_down, w_ple_gate, w_ple_proj):
    batch, seq, d = x.shape
    t = batch * seq
    xt = x.reshape(t, d)
    p_all = p.reshape(DEPTH * t, PLE_DIM)

    vec = lambda a: a.astype(F32).reshape(DEPTH, 1, -1)
    f32 = lambda a: a.astype(F32)
    w_pool_b = jax.vmap(_block_diag)(w_pool).astype(BF16)
    bias_rows = _attn_bias_rows(rel_bias)
    tables = jax.vmap(_s5_tables)(ssm_a_re, ssm_a_im, ssm_log_dt, ssm_b_re, ssm_b_im, ssm_c_re, ssm_c_im)
    wr = w_router.astype(F32)
    wr_hi = wr.astype(BF16)
    wr_lo = (wr - wr_hi.astype(F32)).astype(BF16)
    wr_split = jnp.pad(jnp.concatenate([wr_hi, wr_lo], axis=1), ((0, 0), (0, LANES - 2 * N_EXPERTS)))
    r_bias = router_bias.astype(F32).reshape(N_EXPERTS, 1)
    stack_experts = lambda w: w.astype(BF16).reshape((DEPTH * N_EXPERTS,) + w.shape[2:])
    wg_b, wu_b, wd_b = stack_experts(w_exp_gate), stack_experts(w_exp_up), stack_experts(w_exp_down)
    ng = N_EXPERTS // EXPERTS_PER_GROUP

    for layer in range(DEPTH):
        y_pool, q, k, v, u_ssm = _inproj(xt, f32(w_in), w_pool_b, vec(pool_scale), vec(g_pool), layer, batch)
        y_attn = _attention(q, k, v, bias_rows, vec(g_attn), layer, batch)
        y_ssm = _s5(u_ssm, tables, vec(ssm_d), layer, batch)
        h, comb, pos_rows, cnt = _outproj(xt, y_pool, y_attn, y_ssm, f32(w_glu), vec(b_glu), vec(g_ssm),
                                          f32(w_out), vec(ln1_g), vec(ln1_b), wr_split, r_bias, layer)
        counts = cnt[:, 0].reshape(-1, 8)[:, :ng].reshape(-1)
        xt = _moe(h, comb, pos_rows, counts, p_all, layer, wg_b, wu_b, wd_b, f32(w_ple_gate), f32(w_ple_proj),
                  vec(ln2_g), vec(ln2_b))
    return xt.reshape(batch, seq, d)
```

```python
import functools
import math

import numpy as np
import jax
import jax.numpy as jnp
from jax import lax
from jax.experimental import pallas as pl
from jax.experimental.pallas import tpu as pltpu

F32 = jnp.float32
BF16 = jnp.bfloat16

D_MODEL = 1024
DEPTH = 2
CHUNK = 64
PLE_DIM = 256
POOL_WIDTH = 256
POOL_GROUP_DIM = 64
POOL_WINDOWS = (2, 4, 8, 16)
POOL_HALO = 32
ATTN_HEAD_DIM = 64
ATTN_HEADS = 6
ATTN_WIDTH = 384
N_PREV_CHUNKS = 8
REL_CLIP = 128
SSM_WIDTH = 384
SSM_GROUP_DIM = 16
SSM_GROUPS = 24
SSM_STATE = 64
N_EXPERTS = 16
EXPERTS_PER_GROUP = 4
D_EXPERT = 256
DN_ALPHA = (2 * DEPTH) ** 0.25
NORM_EPS = 1e-5
LOG2_E = math.log2(math.e)

LANES = 128
VMEM_LIMIT_BYTES = 56 * 1024 * 1024

INPROJ_TOKENS = 1024
INPROJ_ROWS = 512
ATTN_Q_CHUNKS = 4
ATTN_Q_TOKENS = ATTN_Q_CHUNKS * CHUNK
ATTN_BAND_TOKENS = 3 * ATTN_Q_TOKENS
ATTN_BLOCKS_PER_STEP = 2
ATTN_BIAS_ROW = 1024
SSM_CHUNK = 16
SSM_CHUNK_WIDTH = SSM_CHUNK * SSM_GROUP_DIM
SSM_ROWS = 128
SSM_SLOTS = LANES // SSM_GROUP_DIM
SSM_LANE_BLOCKS = SSM_WIDTH // LANES
SSM_TIME_BLOCKS = SSM_CHUNK // SSM_SLOTS
SSM_PAIR_WIDTH = 2 * SSM_STATE
SSM_STATE_LANES = SSM_GROUPS * SSM_STATE
OUT_ROWS = 256
MOE_SIDE_ROWS = 256
MOE_TOKENS = 1024
MOE_WINDOW_ROWS = 320
BF16_ROW_PACK = 16
SORT_POS_LANE = N_EXPERTS
COMB_LO_LANE = 32


def _cparams(*sem):
    return pltpu.CompilerParams(dimension_semantics=sem, vmem_limit_bytes=VMEM_LIMIT_BYTES)


def _dot(a, b):
    return jnp.dot(a, b, preferred_element_type=F32)


def _layer_block(a, layer, **kwargs):
    return pl.BlockSpec((None,) + a.shape[1:], lambda *_: (layer,) + (0,) * (a.ndim - 1), **kwargs)


def _layer_norm(v, g, b):
    mu = jnp.mean(v, axis=-1, keepdims=True)
    vc = v - mu
    var = jnp.mean(vc * vc, axis=-1, keepdims=True)
    return vc * lax.rsqrt(var + NORM_EPS) * g + b


def _ssm_piece(t, v):
    lo = SSM_WIDTH * t + LANES * v
    return slice(lo, lo + LANES)


def _pool_mix(x0, buf, lvl_a, lvl_b, pos, w_ref, scale_ref, g_ref):
    n = x0.shape[0] + POOL_HALO
    group = lax.broadcasted_iota(jnp.int32, (1, POOL_WIDTH), 1) // POOL_GROUP_DIM
    mean = jnp.zeros_like(x0)
    src, dst = buf, lvl_a
    for gi, w in enumerate(POOL_WINDOWS):
        lo = 8 * (gi + 1)
        dst[lo:n, :] = src[lo:n, :] + src[lo - w // 2:n - w // 2, :]
        inv_cnt = 1.0 / jnp.minimum(pos + 1, w).astype(F32)
        mean = jnp.where(group == gi, dst[POOL_HALO:n, :] * inv_cnt, mean)
        src, dst = dst, (lvl_b if dst is lvl_a else lvl_a)
    d = (mean - x0).astype(BF16)
    y = _dot(d, w_ref[...]) * scale_ref[...]
    r = lax.rsqrt(jnp.mean(y * y, axis=-1, keepdims=True) + NORM_EPS)
    return (y * r * g_ref[...]).astype(BF16)


def _inproj_body(x_ref, wf_ref, wpool_ref, pscale_ref, pgain_ref, yp_ref, q_ref, k_ref, v_ref, us_ref,
                 w_ref, zs, buf, lvl_a, lvl_b, *, tiles_per_seq):
    tm = x_ref.shape[0]
    tile_in_seq = pl.program_id(0) % tiles_per_seq

    @pl.when(pl.program_id(0) == 0)
    def _():
        w_ref[...] = wf_ref[...].astype(BF16)

    @pl.when(tile_in_seq == 0)
    def _():
        buf[0:POOL_HALO, :] = jnp.zeros((POOL_HALO, POOL_WIDTH), F32)

    for r0 in range(0, tm, INPROJ_ROWS):
        sl = pl.ds(r0, INPROJ_ROWS)
        xb = x_ref[sl, :].astype(BF16)

        def cols(lo, hi):
            return _dot(xb, w_ref[:, lo:hi])

        buf[pl.ds(POOL_HALO + r0, INPROJ_ROWS), :] = cols(0, 256)
        qk = cols(256, 768)
        q_ref[sl, :] = (qk[:, :ATTN_WIDTH] * (ATTN_HEAD_DIM ** -0.5 * LOG2_E)).astype(BF16)
        k_ref[sl, :LANES] = qk[:, ATTN_WIDTH:].astype(BF16)
        k_ref[sl, LANES:] = cols(768, 1024).astype(BF16)
        vs = cols(1024, 1536)
        v_ref[sl, :] = vs[:, :ATTN_WIDTH].astype(BF16)
        zs[0, sl, :] = vs[:, ATTN_WIDTH:]
        s_rest = cols(1536, 1792)
        zs[1, sl, :] = s_rest[:, :LANES]
        zs[2, sl, :] = s_rest[:, LANES:]

    u_pool = buf[POOL_HALO:, :]
    pos = tile_in_seq * tm + lax.broadcasted_iota(jnp.int32, (tm, 1), 0)
    yp_ref[...] = _pool_mix(u_pool, buf, lvl_a, lvl_b, pos, wpool_ref, pscale_ref, pgain_ref)
    buf[0:POOL_HALO, :] = buf[tm:, :]

    chunks = us_ref.shape[0]
    for t in range(SSM_CHUNK):
        for v in range(SSM_LANE_BLOCKS):
            us_ref[:, _ssm_piece(t, v)] = zs[v, pl.ds(t, chunks, stride=SSM_CHUNK), :]


def _inproj(x, w_in, w_pool_blockdiag_bf16, pool_scale, pool_gain, layer, batch):
    t = x.shape[0]
    tm = INPROJ_TOKENS
    row = lambda width: pl.BlockSpec((tm, width), lambda i: (i, 0))
    full = lambda a: _layer_block(a, layer)
    chunk_rows = pl.BlockSpec((tm // SSM_CHUNK, SSM_CHUNK * SSM_WIDTH), lambda i: (i, 0))
    return pl.pallas_call(
        functools.partial(_inproj_body, tiles_per_seq=t // batch // tm),
        grid=(t // tm,),
        in_specs=[row(D_MODEL), _layer_block(w_in, layer, pipeline_mode=pl.Buffered(1)),
                  full(w_pool_blockdiag_bf16), full(pool_scale), full(pool_gain)],
        out_specs=[row(POOL_WIDTH), row(ATTN_WIDTH), row(ATTN_WIDTH), row(ATTN_WIDTH), chunk_rows],
        out_shape=[jax.ShapeDtypeStruct((t, POOL_WIDTH), BF16),
                   jax.ShapeDtypeStruct((t, ATTN_WIDTH), BF16),
                   jax.ShapeDtypeStruct((t, ATTN_WIDTH), BF16),
                   jax.ShapeDtypeStruct((t, ATTN_WIDTH), BF16),
                   jax.ShapeDtypeStruct((t // SSM_CHUNK, SSM_CHUNK * SSM_WIDTH), F32)],
        scratch_shapes=[pltpu.VMEM(w_in.shape[1:], BF16), pltpu.VMEM((SSM_LANE_BLOCKS, tm, LANES), F32)]
        + [pltpu.VMEM((POOL_HALO + tm, POOL_WIDTH), F32)] * 3,
        compiler_params=_cparams("arbitrary"),
        name="inproj",
    )(x, w_in, w_pool_blockdiag_bf16, pool_scale, pool_gain)


def _attn_bias_rows(rel_bias):
    x = np.arange(ATTN_BIAS_ROW)
    x = np.where(x < ATTN_BAND_TOKENS, x, x - ATTN_BIAS_ROW)
    idx = np.clip(N_PREV_CHUNKS * CHUNK - x, -REL_CLIP, REL_CLIP) + REL_CLIP
    return rel_bias.astype(F32)[..., idx] * LOG2_E


def _attn_body(q_ref, kprev_ref, kcur_ref, vprev_ref, vcur_ref, rows_ref, g_ref, o_ref, bias_s):
    b = pl.program_id(0)
    i = pl.program_id(1)
    tq = ATTN_Q_TOKENS

    @pl.when((b == 0) & (i == 0))
    def _():
        qc = lax.broadcasted_iota(jnp.int32, (tq, ATTN_BAND_TOKENS), 0) // CHUNK
        kc = lax.broadcasted_iota(jnp.int32, (tq, ATTN_BAND_TOKENS), 1) // CHUNK
        in_band = (kc >= qc) & (kc <= qc + N_PREV_CHUNKS)
        for head in range(ATTN_HEADS):
            full = jnp.broadcast_to(rows_ref[head:head + 1, :], (tq, ATTN_BIAS_ROW))
            shifted = pltpu.roll(full, 0, 1, stride=1, stride_axis=0)
            bias_s[head // 2, (head % 2) * tq:(head % 2 + 1) * tq, :] = jnp.where(
                in_band, shifted[:, :ATTN_BAND_TOKENS], -jnp.inf)

    upper_half = lax.broadcasted_iota(jnp.int32, (1, LANES), 1) >= ATTN_HEAD_DIM

    def band(prev_ref, cur_ref, blk, sl):
        if blk == 0:
            return jnp.concatenate([prev_ref[:, sl], cur_ref[0:tq, sl]], axis=0)
        return jnp.concatenate([prev_ref[tq:, sl], cur_ref[:, sl]], axis=0)

    def heads(blk, masked_keys):
        rows = pl.ds(blk * tq, tq)
        outs = []
        for pair in range(ATTN_WIDTH // LANES):
            sl = slice(pair * LANES, (pair + 1) * LANES)
            qp = q_ref[rows, sl]
            kp = band(kprev_ref, kcur_ref, blk, sl)
            vp = band(vprev_ref, vcur_ref, blk, sl)
            zero = jnp.zeros_like(qp)
            q2 = jnp.concatenate([jnp.where(upper_half, zero, qp), jnp.where(upper_half, qp, zero)], axis=0)
            s = lax.dot_general(q2, kp, (((1,), (1,)), ((), ())), preferred_element_type=F32)
            s = s + bias_s[pair]
            if masked_keys:
                key = lax.broadcasted_iota(jnp.int32, (1, ATTN_BAND_TOKENS), 1)
                s = jnp.where(key < masked_keys, -jnp.inf, s)
            m = jnp.max(s, axis=-1, keepdims=True)
            p = jnp.exp2(s - m)
            l = jnp.sum(p, axis=-1, keepdims=True)
            o = _dot(p.astype(BF16), vp) * (1.0 / l)
            outs.append(jnp.where(upper_half, o[tq:, :], o[:tq, :]))
        ss = sum(jnp.sum(o * o, axis=-1, keepdims=True) for o in outs)
        r = lax.rsqrt(ss / ATTN_WIDTH + NORM_EPS)
        for pair, o in enumerate(outs):
            sl = slice(pair * LANES, (pair + 1) * LANES)
            o_ref[rows, sl] = (o * r * g_ref[:, sl]).astype(BF16)

    @pl.when(i == 0)
    def _():
        heads(0, 2 * tq)
        heads(1, tq)

    @pl.when(i > 0)
    def _():
        heads(0, 0)
        heads(1, 0)


def _attention(q, k, v, bias_rows, gain, layer, batch):
    t = q.shape[0]
    ts = ATTN_BLOCKS_PER_STEP * ATTN_Q_TOKENS
    steps = t // batch // ts

    def blk(back):
        return pl.BlockSpec((ts, ATTN_WIDTH), lambda b, i: (b * steps + jnp.maximum(i - back, 0), 0))

    return pl.pallas_call(
        _attn_body,
        grid=(batch, steps),
        in_specs=[blk(0), blk(1), blk(0), blk(1), blk(0),
                  _layer_block(bias_rows, layer), _layer_block(gain, layer)],
        out_specs=blk(0),
        out_shape=jax.ShapeDtypeStruct((t, ATTN_WIDTH), BF16),
        scratch_shapes=[pltpu.VMEM((ATTN_HEADS // 2, 2 * ATTN_Q_TOKENS, ATTN_BAND_TOKENS), F32)],
        compiler_params=_cparams("arbitrary", "arbitrary"),
        name="attention",
    )(q, k, k, v, v, bias_rows, gain)


def _s5_position_of_time(g, time):
    return SSM_SLOTS * (time // SSM_SLOTS) + (time % SSM_SLOTS + g) % SSM_SLOTS


def _s5_tables(a_re, a_im, log_dt, b_re, b_im, c_re, c_im):
    hi = lax.Precision.HIGHEST
    tc = SSM_CHUNK
    g, p_dim = a_re.shape
    dt = jnp.exp(log_dt.astype(F32))[:, None]
    ar = a_re.astype(F32)
    ai = a_im.astype(F32)
    mag = jnp.exp(ar * dt)
    abar_re = mag * jnp.cos(ai * dt)
    abar_im = mag * jnp.sin(ai * dt)
    den = ar * ar + ai * ai
    nr = abar_re - 1.0
    ni = abar_im
    coef_re = ((nr * ar + ni * ai) / den)[..., None]
    coef_im = ((ni * ar - nr * ai) / den)[..., None]
    br = b_re.astype(F32)
    bi = b_im.astype(F32)
    bbar_re = coef_re * br - coef_im * bi
    bbar_im = coef_re * bi + coef_im * br
    n = jnp.arange(tc + 1, dtype=F32)
    pmag = jnp.exp((ar * dt)[..., None] * n)
    pw_re = pmag * jnp.cos((ai * dt)[..., None] * n)
    pw_im = pmag * jnp.sin((ai * dt)[..., None] * n)
    cw = tc * SSM_GROUP_DIM
    lag_rep = jnp.asarray(np.kron(np.eye(tc), np.ones((1, SSM_GROUP_DIM))), F32)
    ch_rep = jnp.asarray(np.kron(np.ones((1, tc)), np.eye(SSM_GROUP_DIM)), F32)
    expand = lambda a, rep: jnp.einsum('gpn,nx->gpx', a, rep, precision=hi)
    c_re_rep = expand(c_re.astype(F32).transpose(0, 2, 1), ch_rep)
    c_im_rep = expand(c_im.astype(F32).transpose(0, 2, 1), ch_rep)

    def output_coefficients(first_power, rep, subscripts):
        p_re = jnp.einsum(subscripts, pw_re[..., first_power:first_power + tc], rep, precision=hi)
        p_im = jnp.einsum(subscripts, pw_im[..., first_power:first_power + tc], rep, precision=hi)
        return c_re_rep * p_re - c_im_rep * p_im, -(c_re_rep * p_im + c_im_rep * p_re)

    on_re, on_im = output_coefficients(0, lag_rep, 'gpn,nx->gpx')
    kern = (jnp.einsum('gpk,gpx->gkx', bbar_re, on_re, precision=hi)
            + jnp.einsum('gpk,gpx->gkx', bbar_im, on_im, precision=hi))
    position = np.arange(tc)
    time_at = (SSM_SLOTS * (position // SSM_SLOTS)
               + (position % SSM_SLOTS - np.arange(g)[:, None]) % SSM_SLOTS)
    slot_rep = np.repeat(time_at[:, None, :] == np.arange(tc)[None, :, None], SSM_GROUP_DIM, axis=2)
    inter = jnp.concatenate(output_coefficients(1, jnp.asarray(slot_rep, F32), 'gpn,gnx->gpx'), axis=1)
    pt_re = pw_re[..., :tc].transpose(0, 2, 1)[:, :, None, :]
    pt_im = pw_im[..., :tc].transpose(0, 2, 1)[:, :, None, :]
    bt_re = bbar_re.transpose(0, 2, 1)[:, None]
    bt_im = bbar_im.transpose(0, 2, 1)[:, None]
    est_re = (pt_re * bt_re - pt_im * bt_im).reshape(g, cw, p_dim)
    est_im = (pt_re * bt_im + pt_im * bt_re).reshape(g, cw, p_dim)
    return dict(
        kern=kern,
        est=jnp.concatenate([est_re, est_im], -1),
        est_swapped=jnp.concatenate([est_im, est_re], -1),
        inter=inter,
        apow_re=pw_re[..., tc].reshape(1, g * p_dim), apow_im=pw_im[..., tc].reshape(1, g * p_dim))


def _s5_prepare(kern_ref, est_ref, estsw_ref, int_ref, toep_s, est_s, int_s):
    lane = lax.broadcasted_iota(jnp.int32, (1, LANES), 1)
    slot = lane // SSM_GROUP_DIM
    zero = jnp.zeros((SSM_GROUP_DIM, LANES), F32)
    zero_rows = jnp.zeros((SSM_STATE, SSM_CHUNK_WIDTH), BF16)
    for g in range(SSM_GROUPS):
        mine = (lane >= SSM_STATE) if g % 2 else (lane < SSM_STATE)
        turn = g % SSM_SLOTS
        time_slot = (slot - turn) % SSM_SLOTS
        k0 = [kern_ref[g, :, 0:LANES]]
        k1 = [kern_ref[g, :, LANES:2 * LANES]]
        for r in range(1, SSM_SLOTS):
            k0.append(pltpu.roll(k0[0], SSM_GROUP_DIM * r, 1))
            k1.append(pltpu.roll(k1[0], SSM_GROUP_DIM * r, 1))
        for time in range(SSM_CHUNK):
            r = (time + turn) % SSM_SLOTS
            later = time_slot >= time % SSM_SLOTS
            if time < SSM_SLOTS:
                h0 = jnp.where(later, k0[r], zero)
                h1 = jnp.where(later, k1[r], k0[r])
            else:
                h0 = zero
                h1 = jnp.where(later, k0[r], zero)
            rows = pl.ds(SSM_GROUP_DIM * _s5_position_of_time(g, time), SSM_GROUP_DIM)
            toep_s[g, rows, 0:LANES] = h0.astype(BF16)
            toep_s[g, rows, LANES:2 * LANES] = h1.astype(BF16)
            src = pl.ds(SSM_GROUP_DIM * (SSM_CHUNK - 1 - time), SSM_GROUP_DIM)
            e, e_swapped = est_ref[g, src, :], estsw_ref[g, src, :]
            e_re, e_im = (e_swapped, e) if g % 2 else (e, e_swapped)
            est_s[g, rows, 0:LANES] = jnp.where(mine, e_re, zero).astype(BF16)
            est_s[g, rows, LANES:2 * LANES] = jnp.where(mine, e_im, zero).astype(BF16)
        for part in range(2):
            src = pl.ds(part * SSM_STATE, SSM_STATE)
            base = part * SSM_PAIR_WIDTH
            int_s[g, pl.ds(base + (1 - g % 2) * SSM_STATE, SSM_STATE), :] = zero_rows
            int_s[g, pl.ds(base + (g % 2) * SSM_STATE, SSM_STATE), :] = int_ref[g, src, :].astype(BF16)


def _s5_body(u_ref, kern_ref, estin_ref, estswin_ref, intin_ref, apre_ref, apim_ref, d_ref, y_ref,
             toep_ref, est_ref, int_ref, ub_s, ere_s, eim_s, spre_s, spim_s, sre_s, sim_s):
    rows = u_ref.shape[0]

    @pl.when((pl.program_id(0) == 0) & (pl.program_id(1) == 0))
    def _():
        _s5_prepare(kern_ref, estin_ref, estswin_ref, intin_ref, toep_ref, est_ref, int_ref)

    @pl.when(pl.program_id(1) == 0)
    def _():
        sre_s[...] = jnp.zeros_like(sre_s)
        sim_s[...] = jnp.zeros_like(sim_s)

    slot = lax.broadcasted_iota(jnp.int32, (1, LANES), 1) // SSM_GROUP_DIM
    slot_bits = [(b, (slot & b) != 0) for b in (1, 2, 4)]
    piece = _ssm_piece

    for v in range(SSM_LANE_BLOCKS):
        for m in range(SSM_TIME_BLOCKS):
            rot = []
            for j in range(SSM_SLOTS):
                a = u_ref[:, piece(SSM_SLOTS * m + j, v)]
                rot.append(a if j == 0 else pltpu.roll(a, SSM_GROUP_DIM * j, 1))
            for bit in slot_bits:
                rot = [jnp.where(bit[1], rot[(i + bit[0]) % SSM_SLOTS], rot[i]) for i in range(SSM_SLOTS)]
            for gam in range(SSM_SLOTS):
                ub_s[SSM_SLOTS * v + gam, :, m * LANES:(m + 1) * LANES] = rot[-gam % SSM_SLOTS].astype(BF16)

    for q in range(SSM_GROUPS // 2):
        e = _dot(ub_s[2 * q], est_ref[2 * q]) + _dot(ub_s[2 * q + 1], est_ref[2 * q + 1])
        ere_s[:, q * LANES:(q + 1) * LANES] = e[:, :SSM_PAIR_WIDTH]
        eim_s[:, q * LANES:(q + 1) * LANES] = e[:, SSM_PAIR_WIDTH:]

    a_re = apre_ref[...]
    a_im = apim_ref[...]

    def carry_step(r, carry):
        s_re, s_im = carry
        spre_s[pl.ds(r, 1), :] = s_re
        spim_s[pl.ds(r, 1), :] = s_im
        e_re = ere_s[pl.ds(r, 1), :]
        e_im = eim_s[pl.ds(r, 1), :]
        return (a_re * s_re - a_im * s_im + e_re, a_re * s_im + a_im * s_re + e_im)

    s_re, s_im = lax.fori_loop(0, rows, carry_step, (sre_s[...], sim_s[...]))
    sre_s[...] = s_re
    sim_s[...] = s_im

    for v in range(SSM_LANE_BLOCKS):
        yg = []
        for gam in range(SSM_SLOTS):
            g = SSM_SLOTS * v + gam
            q = g // 2
            sp = jnp.concatenate([spre_s[:, q * LANES:(q + 1) * LANES], spim_s[:, q * LANES:(q + 1) * LANES]],
                                 axis=1).astype(BF16)
            yg.append(_dot(ub_s[g], toep_ref[g]) + _dot(sp, int_ref[g]))
        d = d_ref[:, v * LANES:(v + 1) * LANES]
        for m in range(SSM_TIME_BLOCKS):
            back = [yg[-i % SSM_SLOTS][:, m * LANES:(m + 1) * LANES] for i in range(SSM_SLOTS)]
            for bit in slot_bits:
                back = [jnp.where(bit[1], back[(i - bit[0]) % SSM_SLOTS], back[i]) for i in range(SSM_SLOTS)]
            for j in range(SSM_SLOTS):
                o = back[j]
                if j:
                    o = pltpu.roll(o, LANES - SSM_GROUP_DIM * j, 1)
                sl = piece(SSM_SLOTS * m + j, v)
                y_ref[:, sl] = jax.nn.gelu(o + d * u_ref[:, sl])


def _s5(u_rows, tab, d_skip, layer, batch):
    nch, width = u_rows.shape
    rows = SSM_ROWS
    steps = nch // batch // rows
    once = lambda a: _layer_block(a, layer, pipeline_mode=pl.Buffered(1))
    blk = pl.BlockSpec((rows, width), lambda b, i: (b * steps + i, 0))
    table = pltpu.VMEM((SSM_GROUPS, SSM_CHUNK_WIDTH, SSM_CHUNK_WIDTH), BF16)
    state = pltpu.VMEM((rows, SSM_STATE_LANES), F32)
    carry = pltpu.VMEM((1, SSM_STATE_LANES), F32)
    return pl.pallas_call(
        _s5_body,
        grid=(batch, steps),
        in_specs=[blk, once(tab['kern']), once(tab['est']), once(tab['est_swapped']), once(tab['inter']),
                  once(tab['apow_re']), once(tab['apow_im']), once(d_skip)],
        out_specs=blk,
        out_shape=jax.ShapeDtypeStruct((nch, width), F32),
        scratch_shapes=[table, table, table,
                        pltpu.VMEM((SSM_GROUPS, rows, SSM_CHUNK_WIDTH), BF16), state, state, state, state,
                        carry, carry],
        compiler_params=_cparams("arbitrary", "arbitrary"),
        name="s5",
    )(u_rows, tab['kern'], tab['est'], tab['est_swapped'], tab['inter'], tab['apow_re'], tab['apow_im'], d_skip)


def _route_rows(scores, biased):
    ng = N_EXPERTS // EXPERTS_PER_GROUP
    group_score = []
    for gi in range(ng):
        a, b, c, d = biased[gi * EXPERTS_PER_GROUP:(gi + 1) * EXPERTS_PER_GROUP]
        hi1, lo1 = jnp.maximum(a, b), jnp.minimum(a, b)
        hi2, lo2 = jnp.maximum(c, d), jnp.minimum(c, d)
        top1 = jnp.maximum(hi1, hi2)
        top2 = jnp.maximum(jnp.minimum(hi1, hi2), jnp.maximum(lo1, lo2))
        group_score.append(top1 + top2)
    best = group_score[0]
    best_idx = jnp.zeros_like(best, dtype=jnp.int32)
    for gi in range(1, ng):
        better = group_score[gi] > best
        best = jnp.where(better, group_score[gi], best)
        best_idx = jnp.where(better, gi, best_idx)
    picked = []
    for e in range(N_EXPERTS):
        gi = e // EXPERTS_PER_GROUP
        rank = jnp.zeros_like(best_idx)
        for o in range(gi * EXPERTS_PER_GROUP, (gi + 1) * EXPERTS_PER_GROUP):
            if o == e:
                continue
            ahead = (biased[o] > biased[e]) | ((biased[o] == biased[e]) & (o < e))
            rank = rank + ahead.astype(jnp.int32)
        picked.append((best_idx == gi) & (rank < 2))
    wsum = sum(jnp.where(picked[e], scores[e], 0.0) for e in range(N_EXPERTS))
    return [jnp.where(picked[e], scores[e] / wsum, 0.0) for e in range(N_EXPERTS)], best_idx


def _group_sort_positions(best_idx, before_ref):
    ng = N_EXPERTS // EXPERTS_PER_GROUP
    tokens = best_idx.shape[1]
    member = [(best_idx == gi).astype(F32) for gi in range(ng)]
    stacked = jnp.concatenate(member + [jnp.zeros((8 - ng, tokens), F32)], axis=0)
    parts = []
    run = jnp.zeros((8, 1), F32)
    for blk in range(tokens // LANES):
        piece = stacked[:, blk * LANES:(blk + 1) * LANES]
        parts.append(_dot(piece.astype(BF16), before_ref[...]) + run)
        run = run + jnp.sum(piece, axis=1, keepdims=True)
    earlier = jnp.concatenate(parts, axis=1)
    counts = [run[gi:gi + 1, :] for gi in range(ng)]
    pos = jnp.zeros_like(member[0])
    start = jnp.zeros_like(counts[0])
    for gi in range(ng):
        pos = pos + member[gi] * (start + earlier[gi:gi + 1, :])
        start = start + counts[gi]
    return pos, counts


def _outproj_body(x_ref, yp_ref, ya_ref, ys_ref, wgluf_ref, bglu_ref, gssm_ref, woutf_ref, g_ref, b_ref,
                  wr_ref, rb_ref, before_ref, h_ref, comb_ref, pos_ref, cnt_ref, ys_s, wglu_ref, wout_ref):
    @pl.when(pl.program_id(0) == 0)
    def _():
        wglu_ref[...] = wgluf_ref[...].astype(BF16)
        wout_ref[...] = woutf_ref[...].astype(BF16)

    chunks = ys_ref.shape[0]
    for t in range(SSM_CHUNK):
        for v in range(SSM_LANE_BLOCKS):
            ys_s[v, pl.ds(t, chunks, stride=SSM_CHUNK), :] = ys_ref[:, _ssm_piece(t, v)]
    def rows_block(sl):
        ys = jnp.concatenate([ys_s[v, sl, :] for v in range(SSM_LANE_BLOCKS)], axis=1)
        gate = jax.nn.sigmoid(_dot(ys.astype(BF16), wglu_ref[...]) + bglu_ref[...])
        ys = ys * gate
        r = lax.rsqrt(jnp.mean(ys * ys, axis=-1, keepdims=True) + NORM_EPS)
        ysn = (ys * r * gssm_ref[...]).astype(BF16)
        mix = _dot(jnp.concatenate([yp_ref[sl, :], ya_ref[sl, :], ysn], axis=1), wout_ref[...])
        h = _layer_norm(DN_ALPHA * x_ref[sl, :] + mix, g_ref[...], b_ref[...])
        h_ref[sl, :] = h
        return _dot(h.astype(BF16), wr_ref[...])

    tokens = x_ref.shape[0]
    parts = jnp.concatenate([rows_block(pl.ds(r0, OUT_ROWS)) for r0 in range(0, tokens, OUT_ROWS)], axis=0)
    parts_t = parts.T
    sc = jax.nn.sigmoid(parts_t[:N_EXPERTS, :] + parts_t[N_EXPERTS:2 * N_EXPERTS, :])
    bs = sc + rb_ref[...]
    scores = [sc[e:e + 1, :] for e in range(N_EXPERTS)]
    biased = [bs[e:e + 1, :] for e in range(N_EXPERTS)]
    comb_rows, best_idx = _route_rows(scores, biased)
    pos, counts = _group_sort_positions(best_idx, before_ref)
    comb_t = jnp.concatenate(comb_rows + [pos, jnp.zeros((LANES - N_EXPERTS - 1, tokens), F32)], axis=0)
    comb_ref[...] = comb_t.T
    pos_ref[...] = jnp.concatenate([pos, jnp.zeros((7, tokens), F32)], axis=0)
    cnt_ref[...] = jnp.concatenate(
        [jnp.broadcast_to(c, (1, LANES)) for c in counts]
        + [jnp.zeros((8 - len(counts), LANES), F32)], axis=0).astype(jnp.int32)


def _outproj(x, y_pool, y_attn, y_ssm_rows, w_glu, b_glu, g_ssm, w_out, ln_g, ln_b,
             w_router_split, router_bias, layer):
    t = x.shape[0]
    tm = MOE_TOKENS
    nt = t // tm
    row = lambda width: pl.BlockSpec((tm, width), lambda i: (i, 0))
    full = lambda a: pl.BlockSpec(a.shape, lambda i: (0,) * a.ndim)
    per_layer = lambda a: _layer_block(a, layer)
    once = lambda a: _layer_block(a, layer, pipeline_mode=pl.Buffered(1))
    token = np.arange(LANES)
    before = jnp.asarray(token[:, None] < token[None, :], BF16)
    return pl.pallas_call(
        _outproj_body,
        grid=(nt,),
        in_specs=[row(D_MODEL), row(POOL_WIDTH), row(ATTN_WIDTH),
                  pl.BlockSpec((tm // SSM_CHUNK, SSM_CHUNK * SSM_WIDTH), lambda i: (i, 0)),
                  once(w_glu), per_layer(b_glu), per_layer(g_ssm), once(w_out),
                  per_layer(ln_g), per_layer(ln_b), full(w_router_split), full(router_bias), full(before)],
        out_specs=[row(D_MODEL), row(LANES), pl.BlockSpec((8, tm), lambda i: (0, i)),
                   pl.BlockSpec((8, LANES), lambda i: (i, 0))],
        out_shape=[jax.ShapeDtypeStruct((t, D_MODEL), F32), jax.ShapeDtypeStruct((t, LANES), F32),
                   jax.ShapeDtypeStruct((8, t), F32), jax.ShapeDtypeStruct((8 * nt, LANES), jnp.int32)],
        scratch_shapes=[pltpu.VMEM((SSM_LANE_BLOCKS, tm, LANES), F32),
                        pltpu.VMEM(w_glu.shape[1:], BF16), pltpu.VMEM(w_out.shape[1:], BF16)],
        compiler_params=_cparams("arbitrary"),
        name="outproj",
    )(x, y_pool, y_attn, y_ssm_rows, w_glu, b_glu, g_ssm, w_out, ln_g, ln_b,
      w_router_split, router_bias, before)


def _moe_body(cnt_ref, h_ref, comb_ref, pos_ref, p_ref, wg_ref, wu_ref, wd_ref, wpgf_ref, wppf_ref, g_ref, b_ref,
              o_ref, hs_s, cs_s, acc_s, ple_s, wpg_ref, wpp_ref):
    i = pl.program_id(0)
    group = pl.program_id(1)
    ng = pl.num_programs(1)
    tm = h_ref.shape[0]

    @pl.when((i == 0) & (group == 0))
    def _():
        wpg_ref[...] = wpgf_ref[...].astype(BF16)
        wpp_ref[...] = wppf_ref[...].astype(BF16)

    @pl.when(group == 0)
    def _():
        hb = h_ref[...].astype(BF16)
        comb = comb_ref[...]
        comb_lo = comb - comb.astype(BF16).astype(F32)
        low_lanes = lax.broadcasted_iota(jnp.int32, (1, LANES), 1) < COMB_LO_LANE
        comb_b = jnp.where(low_lanes, comb, pltpu.roll(comb_lo, COMB_LO_LANE, 1)).astype(BF16)
        for r0 in range(0, tm, MOE_SIDE_ROWS):
            sl = pl.ds(r0, MOE_SIDE_ROWS)
            row = r0 + lax.broadcasted_iota(jnp.int32, (MOE_SIDE_ROWS, tm), 0)
            perm = jnp.where(pos_ref[0:1, :] == row.astype(F32), 1.0, 0.0).astype(BF16)
            hs_s[sl, :] = _dot(perm, hb).astype(BF16)
            both = _dot(perm, comb_b)
            cs_s[sl, :] = both + pltpu.roll(both, LANES - COMB_LO_LANE, 1)
        acc_s[...] = jnp.zeros_like(acc_s)

    count = cnt_ref[i * ng + group]
    start = jnp.int32(0)
    for gi in range(N_EXPERTS // EXPERTS_PER_GROUP - 1):
        start = start + jnp.where(group > gi, cnt_ref[i * ng + gi], 0)
    lane = lax.broadcasted_iota(jnp.int32, (1, LANES), 1)
    first = (start // BF16_ROW_PACK) * BF16_ROW_PACK
    windows = (start - first + count + MOE_WINDOW_ROWS - 1) // MOE_WINDOW_ROWS

    def window(w, carry):
        wanted = first + w * MOE_WINDOW_ROWS
        lo = pl.multiple_of(jnp.minimum(wanted, tm - MOE_WINDOW_ROWS), BF16_ROW_PACK)
        rows = pl.ds(lo, MOE_WINDOW_ROWS)
        x = hs_s[rows, :]
        fresh = lo + lax.broadcasted_iota(jnp.int32, (MOE_WINDOW_ROWS, 1), 0) >= wanted
        cs = jnp.where(fresh, cs_s[rows, :], 0.0)
        total = None
        for e in range(EXPERTS_PER_GROUP):
            gate = _dot(x, wg_ref[e])
            up = _dot(x, wu_ref[e])
            c = jnp.sum(jnp.where(lane == group * EXPERTS_PER_GROUP + e, cs, 0.0), axis=1, keepdims=True)
            a = (jax.nn.silu(gate) * up * c).astype(BF16)
            d = _dot(a, wd_ref[e])
            total = d if total is None else total + d
        acc_s[rows, :] += total
        return carry

    lax.fori_loop(0, windows, window, 0)

    half = tm // 2
    for mid in (1, 2):
        @pl.when(group == mid)
        def _():
            for r0 in range((mid - 1) * half, mid * half, MOE_SIDE_ROWS):
                sl = pl.ds(r0, MOE_SIDE_ROWS)
                hb = h_ref[sl, :].astype(BF16)
                ple_s[sl, :] = (jax.nn.sigmoid(_dot(hb, wpg_ref[...]))
                                * _dot(p_ref[sl, :].astype(BF16), wpp_ref[...]))

    @pl.when(group == ng - 1)
    def _():
        sorted_out = acc_s[...].astype(BF16)
        col = lax.broadcasted_iota(jnp.int32, (MOE_SIDE_ROWS, tm), 1).astype(F32)
        for r0 in range(0, tm, MOE_SIDE_ROWS):
            sl = pl.ds(r0, MOE_SIDE_ROWS)
            unperm = jnp.where(comb_ref[sl, SORT_POS_LANE:SORT_POS_LANE + 1] == col, 1.0, 0.0).astype(BF16)
            ffn = _dot(unperm, sorted_out)
            o_ref[sl, :] = _layer_norm(DN_ALPHA * h_ref[sl, :] + ffn + ple_s[sl, :], g_ref[...], b_ref[...])


def _moe(h, comb, pos_rows, counts, p_all, layer, wg_bf16, wu_bf16, wd_bf16, w_ple_gate, w_ple_proj, ln_g, ln_b):
    t = h.shape[0]
    tm = MOE_TOKENS
    nt = t // tm
    ng = N_EXPERTS // EXPERTS_PER_GROUP
    full = lambda a: _layer_block(a, layer)
    once = lambda a: _layer_block(a, layer, pipeline_mode=pl.Buffered(1))
    experts = lambda rows, cols: pl.BlockSpec((EXPERTS_PER_GROUP, rows, cols),
                                              lambda i, g, cnt: (layer * ng + g, 0, 0))
    grid_spec = pltpu.PrefetchScalarGridSpec(
        num_scalar_prefetch=1,
        grid=(nt, ng),
        in_specs=[pl.BlockSpec((tm, D_MODEL), lambda i, g, cnt: (i, 0)),
                  pl.BlockSpec((tm, LANES), lambda i, g, cnt: (i, 0)),
                  pl.BlockSpec((8, tm), lambda i, g, cnt: (0, i)),
                  pl.BlockSpec((tm, PLE_DIM), lambda i, g, cnt: (layer * nt + i, 0)),
                  experts(D_MODEL, D_EXPERT), experts(D_MODEL, D_EXPERT), experts(D_EXPERT, D_MODEL),
                  once(w_ple_gate), once(w_ple_proj), full(ln_g), full(ln_b)],
        out_specs=pl.BlockSpec((tm, D_MODEL), lambda i, g, cnt: (i, 0)),
        scratch_shapes=[pltpu.VMEM((tm, D_MODEL), BF16), pltpu.VMEM((tm, LANES), F32),
                        pltpu.VMEM((tm, D_MODEL), F32), pltpu.VMEM((tm, D_MODEL), F32),
                        pltpu.VMEM(w_ple_gate.shape[1:], BF16), pltpu.VMEM(w_ple_proj.shape[1:], BF16)])
    return pl.pallas_call(
        _moe_body,
        grid_spec=grid_spec,
        out_shape=jax.ShapeDtypeStruct((t, D_MODEL), F32),
        compiler_params=_cparams("arbitrary", "arbitrary"),
        name="moe",
    )(counts, h, comb, pos_rows, p_all, wg_bf16, wu_bf16, wd_bf16, w_ple_gate, w_ple_proj, ln_g, ln_b)


def _block_diag(w):
    g, n, m = w.shape
    eye = jnp.eye(g, dtype=w.dtype)
    return (eye[:, None, :, None] * w[:, :, None, :]).reshape(g * n, g * m)


def kernel(x, p, w_in, w_out, w_pool, pool_scale, rel_bias, ssm_a_re, ssm_a_im, ssm_log_dt, ssm_b_re, ssm_b_im,
           ssm_c_re, ssm_c_im, ssm_d, w_glu, b_glu, g_pool, g_attn, g_ssm, ln1_g, ln1_b, ln2_g, ln2_b,
           w_router, router_bias, w_exp_gate, w_exp_up, w_exp_down, w_ple_gate, w_ple_proj):
    batch, seq, d = x.shape
    t = batch * seq
    xt = x.reshape(t, d)
    p_all = p.reshape(DEPTH * t, PLE_DIM)

    vec = lambda a: a.astype(F32).reshape(DEPTH, 1, -1)
    f32 = lambda a: a.astype(F32)
    w_pool_b = jax.vmap(_block_diag)(w_pool).astype(BF16)
    bias_rows = _attn_bias_rows(rel_bias)
    tables = jax.vmap(_s5_tables)(ssm_a_re, ssm_a_im, ssm_log_dt, ssm_b_re, ssm_b_im, ssm_c_re, ssm_c_im)
    wr = w_router.astype(F32)
    wr_hi = wr.astype(BF16)
    wr_lo = (wr - wr_hi.astype(F32)).astype(BF16)
    wr_split = jnp.pad(jnp.concatenate([wr_hi, wr_lo], axis=1), ((0, 0), (0, LANES - 2 * N_EXPERTS)))
    r_bias = router_bias.astype(F32).reshape(N_EXPERTS, 1)
    stack_experts = lambda w: w.astype(BF16).reshape((DEPTH * N_EXPERTS,) + w.shape[2:])
    wg_b, wu_b, wd_b = stack_experts(w_exp_gate), stack_experts(w_exp_up), stack_experts(w_exp_down)
    ng = N_EXPERTS // EXPERTS_PER_GROUP

    for layer in range(DEPTH):
        y_pool, q, k, v, u_ssm = _inproj(xt, f32(w_in), w_pool_b, vec(pool_scale), vec(g_pool), layer, batch)
        y_attn = _attention(q, k, v, bias_rows, vec(g_attn), layer, batch)
        y_ssm = _s5(u_ssm, tables, vec(ssm_d), layer, batch)
        h, comb, pos_rows, cnt = _outproj(xt, y_pool, y_attn, y_ssm, f32(w_glu), vec(b_glu), vec(g_ssm),
                                          f32(w_out), vec(ln1_g), vec(ln1_b), wr_split, r_bias, layer)
        counts = cnt[:, 0].reshape(-1, 8)[:, :ng].reshape(-1)
        xt = _moe(h, comb, pos_rows, counts, p_all, layer, wg_b, wu_b, wd_b, f32(w_ple_gate), f32(w_ple_proj),
                  vec(ln2_g), vec(ln2_b))
    return xt.reshape(batch, seq, d)
```

```python
import functools
import math

import numpy as np
import jax
import jax.numpy as jnp
from jax import lax
from jax.experimental import pallas as pl
from jax.experimental.pallas import tpu as pltpu

F32 = jnp.float32
BF16 = jnp.bfloat16

D_MODEL = 1024
DEPTH = 2
CHUNK = 64
PLE_DIM = 256
POOL_WIDTH = 256
POOL_GROUP_DIM = 64
POOL_WINDOWS = (2, 4, 8, 16)
POOL_HALO = 32
ATTN_HEAD_DIM = 64
ATTN_HEADS = 6
ATTN_WIDTH = 384
N_PREV_CHUNKS = 8
REL_CLIP = 128
SSM_WIDTH = 384
SSM_GROUP_DIM = 16
SSM_GROUPS = 24
SSM_STATE = 64
N_EXPERTS = 16
EXPERTS_PER_GROUP = 4
D_EXPERT = 256
DN_ALPHA = (2 * DEPTH) ** 0.25
NORM_EPS = 1e-5
LOG2_E = math.log2(math.e)

LANES = 128
VMEM_LIMIT_BYTES = 56 * 1024 * 1024

INPROJ_TOKENS = 1024
INPROJ_ROWS = 512
ATTN_Q_CHUNKS = 4
ATTN_Q_TOKENS = ATTN_Q_CHUNKS * CHUNK
ATTN_BAND_TOKENS = 3 * ATTN_Q_TOKENS
ATTN_BLOCKS_PER_STEP = 4
ATTN_PREV_TOKENS = N_PREV_CHUNKS * CHUNK
ATTN_BIAS_ROW = 1024
SSM_CHUNK = 16
SSM_CHUNK_WIDTH = SSM_CHUNK * SSM_GROUP_DIM
SSM_ROWS = 128
SSM_SLOTS = LANES // SSM_GROUP_DIM
SSM_LANE_BLOCKS = SSM_WIDTH // LANES
SSM_TIME_BLOCKS = SSM_CHUNK // SSM_SLOTS
SSM_PAIR_WIDTH = 2 * SSM_STATE
SSM_STATE_LANES = SSM_GROUPS * SSM_STATE
OUT_ROWS = 256
MOE_SIDE_ROWS = 256
MOE_TOKENS = 1024
MOE_WINDOW_ROWS = 320
BF16_ROW_PACK = 16
SORT_POS_LANE = N_EXPERTS
COMB_LO_LANE = 32


def _cparams(*sem):
    return pltpu.CompilerParams(dimension_semantics=sem, vmem_limit_bytes=VMEM_LIMIT_BYTES)


def _dot(a, b):
    return jnp.dot(a, b, preferred_element_type=F32)


def _layer_block(a, layer, **kwargs):
    return pl.BlockSpec((None,) + a.shape[1:], lambda *_: (layer,) + (0,) * (a.ndim - 1), **kwargs)


def _layer_norm(v, g, b):
    mu = jnp.mean(v, axis=-1, keepdims=True)
    vc = v - mu
    var = jnp.mean(vc * vc, axis=-1, keepdims=True)
    return vc * lax.rsqrt(var + NORM_EPS) * g + b


def _ssm_piece(t, v):
    lo = SSM_WIDTH * t + LANES * v
    return slice(lo, lo + LANES)


def _pool_mix(x0, buf, lvl_a, lvl_b, pos, w_ref, scale_ref, g_ref):
    n = x0.shape[0] + POOL_HALO
    group = lax.broadcasted_iota(jnp.int32, (1, POOL_WIDTH), 1) // POOL_GROUP_DIM
    mean = jnp.zeros_like(x0)
    src, dst = buf, lvl_a
    for gi, w in enumerate(POOL_WINDOWS):
        lo = 8 * (gi + 1)
        dst[lo:n, :] = src[lo:n, :] + src[lo - w // 2:n - w // 2, :]
        inv_cnt = 1.0 / jnp.minimum(pos + 1, w).astype(F32)
        mean = jnp.where(group == gi, dst[POOL_HALO:n, :] * inv_cnt, mean)
        src, dst = dst, (lvl_b if dst is lvl_a else lvl_a)
    d = (mean - x0).astype(BF16)
    y = _dot(d, w_ref[...]) * scale_ref[...]
    r = lax.rsqrt(jnp.mean(y * y, axis=-1, keepdims=True) + NORM_EPS)
    return (y * r * g_ref[...]).astype(BF16)


def _inproj_body(x_ref, wf_ref, wpool_ref, pscale_ref, pgain_ref, yp_ref, q_ref, k_ref, v_ref, us_ref,
                 w_ref, zs, buf, lvl_a, lvl_b, *, tiles_per_seq):
    tm = x_ref.shape[0]
    tile_in_seq = pl.program_id(0) % tiles_per_seq

    @pl.when(pl.program_id(0) == 0)
    def _():
        w_ref[...] = wf_ref[...].astype(BF16)

    @pl.when(tile_in_seq == 0)
    def _():
        buf[0:POOL_HALO, :] = jnp.zeros((POOL_HALO, POOL_WIDTH), F32)

    for r0 in range(0, tm, INPROJ_ROWS):
        sl = pl.ds(r0, INPROJ_ROWS)
        xb = x_ref[sl, :].astype(BF16)

        def cols(lo, hi):
            return _dot(xb, w_ref[:, lo:hi])

        buf[pl.ds(POOL_HALO + r0, INPROJ_ROWS), :] = cols(0, 256)
        qk = cols(256, 768)
        q_ref[sl, :] = (qk[:, :ATTN_WIDTH] * (ATTN_HEAD_DIM ** -0.5 * LOG2_E)).astype(BF16)
        k_ref[sl, :LANES] = qk[:, ATTN_WIDTH:].astype(BF16)
        k_ref[sl, LANES:] = cols(768, 1024).astype(BF16)
        vs = cols(1024, 1536)
        v_ref[sl, :] = vs[:, :ATTN_WIDTH].astype(BF16)
        zs[0, sl, :] = vs[:, ATTN_WIDTH:]
        s_rest = cols(1536, 1792)
        zs[1, sl, :] = s_rest[:, :LANES]
        zs[2, sl, :] = s_rest[:, LANES:]

    u_pool = buf[POOL_HALO:, :]
    pos = tile_in_seq * tm + lax.broadcasted_iota(jnp.int32, (tm, 1), 0)
    yp_ref[...] = _pool_mix(u_pool, buf, lvl_a, lvl_b, pos, wpool_ref, pscale_ref, pgain_ref)
    buf[0:POOL_HALO, :] = buf[tm:, :]

    chunks = us_ref.shape[0]
    for t in range(SSM_CHUNK):
        for v in range(SSM_LANE_BLOCKS):
            us_ref[:, _ssm_piece(t, v)] = zs[v, pl.ds(t, chunks, stride=SSM_CHUNK), :]


def _inproj(x, w_in, w_pool_blockdiag_bf16, pool_scale, pool_gain, layer, batch):
    t = x.shape[0]
    tm = INPROJ_TOKENS
    row = lambda width: pl.BlockSpec((tm, width), lambda i: (i, 0))
    full = lambda a: _layer_block(a, layer)
    chunk_rows = pl.BlockSpec((tm // SSM_CHUNK, SSM_CHUNK * SSM_WIDTH), lambda i: (i, 0))
    return pl.pallas_call(
        functools.partial(_inproj_body, tiles_per_seq=t // batch // tm),
        grid=(t // tm,),
        in_specs=[row(D_MODEL), _layer_block(w_in, layer, pipeline_mode=pl.Buffered(1)),
                  full(w_pool_blockdiag_bf16), full(pool_scale), full(pool_gain)],
        out_specs=[row(POOL_WIDTH), row(ATTN_WIDTH), row(ATTN_WIDTH), row(ATTN_WIDTH), chunk_rows],
        out_shape=[jax.ShapeDtypeStruct((t, POOL_WIDTH), BF16),
                   jax.ShapeDtypeStruct((t, ATTN_WIDTH), BF16),
                   jax.ShapeDtypeStruct((t, ATTN_WIDTH), BF16),
                   jax.ShapeDtypeStruct((t, ATTN_WIDTH), BF16),
                   jax.ShapeDtypeStruct((t // SSM_CHUNK, SSM_CHUNK * SSM_WIDTH), F32)],
        scratch_shapes=[pltpu.VMEM(w_in.shape[1:], BF16), pltpu.VMEM((SSM_LANE_BLOCKS, tm, LANES), F32)]
        + [pltpu.VMEM((POOL_HALO + tm, POOL_WIDTH), F32)] * 3,
        compiler_params=_cparams("arbitrary"),
        name="inproj",
    )(x, w_in, w_pool_blockdiag_bf16, pool_scale, pool_gain)


def _attn_bias_rows(rel_bias):
    x = np.arange(ATTN_BIAS_ROW)
    x = np.where(x < ATTN_BAND_TOKENS, x, x - ATTN_BIAS_ROW)
    idx = np.clip(N_PREV_CHUNKS * CHUNK - x, -REL_CLIP, REL_CLIP) + REL_CLIP
    return rel_bias.astype(F32)[..., idx] * LOG2_E


def _attn_body(q_ref, kprev_ref, kcur_ref, vprev_ref, vcur_ref, rows_ref, g_ref, o_ref, bias_s):
    b = pl.program_id(0)
    i = pl.program_id(1)
    tq = ATTN_Q_TOKENS

    @pl.when((b == 0) & (i == 0))
    def _():
        qc = lax.broadcasted_iota(jnp.int32, (tq, ATTN_BAND_TOKENS), 0) // CHUNK
        kc = lax.broadcasted_iota(jnp.int32, (tq, ATTN_BAND_TOKENS), 1) // CHUNK
        in_band = (kc >= qc) & (kc <= qc + N_PREV_CHUNKS)
        for head in range(ATTN_HEADS):
            full = jnp.broadcast_to(rows_ref[head:head + 1, :], (tq, ATTN_BIAS_ROW))
            shifted = pltpu.roll(full, 0, 1, stride=1, stride_axis=0)
            bias_s[head // 2, (head % 2) * tq:(head % 2 + 1) * tq, :] = jnp.where(
                in_band, shifted[:, :ATTN_BAND_TOKENS], -jnp.inf)

    upper_half = lax.broadcasted_iota(jnp.int32, (1, LANES), 1) >= ATTN_HEAD_DIM

    def band(prev_ref, cur_ref, blk, sl):
        start = blk * tq
        if start < ATTN_PREV_TOKENS:
            return jnp.concatenate([prev_ref[start:, sl], cur_ref[0:start + tq, sl]], axis=0)
        return cur_ref[start - ATTN_PREV_TOKENS:start + tq, sl]

    def heads(blk, masked_keys):
        rows = pl.ds(blk * tq, tq)
        outs = []
        for pair in range(ATTN_WIDTH // LANES):
            sl = slice(pair * LANES, (pair + 1) * LANES)
            qp = q_ref[rows, sl]
            kp = band(kprev_ref, kcur_ref, blk, sl)
            vp = band(vprev_ref, vcur_ref, blk, sl)
            zero = jnp.zeros_like(qp)
            q2 = jnp.concatenate([jnp.where(upper_half, zero, qp), jnp.where(upper_half, qp, zero)], axis=0)
            s = lax.dot_general(q2, kp, (((1,), (1,)), ((), ())), preferred_element_type=F32)
            s = s + bias_s[pair]
            if masked_keys:
                key = lax.broadcasted_iota(jnp.int32, (1, ATTN_BAND_TOKENS), 1)
                s = jnp.where(key < masked_keys, -jnp.inf, s)
            m = jnp.max(s, axis=-1, keepdims=True)
            p = jnp.exp2(s - m)
            l = jnp.sum(p, axis=-1, keepdims=True)
            o = _dot(p.astype(BF16), vp) * (1.0 / l)
            outs.append(jnp.where(upper_half, o[tq:, :], o[:tq, :]))
        ss = sum(jnp.sum(o * o, axis=-1, keepdims=True) for o in outs)
        r = lax.rsqrt(ss / ATTN_WIDTH + NORM_EPS)
        for pair, o in enumerate(outs):
            sl = slice(pair * LANES, (pair + 1) * LANES)
            o_ref[rows, sl] = (o * r * g_ref[:, sl]).astype(BF16)

    @pl.when(i == 0)
    def _():
        for blk in range(ATTN_BLOCKS_PER_STEP):
            heads(blk, max(ATTN_PREV_TOKENS - blk * tq, 0))

    @pl.when(i > 0)
    def _():
        for blk in range(ATTN_BLOCKS_PER_STEP):
            heads(blk, 0)


def _attention(q, k, v, bias_rows, gain, layer, batch):
    t = q.shape[0]
    ts = ATTN_BLOCKS_PER_STEP * ATTN_Q_TOKENS
    steps = t // batch // ts
    prev_per_step = ts // ATTN_PREV_TOKENS

    cur = pl.BlockSpec((ts, ATTN_WIDTH), lambda b, i: (b * steps + i, 0))
    prev = pl.BlockSpec((ATTN_PREV_TOKENS, ATTN_WIDTH),
                        lambda b, i: (jnp.maximum((b * steps + i) * prev_per_step - 1, 0), 0))

    return pl.pallas_call(
        _attn_body,
        grid=(batch, steps),
        in_specs=[cur, prev, cur, prev, cur,
                  _layer_block(bias_rows, layer), _layer_block(gain, layer)],
        out_specs=cur,
        out_shape=jax.ShapeDtypeStruct((t, ATTN_WIDTH), BF16),
        scratch_shapes=[pltpu.VMEM((ATTN_HEADS // 2, 2 * ATTN_Q_TOKENS, ATTN_BAND_TOKENS), F32)],
        compiler_params=_cparams("arbitrary", "arbitrary"),
        name="attention",
    )(q, k, k, v, v, bias_rows, gain)


def _s5_position_of_time(g, time):
    return SSM_SLOTS * (time // SSM_SLOTS) + (time % SSM_SLOTS + g) % SSM_SLOTS


def _s5_tables(a_re, a_im, log_dt, b_re, b_im, c_re, c_im):
    hi = lax.Precision.HIGHEST
    tc = SSM_CHUNK
    g, p_dim = a_re.shape
    dt = jnp.exp(log_dt.astype(F32))[:, None]
    ar = a_re.astype(F32)
    ai = a_im.astype(F32)
    mag = jnp.exp(ar * dt)
    abar_re = mag * jnp.cos(ai * dt)
    abar_im = mag * jnp.sin(ai * dt)
    den = ar * ar + ai * ai
    nr = abar_re - 1.0
    ni = abar_im
    coef_re = ((nr * ar + ni * ai) / den)[..., None]
    coef_im = ((ni * ar - nr * ai) / den)[..., None]
    br = b_re.astype(F32)
    bi = b_im.astype(F32)
    bbar_re = coef_re * br - coef_im * bi
    bbar_im = coef_re * bi + coef_im * br
    n = jnp.arange(tc + 1, dtype=F32)
    pmag = jnp.exp((ar * dt)[..., None] * n)
    pw_re = pmag * jnp.cos((ai * dt)[..., None] * n)
    pw_im = pmag * jnp.sin((ai * dt)[..., None] * n)
    cw = tc * SSM_GROUP_DIM
    lag_rep = jnp.asarray(np.kron(np.eye(tc), np.ones((1, SSM_GROUP_DIM))), F32)
    ch_rep = jnp.asarray(np.kron(np.ones((1, tc)), np.eye(SSM_GROUP_DIM)), F32)
    expand = lambda a, rep: jnp.einsum('gpn,nx->gpx', a, rep, precision=hi)
    c_re_rep = expand(c_re.astype(F32).transpose(0, 2, 1), ch_rep)
    c_im_rep = expand(c_im.astype(F32).transpose(0, 2, 1), ch_rep)

    def output_coefficients(first_power, rep, subscripts):
        p_re = jnp.einsum(subscripts, pw_re[..., first_power:first_power + tc], rep, precision=hi)
        p_im = jnp.einsum(subscripts, pw_im[..., first_power:first_power + tc], rep, precision=hi)
        return c_re_rep * p_re - c_im_rep * p_im, -(c_re_rep * p_im + c_im_rep * p_re)

    on_re, on_im = output_coefficients(0, lag_rep, 'gpn,nx->gpx')
    kern = (jnp.einsum('gpk,gpx->gkx', bbar_re, on_re, precision=hi)
            + jnp.einsum('gpk,gpx->gkx', bbar_im, on_im, precision=hi))
    position = np.arange(tc)
    time_at = (SSM_SLOTS * (position // SSM_SLOTS)
               + (position % SSM_SLOTS - np.arange(g)[:, None]) % SSM_SLOTS)
    slot_rep = np.repeat(time_at[:, None, :] == np.arange(tc)[None, :, None], SSM_GROUP_DIM, axis=2)
    inter = jnp.concatenate(output_coefficients(1, jnp.asarray(slot_rep, F32), 'gpn,gnx->gpx'), axis=1)
    twice = lambda a: jnp.concatenate([a, a], -1)
    pt_re = twice(pw_re[..., :tc].transpose(0, 2, 1))[:, :, None, :]
    pt_im = twice(pw_im[..., :tc].transpose(0, 2, 1))[:, :, None, :]
    bt_re = bbar_re.transpose(0, 2, 1)[:, None]
    bt_im = bbar_im.transpose(0, 2, 1)[:, None]
    side = lambda a, b: jnp.concatenate([a, b], -1)
    est = (pt_re * side(bt_re, bt_im) + pt_im * side(-bt_im, bt_re)).reshape(g, cw, 2 * p_dim)
    est_swapped = (pt_re * side(bt_im, bt_re) + pt_im * side(bt_re, -bt_im)).reshape(g, cw, 2 * p_dim)
    return dict(
        kern=kern,
        est=est,
        est_swapped=est_swapped,
        inter=inter,
        apow_re=pw_re[..., tc].reshape(1, g * p_dim), apow_im=pw_im[..., tc].reshape(1, g * p_dim))


def _s5_prepare(kern_ref, est_ref, estsw_ref, int_ref, toep_s, est_s, int_s):
    lane = lax.broadcasted_iota(jnp.int32, (1, LANES), 1)
    slot = lane // SSM_GROUP_DIM
    zero = jnp.zeros((SSM_GROUP_DIM, LANES), F32)
    zero_rows = jnp.zeros((SSM_STATE, SSM_CHUNK_WIDTH), BF16)
    for g in range(SSM_GROUPS):
        mine = (lane >= SSM_STATE) if g % 2 else (lane < SSM_STATE)
        turn = g % SSM_SLOTS
        time_slot = (slot - turn) % SSM_SLOTS
        k0 = [kern_ref[g, :, 0:LANES]]
        k1 = [kern_ref[g, :, LANES:2 * LANES]]
        for r in range(1, SSM_SLOTS):
            k0.append(pltpu.roll(k0[0], SSM_GROUP_DIM * r, 1))
            k1.append(pltpu.roll(k1[0], SSM_GROUP_DIM * r, 1))
        for time in range(SSM_CHUNK):
            r = (time + turn) % SSM_SLOTS
            later = time_slot >= time % SSM_SLOTS
            if time < SSM_SLOTS:
                h0 = jnp.where(later, k0[r], zero)
                h1 = jnp.where(later, k1[r], k0[r])
            else:
                h0 = zero
                h1 = jnp.where(later, k0[r], zero)
            rows = pl.ds(SSM_GROUP_DIM * _s5_position_of_time(g, time), SSM_GROUP_DIM)
            toep_s[g, rows, 0:LANES] = h0.astype(BF16)
            toep_s[g, rows, LANES:2 * LANES] = h1.astype(BF16)
            src = pl.ds(SSM_GROUP_DIM * (SSM_CHUNK - 1 - time), SSM_GROUP_DIM)
            e, e_swapped = est_ref[g, src, :], estsw_ref[g, src, :]
            e_re, e_im = (e_swapped, e) if g % 2 else (e, e_swapped)
            est_s[g, rows, 0:LANES] = jnp.where(mine, e_re, zero).astype(BF16)
            est_s[g, rows, LANES:2 * LANES] = jnp.where(mine, e_im, zero).astype(BF16)
        for part in range(2):
            src = pl.ds(part * SSM_STATE, SSM_STATE)
            base = part * SSM_PAIR_WIDTH
            int_s[g, pl.ds(base + (1 - g % 2) * SSM_STATE, SSM_STATE), :] = zero_rows
            int_s[g, pl.ds(base + (g % 2) * SSM_STATE, SSM_STATE), :] = int_ref[g, src, :].astype(BF16)


def _s5_body(u_ref, kern_ref, estin_ref, estswin_ref, intin_ref, apre_ref, apim_ref, d_ref, y_ref,
             toep_ref, est_ref, int_ref, ub_s, ere_s, eim_s, spre_s, spim_s, sre_s, sim_s):
    rows = u_ref.shape[0]

    @pl.when((pl.program_id(0) == 0) & (pl.program_id(1) == 0))
    def _():
        _s5_prepare(kern_ref, estin_ref, estswin_ref, intin_ref, toep_ref, est_ref, int_ref)

    @pl.when(pl.program_id(1) == 0)
    def _():
        sre_s[...] = jnp.zeros_like(sre_s)
        sim_s[...] = jnp.zeros_like(sim_s)

    slot = lax.broadcasted_iota(jnp.int32, (1, LANES), 1) // SSM_GROUP_DIM
    slot_bits = [(b, (slot & b) != 0) for b in (1, 2, 4)]
    piece = _ssm_piece

    for v in range(SSM_LANE_BLOCKS):
        for m in range(SSM_TIME_BLOCKS):
            rot = []
            for j in range(SSM_SLOTS):
                a = u_ref[:, piece(SSM_SLOTS * m + j, v)]
                rot.append(a if j == 0 else pltpu.roll(a, SSM_GROUP_DIM * j, 1))
            for bit in slot_bits:
                rot = [jnp.where(bit[1], rot[(i + bit[0]) % SSM_SLOTS], rot[i]) for i in range(SSM_SLOTS)]
            for gam in range(SSM_SLOTS):
                ub_s[SSM_SLOTS * v + gam, :, m * LANES:(m + 1) * LANES] = rot[-gam % SSM_SLOTS].astype(BF16)

    for q in range(SSM_GROUPS // 2):
        e = _dot(ub_s[2 * q], est_ref[2 * q]) + _dot(ub_s[2 * q + 1], est_ref[2 * q + 1])
        ere_s[:, q * LANES:(q + 1) * LANES] = e[:, :SSM_PAIR_WIDTH]
        eim_s[:, q * LANES:(q + 1) * LANES] = e[:, SSM_PAIR_WIDTH:]

    a_re = apre_ref[...]
    a_im = apim_ref[...]

    def carry_step(r, carry):
        s_re, s_im = carry
        spre_s[pl.ds(r, 1), :] = s_re
        spim_s[pl.ds(r, 1), :] = s_im
        e_re = ere_s[pl.ds(r, 1), :]
        e_im = eim_s[pl.ds(r, 1), :]
        return (a_re * s_re - a_im * s_im + e_re, a_re * s_im + a_im * s_re + e_im)

    s_re, s_im = lax.fori_loop(0, rows, carry_step, (sre_s[...], sim_s[...]))
    sre_s[...] = s_re
    sim_s[...] = s_im

    for v in range(SSM_LANE_BLOCKS):
        yg = []
        for gam in range(SSM_SLOTS):
            g = SSM_SLOTS * v + gam
            q = g // 2
            sp = jnp.concatenate([spre_s[:, q * LANES:(q + 1) * LANES], spim_s[:, q * LANES:(q + 1) * LANES]],
                                 axis=1).astype(BF16)
            yg.append(_dot(ub_s[g], toep_ref[g]) + _dot(sp, int_ref[g]))
        d = d_ref[:, v * LANES:(v + 1) * LANES]
        for m in range(SSM_TIME_BLOCKS):
            back = [yg[-i % SSM_SLOTS][:, m * LANES:(m + 1) * LANES] for i in range(SSM_SLOTS)]
            for bit in slot_bits:
                back = [jnp.where(bit[1], back[(i - bit[0]) % SSM_SLOTS], back[i]) for i in range(SSM_SLOTS)]
            for j in range(SSM_SLOTS):
                o = back[j]
                if j:
                    o = pltpu.roll(o, LANES - SSM_GROUP_DIM * j, 1)
                sl = piece(SSM_SLOTS * m + j, v)
                y_ref[:, sl] = jax.nn.gelu(o + d * u_ref[:, sl])


def _s5(u_rows, tab, d_skip, layer, batch):
    nch, width = u_rows.shape
    rows = SSM_ROWS
    steps = nch // batch // rows
    once = lambda a: _layer_block(a, layer, pipeline_mode=pl.Buffered(1))
    blk = pl.BlockSpec((rows, width), lambda b, i: (b * steps + i, 0))
    table = pltpu.VMEM((SSM_GROUPS, SSM_CHUNK_WIDTH, SSM_CHUNK_WIDTH), BF16)
    state = pltpu.VMEM((rows, SSM_STATE_LANES), F32)
    carry = pltpu.VMEM((1, SSM_STATE_LANES), F32)
    return pl.pallas_call(
        _s5_body,
        grid=(batch, steps),
        in_specs=[blk, once(tab['kern']), once(tab['est']), once(tab['est_swapped']), once(tab['inter']),
                  once(tab['apow_re']), once(tab['apow_im']), once(d_skip)],
        out_specs=blk,
        out_shape=jax.ShapeDtypeStruct((nch, width), F32),
        scratch_shapes=[table, table, table,
                        pltpu.VMEM((SSM_GROUPS, rows, SSM_CHUNK_WIDTH), BF16), state, state, state, state,
                        carry, carry],
        compiler_params=_cparams("arbitrary", "arbitrary"),
        name="s5",
    )(u_rows, tab['kern'], tab['est'], tab['est_swapped'], tab['inter'], tab['apow_re'], tab['apow_im'], d_skip)


def _route_rows(scores, biased):
    ng = N_EXPERTS // EXPERTS_PER_GROUP
    group_score = []
    for gi in range(ng):
        a, b, c, d = biased[gi * EXPERTS_PER_GROUP:(gi + 1) * EXPERTS_PER_GROUP]
        hi1, lo1 = jnp.maximum(a, b), jnp.minimum(a, b)
        hi2, lo2 = jnp.maximum(c, d), jnp.minimum(c, d)
        top1 = jnp.maximum(hi1, hi2)
        top2 = jnp.maximum(jnp.minimum(hi1, hi2), jnp.maximum(lo1, lo2))
        group_score.append(top1 + top2)
    best = group_score[0]
    best_idx = jnp.zeros_like(best, dtype=jnp.int32)
    for gi in range(1, ng):
        better = group_score[gi] > best
        best = jnp.where(better, group_score[gi], best)
        best_idx = jnp.where(better, gi, best_idx)
    picked = []
    for e in range(N_EXPERTS):
        gi = e // EXPERTS_PER_GROUP
        rank = jnp.zeros_like(best_idx)
        for o in range(gi * EXPERTS_PER_GROUP, (gi + 1) * EXPERTS_PER_GROUP):
            if o == e:
                continue
            ahead = (biased[o] > biased[e]) | ((biased[o] == biased[e]) & (o < e))
            rank = rank + ahead.astype(jnp.int32)
        picked.append((best_idx == gi) & (rank < 2))
    wsum = sum(jnp.where(picked[e], scores[e], 0.0) for e in range(N_EXPERTS))
    return [jnp.where(picked[e], scores[e] / wsum, 0.0) for e in range(N_EXPERTS)], best_idx


def _group_sort_positions(best_idx, before_ref):
    ng = N_EXPERTS // EXPERTS_PER_GROUP
    tokens = best_idx.shape[1]
    member = [(best_idx == gi).astype(F32) for gi in range(ng)]
    stacked = jnp.concatenate(member + [jnp.zeros((8 - ng, tokens), F32)], axis=0)
    parts = []
    run = jnp.zeros((8, 1), F32)
    for blk in range(tokens // LANES):
        piece = stacked[:, blk * LANES:(blk + 1) * LANES]
        parts.append(_dot(piece.astype(BF16), before_ref[...]) + run)
        run = run + jnp.sum(piece, axis=1, keepdims=True)
    earlier = jnp.concatenate(parts, axis=1)
    counts = [run[gi:gi + 1, :] for gi in range(ng)]
    pos = jnp.zeros_like(member[0])
    start = jnp.zeros_like(counts[0])
    for gi in range(ng):
        pos = pos + member[gi] * (start + earlier[gi:gi + 1, :])
        start = start + counts[gi]
    return pos, counts


def _outproj_body(x_ref, yp_ref, ya_ref, ys_ref, wgluf_ref, bglu_ref, gssm_ref, woutf_ref, g_ref, b_ref,
                  wr_ref, rb_ref, before_ref, h_ref, comb_ref, pos_ref, cnt_ref, ys_s, wglu_ref, wout_ref):
    @pl.when(pl.program_id(0) == 0)
    def _():
        wglu_ref[...] = wgluf_ref[...].astype(BF16)
        wout_ref[...] = woutf_ref[...].astype(BF16)

    chunks = ys_ref.shape[0]
    for t in range(SSM_CHUNK):
        for v in range(SSM_LANE_BLOCKS):
            ys_s[v, pl.ds(t, chunks, stride=SSM_CHUNK), :] = ys_ref[:, _ssm_piece(t, v)]
    def rows_block(sl):
        ys = jnp.concatenate([ys_s[v, sl, :] for v in range(SSM_LANE_BLOCKS)], axis=1)
        gate = jax.nn.sigmoid(_dot(ys.astype(BF16), wglu_ref[...]) + bglu_ref[...])
        ys = ys * gate
        r = lax.rsqrt(jnp.mean(ys * ys, axis=-1, keepdims=True) + NORM_EPS)
        ysn = (ys * r * gssm_ref[...]).astype(BF16)
        mix = _dot(jnp.concatenate([yp_ref[sl, :], ya_ref[sl, :], ysn], axis=1), wout_ref[...])
        h = _layer_norm(DN_ALPHA * x_ref[sl, :] + mix, g_ref[...], b_ref[...])
        h_ref[sl, :] = h
        return _dot(h.astype(BF16), wr_ref[...])

    tokens = x_ref.shape[0]
    parts = jnp.concatenate([rows_block(pl.ds(r0, OUT_ROWS)) for r0 in range(0, tokens, OUT_ROWS)], axis=0)
    parts_t = parts.T
    sc = jax.nn.sigmoid(parts_t[:N_EXPERTS, :] + parts_t[N_EXPERTS:2 * N_EXPERTS, :])
    bs = sc + rb_ref[...]
    scores = [sc[e:e + 1, :] for e in range(N_EXPERTS)]
    biased = [bs[e:e + 1, :] for e in range(N_EXPERTS)]
    comb_rows, best_idx = _route_rows(scores, biased)
    pos, counts = _group_sort_positions(best_idx, before_ref)
    comb_t = jnp.concatenate(comb_rows + [pos, jnp.zeros((LANES - N_EXPERTS - 1, tokens), F32)], axis=0)
    comb_ref[...] = comb_t.T
    pos_ref[...] = jnp.concatenate([pos, jnp.zeros((7, tokens), F32)], axis=0)
    cnt_ref[...] = jnp.concatenate(
        [jnp.broadcast_to(c, (1, LANES)) for c in counts]
        + [jnp.zeros((8 - len(counts), LANES), F32)], axis=0).astype(jnp.int32)


def _outproj(x, y_pool, y_attn, y_ssm_rows, w_glu, b_glu, g_ssm, w_out, ln_g, ln_b,
             w_router_split, router_bias, layer):
    t = x.shape[0]
    tm = MOE_TOKENS
    nt = t // tm
    row = lambda width: pl.BlockSpec((tm, width), lambda i: (i, 0))
    full = lambda a: pl.BlockSpec(a.shape, lambda i: (0,) * a.ndim)
    per_layer = lambda a: _layer_block(a, layer)
    once = lambda a: _layer_block(a, layer, pipeline_mode=pl.Buffered(1))
    token = np.arange(LANES)
    before = jnp.asarray(token[:, None] < token[None, :], BF16)
    return pl.pallas_call(
        _outproj_body,
        grid=(nt,),
        in_specs=[row(D_MODEL), row(POOL_WIDTH), row(ATTN_WIDTH),
                  pl.BlockSpec((tm // SSM_CHUNK, SSM_CHUNK * SSM_WIDTH), lambda i: (i, 0)),
                  once(w_glu), per_layer(b_glu), per_layer(g_ssm), once(w_out),
                  per_layer(ln_g), per_layer(ln_b), full(w_router_split), full(router_bias), full(before)],
        out_specs=[row(D_MODEL), row(LANES), pl.BlockSpec((8, tm), lambda i: (0, i)),
                   pl.BlockSpec((8, LANES), lambda i: (i, 0))],
        out_shape=[jax.ShapeDtypeStruct((t, D_MODEL), F32), jax.ShapeDtypeStruct((t, LANES), F32),
                   jax.ShapeDtypeStruct((8, t), F32), jax.ShapeDtypeStruct((8 * nt, LANES), jnp.int32)],
        scratch_shapes=[pltpu.VMEM((SSM_LANE_BLOCKS, tm, LANES), F32),
                        pltpu.VMEM(w_glu.shape[1:], BF16), pltpu.VMEM(w_out.shape[1:], BF16)],
        compiler_params=_cparams("arbitrary"),
        name="outproj",
    )(x, y_pool, y_attn, y_ssm_rows, w_glu, b_glu, g_ssm, w_out, ln_g, ln_b,
      w_router_split, router_bias, before)


def _moe_body(cnt_ref, h_ref, comb_ref, pos_ref, p_ref, wg_ref, wu_ref, wd_ref, wpgf_ref, wppf_ref, g_ref, b_ref,
              o_ref, hs_s, cs_s, acc_s, ple_s, wpg_ref, wpp_ref):
    i = pl.program_id(0)
    group = pl.program_id(1)
    ng = pl.num_programs(1)
    tm = h_ref.shape[0]

    @pl.when((i == 0) & (group == 0))
    def _():
        wpg_ref[...] = wpgf_ref[...].astype(BF16)
        wpp_ref[...] = wppf_ref[...].astype(BF16)

    @pl.when(group == 0)
    def _():
        hb = h_ref[...].astype(BF16)
        comb = comb_ref[...]
        comb_lo = comb - comb.astype(BF16).astype(F32)
        low_lanes = lax.broadcasted_iota(jnp.int32, (1, LANES), 1) < COMB_LO_LANE
        comb_b = jnp.where(low_lanes, comb, pltpu.roll(comb_lo, COMB_LO_LANE, 1)).astype(BF16)
        for r0 in range(0, tm, MOE_SIDE_ROWS):
            sl = pl.ds(r0, MOE_SIDE_ROWS)
            row = r0 + lax.broadcasted_iota(jnp.int32, (MOE_SIDE_ROWS, tm), 0)
            perm = jnp.where(pos_ref[0:1, :] == row.astype(F32), 1.0, 0.0).astype(BF16)
            hs_s[sl, :] = _dot(perm, hb).astype(BF16)
            both = _dot(perm, comb_b)
            cs_s[sl, :] = both + pltpu.roll(both, LANES - COMB_LO_LANE, 1)
        acc_s[...] = jnp.zeros_like(acc_s)

    count = cnt_ref[i * ng + group]
    start = jnp.int32(0)
    for gi in range(N_EXPERTS // EXPERTS_PER_GROUP - 1):
        start = start + jnp.where(group > gi, cnt_ref[i * ng + gi], 0)
    lane = lax.broadcasted_iota(jnp.int32, (1, LANES), 1)
    first = (start // BF16_ROW_PACK) * BF16_ROW_PACK
    windows = (start - first + count + MOE_WINDOW_ROWS - 1) // MOE_WINDOW_ROWS

    def window(w, carry):
        wanted = first + w * MOE_WINDOW_ROWS
        lo = pl.multiple_of(jnp.minimum(wanted, tm - MOE_WINDOW_ROWS), BF16_ROW_PACK)
        rows = pl.ds(lo, MOE_WINDOW_ROWS)
        x = hs_s[rows, :]
        fresh = lo + lax.broadcasted_iota(jnp.int32, (MOE_WINDOW_ROWS, 1), 0) >= wanted
        cs = jnp.where(fresh, cs_s[rows, :], 0.0)
        total = None
        for e in range(EXPERTS_PER_GROUP):
            gate = _dot(x, wg_ref[e])
            up = _dot(x, wu_ref[e])
            c = jnp.sum(jnp.where(lane == group * EXPERTS_PER_GROUP + e, cs, 0.0), axis=1, keepdims=True)
            a = (jax.nn.silu(gate) * up * c).astype(BF16)
            d = _dot(a, wd_ref[e])
            total = d if total is None else total + d
        acc_s[rows, :] += total
        return carry

    lax.fori_loop(0, windows, window, 0)

    half = tm // 2
    for mid in (1, 2):
        @pl.when(group == mid)
        def _():
            for r0 in range((mid - 1) * half, mid * half, MOE_SIDE_ROWS):
                sl = pl.ds(r0, MOE_SIDE_ROWS)
                hb = h_ref[sl, :].astype(BF16)
                ple_s[sl, :] = (jax.nn.sigmoid(_dot(hb, wpg_ref[...]))
                                * _dot(p_ref[sl, :].astype(BF16), wpp_ref[...]))

    @pl.when(group == ng - 1)
    def _():
        sorted_out = acc_s[...].astype(BF16)
        col = lax.broadcasted_iota(jnp.int32, (MOE_SIDE_ROWS, tm), 1).astype(F32)
        for r0 in range(0, tm, MOE_SIDE_ROWS):
            sl = pl.ds(r0, MOE_SIDE_ROWS)
            unperm = jnp.where(comb_ref[sl, SORT_POS_LANE:SORT_POS_LANE + 1] == col, 1.0, 0.0).astype(BF16)
            ffn = _dot(unperm, sorted_out)
            o_ref[sl, :] = _layer_norm(DN_ALPHA * h_ref[sl, :] + ffn + ple_s[sl, :], g_ref[...], b_ref[...])


def _moe(h, comb, pos_rows, counts, p_all, layer, wg_bf16, wu_bf16, wd_bf16, w_ple_gate, w_ple_proj, ln_g, ln_b):
    t = h.shape[0]
    tm = MOE_TOKENS
    nt = t // tm
    ng = N_EXPERTS // EXPERTS_PER_GROUP
    full = lambda a: _layer_block(a, layer)
    once = lambda a: _layer_block(a, layer, pipeline_mode=pl.Buffered(1))
    experts = lambda rows, cols: pl.BlockSpec((EXPERTS_PER_GROUP, rows, cols),
                                              lambda i, g, cnt: (layer * ng + g, 0, 0))
    grid_spec = pltpu.PrefetchScalarGridSpec(
        num_scalar_prefetch=1,
        grid=(nt, ng),
        in_specs=[pl.BlockSpec((tm, D_MODEL), lambda i, g, cnt: (i, 0)),
                  pl.BlockSpec((tm, LANES), lambda i, g, cnt: (i, 0)),
                  pl.BlockSpec((8, tm), lambda i, g, cnt: (0, i)),
                  pl.BlockSpec((tm, PLE_DIM), lambda i, g, cnt: (layer * nt + i, 0)),
                  experts(D_MODEL, D_EXPERT), experts(D_MODEL, D_EXPERT), experts(D_EXPERT, D_MODEL),
                  once(w_ple_gate), once(w_ple_proj), full(ln_g), full(ln_b)],
        out_specs=pl.BlockSpec((tm, D_MODEL), lambda i, g, cnt: (i, 0)),
        scratch_shapes=[pltpu.VMEM((tm, D_MODEL), BF16), pltpu.VMEM((tm, LANES), F32),
                        pltpu.VMEM((tm, D_MODEL), F32), pltpu.VMEM((tm, D_MODEL), F32),
                        pltpu.VMEM(w_ple_gate.shape[1:], BF16), pltpu.VMEM(w_ple_proj.shape[1:], BF16)])
    return pl.pallas_call(
        _moe_body,
        grid_spec=grid_spec,
        out_shape=jax.ShapeDtypeStruct((t, D_MODEL), F32),
        compiler_params=_cparams("arbitrary", "arbitrary"),
        name="moe",
    )(counts, h, comb, pos_rows, p_all, wg_bf16, wu_bf16, wd_bf16, w_ple_gate, w_ple_proj, ln_g, ln_b)


def _block_diag(w):
    g, n, m = w.shape
    eye = jnp.eye(g, dtype=w.dtype)
    return (eye[:, None, :, None] * w[:, :, None, :]).reshape(g * n, g * m)


def kernel(x, p, w_in, w_out, w_pool, pool_scale, rel_bias, ssm_a_re, ssm_a_im, ssm_log_dt, ssm_b_re, ssm_b_im,
           ssm_c_re, ssm_c_im, ssm_d, w_glu, b_glu, g_pool, g_attn, g_ssm, ln1_g, ln1_b, ln2_g, ln2_b,
           w_router, router_bias, w_exp_gate, w_exp_up, w_exp_down, w_ple_gate, w_ple_proj):
    batch, seq, d = x.shape
    t = batch * seq
    xt = x.reshape(t, d)
    p_all = p.reshape(DEPTH * t, PLE_DIM)

    vec = lambda a: a.astype(F32).reshape(DEPTH, 1, -1)
    f32 = lambda a: a.astype(F32)
    w_pool_b = jax.vmap(_block_diag)(w_pool).astype(BF16)
    bias_rows = _attn_bias_rows(rel_bias)
    tables = jax.vmap(_s5_tables)(ssm_a_re, ssm_a_im, ssm_log_dt, ssm_b_re, ssm_b_im, ssm_c_re, ssm_c_im)
    wr = w_router.astype(F32)
    wr_hi = wr.astype(BF16)
    wr_lo = (wr - wr_hi.astype(F32)).astype(BF16)
    wr_split = jnp.pad(jnp.concatenate([wr_hi, wr_lo], axis=1), ((0, 0), (0, LANES - 2 * N_EXPERTS)))
    r_bias = router_bias.astype(F32).reshape(N_EXPERTS, 1)
    stack_experts = lambda w: w.astype(BF16).reshape((DEPTH * N_EXPERTS,) + w.shape[2:])
    wg_b, wu_b, wd_b = stack_experts(w_exp_gate), stack_experts(w_exp_up), stack_experts(w_exp_down)
    ng = N_EXPERTS // EXPERTS_PER_GROUP

    for layer in range(DEPTH):
        y_pool, q, k, v, u_ssm = _inproj(xt, f32(w_in), w_pool_b, vec(pool_scale), vec(g_pool), layer, batch)
        y_attn = _attention(q, k, v, bias_rows, vec(g_attn), layer, batch)
        y_ssm = _s5(u_ssm, tables, vec(ssm_d), layer, batch)
        h, comb, pos_rows, cnt = _outproj(xt, y_pool, y_attn, y_ssm, f32(w_glu), vec(b_glu), vec(g_ssm),
                                          f32(w_out), vec(ln1_g), vec(ln1_b), wr_split, r_bias, layer)
        counts = cnt[:, 0].reshape(-1, 8)[:, :ng].reshape(-1)
        xt = _moe(h, comb, pos_rows, counts, p_all, layer, wg_b, wu_b, wd_b, f32(w_ple_gate), f32(w_ple_proj),
                  vec(ln2_g), vec(ln2_b))
    return xt.reshape(batch, seq, d)
```

```python
import functools
import math

import numpy as np
import jax
import jax.numpy as jnp
from jax import lax
from jax.experimental import pallas as pl
from jax.experimental.pallas import tpu as pltpu

F32 = jnp.float32
BF16 = jnp.bfloat16

D_MODEL = 1024
DEPTH = 2
CHUNK = 64
PLE_DIM = 256
POOL_WIDTH = 256
POOL_GROUP_DIM = 64
POOL_WINDOWS = (2, 4, 8, 16)
POOL_HALO = 32
ATTN_HEAD_DIM = 64
ATTN_HEADS = 6
ATTN_WIDTH = 384
N_PREV_CHUNKS = 8
REL_CLIP = 128
SSM_WIDTH = 384
SSM_GROUP_DIM = 16
SSM_GROUPS = 24
SSM_STATE = 64
N_EXPERTS = 16
EXPERTS_PER_GROUP = 4
D_EXPERT = 256
DN_ALPHA = (2 * DEPTH) ** 0.25
NORM_EPS = 1e-5
LOG2_E = math.log2(math.e)

LANES = 128
SUBLANES = 8
VMEM_LIMIT_BYTES = 56 * 1024 * 1024

INPROJ_TOKENS = 1024
INPROJ_ROWS = 512
ATTN_Q_CHUNKS = 4
ATTN_Q_TOKENS = ATTN_Q_CHUNKS * CHUNK
ATTN_BAND_TOKENS = 3 * ATTN_Q_TOKENS
ATTN_BLOCKS_PER_STEP = 4
ATTN_PREV_TOKENS = N_PREV_CHUNKS * CHUNK
ATTN_BIAS_ROW = 1024
SSM_CHUNK = 16
SSM_CHUNK_WIDTH = SSM_CHUNK * SSM_GROUP_DIM
SSM_ROWS = 128
SSM_SLOTS = LANES // SSM_GROUP_DIM
SSM_LANE_BLOCKS = SSM_WIDTH // LANES
SSM_TIME_BLOCKS = SSM_CHUNK // SSM_SLOTS
SSM_PAIR_WIDTH = 2 * SSM_STATE
SSM_STATE_LANES = SSM_GROUPS * SSM_STATE
OUT_ROWS = 256
MOE_SIDE_ROWS = 256
MOE_TOKENS = 1024
MOE_WINDOW_ROWS = 320
BF16_ROW_PACK = 16
SORT_POS_LANE = N_EXPERTS
COMB_LO_LANE = 32


def _cparams(*sem):
    return pltpu.CompilerParams(dimension_semantics=sem, vmem_limit_bytes=VMEM_LIMIT_BYTES)


def _dot(a, b):
    return jnp.dot(a, b, preferred_element_type=F32)


def _layer_block(a, layer, **kwargs):
    return pl.BlockSpec((None,) + a.shape[1:], lambda *_: (layer,) + (0,) * (a.ndim - 1), **kwargs)


def _layer_norm(v, g, b):
    mu = jnp.mean(v, axis=-1, keepdims=True)
    vc = v - mu
    var = jnp.mean(vc * vc, axis=-1, keepdims=True)
    return vc * lax.rsqrt(var + NORM_EPS) * g + b


def _ssm_piece(t, v):
    lo = SSM_WIDTH * t + LANES * v
    return slice(lo, lo + LANES)


def _pool_mix(x0, buf, lvl_a, lvl_b, pos, w_ref, scale_ref, g_ref):
    n = x0.shape[0] + POOL_HALO
    group = lax.broadcasted_iota(jnp.int32, (1, POOL_WIDTH), 1) // POOL_GROUP_DIM
    mean = jnp.zeros_like(x0)
    src, dst = buf, lvl_a
    for gi, w in enumerate(POOL_WINDOWS):
        lo = 8 * (gi + 1)
        dst[lo:n, :] = src[lo:n, :] + src[lo - w // 2:n - w // 2, :]
        inv_cnt = 1.0 / jnp.minimum(pos + 1, w).astype(F32)
        mean = jnp.where(group == gi, dst[POOL_HALO:n, :] * inv_cnt, mean)
        src, dst = dst, (lvl_b if dst is lvl_a else lvl_a)
    d = (mean - x0).astype(BF16)
    y = _dot(d, w_ref[...]) * scale_ref[...]
    r = lax.rsqrt(jnp.mean(y * y, axis=-1, keepdims=True) + NORM_EPS)
    return (y * r * g_ref[...]).astype(BF16)


def _inproj_body(x_ref, wf_ref, wpool_ref, pscale_ref, pgain_ref, yp_ref, q_ref, k_ref, v_ref, us_ref,
                 w_ref, zs, buf, lvl_a, lvl_b, *, tiles_per_seq):
    tm = x_ref.shape[0]
    tile_in_seq = pl.program_id(0) % tiles_per_seq

    @pl.when(pl.program_id(0) == 0)
    def _():
        w_ref[...] = wf_ref[...].astype(BF16)

    @pl.when(tile_in_seq == 0)
    def _():
        buf[0:POOL_HALO, :] = jnp.zeros((POOL_HALO, POOL_WIDTH), F32)

    for r0 in range(0, tm, INPROJ_ROWS):
        sl = pl.ds(r0, INPROJ_ROWS)
        xb = x_ref[sl, :].astype(BF16)

        def cols(lo, hi):
            return _dot(xb, w_ref[:, lo:hi])

        buf[pl.ds(POOL_HALO + r0, INPROJ_ROWS), :] = cols(0, 256)
        qk = cols(256, 768)
        q_ref[sl, :] = (qk[:, :ATTN_WIDTH] * (ATTN_HEAD_DIM ** -0.5 * LOG2_E)).astype(BF16)
        k_ref[sl, :LANES] = qk[:, ATTN_WIDTH:].astype(BF16)
        k_ref[sl, LANES:] = cols(768, 1024).astype(BF16)
        vs = cols(1024, 1536)
        v_ref[sl, :] = vs[:, :ATTN_WIDTH].astype(BF16)
        zs[0, sl, :] = vs[:, ATTN_WIDTH:]
        s_rest = cols(1536, 1792)
        zs[1, sl, :] = s_rest[:, :LANES]
        zs[2, sl, :] = s_rest[:, LANES:]

    u_pool = buf[POOL_HALO:, :]
    pos = tile_in_seq * tm + lax.broadcasted_iota(jnp.int32, (tm, 1), 0)
    yp_ref[...] = _pool_mix(u_pool, buf, lvl_a, lvl_b, pos, wpool_ref, pscale_ref, pgain_ref)
    buf[0:POOL_HALO, :] = buf[tm:, :]

    chunks = us_ref.shape[0]
    for t in range(SSM_CHUNK):
        for v in range(SSM_LANE_BLOCKS):
            us_ref[:, _ssm_piece(t, v)] = zs[v, pl.ds(t, chunks, stride=SSM_CHUNK), :]


def _inproj(x, w_in, w_pool_blockdiag_bf16, pool_scale, pool_gain, layer, batch):
    t = x.shape[0]
    tm = INPROJ_TOKENS
    row = lambda width: pl.BlockSpec((tm, width), lambda i: (i, 0))
    full = lambda a: _layer_block(a, layer)
    chunk_rows = pl.BlockSpec((tm // SSM_CHUNK, SSM_CHUNK * SSM_WIDTH), lambda i: (i, 0))
    return pl.pallas_call(
        functools.partial(_inproj_body, tiles_per_seq=t // batch // tm),
        grid=(t // tm,),
        in_specs=[row(D_MODEL), _layer_block(w_in, layer, pipeline_mode=pl.Buffered(1)),
                  full(w_pool_blockdiag_bf16), full(pool_scale), full(pool_gain)],
        out_specs=[row(POOL_WIDTH), row(ATTN_WIDTH), row(ATTN_WIDTH), row(ATTN_WIDTH), chunk_rows],
        out_shape=[jax.ShapeDtypeStruct((t, POOL_WIDTH), BF16),
                   jax.ShapeDtypeStruct((t, ATTN_WIDTH), BF16),
                   jax.ShapeDtypeStruct((t, ATTN_WIDTH), BF16),
                   jax.ShapeDtypeStruct((t, ATTN_WIDTH), BF16),
                   jax.ShapeDtypeStruct((t // SSM_CHUNK, SSM_CHUNK * SSM_WIDTH), F32)],
        scratch_shapes=[pltpu.VMEM(w_in.shape[1:], BF16), pltpu.VMEM((SSM_LANE_BLOCKS, tm, LANES), F32)]
        + [pltpu.VMEM((POOL_HALO + tm, POOL_WIDTH), F32)] * 3,
        compiler_params=_cparams("arbitrary"),
        name="inproj",
    )(x, w_in, w_pool_blockdiag_bf16, pool_scale, pool_gain)


def _attn_bias_rows(rel_bias):
    x = np.arange(ATTN_BIAS_ROW)
    x = np.where(x < ATTN_BAND_TOKENS, x, x - ATTN_BIAS_ROW)
    idx = np.clip(N_PREV_CHUNKS * CHUNK - x, -REL_CLIP, REL_CLIP) + REL_CLIP
    return rel_bias.astype(F32)[..., idx] * LOG2_E


def _attn_body(q_ref, kprev_ref, kcur_ref, vprev_ref, vcur_ref, rows_ref, g_ref, o_ref, bias_s):
    b = pl.program_id(0)
    i = pl.program_id(1)
    tq = ATTN_Q_TOKENS

    @pl.when((b == 0) & (i == 0))
    def _():
        qc = lax.broadcasted_iota(jnp.int32, (tq, ATTN_BAND_TOKENS), 0) // CHUNK
        kc = lax.broadcasted_iota(jnp.int32, (tq, ATTN_BAND_TOKENS), 1) // CHUNK
        in_band = (kc >= qc) & (kc <= qc + N_PREV_CHUNKS)
        for head in range(ATTN_HEADS):
            full = jnp.broadcast_to(rows_ref[head:head + 1, :], (tq, ATTN_BIAS_ROW))
            shifted = pltpu.roll(full, 0, 1, stride=1, stride_axis=0)
            bias_s[head // 2, (head % 2) * tq:(head % 2 + 1) * tq, :] = jnp.where(
                in_band, shifted[:, :ATTN_BAND_TOKENS], -jnp.inf)

    upper_half = lax.broadcasted_iota(jnp.int32, (1, LANES), 1) >= ATTN_HEAD_DIM

    def band(prev_ref, cur_ref, blk, sl):
        start = blk * tq
        if start < ATTN_PREV_TOKENS:
            return jnp.concatenate([prev_ref[start:, sl], cur_ref[0:start + tq, sl]], axis=0)
        return cur_ref[start - ATTN_PREV_TOKENS:start + tq, sl]

    def heads(blk, masked_keys):
        rows = pl.ds(blk * tq, tq)
        outs = []
        for pair in range(ATTN_WIDTH // LANES):
            sl = slice(pair * LANES, (pair + 1) * LANES)
            qp = q_ref[rows, sl]
            kp = band(kprev_ref, kcur_ref, blk, sl)
            vp = band(vprev_ref, vcur_ref, blk, sl)
            zero = jnp.zeros_like(qp)
            q2 = jnp.concatenate([jnp.where(upper_half, zero, qp), jnp.where(upper_half, qp, zero)], axis=0)
            s = lax.dot_general(q2, kp, (((1,), (1,)), ((), ())), preferred_element_type=F32)
            s = s + bias_s[pair]
            if masked_keys:
                key = lax.broadcasted_iota(jnp.int32, (1, ATTN_BAND_TOKENS), 1)
                s = jnp.where(key < masked_keys, -jnp.inf, s)
            m = jnp.max(s, axis=-1, keepdims=True)
            p = jnp.exp2(s - m)
            l = jnp.sum(p, axis=-1, keepdims=True)
            o = _dot(p.astype(BF16), vp) * (1.0 / l)
            outs.append(jnp.where(upper_half, o[tq:, :], o[:tq, :]))
        ss = sum(jnp.sum(o * o, axis=-1, keepdims=True) for o in outs)
        r = lax.rsqrt(ss / ATTN_WIDTH + NORM_EPS)
        for pair, o in enumerate(outs):
            sl = slice(pair * LANES, (pair + 1) * LANES)
            o_ref[rows, sl] = (o * r * g_ref[:, sl]).astype(BF16)

    @pl.when(i == 0)
    def _():
        for blk in range(ATTN_BLOCKS_PER_STEP):
            heads(blk, max(ATTN_PREV_TOKENS - blk * tq, 0))

    @pl.when(i > 0)
    def _():
        for blk in range(ATTN_BLOCKS_PER_STEP):
            heads(blk, 0)


def _attention(q, k, v, bias_rows, gain, layer, batch):
    t = q.shape[0]
    ts = ATTN_BLOCKS_PER_STEP * ATTN_Q_TOKENS
    steps = t // batch // ts
    prev_per_step = ts // ATTN_PREV_TOKENS

    cur = pl.BlockSpec((ts, ATTN_WIDTH), lambda b, i: (b * steps + i, 0))
    prev = pl.BlockSpec((ATTN_PREV_TOKENS, ATTN_WIDTH),
                        lambda b, i: (jnp.maximum((b * steps + i) * prev_per_step - 1, 0), 0))

    return pl.pallas_call(
        _attn_body,
        grid=(batch, steps),
        in_specs=[cur, prev, cur, prev, cur,
                  _layer_block(bias_rows, layer), _layer_block(gain, layer)],
        out_specs=cur,
        out_shape=jax.ShapeDtypeStruct((t, ATTN_WIDTH), BF16),
        scratch_shapes=[pltpu.VMEM((ATTN_HEADS // 2, 2 * ATTN_Q_TOKENS, ATTN_BAND_TOKENS), F32)],
        compiler_params=_cparams("arbitrary", "arbitrary"),
        name="attention",
    )(q, k, k, v, v, bias_rows, gain)


def _s5_position_of_time(g, time):
    return SSM_SLOTS * (time // SSM_SLOTS) + (time % SSM_SLOTS + g) % SSM_SLOTS


def _s5_tables(a_re, a_im, log_dt, b_re, b_im, c_re, c_im):
    hi = lax.Precision.HIGHEST
    tc = SSM_CHUNK
    g, p_dim = a_re.shape
    dt = jnp.exp(log_dt.astype(F32))[:, None]
    ar = a_re.astype(F32)
    ai = a_im.astype(F32)
    mag = jnp.exp(ar * dt)
    abar_re = mag * jnp.cos(ai * dt)
    abar_im = mag * jnp.sin(ai * dt)
    den = ar * ar + ai * ai
    nr = abar_re - 1.0
    ni = abar_im
    coef_re = ((nr * ar + ni * ai) / den)[..., None]
    coef_im = ((ni * ar - nr * ai) / den)[..., None]
    br = b_re.astype(F32)
    bi = b_im.astype(F32)
    bbar_re = coef_re * br - coef_im * bi
    bbar_im = coef_re * bi + coef_im * br
    n = jnp.arange(tc + 1, dtype=F32)
    pmag = jnp.exp((ar * dt)[..., None] * n)
    pw_re = pmag * jnp.cos((ai * dt)[..., None] * n)
    pw_im = pmag * jnp.sin((ai * dt)[..., None] * n)
    cw = tc * SSM_GROUP_DIM
    lag_rep = jnp.asarray(np.kron(np.eye(tc), np.ones((1, SSM_GROUP_DIM))), F32)
    ch_rep = jnp.asarray(np.kron(np.ones((1, tc)), np.eye(SSM_GROUP_DIM)), F32)
    stack = lambda re, im: jnp.concatenate([re, im], axis=1)
    halves = lambda a: (a[:, :p_dim], a[:, p_dim:])
    c_re_rep, c_im_rep = halves(jnp.einsum(
        'gpn,nx->gpx', stack(c_re.astype(F32).transpose(0, 2, 1), c_im.astype(F32).transpose(0, 2, 1)), ch_rep,
        precision=hi))

    def output_coefficients(first_power, rep, subscripts):
        powers = stack(pw_re[..., first_power:first_power + tc], pw_im[..., first_power:first_power + tc])
        p_re, p_im = halves(jnp.einsum(subscripts, powers, rep, precision=hi))
        return c_re_rep * p_re - c_im_rep * p_im, -(c_re_rep * p_im + c_im_rep * p_re)

    kern = jnp.einsum('gpk,gpx->gkx', stack(bbar_re, bbar_im),
                      stack(*output_coefficients(0, lag_rep, 'gpn,nx->gpx')), precision=hi)
    position = np.arange(tc)
    time_at = (SSM_SLOTS * (position // SSM_SLOTS)
               + (position % SSM_SLOTS - np.arange(g)[:, None]) % SSM_SLOTS)
    slot_rep = np.repeat(time_at[:, None, :] == np.arange(tc)[None, :, None], SSM_GROUP_DIM, axis=2)
    inter = jnp.concatenate(output_coefficients(1, jnp.asarray(slot_rep, F32), 'gpn,gnx->gpx'), axis=1)
    twice = lambda a: jnp.concatenate([a, a], -1)
    pt_re = twice(pw_re[..., :tc].transpose(0, 2, 1))[:, :, None, :]
    pt_im = twice(pw_im[..., :tc].transpose(0, 2, 1))[:, :, None, :]
    bt_re = bbar_re.transpose(0, 2, 1)[:, None]
    bt_im = bbar_im.transpose(0, 2, 1)[:, None]
    side = lambda a, b: jnp.concatenate([a, b], -1)
    est = (pt_re * side(bt_re, bt_im) + pt_im * side(-bt_im, bt_re)).reshape(g, cw, 2 * p_dim)
    est_swapped = (pt_re * side(bt_im, bt_re) + pt_im * side(bt_re, -bt_im)).reshape(g, cw, 2 * p_dim)
    return dict(
        kern=kern,
        est=est,
        est_swapped=est_swapped,
        inter=inter,
        apow_re=pw_re[..., tc].reshape(1, g * p_dim), apow_im=pw_im[..., tc].reshape(1, g * p_dim))


def _s5_prepare(kern_ref, est_ref, estsw_ref, int_ref, toep_s, est_s, int_s):
    lane = lax.broadcasted_iota(jnp.int32, (1, LANES), 1)
    slot = lane // SSM_GROUP_DIM
    zero = jnp.zeros((SSM_GROUP_DIM, LANES), F32)
    zero_rows = jnp.zeros((SSM_STATE, SSM_CHUNK_WIDTH), BF16)
    for g in range(SSM_GROUPS):
        mine = (lane >= SSM_STATE) if g % 2 else (lane < SSM_STATE)
        turn = g % SSM_SLOTS
        time_slot = (slot - turn) % SSM_SLOTS
        k0 = [kern_ref[g, :, 0:LANES]]
        k1 = [kern_ref[g, :, LANES:2 * LANES]]
        for r in range(1, SSM_SLOTS):
            k0.append(pltpu.roll(k0[0], SSM_GROUP_DIM * r, 1))
            k1.append(pltpu.roll(k1[0], SSM_GROUP_DIM * r, 1))
        for time in range(SSM_CHUNK):
            r = (time + turn) % SSM_SLOTS
            later = time_slot >= time % SSM_SLOTS
            if time < SSM_SLOTS:
                h0 = jnp.where(later, k0[r], zero)
                h1 = jnp.where(later, k1[r], k0[r])
            else:
                h0 = zero
                h1 = jnp.where(later, k0[r], zero)
            rows = pl.ds(SSM_GROUP_DIM * _s5_position_of_time(g, time), SSM_GROUP_DIM)
            toep_s[g, rows, 0:LANES] = h0.astype(BF16)
            toep_s[g, rows, LANES:2 * LANES] = h1.astype(BF16)
            src = pl.ds(SSM_GROUP_DIM * (SSM_CHUNK - 1 - time), SSM_GROUP_DIM)
            e, e_swapped = est_ref[g, src, :], estsw_ref[g, src, :]
            e_re, e_im = (e_swapped, e) if g % 2 else (e, e_swapped)
            est_s[g, rows, 0:LANES] = jnp.where(mine, e_re, zero).astype(BF16)
            est_s[g, rows, LANES:2 * LANES] = jnp.where(mine, e_im, zero).astype(BF16)
        for part in range(2):
            src = pl.ds(part * SSM_STATE, SSM_STATE)
            base = part * SSM_PAIR_WIDTH
            int_s[g, pl.ds(base + (1 - g % 2) * SSM_STATE, SSM_STATE), :] = zero_rows
            int_s[g, pl.ds(base + (g % 2) * SSM_STATE, SSM_STATE), :] = int_ref[g, src, :].astype(BF16)


def _s5_body(u_ref, kern_ref, estin_ref, estswin_ref, intin_ref, apre_ref, apim_ref, d_ref, y_ref,
             toep_ref, est_ref, int_ref, ub_s, ere_s, eim_s, spre_s, spim_s, sre_s, sim_s):
    rows = u_ref.shape[0]

    @pl.when((pl.program_id(0) == 0) & (pl.program_id(1) == 0))
    def _():
        _s5_prepare(kern_ref, estin_ref, estswin_ref, intin_ref, toep_ref, est_ref, int_ref)

    @pl.when(pl.program_id(1) == 0)
    def _():
        sre_s[...] = jnp.zeros_like(sre_s)
        sim_s[...] = jnp.zeros_like(sim_s)

    slot = lax.broadcasted_iota(jnp.int32, (1, LANES), 1) // SSM_GROUP_DIM
    slot_bits = [(b, (slot & b) != 0) for b in (1, 2, 4)]
    piece = _ssm_piece

    for v in range(SSM_LANE_BLOCKS):
        for m in range(SSM_TIME_BLOCKS):
            rot = []
            for j in range(SSM_SLOTS):
                a = u_ref[:, piece(SSM_SLOTS * m + j, v)]
                rot.append(a if j == 0 else pltpu.roll(a, SSM_GROUP_DIM * j, 1))
            for bit in slot_bits:
                rot = [jnp.where(bit[1], rot[(i + bit[0]) % SSM_SLOTS], rot[i]) for i in range(SSM_SLOTS)]
            for gam in range(SSM_SLOTS):
                ub_s[SSM_SLOTS * v + gam, :, m * LANES:(m + 1) * LANES] = rot[-gam % SSM_SLOTS].astype(BF16)

    for q in range(SSM_GROUPS // 2):
        e = _dot(ub_s[2 * q], est_ref[2 * q]) + _dot(ub_s[2 * q + 1], est_ref[2 * q + 1])
        ere_s[:, q * LANES:(q + 1) * LANES] = e[:, :SSM_PAIR_WIDTH]
        eim_s[:, q * LANES:(q + 1) * LANES] = e[:, SSM_PAIR_WIDTH:]

    a_re = apre_ref[...]
    a_im = apim_ref[...]

    def carry_step(r, carry):
        s_re, s_im = carry
        spre_s[pl.ds(r, 1), :] = s_re
        spim_s[pl.ds(r, 1), :] = s_im
        e_re = ere_s[pl.ds(r, 1), :]
        e_im = eim_s[pl.ds(r, 1), :]
        return (a_re * s_re - a_im * s_im + e_re, a_re * s_im + a_im * s_re + e_im)

    s_re, s_im = lax.fori_loop(0, rows, carry_step, (sre_s[...], sim_s[...]))
    sre_s[...] = s_re
    sim_s[...] = s_im

    for v in range(SSM_LANE_BLOCKS):
        yg = []
        for gam in range(SSM_SLOTS):
            g = SSM_SLOTS * v + gam
            q = g // 2
            sp = jnp.concatenate([spre_s[:, q * LANES:(q + 1) * LANES], spim_s[:, q * LANES:(q + 1) * LANES]],
                                 axis=1).astype(BF16)
            yg.append(_dot(ub_s[g], toep_ref[g]) + _dot(sp, int_ref[g]))
        d = d_ref[:, v * LANES:(v + 1) * LANES]
        for m in range(SSM_TIME_BLOCKS):
            back = [yg[-i % SSM_SLOTS][:, m * LANES:(m + 1) * LANES] for i in range(SSM_SLOTS)]
            for bit in slot_bits:
                back = [jnp.where(bit[1], back[(i - bit[0]) % SSM_SLOTS], back[i]) for i in range(SSM_SLOTS)]
            for j in range(SSM_SLOTS):
                o = back[j]
                if j:
                    o = pltpu.roll(o, LANES - SSM_GROUP_DIM * j, 1)
                sl = piece(SSM_SLOTS * m + j, v)
                y_ref[:, sl] = jax.nn.gelu(o + d * u_ref[:, sl])


def _s5(u_rows, tab, d_skip, layer, batch):
    nch, width = u_rows.shape
    rows = SSM_ROWS
    steps = nch // batch // rows
    once = lambda a: _layer_block(a, layer, pipeline_mode=pl.Buffered(1))
    blk = pl.BlockSpec((rows, width), lambda b, i: (b * steps + i, 0))
    table = pltpu.VMEM((SSM_GROUPS, SSM_CHUNK_WIDTH, SSM_CHUNK_WIDTH), BF16)
    state = pltpu.VMEM((rows, SSM_STATE_LANES), F32)
    carry = pltpu.VMEM((1, SSM_STATE_LANES), F32)
    return pl.pallas_call(
        _s5_body,
        grid=(batch, steps),
        in_specs=[blk, once(tab['kern']), once(tab['est']), once(tab['est_swapped']), once(tab['inter']),
                  once(tab['apow_re']), once(tab['apow_im']), once(d_skip)],
        out_specs=blk,
        out_shape=jax.ShapeDtypeStruct((nch, width), F32),
        scratch_shapes=[table, table, table,
                        pltpu.VMEM((SSM_GROUPS, rows, SSM_CHUNK_WIDTH), BF16), state, state, state, state,
                        carry, carry],
        compiler_params=_cparams("arbitrary", "arbitrary"),
        name="s5",
    )(u_rows, tab['kern'], tab['est'], tab['est_swapped'], tab['inter'], tab['apow_re'], tab['apow_im'], d_skip)


def _route_rows(scores, biased):
    ng = N_EXPERTS // EXPERTS_PER_GROUP
    group_score = []
    for gi in range(ng):
        a, b, c, d = biased[gi * EXPERTS_PER_GROUP:(gi + 1) * EXPERTS_PER_GROUP]
        hi1, lo1 = jnp.maximum(a, b), jnp.minimum(a, b)
        hi2, lo2 = jnp.maximum(c, d), jnp.minimum(c, d)
        top1 = jnp.maximum(hi1, hi2)
        top2 = jnp.maximum(jnp.minimum(hi1, hi2), jnp.maximum(lo1, lo2))
        group_score.append(top1 + top2)
    best = group_score[0]
    best_idx = jnp.zeros_like(best, dtype=jnp.int32)
    for gi in range(1, ng):
        better = group_score[gi] > best
        best = jnp.where(better, group_score[gi], best)
        best_idx = jnp.where(better, gi, best_idx)
    picked = []
    for e in range(N_EXPERTS):
        gi = e // EXPERTS_PER_GROUP
        rank = jnp.zeros_like(best_idx)
        for o in range(gi * EXPERTS_PER_GROUP, (gi + 1) * EXPERTS_PER_GROUP):
            if o == e:
                continue
            ahead = (biased[o] > biased[e]) | ((biased[o] == biased[e]) & (o < e))
            rank = rank + ahead.astype(jnp.int32)
        picked.append((best_idx == gi) & (rank < 2))
    wsum = sum(jnp.where(picked[e], scores[e], 0.0) for e in range(N_EXPERTS))
    return [jnp.where(picked[e], scores[e] / wsum, 0.0) for e in range(N_EXPERTS)], best_idx


def _group_sort_positions(best_idx, before_ref):
    ng = N_EXPERTS // EXPERTS_PER_GROUP
    tokens = best_idx.shape[1]
    member = [(best_idx == gi).astype(F32) for gi in range(ng)]
    stacked = jnp.concatenate(member + [jnp.zeros((SUBLANES - ng, tokens), F32)], axis=0)
    parts = []
    run = jnp.zeros((SUBLANES, 1), F32)
    for blk in range(tokens // LANES):
        piece = stacked[:, blk * LANES:(blk + 1) * LANES]
        parts.append(_dot(piece.astype(BF16), before_ref[...]) + run)
        run = run + jnp.sum(piece, axis=1, keepdims=True)
    earlier = jnp.concatenate(parts, axis=1)
    counts = [run[gi:gi + 1, :] for gi in range(ng)]
    pos = jnp.zeros_like(member[0])
    start = jnp.zeros_like(counts[0])
    for gi in range(ng):
        pos = pos + member[gi] * (start + earlier[gi:gi + 1, :])
        start = start + counts[gi]
    return pos, counts


def _outproj_body(x_ref, yp_ref, ya_ref, ys_ref, wgluf_ref, bglu_ref, gssm_ref, woutf_ref, g_ref, b_ref,
                  wr_ref, rb_ref, before_ref, h_ref, comb_ref, pos_ref, cnt_ref, ys_s, wglu_ref, wout_ref):
    @pl.when(pl.program_id(0) == 0)
    def _():
        wglu_ref[...] = wgluf_ref[...].astype(BF16)
        wout_ref[...] = woutf_ref[...].astype(BF16)

    chunks = ys_ref.shape[0]
    for t in range(SSM_CHUNK):
        for v in range(SSM_LANE_BLOCKS):
            ys_s[v, pl.ds(t, chunks, stride=SSM_CHUNK), :] = ys_ref[:, _ssm_piece(t, v)]
    def rows_block(sl):
        ys = jnp.concatenate([ys_s[v, sl, :] for v in range(SSM_LANE_BLOCKS)], axis=1)
        gate = jax.nn.sigmoid(_dot(ys.astype(BF16), wglu_ref[...]) + bglu_ref[...])
        ys = ys * gate
        r = lax.rsqrt(jnp.mean(ys * ys, axis=-1, keepdims=True) + NORM_EPS)
        ysn = (ys * r * gssm_ref[...]).astype(BF16)
        mix = _dot(jnp.concatenate([yp_ref[sl, :], ya_ref[sl, :], ysn], axis=1), wout_ref[...])
        h = _layer_norm(DN_ALPHA * x_ref[sl, :] + mix, g_ref[...], b_ref[...])
        h_ref[sl, :] = h
        return _dot(h.astype(BF16), wr_ref[...])

    tokens = x_ref.shape[0]
    parts = jnp.concatenate([rows_block(pl.ds(r0, OUT_ROWS)) for r0 in range(0, tokens, OUT_ROWS)], axis=0)
    parts_t = parts.T
    sc = jax.nn.sigmoid(parts_t[:N_EXPERTS, :] + parts_t[N_EXPERTS:2 * N_EXPERTS, :])
    bs = sc + rb_ref[...]
    scores = [sc[e:e + 1, :] for e in range(N_EXPERTS)]
    biased = [bs[e:e + 1, :] for e in range(N_EXPERTS)]
    comb_rows, best_idx = _route_rows(scores, biased)
    pos, counts = _group_sort_positions(best_idx, before_ref)
    comb_t = jnp.concatenate(comb_rows + [pos, jnp.zeros((LANES - N_EXPERTS - 1, tokens), F32)], axis=0)
    comb_ref[...] = comb_t.T
    pos_ref[...] = jnp.concatenate([pos, jnp.zeros((SUBLANES - 1, tokens), F32)], axis=0)
    cnt_ref[...] = jnp.concatenate(
        [jnp.broadcast_to(c, (1, LANES)) for c in counts]
        + [jnp.zeros((SUBLANES - len(counts), LANES), F32)], axis=0).astype(jnp.int32)


def _outproj(x, y_pool, y_attn, y_ssm_rows, w_glu, b_glu, g_ssm, w_out, ln_g, ln_b,
             w_router_split, router_bias, layer):
    t = x.shape[0]
    tm = MOE_TOKENS
    nt = t // tm
    row = lambda width: pl.BlockSpec((tm, width), lambda i: (i, 0))
    full = lambda a: pl.BlockSpec(a.shape, lambda i: (0,) * a.ndim)
    per_layer = lambda a: _layer_block(a, layer)
    once = lambda a: _layer_block(a, layer, pipeline_mode=pl.Buffered(1))
    token = np.arange(LANES)
    before = jnp.asarray(token[:, None] < token[None, :], BF16)
    return pl.pallas_call(
        _outproj_body,
        grid=(nt,),
        in_specs=[row(D_MODEL), row(POOL_WIDTH), row(ATTN_WIDTH),
                  pl.BlockSpec((tm // SSM_CHUNK, SSM_CHUNK * SSM_WIDTH), lambda i: (i, 0)),
                  once(w_glu), per_layer(b_glu), per_layer(g_ssm), once(w_out),
                  per_layer(ln_g), per_layer(ln_b), full(w_router_split), full(router_bias), full(before)],
        out_specs=[row(D_MODEL), row(LANES), pl.BlockSpec((SUBLANES, tm), lambda i: (0, i)),
                   pl.BlockSpec((SUBLANES, LANES), lambda i: (i, 0))],
        out_shape=[jax.ShapeDtypeStruct((t, D_MODEL), F32), jax.ShapeDtypeStruct((t, LANES), F32),
                   jax.ShapeDtypeStruct((SUBLANES, t), F32),
                   jax.ShapeDtypeStruct((SUBLANES * nt, LANES), jnp.int32)],
        scratch_shapes=[pltpu.VMEM((SSM_LANE_BLOCKS, tm, LANES), F32),
                        pltpu.VMEM(w_glu.shape[1:], BF16), pltpu.VMEM(w_out.shape[1:], BF16)],
        compiler_params=_cparams("arbitrary"),
        name="outproj",
    )(x, y_pool, y_attn, y_ssm_rows, w_glu, b_glu, g_ssm, w_out, ln_g, ln_b,
      w_router_split, router_bias, before)


def _moe_body(cnt_ref, h_ref, comb_ref, pos_ref, p_ref, wg_ref, wu_ref, wd_ref, wpgf_ref, wppf_ref, g_ref, b_ref,
              o_ref, hs_s, cs_s, acc_s, ple_s, wpg_ref, wpp_ref):
    i = pl.program_id(0)
    group = pl.program_id(1)
    ng = pl.num_programs(1)
    tm = h_ref.shape[0]

    @pl.when((i == 0) & (group == 0))
    def _():
        wpg_ref[...] = wpgf_ref[...].astype(BF16)
        wpp_ref[...] = wppf_ref[...].astype(BF16)

    @pl.when(group == 0)
    def _():
        hb = h_ref[...].astype(BF16)
        comb = comb_ref[...]
        comb_lo = comb - comb.astype(BF16).astype(F32)
        low_lanes = lax.broadcasted_iota(jnp.int32, (1, LANES), 1) < COMB_LO_LANE
        comb_b = jnp.where(low_lanes, comb, pltpu.roll(comb_lo, COMB_LO_LANE, 1)).astype(BF16)
        for r0 in range(0, tm, MOE_SIDE_ROWS):
            sl = pl.ds(r0, MOE_SIDE_ROWS)
            row = r0 + lax.broadcasted_iota(jnp.int32, (MOE_SIDE_ROWS, tm), 0)
            perm = jnp.where(pos_ref[0:1, :] == row.astype(F32), 1.0, 0.0).astype(BF16)
            hs_s[sl, :] = _dot(perm, hb).astype(BF16)
            both = _dot(perm, comb_b)
            cs_s[sl, :] = both + pltpu.roll(both, LANES - COMB_LO_LANE, 1)
        acc_s[...] = jnp.zeros_like(acc_s)

    count = cnt_ref[i * ng + group]
    start = jnp.int32(0)
    for gi in range(N_EXPERTS // EXPERTS_PER_GROUP - 1):
        start = start + jnp.where(group > gi, cnt_ref[i * ng + gi], 0)
    lane = lax.broadcasted_iota(jnp.int32, (1, LANES), 1)
    first = (start // BF16_ROW_PACK) * BF16_ROW_PACK
    windows = (start - first + count + MOE_WINDOW_ROWS - 1) // MOE_WINDOW_ROWS

    def window(w, carry):
        wanted = first + w * MOE_WINDOW_ROWS
        lo = pl.multiple_of(jnp.minimum(wanted, tm - MOE_WINDOW_ROWS), BF16_ROW_PACK)
        rows = pl.ds(lo, MOE_WINDOW_ROWS)
        x = hs_s[rows, :]
        fresh = lo + lax.broadcasted_iota(jnp.int32, (MOE_WINDOW_ROWS, 1), 0) >= wanted
        cs = jnp.where(fresh, cs_s[rows, :], 0.0)
        total = None
        for e in range(EXPERTS_PER_GROUP):
            gate = _dot(x, wg_ref[e])
            up = _dot(x, wu_ref[e])
            c = jnp.sum(jnp.where(lane == group * EXPERTS_PER_GROUP + e, cs, 0.0), axis=1, keepdims=True)
            a = (jax.nn.silu(gate) * up * c).astype(BF16)
            d = _dot(a, wd_ref[e])
            total = d if total is None else total + d
        acc_s[rows, :] += total
        return carry

    lax.fori_loop(0, windows, window, 0)

    half = tm // 2
    for mid in (1, 2):
        @pl.when(group == mid)
        def _():
            for r0 in range((mid - 1) * half, mid * half, MOE_SIDE_ROWS):
                sl = pl.ds(r0, MOE_SIDE_ROWS)
                hb = h_ref[sl, :].astype(BF16)
                ple_s[sl, :] = (jax.nn.sigmoid(_dot(hb, wpg_ref[...]))
                                * _dot(p_ref[sl, :].astype(BF16), wpp_ref[...]))

    @pl.when(group == ng - 1)
    def _():
        sorted_out = acc_s[...].astype(BF16)
        col = lax.broadcasted_iota(jnp.int32, (MOE_SIDE_ROWS, tm), 1).astype(F32)
        for r0 in range(0, tm, MOE_SIDE_ROWS):
            sl = pl.ds(r0, MOE_SIDE_ROWS)
            unperm = jnp.where(comb_ref[sl, SORT_POS_LANE:SORT_POS_LANE + 1] == col, 1.0, 0.0).astype(BF16)
            ffn = _dot(unperm, sorted_out)
            o_ref[sl, :] = _layer_norm(DN_ALPHA * h_ref[sl, :] + ffn + ple_s[sl, :], g_ref[...], b_ref[...])


def _moe(h, comb, pos_rows, counts, p_all, layer, wg_bf16, wu_bf16, wd_bf16, w_ple_gate, w_ple_proj, ln_g, ln_b):
    t = h.shape[0]
    tm = MOE_TOKENS
    nt = t // tm
    ng = N_EXPERTS // EXPERTS_PER_GROUP
    full = lambda a: _layer_block(a, layer)
    once = lambda a: _layer_block(a, layer, pipeline_mode=pl.Buffered(1))
    experts = lambda rows, cols: pl.BlockSpec((EXPERTS_PER_GROUP, rows, cols),
                                              lambda i, g, cnt: (layer * ng + g, 0, 0))
    grid_spec = pltpu.PrefetchScalarGridSpec(
        num_scalar_prefetch=1,
        grid=(nt, ng),
        in_specs=[pl.BlockSpec((tm, D_MODEL), lambda i, g, cnt: (i, 0)),
                  pl.BlockSpec((tm, LANES), lambda i, g, cnt: (i, 0)),
                  pl.BlockSpec((SUBLANES, tm), lambda i, g, cnt: (0, i)),
                  pl.BlockSpec((tm, PLE_DIM), lambda i, g, cnt: (layer * nt + i, 0)),
                  experts(D_MODEL, D_EXPERT), experts(D_MODEL, D_EXPERT), experts(D_EXPERT, D_MODEL),
                  once(w_ple_gate), once(w_ple_proj), full(ln_g), full(ln_b)],
        out_specs=pl.BlockSpec((tm, D_MODEL), lambda i, g, cnt: (i, 0)),
        scratch_shapes=[pltpu.VMEM((tm, D_MODEL), BF16), pltpu.VMEM((tm, LANES), F32),
                        pltpu.VMEM((tm, D_MODEL), F32), pltpu.VMEM((tm, D_MODEL), F32),
                        pltpu.VMEM(w_ple_gate.shape[1:], BF16), pltpu.VMEM(w_ple_proj.shape[1:], BF16)])
    return pl.pallas_call(
        _moe_body,
        grid_spec=grid_spec,
        out_shape=jax.ShapeDtypeStruct((t, D_MODEL), F32),
        compiler_params=_cparams("arbitrary", "arbitrary"),
        name="moe",
    )(counts, h, comb, pos_rows, p_all, wg_bf16, wu_bf16, wd_bf16, w_ple_gate, w_ple_proj, ln_g, ln_b)


def _block_diag(w):
    g, n, m = w.shape
    eye = jnp.eye(g, dtype=w.dtype)
    return (eye[:, None, :, None] * w[:, :, None, :]).reshape(g * n, g * m)


def kernel(x, p, w_in, w_out, w_pool, pool_scale, rel_bias, ssm_a_re, ssm_a_im, ssm_log_dt, ssm_b_re, ssm_b_im,
           ssm_c_re, ssm_c_im, ssm_d, w_glu, b_glu, g_pool, g_attn, g_ssm, ln1_g, ln1_b, ln2_g, ln2_b,
           w_router, router_bias, w_exp_gate, w_exp_up, w_exp_down, w_ple_gate, w_ple_proj):
    batch, seq, d = x.shape
    t = batch * seq
    xt = x.reshape(t, d)
    p_all = p.reshape(DEPTH * t, PLE_DIM)

    vec = lambda a: a.astype(F32).reshape(DEPTH, 1, -1)
    f32 = lambda a: a.astype(F32)
    w_pool_b = jax.vmap(_block_diag)(w_pool).astype(BF16)
    bias_rows = _attn_bias_rows(rel_bias)
    tables = jax.vmap(_s5_tables)(ssm_a_re, ssm_a_im, ssm_log_dt, ssm_b_re, ssm_b_im, ssm_c_re, ssm_c_im)
    wr = w_router.astype(F32)
    wr_hi = wr.astype(BF16)
    wr_lo = (wr - wr_hi.astype(F32)).astype(BF16)
    wr_split = jnp.pad(jnp.concatenate([wr_hi, wr_lo], axis=1), ((0, 0), (0, LANES - 2 * N_EXPERTS)))
    r_bias = router_bias.astype(F32).reshape(N_EXPERTS, 1)
    stack_experts = lambda w: w.astype(BF16).reshape((DEPTH * N_EXPERTS,) + w.shape[2:])
    wg_b, wu_b, wd_b = stack_experts(w_exp_gate), stack_experts(w_exp_up), stack_experts(w_exp_down)
    ng = N_EXPERTS // EXPERTS_PER_GROUP

    for layer in range(DEPTH):
        y_pool, q, k, v, u_ssm = _inproj(xt, f32(w_in), w_pool_b, vec(pool_scale), vec(g_pool), layer, batch)
        y_attn = _attention(q, k, v, bias_rows, vec(g_attn), layer, batch)
        y_ssm = _s5(u_ssm, tables, vec(ssm_d), layer, batch)
        h, comb, pos_rows, cnt = _outproj(xt, y_pool, y_attn, y_ssm, f32(w_glu), vec(b_glu), vec(g_ssm),
                                          f32(w_out), vec(ln1_g), vec(ln1_b), wr_split, r_bias, layer)
        counts = cnt[:, 0].reshape(-1, SUBLANES)[:, :ng].reshape(-1)
        xt = _moe(h, comb, pos_rows, counts, p_all, layer, wg_b, wu_b, wd_b, f32(w_ple_gate), f32(w_ple_proj),
                  vec(ln2_g), vec(ln2_b))
    return xt.reshape(batch, seq, d)
```

```python
import functools
import math

import numpy as np
import jax
import jax.numpy as jnp
from jax import lax
from jax.experimental import pallas as pl
from jax.experimental.pallas import tpu as pltpu

F32 = jnp.float32
BF16 = jnp.bfloat16

D_MODEL = 1024
DEPTH = 2
CHUNK = 64
PLE_DIM = 256
POOL_WIDTH = 256
POOL_GROUP_DIM = 64
POOL_WINDOWS = (2, 4, 8, 16)
POOL_HALO = 32
ATTN_HEAD_DIM = 64
ATTN_HEADS = 6
ATTN_WIDTH = 384
N_PREV_CHUNKS = 8
REL_CLIP = 128
SSM_WIDTH = 384
SSM_GROUP_DIM = 16
SSM_GROUPS = 24
SSM_STATE = 64
N_EXPERTS = 16
EXPERTS_PER_GROUP = 4
D_EXPERT = 256
DN_ALPHA = (2 * DEPTH) ** 0.25
NORM_EPS = 1e-5
LOG2_E = math.log2(math.e)

LANES = 128
SUBLANES = 8
VMEM_LIMIT_BYTES = 56 * 1024 * 1024

INPROJ_TOKENS = 1024
INPROJ_ROWS = 512
ATTN_Q_CHUNKS = 4
ATTN_Q_TOKENS = ATTN_Q_CHUNKS * CHUNK
ATTN_BAND_TOKENS = 3 * ATTN_Q_TOKENS
ATTN_BLOCKS_PER_STEP = 4
ATTN_PREV_TOKENS = N_PREV_CHUNKS * CHUNK
ATTN_BIAS_ROW = 1024
SSM_CHUNK = 16
SSM_CHUNK_WIDTH = SSM_CHUNK * SSM_GROUP_DIM
SSM_ROWS = 128
SSM_SLOTS = LANES // SSM_GROUP_DIM
SSM_LANE_BLOCKS = SSM_WIDTH // LANES
SSM_TIME_BLOCKS = SSM_CHUNK // SSM_SLOTS
SSM_PAIR_WIDTH = 2 * SSM_STATE
SSM_STATE_LANES = SSM_GROUPS * SSM_STATE
OUT_ROWS = 256
MOE_SIDE_ROWS = 256
MOE_TOKENS = 1024
MOE_WINDOW_ROWS = 320
BF16_ROW_PACK = 16
SORT_POS_LANE = N_EXPERTS
COMB_LO_LANE = 32


def _cparams(*sem):
    return pltpu.CompilerParams(dimension_semantics=sem, vmem_limit_bytes=VMEM_LIMIT_BYTES)


def _dot(a, b):
    return jnp.dot(a, b, preferred_element_type=F32)


def _layer_block(a, layer, **kwargs):
    return pl.BlockSpec((None,) + a.shape[1:], lambda *_: (layer,) + (0,) * (a.ndim - 1), **kwargs)


def _whole(a):
    return pl.BlockSpec(a.shape, lambda *_: (0,) * a.ndim)


def _layer_norm(v, g, b):
    mu = jnp.mean(v, axis=-1, keepdims=True)
    vc = v - mu
    var = jnp.mean(vc * vc, axis=-1, keepdims=True)
    return vc * lax.rsqrt(var + NORM_EPS) * g + b


def _ssm_piece(t, v):
    lo = SSM_WIDTH * t + LANES * v
    return slice(lo, lo + LANES)


def _layer_row(ref, layer):
    return ref[layer:layer + 1, :]


def _pool_mix(x0, buf, lvl_a, lvl_b, pos, w_ref, scale, gain):
    n = x0.shape[0] + POOL_HALO
    group = lax.broadcasted_iota(jnp.int32, (1, POOL_WIDTH), 1) // POOL_GROUP_DIM
    mean = jnp.zeros_like(x0)
    src, dst = buf, lvl_a
    for gi, w in enumerate(POOL_WINDOWS):
        lo = 8 * (gi + 1)
        dst[lo:n, :] = src[lo:n, :] + src[lo - w // 2:n - w // 2, :]
        inv_cnt = 1.0 / jnp.minimum(pos + 1, w).astype(F32)
        mean = jnp.where(group == gi, dst[POOL_HALO:n, :] * inv_cnt, mean)
        src, dst = dst, (lvl_b if dst is lvl_a else lvl_a)
    d = (mean - x0).astype(BF16)
    y = _dot(d, w_ref[...]) * scale
    r = lax.rsqrt(jnp.mean(y * y, axis=-1, keepdims=True) + NORM_EPS)
    return (y * r * gain).astype(BF16)


def _inproj_body(x_ref, wf_ref, wpool_ref, pscale_ref, pgain_ref, yp_ref, q_ref, k_ref, v_ref, us_ref,
                 w_ref, zs, buf, lvl_a, lvl_b, *, tiles_per_seq, layer):
    tm = x_ref.shape[0]
    tile_in_seq = pl.program_id(0) % tiles_per_seq

    @pl.when(pl.program_id(0) == 0)
    def _():
        w_ref[...] = wf_ref[...].astype(BF16)

    @pl.when(tile_in_seq == 0)
    def _():
        buf[0:POOL_HALO, :] = jnp.zeros((POOL_HALO, POOL_WIDTH), F32)

    for r0 in range(0, tm, INPROJ_ROWS):
        sl = pl.ds(r0, INPROJ_ROWS)
        xb = x_ref[sl, :].astype(BF16)

        def cols(lo, hi):
            return _dot(xb, w_ref[:, lo:hi])

        buf[pl.ds(POOL_HALO + r0, INPROJ_ROWS), :] = cols(0, 256)
        qk = cols(256, 768)
        q_ref[sl, :] = (qk[:, :ATTN_WIDTH] * (ATTN_HEAD_DIM ** -0.5 * LOG2_E)).astype(BF16)
        k_ref[sl, :LANES] = qk[:, ATTN_WIDTH:].astype(BF16)
        k_ref[sl, LANES:] = cols(768, 1024).astype(BF16)
        vs = cols(1024, 1536)
        v_ref[sl, :] = vs[:, :ATTN_WIDTH].astype(BF16)
        zs[0, sl, :] = vs[:, ATTN_WIDTH:]
        s_rest = cols(1536, 1792)
        zs[1, sl, :] = s_rest[:, :LANES]
        zs[2, sl, :] = s_rest[:, LANES:]

    u_pool = buf[POOL_HALO:, :]
    pos = tile_in_seq * tm + lax.broadcasted_iota(jnp.int32, (tm, 1), 0)
    yp_ref[...] = _pool_mix(u_pool, buf, lvl_a, lvl_b, pos, wpool_ref, _layer_row(pscale_ref, layer),
                            _layer_row(pgain_ref, layer))
    buf[0:POOL_HALO, :] = buf[tm:, :]

    chunks = us_ref.shape[0]
    for t in range(SSM_CHUNK):
        for v in range(SSM_LANE_BLOCKS):
            us_ref[:, _ssm_piece(t, v)] = zs[v, pl.ds(t, chunks, stride=SSM_CHUNK), :]


def _inproj(x, w_in, w_pool_blockdiag_bf16, pool_scale, pool_gain, layer, batch):
    t = x.shape[0]
    tm = INPROJ_TOKENS
    row = lambda width: pl.BlockSpec((tm, width), lambda i: (i, 0))
    full = lambda a: _layer_block(a, layer)
    chunk_rows = pl.BlockSpec((tm // SSM_CHUNK, SSM_CHUNK * SSM_WIDTH), lambda i: (i, 0))
    return pl.pallas_call(
        functools.partial(_inproj_body, tiles_per_seq=t // batch // tm, layer=layer),
        grid=(t // tm,),
        in_specs=[row(D_MODEL), _layer_block(w_in, layer, pipeline_mode=pl.Buffered(1)),
                  full(w_pool_blockdiag_bf16), _whole(pool_scale), _whole(pool_gain)],
        out_specs=[row(POOL_WIDTH), row(ATTN_WIDTH), row(ATTN_WIDTH), row(ATTN_WIDTH), chunk_rows],
        out_shape=[jax.ShapeDtypeStruct((t, POOL_WIDTH), BF16),
                   jax.ShapeDtypeStruct((t, ATTN_WIDTH), BF16),
                   jax.ShapeDtypeStruct((t, ATTN_WIDTH), BF16),
                   jax.ShapeDtypeStruct((t, ATTN_WIDTH), BF16),
                   jax.ShapeDtypeStruct((t // SSM_CHUNK, SSM_CHUNK * SSM_WIDTH), F32)],
        scratch_shapes=[pltpu.VMEM(w_in.shape[1:], BF16), pltpu.VMEM((SSM_LANE_BLOCKS, tm, LANES), F32)]
        + [pltpu.VMEM((POOL_HALO + tm, POOL_WIDTH), F32)] * 3,
        compiler_params=_cparams("arbitrary"),
        name="inproj",
    )(x, w_in, w_pool_blockdiag_bf16, pool_scale, pool_gain)


def _attn_bias_rows(rel_bias):
    x = np.arange(ATTN_BIAS_ROW)
    x = np.where(x < ATTN_BAND_TOKENS, x, x - ATTN_BIAS_ROW)
    idx = np.clip(N_PREV_CHUNKS * CHUNK - x, -REL_CLIP, REL_CLIP) + REL_CLIP
    return rel_bias.astype(F32)[..., idx] * LOG2_E


def _attn_body(q_ref, kprev_ref, kcur_ref, vprev_ref, vcur_ref, rows_ref, g_ref, o_ref, bias_s, *, layer):
    b = pl.program_id(0)
    i = pl.program_id(1)
    tq = ATTN_Q_TOKENS

    @pl.when((b == 0) & (i == 0))
    def _():
        qc = lax.broadcasted_iota(jnp.int32, (tq, ATTN_BAND_TOKENS), 0) // CHUNK
        kc = lax.broadcasted_iota(jnp.int32, (tq, ATTN_BAND_TOKENS), 1) // CHUNK
        in_band = (kc >= qc) & (kc <= qc + N_PREV_CHUNKS)
        for head in range(ATTN_HEADS):
            full = jnp.broadcast_to(rows_ref[head:head + 1, :], (tq, ATTN_BIAS_ROW))
            shifted = pltpu.roll(full, 0, 1, stride=1, stride_axis=0)
            bias_s[head // 2, (head % 2) * tq:(head % 2 + 1) * tq, :] = jnp.where(
                in_band, shifted[:, :ATTN_BAND_TOKENS], -jnp.inf)

    upper_half = lax.broadcasted_iota(jnp.int32, (1, LANES), 1) >= ATTN_HEAD_DIM

    def band(prev_ref, cur_ref, blk, sl):
        start = blk * tq
        if start < ATTN_PREV_TOKENS:
            return jnp.concatenate([prev_ref[start:, sl], cur_ref[0:start + tq, sl]], axis=0)
        return cur_ref[start - ATTN_PREV_TOKENS:start + tq, sl]

    def heads(blk, masked_keys):
        rows = pl.ds(blk * tq, tq)
        outs = []
        for pair in range(ATTN_WIDTH // LANES):
            sl = slice(pair * LANES, (pair + 1) * LANES)
            qp = q_ref[rows, sl]
            kp = band(kprev_ref, kcur_ref, blk, sl)
            vp = band(vprev_ref, vcur_ref, blk, sl)
            zero = jnp.zeros_like(qp)
            q2 = jnp.concatenate([jnp.where(upper_half, zero, qp), jnp.where(upper_half, qp, zero)], axis=0)
            s = lax.dot_general(q2, kp, (((1,), (1,)), ((), ())), preferred_element_type=F32)
            s = s + bias_s[pair]
            if masked_keys:
                key = lax.broadcasted_iota(jnp.int32, (1, ATTN_BAND_TOKENS), 1)
                s = jnp.where(key < masked_keys, -jnp.inf, s)
            m = jnp.max(s, axis=-1, keepdims=True)
            p = jnp.exp2(s - m)
            l = jnp.sum(p, axis=-1, keepdims=True)
            o = _dot(p.astype(BF16), vp) * (1.0 / l)
            outs.append(jnp.where(upper_half, o[tq:, :], o[:tq, :]))
        ss = sum(jnp.sum(o * o, axis=-1, keepdims=True) for o in outs)
        r = lax.rsqrt(ss / ATTN_WIDTH + NORM_EPS)
        for pair, o in enumerate(outs):
            sl = slice(pair * LANES, (pair + 1) * LANES)
            o_ref[rows, sl] = (o * r * g_ref[layer:layer + 1, sl]).astype(BF16)

    @pl.when(i == 0)
    def _():
        for blk in range(ATTN_BLOCKS_PER_STEP):
            heads(blk, max(ATTN_PREV_TOKENS - blk * tq, 0))

    @pl.when(i > 0)
    def _():
        for blk in range(ATTN_BLOCKS_PER_STEP):
            heads(blk, 0)


def _attention(q, k, v, bias_rows, gain, layer, batch):
    t = q.shape[0]
    ts = ATTN_BLOCKS_PER_STEP * ATTN_Q_TOKENS
    steps = t // batch // ts
    prev_per_step = ts // ATTN_PREV_TOKENS

    cur = pl.BlockSpec((ts, ATTN_WIDTH), lambda b, i: (b * steps + i, 0))
    prev = pl.BlockSpec((ATTN_PREV_TOKENS, ATTN_WIDTH),
                        lambda b, i: (jnp.maximum((b * steps + i) * prev_per_step - 1, 0), 0))

    return pl.pallas_call(
        functools.partial(_attn_body, layer=layer),
        grid=(batch, steps),
        in_specs=[cur, prev, cur, prev, cur,
                  _layer_block(bias_rows, layer), _whole(gain)],
        out_specs=cur,
        out_shape=jax.ShapeDtypeStruct((t, ATTN_WIDTH), BF16),
        scratch_shapes=[pltpu.VMEM((ATTN_HEADS // 2, 2 * ATTN_Q_TOKENS, ATTN_BAND_TOKENS), F32)],
        compiler_params=_cparams("arbitrary", "arbitrary"),
        name="attention",
    )(q, k, k, v, v, bias_rows, gain)


def _s5_position_of_time(g, time):
    return SSM_SLOTS * (time // SSM_SLOTS) + (time % SSM_SLOTS + g) % SSM_SLOTS


def _s5_tables(a_re, a_im, log_dt, b_re, b_im, c_re, c_im):
    hi = lax.Precision.HIGHEST
    tc = SSM_CHUNK
    g, p_dim = a_re.shape
    dt = jnp.exp(log_dt.astype(F32))[:, None]
    ar = a_re.astype(F32)
    ai = a_im.astype(F32)
    mag = jnp.exp(ar * dt)
    abar_re = mag * jnp.cos(ai * dt)
    abar_im = mag * jnp.sin(ai * dt)
    den = ar * ar + ai * ai
    nr = abar_re - 1.0
    ni = abar_im
    coef_re = ((nr * ar + ni * ai) / den)[..., None]
    coef_im = ((ni * ar - nr * ai) / den)[..., None]
    br = b_re.astype(F32)
    bi = b_im.astype(F32)
    bbar_re = coef_re * br - coef_im * bi
    bbar_im = coef_re * bi + coef_im * br
    n = jnp.arange(tc + 1, dtype=F32)
    pmag = jnp.exp((ar * dt)[..., None] * n)
    pw_re = pmag * jnp.cos((ai * dt)[..., None] * n)
    pw_im = pmag * jnp.sin((ai * dt)[..., None] * n)
    cw = tc * SSM_GROUP_DIM
    lag_rep = jnp.asarray(np.kron(np.eye(tc), np.ones((1, SSM_GROUP_DIM))), F32)
    ch_rep = jnp.asarray(np.kron(np.ones((1, tc)), np.eye(SSM_GROUP_DIM)), F32)
    expand = lambda a, rep: jnp.einsum('gpn,nx->gpx', a, rep, precision=hi)
    c_re_rep = expand(c_re.astype(F32).transpose(0, 2, 1), ch_rep)
    c_im_rep = expand(c_im.astype(F32).transpose(0, 2, 1), ch_rep)

    def output_coefficients(first_power, rep, subscripts):
        p_re = jnp.einsum(subscripts, pw_re[..., first_power:first_power + tc], rep, precision=hi)
        p_im = jnp.einsum(subscripts, pw_im[..., first_power:first_power + tc], rep, precision=hi)
        return c_re_rep * p_re - c_im_rep * p_im, -(c_re_rep * p_im + c_im_rep * p_re)

    on_re, on_im = output_coefficients(0, lag_rep, 'gpn,nx->gpx')
    kern = (jnp.einsum('gpk,gpx->gkx', bbar_re, on_re, precision=hi)
            + jnp.einsum('gpk,gpx->gkx', bbar_im, on_im, precision=hi))
    position = np.arange(tc)
    time_at = (SSM_SLOTS * (position // SSM_SLOTS)
               + (position % SSM_SLOTS - np.arange(g)[:, None]) % SSM_SLOTS)
    slot_rep = np.repeat(time_at[:, None, :] == np.arange(tc)[None, :, None], SSM_GROUP_DIM, axis=2)
    inter = jnp.concatenate(output_coefficients(1, jnp.asarray(slot_rep, F32), 'gpn,gnx->gpx'), axis=1)
    twice = lambda a: jnp.concatenate([a, a], -1)
    pt_re = twice(pw_re[..., :tc].transpose(0, 2, 1))[:, :, None, :]
    pt_im = twice(pw_im[..., :tc].transpose(0, 2, 1))[:, :, None, :]
    bt_re = bbar_re.transpose(0, 2, 1)[:, None]
    bt_im = bbar_im.transpose(0, 2, 1)[:, None]
    side = lambda a, b: jnp.concatenate([a, b], -1)
    est = (pt_re * side(bt_re, bt_im) + pt_im * side(-bt_im, bt_re)).reshape(g, cw, 2 * p_dim)
    est_swapped = (pt_re * side(bt_im, bt_re) + pt_im * side(bt_re, -bt_im)).reshape(g, cw, 2 * p_dim)
    return dict(
        kern=kern,
        est=est,
        est_swapped=est_swapped,
        inter=inter,
        apow_re=pw_re[..., tc].reshape(1, g * p_dim), apow_im=pw_im[..., tc].reshape(1, g * p_dim))


def _s5_prepare(kern_ref, est_ref, estsw_ref, int_ref, toep_s, est_s, int_s):
    lane = lax.broadcasted_iota(jnp.int32, (1, LANES), 1)
    slot = lane // SSM_GROUP_DIM
    zero = jnp.zeros((SSM_GROUP_DIM, LANES), F32)
    zero_rows = jnp.zeros((SSM_STATE, SSM_CHUNK_WIDTH), BF16)
    for g in range(SSM_GROUPS):
        mine = (lane >= SSM_STATE) if g % 2 else (lane < SSM_STATE)
        turn = g % SSM_SLOTS
        time_slot = (slot - turn) % SSM_SLOTS
        k0 = [kern_ref[g, :, 0:LANES]]
        k1 = [kern_ref[g, :, LANES:2 * LANES]]
        for r in range(1, SSM_SLOTS):
            k0.append(pltpu.roll(k0[0], SSM_GROUP_DIM * r, 1))
            k1.append(pltpu.roll(k1[0], SSM_GROUP_DIM * r, 1))
        for time in range(SSM_CHUNK):
            r = (time + turn) % SSM_SLOTS
            later = time_slot >= time % SSM_SLOTS
            if time < SSM_SLOTS:
                h0 = jnp.where(later, k0[r], zero)
                h1 = jnp.where(later, k1[r], k0[r])
            else:
                h0 = zero
                h1 = jnp.where(later, k0[r], zero)
            rows = pl.ds(SSM_GROUP_DIM * _s5_position_of_time(g, time), SSM_GROUP_DIM)
            toep_s[g, rows, 0:LANES] = h0.astype(BF16)
            toep_s[g, rows, LANES:2 * LANES] = h1.astype(BF16)
            src = pl.ds(SSM_GROUP_DIM * (SSM_CHUNK - 1 - time), SSM_GROUP_DIM)
            e, e_swapped = est_ref[g, src, :], estsw_ref[g, src, :]
            e_re, e_im = (e_swapped, e) if g % 2 else (e, e_swapped)
            est_s[g, rows, 0:LANES] = jnp.where(mine, e_re, zero).astype(BF16)
            est_s[g, rows, LANES:2 * LANES] = jnp.where(mine, e_im, zero).astype(BF16)
        for part in range(2):
            src = pl.ds(part * SSM_STATE, SSM_STATE)
            base = part * SSM_PAIR_WIDTH
            int_s[g, pl.ds(base + (1 - g % 2) * SSM_STATE, SSM_STATE), :] = zero_rows
            int_s[g, pl.ds(base + (g % 2) * SSM_STATE, SSM_STATE), :] = int_ref[g, src, :].astype(BF16)


def _s5_body(u_ref, kern_ref, estin_ref, estswin_ref, intin_ref, apre_ref, apim_ref, d_ref, y_ref,
             toep_ref, est_ref, int_ref, ub_s, ere_s, eim_s, spre_s, spim_s, sre_s, sim_s, *, layer):
    rows = u_ref.shape[0]

    @pl.when((pl.program_id(0) == 0) & (pl.program_id(1) == 0))
    def _():
        _s5_prepare(kern_ref, estin_ref, estswin_ref, intin_ref, toep_ref, est_ref, int_ref)

    @pl.when(pl.program_id(1) == 0)
    def _():
        sre_s[...] = jnp.zeros_like(sre_s)
        sim_s[...] = jnp.zeros_like(sim_s)

    slot = lax.broadcasted_iota(jnp.int32, (1, LANES), 1) // SSM_GROUP_DIM
    slot_bits = [(b, (slot & b) != 0) for b in (1, 2, 4)]
    piece = _ssm_piece

    for v in range(SSM_LANE_BLOCKS):
        for m in range(SSM_TIME_BLOCKS):
            rot = []
            for j in range(SSM_SLOTS):
                a = u_ref[:, piece(SSM_SLOTS * m + j, v)]
                rot.append(a if j == 0 else pltpu.roll(a, SSM_GROUP_DIM * j, 1))
            for bit in slot_bits:
                rot = [jnp.where(bit[1], rot[(i + bit[0]) % SSM_SLOTS], rot[i]) for i in range(SSM_SLOTS)]
            for gam in range(SSM_SLOTS):
                ub_s[SSM_SLOTS * v + gam, :, m * LANES:(m + 1) * LANES] = rot[-gam % SSM_SLOTS].astype(BF16)

    for q in range(SSM_GROUPS // 2):
        e = _dot(ub_s[2 * q], est_ref[2 * q]) + _dot(ub_s[2 * q + 1], est_ref[2 * q + 1])
        ere_s[:, q * LANES:(q + 1) * LANES] = e[:, :SSM_PAIR_WIDTH]
        eim_s[:, q * LANES:(q + 1) * LANES] = e[:, SSM_PAIR_WIDTH:]

    a_re = apre_ref[...]
    a_im = apim_ref[...]

    def carry_step(r, carry):
        s_re, s_im = carry
        spre_s[pl.ds(r, 1), :] = s_re
        spim_s[pl.ds(r, 1), :] = s_im
        e_re = ere_s[pl.ds(r, 1), :]
        e_im = eim_s[pl.ds(r, 1), :]
        return (a_re * s_re - a_im * s_im + e_re, a_re * s_im + a_im * s_re + e_im)

    s_re, s_im = lax.fori_loop(0, rows, carry_step, (sre_s[...], sim_s[...]))
    sre_s[...] = s_re
    sim_s[...] = s_im

    for v in range(SSM_LANE_BLOCKS):
        yg = []
        for gam in range(SSM_SLOTS):
            g = SSM_SLOTS * v + gam
            q = g // 2
            sp = jnp.concatenate([spre_s[:, q * LANES:(q + 1) * LANES], spim_s[:, q * LANES:(q + 1) * LANES]],
                                 axis=1).astype(BF16)
            yg.append(_dot(ub_s[g], toep_ref[g]) + _dot(sp, int_ref[g]))
        d = d_ref[layer:layer + 1, v * LANES:(v + 1) * LANES]
        for m in range(SSM_TIME_BLOCKS):
            back = [yg[-i % SSM_SLOTS][:, m * LANES:(m + 1) * LANES] for i in range(SSM_SLOTS)]
            for bit in slot_bits:
                back = [jnp.where(bit[1], back[(i - bit[0]) % SSM_SLOTS], back[i]) for i in range(SSM_SLOTS)]
            for j in range(SSM_SLOTS):
                o = back[j]
                if j:
                    o = pltpu.roll(o, LANES - SSM_GROUP_DIM * j, 1)
                sl = piece(SSM_SLOTS * m + j, v)
                y_ref[:, sl] = jax.nn.gelu(o + d * u_ref[:, sl])


def _s5(u_rows, tab, d_skip, layer, batch):
    nch, width = u_rows.shape
    rows = SSM_ROWS
    steps = nch // batch // rows
    once = lambda a: _layer_block(a, layer, pipeline_mode=pl.Buffered(1))
    blk = pl.BlockSpec((rows, width), lambda b, i: (b * steps + i, 0))
    table = pltpu.VMEM((SSM_GROUPS, SSM_CHUNK_WIDTH, SSM_CHUNK_WIDTH), BF16)
    state = pltpu.VMEM((rows, SSM_STATE_LANES), F32)
    carry = pltpu.VMEM((1, SSM_STATE_LANES), F32)
    return pl.pallas_call(
        functools.partial(_s5_body, layer=layer),
        grid=(batch, steps),
        in_specs=[blk, once(tab['kern']), once(tab['est']), once(tab['est_swapped']), once(tab['inter']),
                  once(tab['apow_re']), once(tab['apow_im']), _whole(d_skip)],
        out_specs=blk,
        out_shape=jax.ShapeDtypeStruct((nch, width), F32),
        scratch_shapes=[table, table, table,
                        pltpu.VMEM((SSM_GROUPS, rows, SSM_CHUNK_WIDTH), BF16), state, state, state, state,
                        carry, carry],
        compiler_params=_cparams("arbitrary", "arbitrary"),
        name="s5",
    )(u_rows, tab['kern'], tab['est'], tab['est_swapped'], tab['inter'], tab['apow_re'], tab['apow_im'], d_skip)


def _route_rows(scores, biased):
    ng = N_EXPERTS // EXPERTS_PER_GROUP
    group_score = []
    for gi in range(ng):
        a, b, c, d = biased[gi * EXPERTS_PER_GROUP:(gi + 1) * EXPERTS_PER_GROUP]
        hi1, lo1 = jnp.maximum(a, b), jnp.minimum(a, b)
        hi2, lo2 = jnp.maximum(c, d), jnp.minimum(c, d)
        top1 = jnp.maximum(hi1, hi2)
        top2 = jnp.maximum(jnp.minimum(hi1, hi2), jnp.maximum(lo1, lo2))
        group_score.append(top1 + top2)
    best = group_score[0]
    best_idx = jnp.zeros_like(best, dtype=jnp.int32)
    for gi in range(1, ng):
        better = group_score[gi] > best
        best = jnp.where(better, group_score[gi], best)
        best_idx = jnp.where(better, gi, best_idx)
    picked = []
    for e in range(N_EXPERTS):
        gi = e // EXPERTS_PER_GROUP
        rank = jnp.zeros_like(best_idx)
        for o in range(gi * EXPERTS_PER_GROUP, (gi + 1) * EXPERTS_PER_GROUP):
            if o == e:
                continue
            ahead = (biased[o] > biased[e]) | ((biased[o] == biased[e]) & (o < e))
            rank = rank + ahead.astype(jnp.int32)
        picked.append((best_idx == gi) & (rank < 2))
    wsum = sum(jnp.where(picked[e], scores[e], 0.0) for e in range(N_EXPERTS))
    return [jnp.where(picked[e], scores[e] / wsum, 0.0) for e in range(N_EXPERTS)], best_idx


def _group_sort_positions(best_idx, before_ref):
    ng = N_EXPERTS // EXPERTS_PER_GROUP
    tokens = best_idx.shape[1]
    member = [(best_idx == gi).astype(F32) for gi in range(ng)]
    stacked = jnp.concatenate(member + [jnp.zeros((SUBLANES - ng, tokens), F32)], axis=0)
    parts = []
    run = jnp.zeros((SUBLANES, 1), F32)
    for blk in range(tokens // LANES):
        piece = stacked[:, blk * LANES:(blk + 1) * LANES]
        parts.append(_dot(piece.astype(BF16), before_ref[...]) + run)
        run = run + jnp.sum(piece, axis=1, keepdims=True)
    earlier = jnp.concatenate(parts, axis=1)
    counts = [run[gi:gi + 1, :] for gi in range(ng)]
    pos = jnp.zeros_like(member[0])
    start = jnp.zeros_like(counts[0])
    for gi in range(ng):
        pos = pos + member[gi] * (start + earlier[gi:gi + 1, :])
        start = start + counts[gi]
    return pos, counts


def _outproj_body(x_ref, yp_ref, ya_ref, ys_ref, wgluf_ref, bglu_ref, gssm_ref, woutf_ref, g_ref, b_ref,
                  wr_ref, rb_ref, before_ref, h_ref, comb_ref, pos_ref, cnt_ref, ys_s, wglu_ref, wout_ref,
                  *, layer):
    @pl.when(pl.program_id(0) == 0)
    def _():
        wglu_ref[...] = wgluf_ref[...].astype(BF16)
        wout_ref[...] = woutf_ref[...].astype(BF16)

    chunks = ys_ref.shape[0]
    for t in range(SSM_CHUNK):
        for v in range(SSM_LANE_BLOCKS):
            ys_s[v, pl.ds(t, chunks, stride=SSM_CHUNK), :] = ys_ref[:, _ssm_piece(t, v)]
    def rows_block(sl):
        ys = jnp.concatenate([ys_s[v, sl, :] for v in range(SSM_LANE_BLOCKS)], axis=1)
        gate = jax.nn.sigmoid(_dot(ys.astype(BF16), wglu_ref[...]) + _layer_row(bglu_ref, layer))
        ys = ys * gate
        r = lax.rsqrt(jnp.mean(ys * ys, axis=-1, keepdims=True) + NORM_EPS)
        ysn = (ys * r * _layer_row(gssm_ref, layer)).astype(BF16)
        mix = _dot(jnp.concatenate([yp_ref[sl, :], ya_ref[sl, :], ysn], axis=1), wout_ref[...])
        h = _layer_norm(DN_ALPHA * x_ref[sl, :] + mix, _layer_row(g_ref, layer), _layer_row(b_ref, layer))
        h_ref[sl, :] = h
        return _dot(h.astype(BF16), wr_ref[...])

    tokens = x_ref.shape[0]
    parts = jnp.concatenate([rows_block(pl.ds(r0, OUT_ROWS)) for r0 in range(0, tokens, OUT_ROWS)], axis=0)
    parts_t = parts.T
    sc = jax.nn.sigmoid(parts_t[:N_EXPERTS, :] + parts_t[N_EXPERTS:2 * N_EXPERTS, :])
    bs = sc + rb_ref[...]
    scores = [sc[e:e + 1, :] for e in range(N_EXPERTS)]
    biased = [bs[e:e + 1, :] for e in range(N_EXPERTS)]
    comb_rows, best_idx = _route_rows(scores, biased)
    pos, counts = _group_sort_positions(best_idx, before_ref)
    comb_t = jnp.concatenate(comb_rows + [pos, jnp.zeros((LANES - N_EXPERTS - 1, tokens), F32)], axis=0)
    comb_ref[...] = comb_t.T
    pos_ref[...] = jnp.concatenate([pos, jnp.zeros((SUBLANES - 1, tokens), F32)], axis=0)
    cnt_ref[...] = jnp.concatenate(
        [jnp.broadcast_to(c, (1, LANES)) for c in counts]
        + [jnp.zeros((SUBLANES - len(counts), LANES), F32)], axis=0).astype(jnp.int32)


def _outproj(x, y_pool, y_attn, y_ssm_rows, w_glu, b_glu, g_ssm, w_out, ln_g, ln_b,
             w_router_split, router_bias, layer):
    t = x.shape[0]
    tm = MOE_TOKENS
    nt = t // tm
    row = lambda width: pl.BlockSpec((tm, width), lambda i: (i, 0))
    full = _whole
    once = lambda a: _layer_block(a, layer, pipeline_mode=pl.Buffered(1))
    token = np.arange(LANES)
    before = jnp.asarray(token[:, None] < token[None, :], BF16)
    return pl.pallas_call(
        functools.partial(_outproj_body, layer=layer),
        grid=(nt,),
        in_specs=[row(D_MODEL), row(POOL_WIDTH), row(ATTN_WIDTH),
                  pl.BlockSpec((tm // SSM_CHUNK, SSM_CHUNK * SSM_WIDTH), lambda i: (i, 0)),
                  once(w_glu), _whole(b_glu), _whole(g_ssm), once(w_out), _whole(ln_g), _whole(ln_b), full(w_router_split), full(router_bias), full(before)],
        out_specs=[row(D_MODEL), row(LANES), pl.BlockSpec((SUBLANES, tm), lambda i: (0, i)),
                   pl.BlockSpec((SUBLANES, LANES), lambda i: (i, 0))],
        out_shape=[jax.ShapeDtypeStruct((t, D_MODEL), F32), jax.ShapeDtypeStruct((t, LANES), F32),
                   jax.ShapeDtypeStruct((SUBLANES, t), F32),
                   jax.ShapeDtypeStruct((SUBLANES * nt, LANES), jnp.int32)],
        scratch_shapes=[pltpu.VMEM((SSM_LANE_BLOCKS, tm, LANES), F32),
                        pltpu.VMEM(w_glu.shape[1:], BF16), pltpu.VMEM(w_out.shape[1:], BF16)],
        compiler_params=_cparams("arbitrary"),
        name="outproj",
    )(x, y_pool, y_attn, y_ssm_rows, w_glu, b_glu, g_ssm, w_out, ln_g, ln_b,
      w_router_split, router_bias, before)


def _moe_body(cnt_ref, h_ref, comb_ref, pos_ref, p_ref, wg_ref, wu_ref, wd_ref, wpgf_ref, wppf_ref, g_ref, b_ref,
              o_ref, hs_s, cs_s, acc_s, ple_s, wpg_ref, wpp_ref, *, layer):
    i = pl.program_id(0)
    group = pl.program_id(1)
    ng = pl.num_programs(1)
    tm = h_ref.shape[0]

    @pl.when((i == 0) & (group == 0))
    def _():
        wpg_ref[...] = wpgf_ref[...].astype(BF16)
        wpp_ref[...] = wppf_ref[...].astype(BF16)

    @pl.when(group == 0)
    def _():
        hb = h_ref[...].astype(BF16)
        comb = comb_ref[...]
        comb_lo = comb - comb.astype(BF16).astype(F32)
        low_lanes = lax.broadcasted_iota(jnp.int32, (1, LANES), 1) < COMB_LO_LANE
        comb_b = jnp.where(low_lanes, comb, pltpu.roll(comb_lo, COMB_LO_LANE, 1)).astype(BF16)
        for r0 in range(0, tm, MOE_SIDE_ROWS):
            sl = pl.ds(r0, MOE_SIDE_ROWS)
            row = r0 + lax.broadcasted_iota(jnp.int32, (MOE_SIDE_ROWS, tm), 0)
            perm = jnp.where(pos_ref[0:1, :] == row.astype(F32), 1.0, 0.0).astype(BF16)
            hs_s[sl, :] = _dot(perm, hb).astype(BF16)
            both = _dot(perm, comb_b)
            cs_s[sl, :] = both + pltpu.roll(both, LANES - COMB_LO_LANE, 1)
        acc_s[...] = jnp.zeros_like(acc_s)

    count = cnt_ref[i * ng + group]
    start = jnp.int32(0)
    for gi in range(N_EXPERTS // EXPERTS_PER_GROUP - 1):
        start = start + jnp.where(group > gi, cnt_ref[i * ng + gi], 0)
    lane = lax.broadcasted_iota(jnp.int32, (1, LANES), 1)
    first = (start // BF16_ROW_PACK) * BF16_ROW_PACK
    windows = (start - first + count + MOE_WINDOW_ROWS - 1) // MOE_WINDOW_ROWS

    def window(w, carry):
        wanted = first + w * MOE_WINDOW_ROWS
        lo = pl.multiple_of(jnp.minimum(wanted, tm - MOE_WINDOW_ROWS), BF16_ROW_PACK)
        rows = pl.ds(lo, MOE_WINDOW_ROWS)
        x = hs_s[rows, :]
        fresh = lo + lax.broadcasted_iota(jnp.int32, (MOE_WINDOW_ROWS, 1), 0) >= wanted
        cs = jnp.where(fresh, cs_s[rows, :], 0.0)
        total = None
        for e in range(EXPERTS_PER_GROUP):
            gate = _dot(x, wg_ref[e])
            up = _dot(x, wu_ref[e])
            c = jnp.sum(jnp.where(lane == group * EXPERTS_PER_GROUP + e, cs, 0.0), axis=1, keepdims=True)
            a = (jax.nn.silu(gate) * up * c).astype(BF16)
            d = _dot(a, wd_ref[e])
            total = d if total is None else total + d
        acc_s[rows, :] += total
        return carry

    lax.fori_loop(0, windows, window, 0)

    half = tm // 2
    for mid in (1, 2):
        @pl.when(group == mid)
        def _():
            for r0 in range((mid - 1) * half, mid * half, MOE_SIDE_ROWS):
                sl = pl.ds(r0, MOE_SIDE_ROWS)
                hb = h_ref[sl, :].astype(BF16)
                ple_s[sl, :] = (jax.nn.sigmoid(_dot(hb, wpg_ref[...]))
                                * _dot(p_ref[sl, :].astype(BF16), wpp_ref[...]))

    @pl.when(group == ng - 1)
    def _():
        sorted_out = acc_s[...].astype(BF16)
        col = lax.broadcasted_iota(jnp.int32, (MOE_SIDE_ROWS, tm), 1).astype(F32)
        for r0 in range(0, tm, MOE_SIDE_ROWS):
            sl = pl.ds(r0, MOE_SIDE_ROWS)
            unperm = jnp.where(comb_ref[sl, SORT_POS_LANE:SORT_POS_LANE + 1] == col, 1.0, 0.0).astype(BF16)
            ffn = _dot(unperm, sorted_out)
            o_ref[sl, :] = _layer_norm(DN_ALPHA * h_ref[sl, :] + ffn + ple_s[sl, :], _layer_row(g_ref, layer),
                                      _layer_row(b_ref, layer))


def _moe(h, comb, pos_rows, counts, p_all, layer, wg_bf16, wu_bf16, wd_bf16, w_ple_gate, w_ple_proj, ln_g, ln_b):
    t = h.shape[0]
    tm = MOE_TOKENS
    nt = t // tm
    ng = N_EXPERTS // EXPERTS_PER_GROUP
    once = lambda a: _layer_block(a, layer, pipeline_mode=pl.Buffered(1))
    experts = lambda rows, cols: pl.BlockSpec((EXPERTS_PER_GROUP, rows, cols),
                                              lambda i, g, cnt: (layer * ng + g, 0, 0))
    grid_spec = pltpu.PrefetchScalarGridSpec(
        num_scalar_prefetch=1,
        grid=(nt, ng),
        in_specs=[pl.BlockSpec((tm, D_MODEL), lambda i, g, cnt: (i, 0)),
                  pl.BlockSpec((tm, LANES), lambda i, g, cnt: (i, 0)),
                  pl.BlockSpec((SUBLANES, tm), lambda i, g, cnt: (0, i)),
                  pl.BlockSpec((tm, PLE_DIM), lambda i, g, cnt: (layer * nt + i, 0)),
                  experts(D_MODEL, D_EXPERT), experts(D_MODEL, D_EXPERT), experts(D_EXPERT, D_MODEL),
                  once(w_ple_gate), once(w_ple_proj), _whole(ln_g), _whole(ln_b)],
        out_specs=pl.BlockSpec((tm, D_MODEL), lambda i, g, cnt: (i, 0)),
        scratch_shapes=[pltpu.VMEM((tm, D_MODEL), BF16), pltpu.VMEM((tm, LANES), F32),
                        pltpu.VMEM((tm, D_MODEL), F32), pltpu.VMEM((tm, D_MODEL), F32),
                        pltpu.VMEM(w_ple_gate.shape[1:], BF16), pltpu.VMEM(w_ple_proj.shape[1:], BF16)])
    return pl.pallas_call(
        functools.partial(_moe_body, layer=layer),
        grid_spec=grid_spec,
        out_shape=jax.ShapeDtypeStruct((t, D_MODEL), F32),
        compiler_params=_cparams("arbitrary", "arbitrary"),
        name="moe",
    )(counts, h, comb, pos_rows, p_all, wg_bf16, wu_bf16, wd_bf16, w_ple_gate, w_ple_proj, ln_g, ln_b)


def _block_diag(w):
    g, n, m = w.shape
    eye = jnp.eye(g, dtype=w.dtype)
    return (eye[:, None, :, None] * w[:, :, None, :]).reshape(g * n, g * m)


def kernel(x, p, w_in, w_out, w_pool, pool_scale, rel_bias, ssm_a_re, ssm_a_im, ssm_log_dt, ssm_b_re, ssm_b_im,
           ssm_c_re, ssm_c_im, ssm_d, w_glu, b_glu, g_pool, g_attn, g_ssm, ln1_g, ln1_b, ln2_g, ln2_b,
           w_router, router_bias, w_exp_gate, w_exp_up, w_exp_down, w_ple_gate, w_ple_proj):
    batch, seq, d = x.shape
    t = batch * seq
    xt = x.reshape(t, d)
    p_all = p.reshape(DEPTH * t, PLE_DIM)

    f32 = vec = lambda a: a.astype(F32)
    w_pool_b = jax.vmap(_block_diag)(w_pool).astype(BF16)
    bias_rows = _attn_bias_rows(rel_bias)
    tables = jax.vmap(_s5_tables)(ssm_a_re, ssm_a_im, ssm_log_dt, ssm_b_re, ssm_b_im, ssm_c_re, ssm_c_im)
    wr = w_router.astype(F32)
    wr_hi = wr.astype(BF16)
    wr_lo = (wr - wr_hi.astype(F32)).astype(BF16)
    wr_split = jnp.pad(jnp.concatenate([wr_hi, wr_lo], axis=1), ((0, 0), (0, LANES - 2 * N_EXPERTS)))
    r_bias = router_bias.astype(F32).reshape(N_EXPERTS, 1)
    stack_experts = lambda w: w.astype(BF16).reshape((DEPTH * N_EXPERTS,) + w.shape[2:])
    wg_b, wu_b, wd_b = stack_experts(w_exp_gate), stack_experts(w_exp_up), stack_experts(w_exp_down)
    ng = N_EXPERTS // EXPERTS_PER_GROUP

    for layer in range(DEPTH):
        y_pool, q, k, v, u_ssm = _inproj(xt, f32(w_in), w_pool_b, vec(pool_scale), vec(g_pool), layer, batch)
        y_attn = _attention(q, k, v, bias_rows, vec(g_attn), layer, batch)
        y_ssm = _s5(u_ssm, tables, vec(ssm_d), layer, batch)
        h, comb, pos_rows, cnt = _outproj(xt, y_pool, y_attn, y_ssm, f32(w_glu), vec(b_glu), vec(g_ssm),
                                          f32(w_out), vec(ln1_g), vec(ln1_b), wr_split, r_bias, layer)
        counts = cnt[:, 0].reshape(-1, SUBLANES)[:, :ng].reshape(-1)
        xt = _moe(h, comb, pos_rows, counts, p_all, layer, wg_b, wu_b, wd_b, f32(w_ple_gate), f32(w_ple_proj),
                  vec(ln2_g), vec(ln2_b))
    return xt.reshape(batch, seq, d)
```

```python
import functools
import math

import numpy as np
import jax
import jax.numpy as jnp
from jax import lax
from jax.experimental import pallas as pl
from jax.experimental.pallas import tpu as pltpu

F32 = jnp.float32
BF16 = jnp.bfloat16

D_MODEL = 1024
DEPTH = 2
CHUNK = 64
PLE_DIM = 256
POOL_WIDTH = 256
POOL_GROUP_DIM = 64
POOL_WINDOWS = (2, 4, 8, 16)
POOL_HALO = 32
ATTN_HEAD_DIM = 64
ATTN_HEADS = 6
ATTN_WIDTH = 384
N_PREV_CHUNKS = 8
REL_CLIP = 128
SSM_WIDTH = 384
SSM_GROUP_DIM = 16
SSM_GROUPS = 24
SSM_STATE = 64
N_EXPERTS = 16
EXPERTS_PER_GROUP = 4
D_EXPERT = 256
DN_ALPHA = (2 * DEPTH) ** 0.25
NORM_EPS = 1e-5
LOG2_E = math.log2(math.e)

LANES = 128
SUBLANES = 8
VMEM_LIMIT_BYTES = 56 * 1024 * 1024

INPROJ_TOKENS = 1024
INPROJ_ROWS = 512
ATTN_Q_CHUNKS = 4
ATTN_Q_TOKENS = ATTN_Q_CHUNKS * CHUNK
ATTN_BAND_TOKENS = 3 * ATTN_Q_TOKENS
ATTN_BLOCKS_PER_STEP = 4
ATTN_PREV_TOKENS = N_PREV_CHUNKS * CHUNK
ATTN_BIAS_ROW = 1024
SSM_CHUNK = 16
SSM_CHUNK_WIDTH = SSM_CHUNK * SSM_GROUP_DIM
SSM_ROWS = 128
SSM_SLOTS = LANES // SSM_GROUP_DIM
SSM_LANE_BLOCKS = SSM_WIDTH // LANES
SSM_TIME_BLOCKS = SSM_CHUNK // SSM_SLOTS
SSM_PAIR_WIDTH = 2 * SSM_STATE
SSM_STATE_LANES = SSM_GROUPS * SSM_STATE
OUT_ROWS = 256
MOE_SIDE_ROWS = 256
MOE_TOKENS = 1024
MOE_WINDOW_SIZES = (256, 320, 384)
BF16_ROW_PACK = 16
SORT_POS_LANE = N_EXPERTS
COMB_LO_LANE = 32


def _cparams(*sem):
    return pltpu.CompilerParams(dimension_semantics=sem, vmem_limit_bytes=VMEM_LIMIT_BYTES)


def _dot(a, b):
    return jnp.dot(a, b, preferred_element_type=F32)


def _layer_block(a, layer, **kwargs):
    return pl.BlockSpec((None,) + a.shape[1:], lambda *_: (layer,) + (0,) * (a.ndim - 1), **kwargs)


def _whole(a):
    return pl.BlockSpec(a.shape, lambda *_: (0,) * a.ndim)


def _layer_norm(v, g, b):
    mu = jnp.mean(v, axis=-1, keepdims=True)
    vc = v - mu
    var = jnp.mean(vc * vc, axis=-1, keepdims=True)
    return vc * lax.rsqrt(var + NORM_EPS) * g + b


def _ssm_piece(t, v):
    lo = SSM_WIDTH * t + LANES * v
    return slice(lo, lo + LANES)


def _layer_row(ref, layer):
    return ref[layer:layer + 1, :]


def _pool_mix(x0, buf, lvl_a, lvl_b, pos, w_ref, scale, gain):
    n = x0.shape[0] + POOL_HALO
    group = lax.broadcasted_iota(jnp.int32, (1, POOL_WIDTH), 1) // POOL_GROUP_DIM
    mean = jnp.zeros_like(x0)
    src, dst = buf, lvl_a
    for gi, w in enumerate(POOL_WINDOWS):
        lo = 8 * (gi + 1)
        dst[lo:n, :] = src[lo:n, :] + src[lo - w // 2:n - w // 2, :]
        inv_cnt = 1.0 / jnp.minimum(pos + 1, w).astype(F32)
        mean = jnp.where(group == gi, dst[POOL_HALO:n, :] * inv_cnt, mean)
        src, dst = dst, (lvl_b if dst is lvl_a else lvl_a)
    d = (mean - x0).astype(BF16)
    y = _dot(d, w_ref[...]) * scale
    r = lax.rsqrt(jnp.mean(y * y, axis=-1, keepdims=True) + NORM_EPS)
    return (y * r * gain).astype(BF16)


def _inproj_body(x_ref, wf_ref, wpool_ref, pscale_ref, pgain_ref, yp_ref, q_ref, k_ref, v_ref, us_ref,
                 w_ref, zs, buf, lvl_a, lvl_b, *, tiles_per_seq, layer):
    tm = x_ref.shape[0]
    tile_in_seq = pl.program_id(0) % tiles_per_seq

    @pl.when(pl.program_id(0) == 0)
    def _():
        w_ref[...] = wf_ref[...].astype(BF16)

    @pl.when(tile_in_seq == 0)
    def _():
        buf[0:POOL_HALO, :] = jnp.zeros((POOL_HALO, POOL_WIDTH), F32)

    for r0 in range(0, tm, INPROJ_ROWS):
        sl = pl.ds(r0, INPROJ_ROWS)
        xb = x_ref[sl, :].astype(BF16)

        def cols(lo, hi):
            return _dot(xb, w_ref[:, lo:hi])

        buf[pl.ds(POOL_HALO + r0, INPROJ_ROWS), :] = cols(0, 256)
        qk = cols(256, 768)
        q_ref[sl, :] = (qk[:, :ATTN_WIDTH] * (ATTN_HEAD_DIM ** -0.5 * LOG2_E)).astype(BF16)
        k_ref[sl, :LANES] = qk[:, ATTN_WIDTH:].astype(BF16)
        k_ref[sl, LANES:] = cols(768, 1024).astype(BF16)
        vs = cols(1024, 1536)
        v_ref[sl, :] = vs[:, :ATTN_WIDTH].astype(BF16)
        zs[0, sl, :] = vs[:, ATTN_WIDTH:]
        s_rest = cols(1536, 1792)
        zs[1, sl, :] = s_rest[:, :LANES]
        zs[2, sl, :] = s_rest[:, LANES:]

    u_pool = buf[POOL_HALO:, :]
    pos = tile_in_seq * tm + lax.broadcasted_iota(jnp.int32, (tm, 1), 0)
    yp_ref[...] = _pool_mix(u_pool, buf, lvl_a, lvl_b, pos, wpool_ref, _layer_row(pscale_ref, layer),
                            _layer_row(pgain_ref, layer))
    buf[0:POOL_HALO, :] = buf[tm:, :]

    chunks = us_ref.shape[0]
    for t in range(SSM_CHUNK):
        for v in range(SSM_LANE_BLOCKS):
            us_ref[:, _ssm_piece(t, v)] = zs[v, pl.ds(t, chunks, stride=SSM_CHUNK), :]


def _inproj(x, w_in, w_pool_blockdiag_bf16, pool_scale, pool_gain, layer, batch):
    t = x.shape[0]
    tm = INPROJ_TOKENS
    row = lambda width: pl.BlockSpec((tm, width), lambda i: (i, 0))
    full = lambda a: _layer_block(a, layer)
    chunk_rows = pl.BlockSpec((tm // SSM_CHUNK, SSM_CHUNK * SSM_WIDTH), lambda i: (i, 0))
    return pl.pallas_call(
        functools.partial(_inproj_body, tiles_per_seq=t // batch // tm, layer=layer),
        grid=(t // tm,),
        in_specs=[row(D_MODEL), _layer_block(w_in, layer, pipeline_mode=pl.Buffered(1)),
                  full(w_pool_blockdiag_bf16), _whole(pool_scale), _whole(pool_gain)],
        out_specs=[row(POOL_WIDTH), row(ATTN_WIDTH), row(ATTN_WIDTH), row(ATTN_WIDTH), chunk_rows],
        out_shape=[jax.ShapeDtypeStruct((t, POOL_WIDTH), BF16),
                   jax.ShapeDtypeStruct((t, ATTN_WIDTH), BF16),
                   jax.ShapeDtypeStruct((t, ATTN_WIDTH), BF16),
                   jax.ShapeDtypeStruct((t, ATTN_WIDTH), BF16),
                   jax.ShapeDtypeStruct((t // SSM_CHUNK, SSM_CHUNK * SSM_WIDTH), F32)],
        scratch_shapes=[pltpu.VMEM(w_in.shape[1:], BF16), pltpu.VMEM((SSM_LANE_BLOCKS, tm, LANES), F32)]
        + [pltpu.VMEM((POOL_HALO + tm, POOL_WIDTH), F32)] * 3,
        compiler_params=_cparams("arbitrary"),
        name="inproj",
    )(x, w_in, w_pool_blockdiag_bf16, pool_scale, pool_gain)


def _attn_bias_rows(rel_bias):
    x = np.arange(ATTN_BIAS_ROW)
    x = np.where(x < ATTN_BAND_TOKENS, x, x - ATTN_BIAS_ROW)
    idx = np.clip(N_PREV_CHUNKS * CHUNK - x, -REL_CLIP, REL_CLIP) + REL_CLIP
    return rel_bias.astype(F32)[..., idx] * LOG2_E


def _attn_body(q_ref, kprev_ref, kcur_ref, vprev_ref, vcur_ref, rows_ref, g_ref, o_ref, bias_s, *, layer):
    b = pl.program_id(0)
    i = pl.program_id(1)
    tq = ATTN_Q_TOKENS

    @pl.when((b == 0) & (i == 0))
    def _():
        qc = lax.broadcasted_iota(jnp.int32, (tq, ATTN_BAND_TOKENS), 0) // CHUNK
        kc = lax.broadcasted_iota(jnp.int32, (tq, ATTN_BAND_TOKENS), 1) // CHUNK
        in_band = (kc >= qc) & (kc <= qc + N_PREV_CHUNKS)
        for head in range(ATTN_HEADS):
            full = jnp.broadcast_to(rows_ref[head:head + 1, :], (tq, ATTN_BIAS_ROW))
            shifted = pltpu.roll(full, 0, 1, stride=1, stride_axis=0)
            bias_s[head // 2, (head % 2) * tq:(head % 2 + 1) * tq, :] = jnp.where(
                in_band, shifted[:, :ATTN_BAND_TOKENS], -jnp.inf)

    upper_half = lax.broadcasted_iota(jnp.int32, (1, LANES), 1) >= ATTN_HEAD_DIM

    def band(prev_ref, cur_ref, blk, sl):
        start = blk * tq
        if start < ATTN_PREV_TOKENS:
            return jnp.concatenate([prev_ref[start:, sl], cur_ref[0:start + tq, sl]], axis=0)
        return cur_ref[start - ATTN_PREV_TOKENS:start + tq, sl]

    def heads(blk, masked_keys):
        rows = pl.ds(blk * tq, tq)
        outs = []
        for pair in range(ATTN_WIDTH // LANES):
            sl = slice(pair * LANES, (pair + 1) * LANES)
            qp = q_ref[rows, sl]
            kp = band(kprev_ref, kcur_ref, blk, sl)
            vp = band(vprev_ref, vcur_ref, blk, sl)
            zero = jnp.zeros_like(qp)
            q2 = jnp.concatenate([jnp.where(upper_half, zero, qp), jnp.where(upper_half, qp, zero)], axis=0)
            s = lax.dot_general(q2, kp, (((1,), (1,)), ((), ())), preferred_element_type=F32)
            s = s + bias_s[pair]
            if masked_keys:
                key = lax.broadcasted_iota(jnp.int32, (1, ATTN_BAND_TOKENS), 1)
                s = jnp.where(key < masked_keys, -jnp.inf, s)
            m = jnp.max(s, axis=-1, keepdims=True)
            p = jnp.exp2(s - m)
            l = jnp.sum(p, axis=-1, keepdims=True)
            o = _dot(p.astype(BF16), vp) * (1.0 / l)
            outs.append(jnp.where(upper_half, o[tq:, :], o[:tq, :]))
        ss = sum(jnp.sum(o * o, axis=-1, keepdims=True) for o in outs)
        r = lax.rsqrt(ss / ATTN_WIDTH + NORM_EPS)
        for pair, o in enumerate(outs):
            sl = slice(pair * LANES, (pair + 1) * LANES)
            o_ref[rows, sl] = (o * r * g_ref[layer:layer + 1, sl]).astype(BF16)

    @pl.when(i == 0)
    def _():
        for blk in range(ATTN_BLOCKS_PER_STEP):
            heads(blk, max(ATTN_PREV_TOKENS - blk * tq, 0))

    @pl.when(i > 0)
    def _():
        for blk in range(ATTN_BLOCKS_PER_STEP):
            heads(blk, 0)


def _attention(q, k, v, bias_rows, gain, layer, batch):
    t = q.shape[0]
    ts = ATTN_BLOCKS_PER_STEP * ATTN_Q_TOKENS
    steps = t // batch // ts
    prev_per_step = ts // ATTN_PREV_TOKENS

    cur = pl.BlockSpec((ts, ATTN_WIDTH), lambda b, i: (b * steps + i, 0))
    prev = pl.BlockSpec((ATTN_PREV_TOKENS, ATTN_WIDTH),
                        lambda b, i: (jnp.maximum((b * steps + i) * prev_per_step - 1, 0), 0))

    return pl.pallas_call(
        functools.partial(_attn_body, layer=layer),
        grid=(batch, steps),
        in_specs=[cur, prev, cur, prev, cur,
                  _layer_block(bias_rows, layer), _whole(gain)],
        out_specs=cur,
        out_shape=jax.ShapeDtypeStruct((t, ATTN_WIDTH), BF16),
        scratch_shapes=[pltpu.VMEM((ATTN_HEADS // 2, 2 * ATTN_Q_TOKENS, ATTN_BAND_TOKENS), F32)],
        compiler_params=_cparams("arbitrary", "arbitrary"),
        name="attention",
    )(q, k, k, v, v, bias_rows, gain)


def _s5_position_of_time(g, time):
    return SSM_SLOTS * (time // SSM_SLOTS) + (time % SSM_SLOTS + g) % SSM_SLOTS


def _s5_tables(a_re, a_im, log_dt, b_re, b_im, c_re, c_im):
    hi = lax.Precision.HIGHEST
    tc = SSM_CHUNK
    g, p_dim = a_re.shape
    dt = jnp.exp(log_dt.astype(F32))[:, None]
    ar = a_re.astype(F32)
    ai = a_im.astype(F32)
    mag = jnp.exp(ar * dt)
    abar_re = mag * jnp.cos(ai * dt)
    abar_im = mag * jnp.sin(ai * dt)
    den = ar * ar + ai * ai
    nr = abar_re - 1.0
    ni = abar_im
    coef_re = ((nr * ar + ni * ai) / den)[..., None]
    coef_im = ((ni * ar - nr * ai) / den)[..., None]
    br = b_re.astype(F32)
    bi = b_im.astype(F32)
    bbar_re = coef_re * br - coef_im * bi
    bbar_im = coef_re * bi + coef_im * br
    n = jnp.arange(tc + 1, dtype=F32)
    pmag = jnp.exp((ar * dt)[..., None] * n)
    pw_re = pmag * jnp.cos((ai * dt)[..., None] * n)
    pw_im = pmag * jnp.sin((ai * dt)[..., None] * n)
    cw = tc * SSM_GROUP_DIM
    lag_rep = jnp.asarray(np.kron(np.eye(tc), np.ones((1, SSM_GROUP_DIM))), F32)
    ch_rep = jnp.asarray(np.kron(np.ones((1, tc)), np.eye(SSM_GROUP_DIM)), F32)
    expand = lambda a, rep: jnp.einsum('gpn,nx->gpx', a, rep, precision=hi)
    c_re_rep = expand(c_re.astype(F32).transpose(0, 2, 1), ch_rep)
    c_im_rep = expand(c_im.astype(F32).transpose(0, 2, 1), ch_rep)

    def output_coefficients(first_power, rep, subscripts):
        p_re = jnp.einsum(subscripts, pw_re[..., first_power:first_power + tc], rep, precision=hi)
        p_im = jnp.einsum(subscripts, pw_im[..., first_power:first_power + tc], rep, precision=hi)
        return c_re_rep * p_re - c_im_rep * p_im, -(c_re_rep * p_im + c_im_rep * p_re)

    on_re, on_im = output_coefficients(0, lag_rep, 'gpn,nx->gpx')
    kern = (jnp.einsum('gpk,gpx->gkx', bbar_re, on_re, precision=hi)
            + jnp.einsum('gpk,gpx->gkx', bbar_im, on_im, precision=hi))
    position = np.arange(tc)
    time_at = (SSM_SLOTS * (position // SSM_SLOTS)
               + (position % SSM_SLOTS - np.arange(g)[:, None]) % SSM_SLOTS)
    slot_rep = np.repeat(time_at[:, None, :] == np.arange(tc)[None, :, None], SSM_GROUP_DIM, axis=2)
    inter = jnp.concatenate(output_coefficients(1, jnp.asarray(slot_rep, F32), 'gpn,gnx->gpx'), axis=1)
    twice = lambda a: jnp.concatenate([a, a], -1)
    pt_re = twice(pw_re[..., :tc].transpose(0, 2, 1))[:, :, None, :]
    pt_im = twice(pw_im[..., :tc].transpose(0, 2, 1))[:, :, None, :]
    bt_re = bbar_re.transpose(0, 2, 1)[:, None]
    bt_im = bbar_im.transpose(0, 2, 1)[:, None]
    side = lambda a, b: jnp.concatenate([a, b], -1)
    est = (pt_re * side(bt_re, bt_im) + pt_im * side(-bt_im, bt_re)).reshape(g, cw, 2 * p_dim)
    est_swapped = (pt_re * side(bt_im, bt_re) + pt_im * side(bt_re, -bt_im)).reshape(g, cw, 2 * p_dim)
    return dict(
        kern=kern,
        est=est,
        est_swapped=est_swapped,
        inter=inter,
        apow_re=pw_re[..., tc].reshape(1, g * p_dim), apow_im=pw_im[..., tc].reshape(1, g * p_dim))


def _s5_prepare(kern_ref, est_ref, estsw_ref, int_ref, toep_s, est_s, int_s):
    lane = lax.broadcasted_iota(jnp.int32, (1, LANES), 1)
    slot = lane // SSM_GROUP_DIM
    zero = jnp.zeros((SSM_GROUP_DIM, LANES), F32)
    zero_rows = jnp.zeros((SSM_STATE, SSM_CHUNK_WIDTH), BF16)
    for g in range(SSM_GROUPS):
        mine = (lane >= SSM_STATE) if g % 2 else (lane < SSM_STATE)
        turn = g % SSM_SLOTS
        time_slot = (slot - turn) % SSM_SLOTS
        k0 = [kern_ref[g, :, 0:LANES]]
        k1 = [kern_ref[g, :, LANES:2 * LANES]]
        for r in range(1, SSM_SLOTS):
            k0.append(pltpu.roll(k0[0], SSM_GROUP_DIM * r, 1))
            k1.append(pltpu.roll(k1[0], SSM_GROUP_DIM * r, 1))
        for time in range(SSM_CHUNK):
            r = (time + turn) % SSM_SLOTS
            later = time_slot >= time % SSM_SLOTS
            if time < SSM_SLOTS:
                h0 = jnp.where(later, k0[r], zero)
                h1 = jnp.where(later, k1[r], k0[r])
            else:
                h0 = zero
                h1 = jnp.where(later, k0[r], zero)
            rows = pl.ds(SSM_GROUP_DIM * _s5_position_of_time(g, time), SSM_GROUP_DIM)
            toep_s[g, rows, 0:LANES] = h0.astype(BF16)
            toep_s[g, rows, LANES:2 * LANES] = h1.astype(BF16)
            src = pl.ds(SSM_GROUP_DIM * (SSM_CHUNK - 1 - time), SSM_GROUP_DIM)
            e, e_swapped = est_ref[g, src, :], estsw_ref[g, src, :]
            e_re, e_im = (e_swapped, e) if g % 2 else (e, e_swapped)
            est_s[g, rows, 0:LANES] = jnp.where(mine, e_re, zero).astype(BF16)
            est_s[g, rows, LANES:2 * LANES] = jnp.where(mine, e_im, zero).astype(BF16)
        for part in range(2):
            src = pl.ds(part * SSM_STATE, SSM_STATE)
            base = part * SSM_PAIR_WIDTH
            int_s[g, pl.ds(base + (1 - g % 2) * SSM_STATE, SSM_STATE), :] = zero_rows
            int_s[g, pl.ds(base + (g % 2) * SSM_STATE, SSM_STATE), :] = int_ref[g, src, :].astype(BF16)


def _s5_body(u_ref, kern_ref, estin_ref, estswin_ref, intin_ref, apre_ref, apim_ref, d_ref, y_ref,
             toep_ref, est_ref, int_ref, ub_s, ere_s, eim_s, spre_s, spim_s, sre_s, sim_s, *, layer):
    rows = u_ref.shape[0]

    @pl.when((pl.program_id(0) == 0) & (pl.program_id(1) == 0))
    def _():
        _s5_prepare(kern_ref, estin_ref, estswin_ref, intin_ref, toep_ref, est_ref, int_ref)

    @pl.when(pl.program_id(1) == 0)
    def _():
        sre_s[...] = jnp.zeros_like(sre_s)
        sim_s[...] = jnp.zeros_like(sim_s)

    slot = lax.broadcasted_iota(jnp.int32, (1, LANES), 1) // SSM_GROUP_DIM
    slot_bits = [(b, (slot & b) != 0) for b in (1, 2, 4)]
    piece = _ssm_piece

    for v in range(SSM_LANE_BLOCKS):
        for m in range(SSM_TIME_BLOCKS):
            rot = []
            for j in range(SSM_SLOTS):
                a = u_ref[:, piece(SSM_SLOTS * m + j, v)]
                rot.append(a if j == 0 else pltpu.roll(a, SSM_GROUP_DIM * j, 1))
            for bit in slot_bits:
                rot = [jnp.where(bit[1], rot[(i + bit[0]) % SSM_SLOTS], rot[i]) for i in range(SSM_SLOTS)]
            for gam in range(SSM_SLOTS):
                ub_s[SSM_SLOTS * v + gam, :, m * LANES:(m + 1) * LANES] = rot[-gam % SSM_SLOTS].astype(BF16)

    for q in range(SSM_GROUPS // 2):
        e = _dot(ub_s[2 * q], est_ref[2 * q]) + _dot(ub_s[2 * q + 1], est_ref[2 * q + 1])
        ere_s[:, q * LANES:(q + 1) * LANES] = e[:, :SSM_PAIR_WIDTH]
        eim_s[:, q * LANES:(q + 1) * LANES] = e[:, SSM_PAIR_WIDTH:]

    a_re = apre_ref[...]
    a_im = apim_ref[...]

    def carry_step(r, carry):
        s_re, s_im = carry
        spre_s[pl.ds(r, 1), :] = s_re
        spim_s[pl.ds(r, 1), :] = s_im
        e_re = ere_s[pl.ds(r, 1), :]
        e_im = eim_s[pl.ds(r, 1), :]
        return (a_re * s_re - a_im * s_im + e_re, a_re * s_im + a_im * s_re + e_im)

    s_re, s_im = lax.fori_loop(0, rows, carry_step, (sre_s[...], sim_s[...]))
    sre_s[...] = s_re
    sim_s[...] = s_im

    for v in range(SSM_LANE_BLOCKS):
        yg = []
        for gam in range(SSM_SLOTS):
            g = SSM_SLOTS * v + gam
            q = g // 2
            sp = jnp.concatenate([spre_s[:, q * LANES:(q + 1) * LANES], spim_s[:, q * LANES:(q + 1) * LANES]],
                                 axis=1).astype(BF16)
            yg.append(_dot(ub_s[g], toep_ref[g]) + _dot(sp, int_ref[g]))
        d = d_ref[layer:layer + 1, v * LANES:(v + 1) * LANES]
        for m in range(SSM_TIME_BLOCKS):
            back = [yg[-i % SSM_SLOTS][:, m * LANES:(m + 1) * LANES] for i in range(SSM_SLOTS)]
            for bit in slot_bits:
                back = [jnp.where(bit[1], back[(i - bit[0]) % SSM_SLOTS], back[i]) for i in range(SSM_SLOTS)]
            for j in range(SSM_SLOTS):
                o = back[j]
                if j:
                    o = pltpu.roll(o, LANES - SSM_GROUP_DIM * j, 1)
                sl = piece(SSM_SLOTS * m + j, v)
                y_ref[:, sl] = jax.nn.gelu(o + d * u_ref[:, sl])


def _s5(u_rows, tab, d_skip, layer, batch):
    nch, width = u_rows.shape
    rows = SSM_ROWS
    steps = nch // batch // rows
    once = lambda a: _layer_block(a, layer, pipeline_mode=pl.Buffered(1))
    blk = pl.BlockSpec((rows, width), lambda b, i: (b * steps + i, 0))
    table = pltpu.VMEM((SSM_GROUPS, SSM_CHUNK_WIDTH, SSM_CHUNK_WIDTH), BF16)
    state = pltpu.VMEM((rows, SSM_STATE_LANES), F32)
    carry = pltpu.VMEM((1, SSM_STATE_LANES), F32)
    return pl.pallas_call(
        functools.partial(_s5_body, layer=layer),
        grid=(batch, steps),
        in_specs=[blk, once(tab['kern']), once(tab['est']), once(tab['est_swapped']), once(tab['inter']),
                  once(tab['apow_re']), once(tab['apow_im']), _whole(d_skip)],
        out_specs=blk,
        out_shape=jax.ShapeDtypeStruct((nch, width), F32),
        scratch_shapes=[table, table, table,
                        pltpu.VMEM((SSM_GROUPS, rows, SSM_CHUNK_WIDTH), BF16), state, state, state, state,
                        carry, carry],
        compiler_params=_cparams("arbitrary", "arbitrary"),
        name="s5",
    )(u_rows, tab['kern'], tab['est'], tab['est_swapped'], tab['inter'], tab['apow_re'], tab['apow_im'], d_skip)


def _route_rows(scores, biased):
    ng = N_EXPERTS // EXPERTS_PER_GROUP
    group_score = []
    for gi in range(ng):
        a, b, c, d = biased[gi * EXPERTS_PER_GROUP:(gi + 1) * EXPERTS_PER_GROUP]
        hi1, lo1 = jnp.maximum(a, b), jnp.minimum(a, b)
        hi2, lo2 = jnp.maximum(c, d), jnp.minimum(c, d)
        top1 = jnp.maximum(hi1, hi2)
        top2 = jnp.maximum(jnp.minimum(hi1, hi2), jnp.maximum(lo1, lo2))
        group_score.append(top1 + top2)
    best = group_score[0]
    best_idx = jnp.zeros_like(best, dtype=jnp.int32)
    for gi in range(1, ng):
        better = group_score[gi] > best
        best = jnp.where(better, group_score[gi], best)
        best_idx = jnp.where(better, gi, best_idx)
    picked = []
    for e in range(N_EXPERTS):
        gi = e // EXPERTS_PER_GROUP
        rank = jnp.zeros_like(best_idx)
        for o in range(gi * EXPERTS_PER_GROUP, (gi + 1) * EXPERTS_PER_GROUP):
            if o == e:
                continue
            ahead = (biased[o] > biased[e]) | ((biased[o] == biased[e]) & (o < e))
            rank = rank + ahead.astype(jnp.int32)
        picked.append((best_idx == gi) & (rank < 2))
    wsum = sum(jnp.where(picked[e], scores[e], 0.0) for e in range(N_EXPERTS))
    return [jnp.where(picked[e], scores[e] / wsum, 0.0) for e in range(N_EXPERTS)], best_idx


def _group_sort_positions(best_idx, before_ref):
    ng = N_EXPERTS // EXPERTS_PER_GROUP
    tokens = best_idx.shape[1]
    member = [(best_idx == gi).astype(F32) for gi in range(ng)]
    stacked = jnp.concatenate(member + [jnp.zeros((SUBLANES - ng, tokens), F32)], axis=0)
    parts = []
    run = jnp.zeros((SUBLANES, 1), F32)
    for blk in range(tokens // LANES):
        piece = stacked[:, blk * LANES:(blk + 1) * LANES]
        parts.append(_dot(piece.astype(BF16), before_ref[...]) + run)
        run = run + jnp.sum(piece, axis=1, keepdims=True)
    earlier = jnp.concatenate(parts, axis=1)
    counts = [run[gi:gi + 1, :] for gi in range(ng)]
    pos = jnp.zeros_like(member[0])
    start = jnp.zeros_like(counts[0])
    for gi in range(ng):
        pos = pos + member[gi] * (start + earlier[gi:gi + 1, :])
        start = start + counts[gi]
    return pos, counts


def _outproj_body(x_ref, yp_ref, ya_ref, ys_ref, wgluf_ref, bglu_ref, gssm_ref, woutf_ref, g_ref, b_ref,
                  wr_ref, rb_ref, before_ref, h_ref, comb_ref, pos_ref, cnt_ref, ys_s, wglu_ref, wout_ref,
                  *, layer):
    @pl.when(pl.program_id(0) == 0)
    def _():
        wglu_ref[...] = wgluf_ref[...].astype(BF16)
        wout_ref[...] = woutf_ref[...].astype(BF16)

    chunks = ys_ref.shape[0]
    for t in range(SSM_CHUNK):
        for v in range(SSM_LANE_BLOCKS):
            ys_s[v, pl.ds(t, chunks, stride=SSM_CHUNK), :] = ys_ref[:, _ssm_piece(t, v)]
    def rows_block(sl):
        ys = jnp.concatenate([ys_s[v, sl, :] for v in range(SSM_LANE_BLOCKS)], axis=1)
        gate = jax.nn.sigmoid(_dot(ys.astype(BF16), wglu_ref[...]) + _layer_row(bglu_ref, layer))
        ys = ys * gate
        r = lax.rsqrt(jnp.mean(ys * ys, axis=-1, keepdims=True) + NORM_EPS)
        ysn = (ys * r * _layer_row(gssm_ref, layer)).astype(BF16)
        mix = _dot(jnp.concatenate([yp_ref[sl, :], ya_ref[sl, :], ysn], axis=1), wout_ref[...])
        h = _layer_norm(DN_ALPHA * x_ref[sl, :] + mix, _layer_row(g_ref, layer), _layer_row(b_ref, layer))
        h_ref[sl, :] = h
        return _dot(h.astype(BF16), wr_ref[...])

    tokens = x_ref.shape[0]
    parts = jnp.concatenate([rows_block(pl.ds(r0, OUT_ROWS)) for r0 in range(0, tokens, OUT_ROWS)], axis=0)
    parts_t = parts.T
    sc = jax.nn.sigmoid(parts_t[:N_EXPERTS, :] + parts_t[N_EXPERTS:2 * N_EXPERTS, :])
    bs = sc + rb_ref[...]
    scores = [sc[e:e + 1, :] for e in range(N_EXPERTS)]
    biased = [bs[e:e + 1, :] for e in range(N_EXPERTS)]
    comb_rows, best_idx = _route_rows(scores, biased)
    pos, counts = _group_sort_positions(best_idx, before_ref)
    comb_t = jnp.concatenate(comb_rows + [pos, jnp.zeros((LANES - N_EXPERTS - 1, tokens), F32)], axis=0)
    comb_ref[...] = comb_t.T
    pos_ref[...] = jnp.concatenate([pos, jnp.zeros((SUBLANES - 1, tokens), F32)], axis=0)
    cnt_ref[...] = jnp.concatenate(
        [jnp.broadcast_to(c, (1, LANES)) for c in counts]
        + [jnp.zeros((SUBLANES - len(counts), LANES), F32)], axis=0).astype(jnp.int32)


def _outproj(x, y_pool, y_attn, y_ssm_rows, w_glu, b_glu, g_ssm, w_out, ln_g, ln_b,
             w_router_split, router_bias, layer):
    t = x.shape[0]
    tm = MOE_TOKENS
    nt = t // tm
    row = lambda width: pl.BlockSpec((tm, width), lambda i: (i, 0))
    full = _whole
    once = lambda a: _layer_block(a, layer, pipeline_mode=pl.Buffered(1))
    token = np.arange(LANES)
    before = jnp.asarray(token[:, None] < token[None, :], BF16)
    return pl.pallas_call(
        functools.partial(_outproj_body, layer=layer),
        grid=(nt,),
        in_specs=[row(D_MODEL), row(POOL_WIDTH), row(ATTN_WIDTH),
                  pl.BlockSpec((tm // SSM_CHUNK, SSM_CHUNK * SSM_WIDTH), lambda i: (i, 0)),
                  once(w_glu), _whole(b_glu), _whole(g_ssm), once(w_out), _whole(ln_g), _whole(ln_b), full(w_router_split), full(router_bias), full(before)],
        out_specs=[row(D_MODEL), row(LANES), pl.BlockSpec((SUBLANES, tm), lambda i: (0, i)),
                   pl.BlockSpec((SUBLANES, LANES), lambda i: (i, 0))],
        out_shape=[jax.ShapeDtypeStruct((t, D_MODEL), F32), jax.ShapeDtypeStruct((t, LANES), F32),
                   jax.ShapeDtypeStruct((SUBLANES, t), F32),
                   jax.ShapeDtypeStruct((SUBLANES * nt, LANES), jnp.int32)],
        scratch_shapes=[pltpu.VMEM((SSM_LANE_BLOCKS, tm, LANES), F32),
                        pltpu.VMEM(w_glu.shape[1:], BF16), pltpu.VMEM(w_out.shape[1:], BF16)],
        compiler_params=_cparams("arbitrary"),
        name="outproj",
    )(x, y_pool, y_attn, y_ssm_rows, w_glu, b_glu, g_ssm, w_out, ln_g, ln_b,
      w_router_split, router_bias, before)


def _moe_body(cnt_ref, h_ref, comb_ref, pos_ref, p_ref, wg_ref, wu_ref, wd_ref, wpgf_ref, wppf_ref, g_ref, b_ref,
              o_ref, hs_s, cs_s, acc_s, ple_s, wpg_ref, wpp_ref, *, layer):
    i = pl.program_id(0)
    group = pl.program_id(1)
    ng = pl.num_programs(1)
    tm = h_ref.shape[0]

    @pl.when((i == 0) & (group == 0))
    def _():
        wpg_ref[...] = wpgf_ref[...].astype(BF16)
        wpp_ref[...] = wppf_ref[...].astype(BF16)

    @pl.when(group == 0)
    def _():
        hb = h_ref[...].astype(BF16)
        comb = comb_ref[...]
        comb_lo = comb - comb.astype(BF16).astype(F32)
        low_lanes = lax.broadcasted_iota(jnp.int32, (1, LANES), 1) < COMB_LO_LANE
        comb_b = jnp.where(low_lanes, comb, pltpu.roll(comb_lo, COMB_LO_LANE, 1)).astype(BF16)
        for r0 in range(0, tm, MOE_SIDE_ROWS):
            sl = pl.ds(r0, MOE_SIDE_ROWS)
            row = r0 + lax.broadcasted_iota(jnp.int32, (MOE_SIDE_ROWS, tm), 0)
            perm = jnp.where(pos_ref[0:1, :] == row.astype(F32), 1.0, 0.0).astype(BF16)
            hs_s[sl, :] = _dot(perm, hb).astype(BF16)
            both = _dot(perm, comb_b)
            cs_s[sl, :] = both + pltpu.roll(both, LANES - COMB_LO_LANE, 1)
        acc_s[...] = jnp.zeros_like(acc_s)

    count = cnt_ref[i * ng + group]
    start = jnp.int32(0)
    for gi in range(N_EXPERTS // EXPERTS_PER_GROUP - 1):
        start = start + jnp.where(group > gi, cnt_ref[i * ng + gi], 0)
    lane = lax.broadcasted_iota(jnp.int32, (1, LANES), 1)
    first = (start // BF16_ROW_PACK) * BF16_ROW_PACK
    need = start - first + count

    def window(w, size):
        wanted = first + w * size
        lo = pl.multiple_of(jnp.minimum(wanted, tm - size), BF16_ROW_PACK)
        rows = pl.ds(lo, size)
        x = hs_s[rows, :]
        fresh = lo + lax.broadcasted_iota(jnp.int32, (size, 1), 0) >= wanted
        cs = jnp.where(fresh, cs_s[rows, :], 0.0)
        total = None
        for e in range(EXPERTS_PER_GROUP):
            gate = _dot(x, wg_ref[e])
            up = _dot(x, wu_ref[e])
            c = jnp.sum(jnp.where(lane == group * EXPERTS_PER_GROUP + e, cs, 0.0), axis=1, keepdims=True)
            a = (jax.nn.silu(gate) * up * c).astype(BF16)
            d = _dot(a, wd_ref[e])
            total = d if total is None else total + d
        acc_s[rows, :] += total

    below = 0
    for size in MOE_WINDOW_SIZES[:-1]:
        @pl.when((need > below) & (need <= size))
        def _():
            window(0, size)
        below = size
    largest = MOE_WINDOW_SIZES[-1]

    @pl.when(need > below)
    def _():
        def step(w, carry):
            window(w, largest)
            return carry
        lax.fori_loop(0, (need + largest - 1) // largest, step, 0)

    half = tm // 2
    for mid in (1, 2):
        @pl.when(group == mid)
        def _():
            for r0 in range((mid - 1) * half, mid * half, MOE_SIDE_ROWS):
                sl = pl.ds(r0, MOE_SIDE_ROWS)
                hb = h_ref[sl, :].astype(BF16)
                ple_s[sl, :] = (jax.nn.sigmoid(_dot(hb, wpg_ref[...]))
                                * _dot(p_ref[sl, :].astype(BF16), wpp_ref[...]))

    @pl.when(group == ng - 1)
    def _():
        sorted_out = acc_s[...].astype(BF16)
        col = lax.broadcasted_iota(jnp.int32, (MOE_SIDE_ROWS, tm), 1).astype(F32)
        for r0 in range(0, tm, MOE_SIDE_ROWS):
            sl = pl.ds(r0, MOE_SIDE_ROWS)
            unperm = jnp.where(comb_ref[sl, SORT_POS_LANE:SORT_POS_LANE + 1] == col, 1.0, 0.0).astype(BF16)
            ffn = _dot(unperm, sorted_out)
            o_ref[sl, :] = _layer_norm(DN_ALPHA * h_ref[sl, :] + ffn + ple_s[sl, :], _layer_row(g_ref, layer),
                                      _layer_row(b_ref, layer))


def _moe(h, comb, pos_rows, counts, p_all, layer, wg_bf16, wu_bf16, wd_bf16, w_ple_gate, w_ple_proj, ln_g, ln_b):
    t = h.shape[0]
    tm = MOE_TOKENS
    nt = t // tm
    ng = N_EXPERTS // EXPERTS_PER_GROUP
    once = lambda a: _layer_block(a, layer, pipeline_mode=pl.Buffered(1))
    experts = lambda rows, cols: pl.BlockSpec((EXPERTS_PER_GROUP, rows, cols),
                                              lambda i, g, cnt: (layer * ng + g, 0, 0))
    grid_spec = pltpu.PrefetchScalarGridSpec(
        num_scalar_prefetch=1,
        grid=(nt, ng),
        in_specs=[pl.BlockSpec((tm, D_MODEL), lambda i, g, cnt: (i, 0)),
                  pl.BlockSpec((tm, LANES), lambda i, g, cnt: (i, 0)),
                  pl.BlockSpec((SUBLANES, tm), lambda i, g, cnt: (0, i)),
                  pl.BlockSpec((tm, PLE_DIM), lambda i, g, cnt: (layer * nt + i, 0)),
                  experts(D_MODEL, D_EXPERT), experts(D_MODEL, D_EXPERT), experts(D_EXPERT, D_MODEL),
                  once(w_ple_gate), once(w_ple_proj), _whole(ln_g), _whole(ln_b)],
        out_specs=pl.BlockSpec((tm, D_MODEL), lambda i, g, cnt: (i, 0)),
        scratch_shapes=[pltpu.VMEM((tm, D_MODEL), BF16), pltpu.VMEM((tm, LANES), F32),
                        pltpu.VMEM((tm, D_MODEL), F32), pltpu.VMEM((tm, D_MODEL), F32),
                        pltpu.VMEM(w_ple_gate.shape[1:], BF16), pltpu.VMEM(w_ple_proj.shape[1:], BF16)])
    return pl.pallas_call(
        functools.partial(_moe_body, layer=layer),
        grid_spec=grid_spec,
        out_shape=jax.ShapeDtypeStruct((t, D_MODEL), F32),
        compiler_params=_cparams("arbitrary", "arbitrary"),
        name="moe",
    )(counts, h, comb, pos_rows, p_all, wg_bf16, wu_bf16, wd_bf16, w_ple_gate, w_ple_proj, ln_g, ln_b)


def _block_diag(w):
    g, n, m = w.shape
    eye = jnp.eye(g, dtype=w.dtype)
    return (eye[:, None, :, None] * w[:, :, None, :]).reshape(g * n, g * m)


def kernel(x, p, w_in, w_out, w_pool, pool_scale, rel_bias, ssm_a_re, ssm_a_im, ssm_log_dt, ssm_b_re, ssm_b_im,
           ssm_c_re, ssm_c_im, ssm_d, w_glu, b_glu, g_pool, g_attn, g_ssm, ln1_g, ln1_b, ln2_g, ln2_b,
           w_router, router_bias, w_exp_gate, w_exp_up, w_exp_down, w_ple_gate, w_ple_proj):
    batch, seq, d = x.shape
    t = batch * seq
    xt = x.reshape(t, d)
    p_all = p.reshape(DEPTH * t, PLE_DIM)

    f32 = vec = lambda a: a.astype(F32)
    w_pool_b = jax.vmap(_block_diag)(w_pool).astype(BF16)
    bias_rows = _attn_bias_rows(rel_bias)
    tables = jax.vmap(_s5_tables)(ssm_a_re, ssm_a_im, ssm_log_dt, ssm_b_re, ssm_b_im, ssm_c_re, ssm_c_im)
    wr = w_router.astype(F32)
    wr_hi = wr.astype(BF16)
    wr_lo = (wr - wr_hi.astype(F32)).astype(BF16)
    wr_split = jnp.pad(jnp.concatenate([wr_hi, wr_lo], axis=1), ((0, 0), (0, LANES - 2 * N_EXPERTS)))
    r_bias = router_bias.astype(F32).reshape(N_EXPERTS, 1)
    stack_experts = lambda w: w.astype(BF16).reshape((DEPTH * N_EXPERTS,) + w.shape[2:])
    wg_b, wu_b, wd_b = stack_experts(w_exp_gate), stack_experts(w_exp_up), stack_experts(w_exp_down)
    ng = N_EXPERTS // EXPERTS_PER_GROUP

    for layer in range(DEPTH):
        y_pool, q, k, v, u_ssm = _inproj(xt, f32(w_in), w_pool_b, vec(pool_scale), vec(g_pool), layer, batch)
        y_attn = _attention(q, k, v, bias_rows, vec(g_attn), layer, batch)
        y_ssm = _s5(u_ssm, tables, vec(ssm_d), layer, batch)
        h, comb, pos_rows, cnt = _outproj(xt, y_pool, y_attn, y_ssm, f32(w_glu), vec(b_glu), vec(g_ssm),
                                          f32(w_out), vec(ln1_g), vec(ln1_b), wr_split, r_bias, layer)
        counts = cnt[:, 0].reshape(-1, SUBLANES)[:, :ng].reshape(-1)
        xt = _moe(h, comb, pos_rows, counts, p_all, layer, wg_b, wu_b, wd_b, f32(w_ple_gate), f32(w_ple_proj),
                  vec(ln2_g), vec(ln2_b))
    return xt.reshape(batch, seq, d)
```

```python
import functools
import math

import numpy as np
import jax
import jax.numpy as jnp
from jax import lax
from jax.experimental import pallas as pl
from jax.experimental.pallas import tpu as pltpu

F32 = jnp.float32
BF16 = jnp.bfloat16

D_MODEL = 1024
DEPTH = 2
CHUNK = 64
PLE_DIM = 256
POOL_WIDTH = 256
POOL_GROUP_DIM = 64
POOL_WINDOWS = (2, 4, 8, 16)
POOL_HALO = 32
ATTN_HEAD_DIM = 64
ATTN_HEADS = 6
ATTN_WIDTH = 384
N_PREV_CHUNKS = 8
REL_CLIP = 128
SSM_WIDTH = 384
SSM_GROUP_DIM = 16
SSM_GROUPS = 24
SSM_STATE = 64
N_EXPERTS = 16
EXPERTS_PER_GROUP = 4
D_EXPERT = 256
DN_ALPHA = (2 * DEPTH) ** 0.25
NORM_EPS = 1e-5
LOG2_E = math.log2(math.e)

LANES = 128
SUBLANES = 8
VMEM_LIMIT_BYTES = 56 * 1024 * 1024

INPROJ_TOKENS = 1024
INPROJ_ROWS = 512
ATTN_Q_CHUNKS = 4
ATTN_Q_TOKENS = ATTN_Q_CHUNKS * CHUNK
ATTN_BAND_TOKENS = 3 * ATTN_Q_TOKENS
ATTN_BLOCKS_PER_STEP = 4
ATTN_PREV_TOKENS = N_PREV_CHUNKS * CHUNK
ATTN_BIAS_ROW = 1024
SSM_CHUNK = 16
SSM_CHUNK_WIDTH = SSM_CHUNK * SSM_GROUP_DIM
SSM_ROWS = 128
SSM_SLOTS = LANES // SSM_GROUP_DIM
SSM_LANE_BLOCKS = SSM_WIDTH // LANES
SSM_TIME_BLOCKS = SSM_CHUNK // SSM_SLOTS
SSM_PAIR_WIDTH = 2 * SSM_STATE
SSM_STATE_LANES = SSM_GROUPS * SSM_STATE
OUT_ROWS = 256
MOE_SIDE_ROWS = 256
MOE_TOKENS = 1024
MOE_WINDOW_SIZES = (192, 224, 256, 288, 320, 352, 384)
BF16_ROW_PACK = 16
SORT_POS_LANE = N_EXPERTS
COMB_LO_LANE = 32


def _cparams(*sem):
    return pltpu.CompilerParams(dimension_semantics=sem, vmem_limit_bytes=VMEM_LIMIT_BYTES)


def _dot(a, b):
    return jnp.dot(a, b, preferred_element_type=F32)


def _layer_block(a, layer, **kwargs):
    return pl.BlockSpec((None,) + a.shape[1:], lambda *_: (layer,) + (0,) * (a.ndim - 1), **kwargs)


def _whole(a):
    return pl.BlockSpec(a.shape, lambda *_: (0,) * a.ndim)


def _layer_norm(v, g, b):
    mu = jnp.mean(v, axis=-1, keepdims=True)
    vc = v - mu
    var = jnp.mean(vc * vc, axis=-1, keepdims=True)
    return vc * lax.rsqrt(var + NORM_EPS) * g + b


def _ssm_piece(t, v):
    lo = SSM_WIDTH * t + LANES * v
    return slice(lo, lo + LANES)


def _layer_row(ref, layer):
    return ref[layer:layer + 1, :]


def _pool_mix(x0, buf, lvl_a, lvl_b, pos, w_ref, scale, gain):
    n = x0.shape[0] + POOL_HALO
    group = lax.broadcasted_iota(jnp.int32, (1, POOL_WIDTH), 1) // POOL_GROUP_DIM
    mean = jnp.zeros_like(x0)
    src, dst = buf, lvl_a
    for gi, w in enumerate(POOL_WINDOWS):
        lo = 8 * (gi + 1)
        dst[lo:n, :] = src[lo:n, :] + src[lo - w // 2:n - w // 2, :]
        inv_cnt = 1.0 / jnp.minimum(pos + 1, w).astype(F32)
        mean = jnp.where(group == gi, dst[POOL_HALO:n, :] * inv_cnt, mean)
        src, dst = dst, (lvl_b if dst is lvl_a else lvl_a)
    d = (mean - x0).astype(BF16)
    y = _dot(d, w_ref[...]) * scale
    r = lax.rsqrt(jnp.mean(y * y, axis=-1, keepdims=True) + NORM_EPS)
    return (y * r * gain).astype(BF16)


def _inproj_body(x_ref, wf_ref, wpool_ref, pscale_ref, pgain_ref, yp_ref, q_ref, k_ref, v_ref, us_ref,
                 w_ref, zs, buf, lvl_a, lvl_b, *, tiles_per_seq, layer):
    tm = x_ref.shape[0]
    tile_in_seq = pl.program_id(0) % tiles_per_seq

    @pl.when(pl.program_id(0) == 0)
    def _():
        w_ref[...] = wf_ref[...].astype(BF16)

    @pl.when(tile_in_seq == 0)
    def _():
        buf[0:POOL_HALO, :] = jnp.zeros((POOL_HALO, POOL_WIDTH), F32)

    for r0 in range(0, tm, INPROJ_ROWS):
        sl = pl.ds(r0, INPROJ_ROWS)
        xb = x_ref[sl, :].astype(BF16)

        def cols(lo, hi):
            return _dot(xb, w_ref[:, lo:hi])

        buf[pl.ds(POOL_HALO + r0, INPROJ_ROWS), :] = cols(0, 256)
        qk = cols(256, 768)
        q_ref[sl, :] = (qk[:, :ATTN_WIDTH] * (ATTN_HEAD_DIM ** -0.5 * LOG2_E)).astype(BF16)
        k_ref[sl, :LANES] = qk[:, ATTN_WIDTH:].astype(BF16)
        k_ref[sl, LANES:] = cols(768, 1024).astype(BF16)
        vs = cols(1024, 1536)
        v_ref[sl, :] = vs[:, :ATTN_WIDTH].astype(BF16)
        zs[0, sl, :] = vs[:, ATTN_WIDTH:]
        s_rest = cols(1536, 1792)
        zs[1, sl, :] = s_rest[:, :LANES]
        zs[2, sl, :] = s_rest[:, LANES:]

    u_pool = buf[POOL_HALO:, :]
    pos = tile_in_seq * tm + lax.broadcasted_iota(jnp.int32, (tm, 1), 0)
    yp_ref[...] = _pool_mix(u_pool, buf, lvl_a, lvl_b, pos, wpool_ref, _layer_row(pscale_ref, layer),
                            _layer_row(pgain_ref, layer))
    buf[0:POOL_HALO, :] = buf[tm:, :]

    chunks = us_ref.shape[0]
    for t in range(SSM_CHUNK):
        for v in range(SSM_LANE_BLOCKS):
            us_ref[:, _ssm_piece(t, v)] = zs[v, pl.ds(t, chunks, stride=SSM_CHUNK), :]


def _inproj(x, w_in, w_pool_blockdiag_bf16, pool_scale, pool_gain, layer, batch):
    t = x.shape[0]
    tm = INPROJ_TOKENS
    row = lambda width: pl.BlockSpec((tm, width), lambda i: (i, 0))
    full = lambda a: _layer_block(a, layer)
    chunk_rows = pl.BlockSpec((tm // SSM_CHUNK, SSM_CHUNK * SSM_WIDTH), lambda i: (i, 0))
    return pl.pallas_call(
        functools.partial(_inproj_body, tiles_per_seq=t // batch // tm, layer=layer),
        grid=(t // tm,),
        in_specs=[row(D_MODEL), _layer_block(w_in, layer, pipeline_mode=pl.Buffered(1)),
                  full(w_pool_blockdiag_bf16), _whole(pool_scale), _whole(pool_gain)],
        out_specs=[row(POOL_WIDTH), row(ATTN_WIDTH), row(ATTN_WIDTH), row(ATTN_WIDTH), chunk_rows],
        out_shape=[jax.ShapeDtypeStruct((t, POOL_WIDTH), BF16),
                   jax.ShapeDtypeStruct((t, ATTN_WIDTH), BF16),
                   jax.ShapeDtypeStruct((t, ATTN_WIDTH), BF16),
                   jax.ShapeDtypeStruct((t, ATTN_WIDTH), BF16),
                   jax.ShapeDtypeStruct((t // SSM_CHUNK, SSM_CHUNK * SSM_WIDTH), F32)],
        scratch_shapes=[pltpu.VMEM(w_in.shape[1:], BF16), pltpu.VMEM((SSM_LANE_BLOCKS, tm, LANES), F32)]
        + [pltpu.VMEM((POOL_HALO + tm, POOL_WIDTH), F32)] * 3,
        compiler_params=_cparams("arbitrary"),
        name="inproj",
    )(x, w_in, w_pool_blockdiag_bf16, pool_scale, pool_gain)


def _attn_bias_rows(rel_bias):
    x = np.arange(ATTN_BIAS_ROW)
    x = np.where(x < ATTN_BAND_TOKENS, x, x - ATTN_BIAS_ROW)
    idx = np.clip(N_PREV_CHUNKS * CHUNK - x, -REL_CLIP, REL_CLIP) + REL_CLIP
    return rel_bias.astype(F32)[..., idx] * LOG2_E


def _attn_body(q_ref, kprev_ref, kcur_ref, vprev_ref, vcur_ref, rows_ref, g_ref, o_ref, bias_s, *, layer):
    b = pl.program_id(0)
    i = pl.program_id(1)
    tq = ATTN_Q_TOKENS

    @pl.when((b == 0) & (i == 0))
    def _():
        qc = lax.broadcasted_iota(jnp.int32, (tq, ATTN_BAND_TOKENS), 0) // CHUNK
        kc = lax.broadcasted_iota(jnp.int32, (tq, ATTN_BAND_TOKENS), 1) // CHUNK
        in_band = (kc >= qc) & (kc <= qc + N_PREV_CHUNKS)
        for head in range(ATTN_HEADS):
            full = jnp.broadcast_to(rows_ref[head:head + 1, :], (tq, ATTN_BIAS_ROW))
            shifted = pltpu.roll(full, 0, 1, stride=1, stride_axis=0)
            bias_s[head // 2, (head % 2) * tq:(head % 2 + 1) * tq, :] = jnp.where(
                in_band, shifted[:, :ATTN_BAND_TOKENS], -jnp.inf)

    upper_half = lax.broadcasted_iota(jnp.int32, (1, LANES), 1) >= ATTN_HEAD_DIM

    def band(prev_ref, cur_ref, blk, sl):
        start = blk * tq
        if start < ATTN_PREV_TOKENS:
            return jnp.concatenate([prev_ref[start:, sl], cur_ref[0:start + tq, sl]], axis=0)
        return cur_ref[start - ATTN_PREV_TOKENS:start + tq, sl]

    def heads(blk, masked_keys):
        rows = pl.ds(blk * tq, tq)
        outs = []
        for pair in range(ATTN_WIDTH // LANES):
            sl = slice(pair * LANES, (pair + 1) * LANES)
            qp = q_ref[rows, sl]
            kp = band(kprev_ref, kcur_ref, blk, sl)
            vp = band(vprev_ref, vcur_ref, blk, sl)
            zero = jnp.zeros_like(qp)
            q2 = jnp.concatenate([jnp.where(upper_half, zero, qp), jnp.where(upper_half, qp, zero)], axis=0)
            s = lax.dot_general(q2, kp, (((1,), (1,)), ((), ())), preferred_element_type=F32)
            s = s + bias_s[pair]
            if masked_keys:
                key = lax.broadcasted_iota(jnp.int32, (1, ATTN_BAND_TOKENS), 1)
                s = jnp.where(key < masked_keys, -jnp.inf, s)
            m = jnp.max(s, axis=-1, keepdims=True)
            p = jnp.exp2(s - m)
            l = jnp.sum(p, axis=-1, keepdims=True)
            o = _dot(p.astype(BF16), vp) * (1.0 / l)
            outs.append(jnp.where(upper_half, o[tq:, :], o[:tq, :]))
        ss = sum(jnp.sum(o * o, axis=-1, keepdims=True) for o in outs)
        r = lax.rsqrt(ss / ATTN_WIDTH + NORM_EPS)
        for pair, o in enumerate(outs):
            sl = slice(pair * LANES, (pair + 1) * LANES)
            o_ref[rows, sl] = (o * r * g_ref[layer:layer + 1, sl]).astype(BF16)

    @pl.when(i == 0)
    def _():
        for blk in range(ATTN_BLOCKS_PER_STEP):
            heads(blk, max(ATTN_PREV_TOKENS - blk * tq, 0))

    @pl.when(i > 0)
    def _():
        for blk in range(ATTN_BLOCKS_PER_STEP):
            heads(blk, 0)


def _attention(q, k, v, bias_rows, gain, layer, batch):
    t = q.shape[0]
    ts = ATTN_BLOCKS_PER_STEP * ATTN_Q_TOKENS
    steps = t // batch // ts
    prev_per_step = ts // ATTN_PREV_TOKENS

    cur = pl.BlockSpec((ts, ATTN_WIDTH), lambda b, i: (b * steps + i, 0))
    prev = pl.BlockSpec((ATTN_PREV_TOKENS, ATTN_WIDTH),
                        lambda b, i: (jnp.maximum((b * steps + i) * prev_per_step - 1, 0), 0))

    return pl.pallas_call(
        functools.partial(_attn_body, layer=layer),
        grid=(batch, steps),
        in_specs=[cur, prev, cur, prev, cur,
                  _layer_block(bias_rows, layer), _whole(gain)],
        out_specs=cur,
        out_shape=jax.ShapeDtypeStruct((t, ATTN_WIDTH), BF16),
        scratch_shapes=[pltpu.VMEM((ATTN_HEADS // 2, 2 * ATTN_Q_TOKENS, ATTN_BAND_TOKENS), F32)],
        compiler_params=_cparams("arbitrary", "arbitrary"),
        name="attention",
    )(q, k, k, v, v, bias_rows, gain)


def _s5_position_of_time(g, time):
    return SSM_SLOTS * (time // SSM_SLOTS) + (time % SSM_SLOTS + g) % SSM_SLOTS


def _s5_tables(a_re, a_im, log_dt, b_re, b_im, c_re, c_im):
    hi = lax.Precision.HIGHEST
    tc = SSM_CHUNK
    g, p_dim = a_re.shape
    dt = jnp.exp(log_dt.astype(F32))[:, None]
    ar = a_re.astype(F32)
    ai = a_im.astype(F32)
    mag = jnp.exp(ar * dt)
    abar_re = mag * jnp.cos(ai * dt)
    abar_im = mag * jnp.sin(ai * dt)
    den = ar * ar + ai * ai
    nr = abar_re - 1.0
    ni = abar_im
    coef_re = ((nr * ar + ni * ai) / den)[..., None]
    coef_im = ((ni * ar - nr * ai) / den)[..., None]
    br = b_re.astype(F32)
    bi = b_im.astype(F32)
    bbar_re = coef_re * br - coef_im * bi
    bbar_im = coef_re * bi + coef_im * br
    n = jnp.arange(tc + 1, dtype=F32)
    pmag = jnp.exp((ar * dt)[..., None] * n)
    pw_re = pmag * jnp.cos((ai * dt)[..., None] * n)
    pw_im = pmag * jnp.sin((ai * dt)[..., None] * n)
    cw = tc * SSM_GROUP_DIM
    lag_rep = jnp.asarray(np.kron(np.eye(tc), np.ones((1, SSM_GROUP_DIM))), F32)
    ch_rep = jnp.asarray(np.kron(np.ones((1, tc)), np.eye(SSM_GROUP_DIM)), F32)
    expand = lambda a, rep: jnp.einsum('gpn,nx->gpx', a, rep, precision=hi)
    c_re_rep = expand(c_re.astype(F32).transpose(0, 2, 1), ch_rep)
    c_im_rep = expand(c_im.astype(F32).transpose(0, 2, 1), ch_rep)

    def output_coefficients(first_power, rep, subscripts):
        p_re = jnp.einsum(subscripts, pw_re[..., first_power:first_power + tc], rep, precision=hi)
        p_im = jnp.einsum(subscripts, pw_im[..., first_power:first_power + tc], rep, precision=hi)
        return c_re_rep * p_re - c_im_rep * p_im, -(c_re_rep * p_im + c_im_rep * p_re)

    on_re, on_im = output_coefficients(0, lag_rep, 'gpn,nx->gpx')
    kern = (jnp.einsum('gpk,gpx->gkx', bbar_re, on_re, precision=hi)
            + jnp.einsum('gpk,gpx->gkx', bbar_im, on_im, precision=hi))
    position = np.arange(tc)
    time_at = (SSM_SLOTS * (position // SSM_SLOTS)
               + (position % SSM_SLOTS - np.arange(g)[:, None]) % SSM_SLOTS)
    slot_rep = np.repeat(time_at[:, None, :] == np.arange(tc)[None, :, None], SSM_GROUP_DIM, axis=2)
    inter = jnp.concatenate(output_coefficients(1, jnp.asarray(slot_rep, F32), 'gpn,gnx->gpx'), axis=1)
    twice = lambda a: jnp.concatenate([a, a], -1)
    pt_re = twice(pw_re[..., :tc].transpose(0, 2, 1))[:, :, None, :]
    pt_im = twice(pw_im[..., :tc].transpose(0, 2, 1))[:, :, None, :]
    bt_re = bbar_re.transpose(0, 2, 1)[:, None]
    bt_im = bbar_im.transpose(0, 2, 1)[:, None]
    side = lambda a, b: jnp.concatenate([a, b], -1)
    est = (pt_re * side(bt_re, bt_im) + pt_im * side(-bt_im, bt_re)).reshape(g, cw, 2 * p_dim)
    est_swapped = (pt_re * side(bt_im, bt_re) + pt_im * side(bt_re, -bt_im)).reshape(g, cw, 2 * p_dim)
    return dict(
        kern=kern,
        est=est,
        est_swapped=est_swapped,
        inter=inter,
        apow_re=pw_re[..., tc].reshape(1, g * p_dim), apow_im=pw_im[..., tc].reshape(1, g * p_dim))


def _s5_prepare(kern_ref, est_ref, estsw_ref, int_ref, toep_s, est_s, int_s):
    lane = lax.broadcasted_iota(jnp.int32, (1, LANES), 1)
    slot = lane // SSM_GROUP_DIM
    zero = jnp.zeros((SSM_GROUP_DIM, LANES), F32)
    zero_rows = jnp.zeros((SSM_STATE, SSM_CHUNK_WIDTH), BF16)
    for g in range(SSM_GROUPS):
        mine = (lane >= SSM_STATE) if g % 2 else (lane < SSM_STATE)
        turn = g % SSM_SLOTS
        time_slot = (slot - turn) % SSM_SLOTS
        k0 = [kern_ref[g, :, 0:LANES]]
        k1 = [kern_ref[g, :, LANES:2 * LANES]]
        for r in range(1, SSM_SLOTS):
            k0.append(pltpu.roll(k0[0], SSM_GROUP_DIM * r, 1))
            k1.append(pltpu.roll(k1[0], SSM_GROUP_DIM * r, 1))
        for time in range(SSM_CHUNK):
            r = (time + turn) % SSM_SLOTS
            later = time_slot >= time % SSM_SLOTS
            if time < SSM_SLOTS:
                h0 = jnp.where(later, k0[r], zero)
                h1 = jnp.where(later, k1[r], k0[r])
            else:
                h0 = zero
                h1 = jnp.where(later, k0[r], zero)
            rows = pl.ds(SSM_GROUP_DIM * _s5_position_of_time(g, time), SSM_GROUP_DIM)
            toep_s[g, rows, 0:LANES] = h0.astype(BF16)
            toep_s[g, rows, LANES:2 * LANES] = h1.astype(BF16)
            src = pl.ds(SSM_GROUP_DIM * (SSM_CHUNK - 1 - time), SSM_GROUP_DIM)
            e, e_swapped = est_ref[g, src, :], estsw_ref[g, src, :]
            e_re, e_im = (e_swapped, e) if g % 2 else (e, e_swapped)
            est_s[g, rows, 0:LANES] = jnp.where(mine, e_re, zero).astype(BF16)
            est_s[g, rows, LANES:2 * LANES] = jnp.where(mine, e_im, zero).astype(BF16)
        for part in range(2):
            src = pl.ds(part * SSM_STATE, SSM_STATE)
            base = part * SSM_PAIR_WIDTH
            int_s[g, pl.ds(base + (1 - g % 2) * SSM_STATE, SSM_STATE), :] = zero_rows
            int_s[g, pl.ds(base + (g % 2) * SSM_STATE, SSM_STATE), :] = int_ref[g, src, :].astype(BF16)


def _s5_body(u_ref, kern_ref, estin_ref, estswin_ref, intin_ref, apre_ref, apim_ref, d_ref, y_ref,
             toep_ref, est_ref, int_ref, ub_s, ere_s, eim_s, spre_s, spim_s, sre_s, sim_s, *, layer):
    rows = u_ref.shape[0]

    @pl.when((pl.program_id(0) == 0) & (pl.program_id(1) == 0))
    def _():
        _s5_prepare(kern_ref, estin_ref, estswin_ref, intin_ref, toep_ref, est_ref, int_ref)

    @pl.when(pl.program_id(1) == 0)
    def _():
        sre_s[...] = jnp.zeros_like(sre_s)
        sim_s[...] = jnp.zeros_like(sim_s)

    slot = lax.broadcasted_iota(jnp.int32, (1, LANES), 1) // SSM_GROUP_DIM
    slot_bits = [(b, (slot & b) != 0) for b in (1, 2, 4)]
    piece = _ssm_piece

    for v in range(SSM_LANE_BLOCKS):
        for m in range(SSM_TIME_BLOCKS):
            rot = []
            for j in range(SSM_SLOTS):
                a = u_ref[:, piece(SSM_SLOTS * m + j, v)]
                rot.append(a if j == 0 else pltpu.roll(a, SSM_GROUP_DIM * j, 1))
            for bit in slot_bits:
                rot = [jnp.where(bit[1], rot[(i + bit[0]) % SSM_SLOTS], rot[i]) for i in range(SSM_SLOTS)]
            for gam in range(SSM_SLOTS):
                ub_s[SSM_SLOTS * v + gam, :, m * LANES:(m + 1) * LANES] = rot[-gam % SSM_SLOTS].astype(BF16)

    for q in range(SSM_GROUPS // 2):
        e = _dot(ub_s[2 * q], est_ref[2 * q]) + _dot(ub_s[2 * q + 1], est_ref[2 * q + 1])
        ere_s[:, q * LANES:(q + 1) * LANES] = e[:, :SSM_PAIR_WIDTH]
        eim_s[:, q * LANES:(q + 1) * LANES] = e[:, SSM_PAIR_WIDTH:]

    a_re = apre_ref[...]
    a_im = apim_ref[...]

    def carry_step(r, carry):
        s_re, s_im = carry
        spre_s[pl.ds(r, 1), :] = s_re
        spim_s[pl.ds(r, 1), :] = s_im
        e_re = ere_s[pl.ds(r, 1), :]
        e_im = eim_s[pl.ds(r, 1), :]
        return (a_re * s_re - a_im * s_im + e_re, a_re * s_im + a_im * s_re + e_im)

    s_re, s_im = lax.fori_loop(0, rows, carry_step, (sre_s[...], sim_s[...]))
    sre_s[...] = s_re
    sim_s[...] = s_im

    for v in range(SSM_LANE_BLOCKS):
        yg = []
        for gam in range(SSM_SLOTS):
            g = SSM_SLOTS * v + gam
            q = g // 2
            sp = jnp.concatenate([spre_s[:, q * LANES:(q + 1) * LANES], spim_s[:, q * LANES:(q + 1) * LANES]],
                                 axis=1).astype(BF16)
            yg.append(_dot(ub_s[g], toep_ref[g]) + _dot(sp, int_ref[g]))
        d = d_ref[layer:layer + 1, v * LANES:(v + 1) * LANES]
        for m in range(SSM_TIME_BLOCKS):
            back = [yg[-i % SSM_SLOTS][:, m * LANES:(m + 1) * LANES] for i in range(SSM_SLOTS)]
            for bit in slot_bits:
                back = [jnp.where(bit[1], back[(i - bit[0]) % SSM_SLOTS], back[i]) for i in range(SSM_SLOTS)]
            for j in range(SSM_SLOTS):
                o = back[j]
                if j:
                    o = pltpu.roll(o, LANES - SSM_GROUP_DIM * j, 1)
                sl = piece(SSM_SLOTS * m + j, v)
                y_ref[:, sl] = jax.nn.gelu(o + d * u_ref[:, sl])


def _s5(u_rows, tab, d_skip, layer, batch):
    nch, width = u_rows.shape
    rows = SSM_ROWS
    steps = nch // batch // rows
    once = lambda a: _layer_block(a, layer, pipeline_mode=pl.Buffered(1))
    blk = pl.BlockSpec((rows, width), lambda b, i: (b * steps + i, 0))
    table = pltpu.VMEM((SSM_GROUPS, SSM_CHUNK_WIDTH, SSM_CHUNK_WIDTH), BF16)
    state = pltpu.VMEM((rows, SSM_STATE_LANES), F32)
    carry = pltpu.VMEM((1, SSM_STATE_LANES), F32)
    return pl.pallas_call(
        functools.partial(_s5_body, layer=layer),
        grid=(batch, steps),
        in_specs=[blk, once(tab['kern']), once(tab['est']), once(tab['est_swapped']), once(tab['inter']),
                  once(tab['apow_re']), once(tab['apow_im']), _whole(d_skip)],
        out_specs=blk,
        out_shape=jax.ShapeDtypeStruct((nch, width), F32),
        scratch_shapes=[table, table, table,
                        pltpu.VMEM((SSM_GROUPS, rows, SSM_CHUNK_WIDTH), BF16), state, state, state, state,
                        carry, carry],
        compiler_params=_cparams("arbitrary", "arbitrary"),
        name="s5",
    )(u_rows, tab['kern'], tab['est'], tab['est_swapped'], tab['inter'], tab['apow_re'], tab['apow_im'], d_skip)


def _route_rows(scores, biased):
    ng = N_EXPERTS // EXPERTS_PER_GROUP
    group_score = []
    for gi in range(ng):
        a, b, c, d = biased[gi * EXPERTS_PER_GROUP:(gi + 1) * EXPERTS_PER_GROUP]
        hi1, lo1 = jnp.maximum(a, b), jnp.minimum(a, b)
        hi2, lo2 = jnp.maximum(c, d), jnp.minimum(c, d)
        top1 = jnp.maximum(hi1, hi2)
        top2 = jnp.maximum(jnp.minimum(hi1, hi2), jnp.maximum(lo1, lo2))
        group_score.append(top1 + top2)
    best = group_score[0]
    best_idx = jnp.zeros_like(best, dtype=jnp.int32)
    for gi in range(1, ng):
        better = group_score[gi] > best
        best = jnp.where(better, group_score[gi], best)
        best_idx = jnp.where(better, gi, best_idx)
    picked = []
    for e in range(N_EXPERTS):
        gi = e // EXPERTS_PER_GROUP
        rank = jnp.zeros_like(best_idx)
        for o in range(gi * EXPERTS_PER_GROUP, (gi + 1) * EXPERTS_PER_GROUP):
            if o == e:
                continue
            ahead = (biased[o] > biased[e]) | ((biased[o] == biased[e]) & (o < e))
            rank = rank + ahead.astype(jnp.int32)
        picked.append((best_idx == gi) & (rank < 2))
    wsum = sum(jnp.where(picked[e], scores[e], 0.0) for e in range(N_EXPERTS))
    return [jnp.where(picked[e], scores[e] / wsum, 0.0) for e in range(N_EXPERTS)], best_idx


def _group_sort_positions(best_idx, before_ref):
    ng = N_EXPERTS // EXPERTS_PER_GROUP
    tokens = best_idx.shape[1]
    member = [(best_idx == gi).astype(F32) for gi in range(ng)]
    stacked = jnp.concatenate(member + [jnp.zeros((SUBLANES - ng, tokens), F32)], axis=0)
    parts = []
    run = jnp.zeros((SUBLANES, 1), F32)
    for blk in range(tokens // LANES):
        piece = stacked[:, blk * LANES:(blk + 1) * LANES]
        parts.append(_dot(piece.astype(BF16), before_ref[...]) + run)
        run = run + jnp.sum(piece, axis=1, keepdims=True)
    earlier = jnp.concatenate(parts, axis=1)
    counts = [run[gi:gi + 1, :] for gi in range(ng)]
    pos = jnp.zeros_like(member[0])
    start = jnp.zeros_like(counts[0])
    for gi in range(ng):
        pos = pos + member[gi] * (start + earlier[gi:gi + 1, :])
        start = start + counts[gi]
    return pos, counts


def _outproj_body(x_ref, yp_ref, ya_ref, ys_ref, wgluf_ref, bglu_ref, gssm_ref, woutf_ref, g_ref, b_ref,
                  wr_ref, rb_ref, before_ref, h_ref, comb_ref, pos_ref, cnt_ref, ys_s, wglu_ref, wout_ref,
                  *, layer):
    @pl.when(pl.program_id(0) == 0)
    def _():
        wglu_ref[...] = wgluf_ref[...].astype(BF16)
        wout_ref[...] = woutf_ref[...].astype(BF16)

    chunks = ys_ref.shape[0]
    for t in range(SSM_CHUNK):
        for v in range(SSM_LANE_BLOCKS):
            ys_s[v, pl.ds(t, chunks, stride=SSM_CHUNK), :] = ys_ref[:, _ssm_piece(t, v)]
    def rows_block(sl):
        ys = jnp.concatenate([ys_s[v, sl, :] for v in range(SSM_LANE_BLOCKS)], axis=1)
        gate = jax.nn.sigmoid(_dot(ys.astype(BF16), wglu_ref[...]) + _layer_row(bglu_ref, layer))
        ys = ys * gate
        r = lax.rsqrt(jnp.mean(ys * ys, axis=-1, keepdims=True) + NORM_EPS)
        ysn = (ys * r * _layer_row(gssm_ref, layer)).astype(BF16)
        mix = _dot(jnp.concatenate([yp_ref[sl, :], ya_ref[sl, :], ysn], axis=1), wout_ref[...])
        h = _layer_norm(DN_ALPHA * x_ref[sl, :] + mix, _layer_row(g_ref, layer), _layer_row(b_ref, layer))
        h_ref[sl, :] = h
        return _dot(h.astype(BF16), wr_ref[...])

    tokens = x_ref.shape[0]
    parts = jnp.concatenate([rows_block(pl.ds(r0, OUT_ROWS)) for r0 in range(0, tokens, OUT_ROWS)], axis=0)
    parts_t = parts.T
    sc = jax.nn.sigmoid(parts_t[:N_EXPERTS, :] + parts_t[N_EXPERTS:2 * N_EXPERTS, :])
    bs = sc + rb_ref[...]
    scores = [sc[e:e + 1, :] for e in range(N_EXPERTS)]
    biased = [bs[e:e + 1, :] for e in range(N_EXPERTS)]
    comb_rows, best_idx = _route_rows(scores, biased)
    pos, counts = _group_sort_positions(best_idx, before_ref)
    comb_t = jnp.concatenate(comb_rows + [pos, jnp.zeros((LANES - N_EXPERTS - 1, tokens), F32)], axis=0)
    comb_ref[...] = comb_t.T
    pos_ref[...] = jnp.concatenate([pos, jnp.zeros((SUBLANES - 1, tokens), F32)], axis=0)
    cnt_ref[...] = jnp.concatenate(
        [jnp.broadcast_to(c, (1, LANES)) for c in counts]
        + [jnp.zeros((SUBLANES - len(counts), LANES), F32)], axis=0).astype(jnp.int32)


def _outproj(x, y_pool, y_attn, y_ssm_rows, w_glu, b_glu, g_ssm, w_out, ln_g, ln_b,
             w_router_split, router_bias, layer):
    t = x.shape[0]
    tm = MOE_TOKENS
    nt = t // tm
    row = lambda width: pl.BlockSpec((tm, width), lambda i: (i, 0))
    full = _whole
    once = lambda a: _layer_block(a, layer, pipeline_mode=pl.Buffered(1))
    token = np.arange(LANES)
    before = jnp.asarray(token[:, None] < token[None, :], BF16)
    return pl.pallas_call(
        functools.partial(_outproj_body, layer=layer),
        grid=(nt,),
        in_specs=[row(D_MODEL), row(POOL_WIDTH), row(ATTN_WIDTH),
                  pl.BlockSpec((tm // SSM_CHUNK, SSM_CHUNK * SSM_WIDTH), lambda i: (i, 0)),
                  once(w_glu), _whole(b_glu), _whole(g_ssm), once(w_out), _whole(ln_g), _whole(ln_b), full(w_router_split), full(router_bias), full(before)],
        out_specs=[row(D_MODEL), row(LANES), pl.BlockSpec((SUBLANES, tm), lambda i: (0, i)),
                   pl.BlockSpec((SUBLANES, LANES), lambda i: (i, 0))],
        out_shape=[jax.ShapeDtypeStruct((t, D_MODEL), F32), jax.ShapeDtypeStruct((t, LANES), F32),
                   jax.ShapeDtypeStruct((SUBLANES, t), F32),
                   jax.ShapeDtypeStruct((SUBLANES * nt, LANES), jnp.int32)],
        scratch_shapes=[pltpu.VMEM((SSM_LANE_BLOCKS, tm, LANES), F32),
                        pltpu.VMEM(w_glu.shape[1:], BF16), pltpu.VMEM(w_out.shape[1:], BF16)],
        compiler_params=_cparams("arbitrary"),
        name="outproj",
    )(x, y_pool, y_attn, y_ssm_rows, w_glu, b_glu, g_ssm, w_out, ln_g, ln_b,
      w_router_split, router_bias, before)


def _moe_body(cnt_ref, h_ref, comb_ref, pos_ref, p_ref, wg_ref, wu_ref, wd_ref, wpgf_ref, wppf_ref, g_ref, b_ref,
              o_ref, hs_s, cs_s, acc_s, ple_s, wpg_ref, wpp_ref, *, layer):
    i = pl.program_id(0)
    group = pl.program_id(1)
    ng = pl.num_programs(1)
    tm = h_ref.shape[0]

    @pl.when((i == 0) & (group == 0))
    def _():
        wpg_ref[...] = wpgf_ref[...].astype(BF16)
        wpp_ref[...] = wppf_ref[...].astype(BF16)

    @pl.when(group == 0)
    def _():
        hb = h_ref[...].astype(BF16)
        comb = comb_ref[...]
        comb_lo = comb - comb.astype(BF16).astype(F32)
        low_lanes = lax.broadcasted_iota(jnp.int32, (1, LANES), 1) < COMB_LO_LANE
        comb_b = jnp.where(low_lanes, comb, pltpu.roll(comb_lo, COMB_LO_LANE, 1)).astype(BF16)
        for r0 in range(0, tm, MOE_SIDE_ROWS):
            sl = pl.ds(r0, MOE_SIDE_ROWS)
            row = r0 + lax.broadcasted_iota(jnp.int32, (MOE_SIDE_ROWS, tm), 0)
            perm = jnp.where(pos_ref[0:1, :] == row.astype(F32), 1.0, 0.0).astype(BF16)
            hs_s[sl, :] = _dot(perm, hb).astype(BF16)
            both = _dot(perm, comb_b)
            cs_s[sl, :] = both + pltpu.roll(both, LANES - COMB_LO_LANE, 1)
        acc_s[...] = jnp.zeros_like(acc_s)

    count = cnt_ref[i * ng + group]
    start = jnp.int32(0)
    for gi in range(N_EXPERTS // EXPERTS_PER_GROUP - 1):
        start = start + jnp.where(group > gi, cnt_ref[i * ng + gi], 0)
    lane = lax.broadcasted_iota(jnp.int32, (1, LANES), 1)
    first = (start // BF16_ROW_PACK) * BF16_ROW_PACK
    need = start - first + count

    def window(w, size):
        wanted = first + w * size
        lo = pl.multiple_of(jnp.minimum(wanted, tm - size), BF16_ROW_PACK)
        rows = pl.ds(lo, size)
        x = hs_s[rows, :]
        fresh = lo + lax.broadcasted_iota(jnp.int32, (size, 1), 0) >= wanted
        cs = jnp.where(fresh, cs_s[rows, :], 0.0)
        total = None
        for e in range(EXPERTS_PER_GROUP):
            gate = _dot(x, wg_ref[e])
            up = _dot(x, wu_ref[e])
            c = jnp.sum(jnp.where(lane == group * EXPERTS_PER_GROUP + e, cs, 0.0), axis=1, keepdims=True)
            a = (jax.nn.silu(gate) * up * c).astype(BF16)
            d = _dot(a, wd_ref[e])
            total = d if total is None else total + d
        acc_s[rows, :] += total

    below = 0
    for size in MOE_WINDOW_SIZES[:-1]:
        @pl.when((need > below) & (need <= size))
        def _():
            window(0, size)
        below = size
    largest = MOE_WINDOW_SIZES[-1]

    @pl.when(need > below)
    def _():
        def step(w, carry):
            window(w, largest)
            return carry
        lax.fori_loop(0, (need + largest - 1) // largest, step, 0)

    half = tm // 2
    for mid in (1, 2):
        @pl.when(group == mid)
        def _():
            for r0 in range((mid - 1) * half, mid * half, MOE_SIDE_ROWS):
                sl = pl.ds(r0, MOE_SIDE_ROWS)
                hb = h_ref[sl, :].astype(BF16)
                ple_s[sl, :] = (jax.nn.sigmoid(_dot(hb, wpg_ref[...]))
                                * _dot(p_ref[sl, :].astype(BF16), wpp_ref[...]))

    @pl.when(group == ng - 1)
    def _():
        sorted_out = acc_s[...].astype(BF16)
        col = lax.broadcasted_iota(jnp.int32, (MOE_SIDE_ROWS, tm), 1).astype(F32)
        for r0 in range(0, tm, MOE_SIDE_ROWS):
            sl = pl.ds(r0, MOE_SIDE_ROWS)
            unperm = jnp.where(comb_ref[sl, SORT_POS_LANE:SORT_POS_LANE + 1] == col, 1.0, 0.0).astype(BF16)
            ffn = _dot(unperm, sorted_out)
            o_ref[sl, :] = _layer_norm(DN_ALPHA * h_ref[sl, :] + ffn + ple_s[sl, :], _layer_row(g_ref, layer),
                                      _layer_row(b_ref, layer))


def _moe(h, comb, pos_rows, counts, p_all, layer, wg_bf16, wu_bf16, wd_bf16, w_ple_gate, w_ple_proj, ln_g, ln_b):
    t = h.shape[0]
    tm = MOE_TOKENS
    nt = t // tm
    ng = N_EXPERTS // EXPERTS_PER_GROUP
    once = lambda a: _layer_block(a, layer, pipeline_mode=pl.Buffered(1))
    experts = lambda rows, cols: pl.BlockSpec((EXPERTS_PER_GROUP, rows, cols),
                                              lambda i, g, cnt: (layer * ng + g, 0, 0))
    grid_spec = pltpu.PrefetchScalarGridSpec(
        num_scalar_prefetch=1,
        grid=(nt, ng),
        in_specs=[pl.BlockSpec((tm, D_MODEL), lambda i, g, cnt: (i, 0)),
                  pl.BlockSpec((tm, LANES), lambda i, g, cnt: (i, 0)),
                  pl.BlockSpec((SUBLANES, tm), lambda i, g, cnt: (0, i)),
                  pl.BlockSpec((tm, PLE_DIM), lambda i, g, cnt: (layer * nt + i, 0)),
                  experts(D_MODEL, D_EXPERT), experts(D_MODEL, D_EXPERT), experts(D_EXPERT, D_MODEL),
                  once(w_ple_gate), once(w_ple_proj), _whole(ln_g), _whole(ln_b)],
        out_specs=pl.BlockSpec((tm, D_MODEL), lambda i, g, cnt: (i, 0)),
        scratch_shapes=[pltpu.VMEM((tm, D_MODEL), BF16), pltpu.VMEM((tm, LANES), F32),
                        pltpu.VMEM((tm, D_MODEL), F32), pltpu.VMEM((tm, D_MODEL), F32),
                        pltpu.VMEM(w_ple_gate.shape[1:], BF16), pltpu.VMEM(w_ple_proj.shape[1:], BF16)])
    return pl.pallas_call(
        functools.partial(_moe_body, layer=layer),
        grid_spec=grid_spec,
        out_shape=jax.ShapeDtypeStruct((t, D_MODEL), F32),
        compiler_params=_cparams("arbitrary", "arbitrary"),
        name="moe",
    )(counts, h, comb, pos_rows, p_all, wg_bf16, wu_bf16, wd_bf16, w_ple_gate, w_ple_proj, ln_g, ln_b)


def _block_diag(w):
    g, n, m = w.shape
    eye = jnp.eye(g, dtype=w.dtype)
    return (eye[:, None, :, None] * w[:, :, None, :]).reshape(g * n, g * m)


def kernel(x, p, w_in, w_out, w_pool, pool_scale, rel_bias, ssm_a_re, ssm_a_im, ssm_log_dt, ssm_b_re, ssm_b_im,
           ssm_c_re, ssm_c_im, ssm_d, w_glu, b_glu, g_pool, g_attn, g_ssm, ln1_g, ln1_b, ln2_g, ln2_b,
           w_router, router_bias, w_exp_gate, w_exp_up, w_exp_down, w_ple_gate, w_ple_proj):
    batch, seq, d = x.shape
    t = batch * seq
    xt = x.reshape(t, d)
    p_all = p.reshape(DEPTH * t, PLE_DIM)

    f32 = vec = lambda a: a.astype(F32)
    w_pool_b = jax.vmap(_block_diag)(w_pool).astype(BF16)
    bias_rows = _attn_bias_rows(rel_bias)
    tables = jax.vmap(_s5_tables)(ssm_a_re, ssm_a_im, ssm_log_dt, ssm_b_re, ssm_b_im, ssm_c_re, ssm_c_im)
    wr = w_router.astype(F32)
    wr_hi = wr.astype(BF16)
    wr_lo = (wr - wr_hi.astype(F32)).astype(BF16)
    wr_split = jnp.pad(jnp.concatenate([wr_hi, wr_lo], axis=1), ((0, 0), (0, LANES - 2 * N_EXPERTS)))
    r_bias = router_bias.astype(F32).reshape(N_EXPERTS, 1)
    stack_experts = lambda w: w.astype(BF16).reshape((DEPTH * N_EXPERTS,) + w.shape[2:])
    wg_b, wu_b, wd_b = stack_experts(w_exp_gate), stack_experts(w_exp_up), stack_experts(w_exp_down)
    ng = N_EXPERTS // EXPERTS_PER_GROUP

    for layer in range(DEPTH):
        y_pool, q, k, v, u_ssm = _inproj(xt, f32(w_in), w_pool_b, vec(pool_scale), vec(g_pool), layer, batch)
        y_attn = _attention(q, k, v, bias_rows, vec(g_attn), layer, batch)
        y_ssm = _s5(u_ssm, tables, vec(ssm_d), layer, batch)
        h, comb, pos_rows, cnt = _outproj(xt, y_pool, y_attn, y_ssm, f32(w_glu), vec(b_glu), vec(g_ssm),
                                          f32(w_out), vec(ln1_g), vec(ln1_b), wr_split, r_bias, layer)
        counts = cnt[:, 0].reshape(-1, SUBLANES)[:, :ng].reshape(-1)
        xt = _moe(h, comb, pos_rows, counts, p_all, layer, wg_b, wu_b, wd_b, f32(w_ple_gate), f32(w_ple_proj),
                  vec(ln2_g), vec(ln2_b))
    return xt.reshape(batch, seq, d)
```

```python
import functools
import math

import numpy as np
import jax
import jax.numpy as jnp
from jax import lax
from jax.experimental import pallas as pl
from jax.experimental.pallas import tpu as pltpu

F32 = jnp.float32
BF16 = jnp.bfloat16

D_MODEL = 1024
DEPTH = 2
CHUNK = 64
PLE_DIM = 256
POOL_WIDTH = 256
POOL_GROUP_DIM = 64
POOL_WINDOWS = (2, 4, 8, 16)
POOL_HALO = 32
ATTN_HEAD_DIM = 64
ATTN_HEADS = 6
ATTN_WIDTH = 384
N_PREV_CHUNKS = 8
REL_CLIP = 128
SSM_WIDTH = 384
SSM_GROUP_DIM = 16
SSM_GROUPS = 24
SSM_STATE = 64
N_EXPERTS = 16
EXPERTS_PER_GROUP = 4
D_EXPERT = 256
DN_ALPHA = (2 * DEPTH) ** 0.25
NORM_EPS = 1e-5
LOG2_E = math.log2(math.e)

LANES = 128
SUBLANES = 8
VMEM_LIMIT_BYTES = 56 * 1024 * 1024

INPROJ_TOKENS = 1024
INPROJ_ROWS = 512
ATTN_Q_CHUNKS = 4
ATTN_Q_TOKENS = ATTN_Q_CHUNKS * CHUNK
ATTN_BAND_TOKENS = 3 * ATTN_Q_TOKENS
ATTN_BLOCKS_PER_STEP = 4
ATTN_PREV_TOKENS = N_PREV_CHUNKS * CHUNK
ATTN_BIAS_ROW = 1024
SSM_CHUNK = 16
SSM_CHUNK_WIDTH = SSM_CHUNK * SSM_GROUP_DIM
SSM_ROWS = 128
SSM_SLOTS = LANES // SSM_GROUP_DIM
SSM_LANE_BLOCKS = SSM_WIDTH // LANES
SSM_TIME_BLOCKS = SSM_CHUNK // SSM_SLOTS
SSM_PAIR_WIDTH = 2 * SSM_STATE
SSM_STATE_LANES = SSM_GROUPS * SSM_STATE
OUT_ROWS = 256
MOE_SIDE_ROWS = 256
MOE_TOKENS = 1024
MOE_WINDOW_SIZES = (256, 320, 384)
BF16_ROW_PACK = 16
SORT_POS_LANE = N_EXPERTS
COMB_LO_LANE = 32


def _cparams(*sem):
    return pltpu.CompilerParams(dimension_semantics=sem, vmem_limit_bytes=VMEM_LIMIT_BYTES)


def _dot(a, b):
    return jnp.dot(a, b, preferred_element_type=F32)


def _layer_block(a, layer, **kwargs):
    return pl.BlockSpec((None,) + a.shape[1:], lambda *_: (layer,) + (0,) * (a.ndim - 1), **kwargs)


def _whole(a):
    return pl.BlockSpec(a.shape, lambda *_: (0,) * a.ndim)


def _layer_norm(v, g, b):
    mu = jnp.mean(v, axis=-1, keepdims=True)
    vc = v - mu
    var = jnp.mean(vc * vc, axis=-1, keepdims=True)
    return vc * lax.rsqrt(var + NORM_EPS) * g + b


def _ssm_piece(t, v):
    lo = SSM_WIDTH * t + LANES * v
    return slice(lo, lo + LANES)


def _layer_row(ref, layer):
    return ref[layer:layer + 1, :]


def _pool_mix(x0, buf, lvl_a, lvl_b, pos, w_ref, scale, gain):
    n = x0.shape[0] + POOL_HALO
    group = lax.broadcasted_iota(jnp.int32, (1, POOL_WIDTH), 1) // POOL_GROUP_DIM
    mean = jnp.zeros_like(x0)
    src, dst = buf, lvl_a
    for gi, w in enumerate(POOL_WINDOWS):
        lo = 8 * (gi + 1)
        dst[lo:n, :] = src[lo:n, :] + src[lo - w // 2:n - w // 2, :]
        inv_cnt = 1.0 / jnp.minimum(pos + 1, w).astype(F32)
        mean = jnp.where(group == gi, dst[POOL_HALO:n, :] * inv_cnt, mean)
        src, dst = dst, (lvl_b if dst is lvl_a else lvl_a)
    d = (mean - x0).astype(BF16)
    y = _dot(d, w_ref[...]) * scale
    r = lax.rsqrt(jnp.mean(y * y, axis=-1, keepdims=True) + NORM_EPS)
    return (y * r * gain).astype(BF16)


def _inproj_body(x_ref, wf_ref, wpool_ref, pscale_ref, pgain_ref, yp_ref, q_ref, k_ref, v_ref, us_ref,
                 w_ref, zs, buf, lvl_a, lvl_b, *, tiles_per_seq, layer):
    tm = x_ref.shape[0]
    tile_in_seq = pl.program_id(0) % tiles_per_seq

    @pl.when(pl.program_id(0) == 0)
    def _():
        w_ref[...] = wf_ref[...].astype(BF16)

    @pl.when(tile_in_seq == 0)
    def _():
        buf[0:POOL_HALO, :] = jnp.zeros((POOL_HALO, POOL_WIDTH), F32)

    for r0 in range(0, tm, INPROJ_ROWS):
        sl = pl.ds(r0, INPROJ_ROWS)
        xb = x_ref[sl, :].astype(BF16)

        def cols(lo, hi):
            return _dot(xb, w_ref[:, lo:hi])

        buf[pl.ds(POOL_HALO + r0, INPROJ_ROWS), :] = cols(0, 256)
        qk = cols(256, 768)
        q_ref[sl, :] = (qk[:, :ATTN_WIDTH] * (ATTN_HEAD_DIM ** -0.5 * LOG2_E)).astype(BF16)
        k_ref[sl, :LANES] = qk[:, ATTN_WIDTH:].astype(BF16)
        k_ref[sl, LANES:] = cols(768, 1024).astype(BF16)
        vs = cols(1024, 1536)
        v_ref[sl, :] = vs[:, :ATTN_WIDTH].astype(BF16)
        zs[0, sl, :] = vs[:, ATTN_WIDTH:]
        s_rest = cols(1536, 1792)
        zs[1, sl, :] = s_rest[:, :LANES]
        zs[2, sl, :] = s_rest[:, LANES:]

    u_pool = buf[POOL_HALO:, :]
    pos = tile_in_seq * tm + lax.broadcasted_iota(jnp.int32, (tm, 1), 0)
    yp_ref[...] = _pool_mix(u_pool, buf, lvl_a, lvl_b, pos, wpool_ref, _layer_row(pscale_ref, layer),
                            _layer_row(pgain_ref, layer))
    buf[0:POOL_HALO, :] = buf[tm:, :]

    chunks = us_ref.shape[0]
    for t in range(SSM_CHUNK):
        for v in range(SSM_LANE_BLOCKS):
            us_ref[:, _ssm_piece(t, v)] = zs[v, pl.ds(t, chunks, stride=SSM_CHUNK), :]


def _inproj(x, w_in, w_pool_blockdiag_bf16, pool_scale, pool_gain, layer, batch):
    t = x.shape[0]
    tm = INPROJ_TOKENS
    row = lambda width: pl.BlockSpec((tm, width), lambda i: (i, 0))
    full = lambda a: _layer_block(a, layer)
    chunk_rows = pl.BlockSpec((tm // SSM_CHUNK, SSM_CHUNK * SSM_WIDTH), lambda i: (i, 0))
    return pl.pallas_call(
        functools.partial(_inproj_body, tiles_per_seq=t // batch // tm, layer=layer),
        grid=(t // tm,),
        in_specs=[row(D_MODEL), _layer_block(w_in, layer, pipeline_mode=pl.Buffered(1)),
                  full(w_pool_blockdiag_bf16), _whole(pool_scale), _whole(pool_gain)],
        out_specs=[row(POOL_WIDTH), row(ATTN_WIDTH), row(ATTN_WIDTH), row(ATTN_WIDTH), chunk_rows],
        out_shape=[jax.ShapeDtypeStruct((t, POOL_WIDTH), BF16),
                   jax.ShapeDtypeStruct((t, ATTN_WIDTH), BF16),
                   jax.ShapeDtypeStruct((t, ATTN_WIDTH), BF16),
                   jax.ShapeDtypeStruct((t, ATTN_WIDTH), BF16),
                   jax.ShapeDtypeStruct((t // SSM_CHUNK, SSM_CHUNK * SSM_WIDTH), F32)],
        scratch_shapes=[pltpu.VMEM(w_in.shape[1:], BF16), pltpu.VMEM((SSM_LANE_BLOCKS, tm, LANES), F32)]
        + [pltpu.VMEM((POOL_HALO + tm, POOL_WIDTH), F32)] * 3,
        compiler_params=_cparams("arbitrary"),
        name="inproj",
    )(x, w_in, w_pool_blockdiag_bf16, pool_scale, pool_gain)


def _attn_bias_rows(rel_bias):
    x = np.arange(ATTN_BIAS_ROW)
    x = np.where(x < ATTN_BAND_TOKENS, x, x - ATTN_BIAS_ROW)
    idx = np.clip(N_PREV_CHUNKS * CHUNK - x, -REL_CLIP, REL_CLIP) + REL_CLIP
    return rel_bias.astype(F32)[..., idx] * LOG2_E


def _attn_body(q_ref, kprev_ref, kcur_ref, vprev_ref, vcur_ref, rows_ref, g_ref, o_ref, bias_s, *, layer):
    b = pl.program_id(0)
    i = pl.program_id(1)
    tq = ATTN_Q_TOKENS

    @pl.when((b == 0) & (i == 0))
    def _():
        qc = lax.broadcasted_iota(jnp.int32, (tq, ATTN_BAND_TOKENS), 0) // CHUNK
        kc = lax.broadcasted_iota(jnp.int32, (tq, ATTN_BAND_TOKENS), 1) // CHUNK
        in_band = (kc >= qc) & (kc <= qc + N_PREV_CHUNKS)
        for head in range(ATTN_HEADS):
            full = jnp.broadcast_to(rows_ref[head:head + 1, :], (tq, ATTN_BIAS_ROW))
            shifted = pltpu.roll(full, 0, 1, stride=1, stride_axis=0)
            bias_s[head // 2, (head % 2) * tq:(head % 2 + 1) * tq, :] = jnp.where(
                in_band, shifted[:, :ATTN_BAND_TOKENS], -jnp.inf)

    upper_half = lax.broadcasted_iota(jnp.int32, (1, LANES), 1) >= ATTN_HEAD_DIM

    def band(prev_ref, cur_ref, blk, sl):
        start = blk * tq
        if start < ATTN_PREV_TOKENS:
            return jnp.concatenate([prev_ref[start:, sl], cur_ref[0:start + tq, sl]], axis=0)
        return cur_ref[start - ATTN_PREV_TOKENS:start + tq, sl]

    def heads(blk, masked_keys):
        rows = pl.ds(blk * tq, tq)
        outs = []
        for pair in range(ATTN_WIDTH // LANES):
            sl = slice(pair * LANES, (pair + 1) * LANES)
            qp = q_ref[rows, sl]
            kp = band(kprev_ref, kcur_ref, blk, sl)
            vp = band(vprev_ref, vcur_ref, blk, sl)
            zero = jnp.zeros_like(qp)
            q2 = jnp.concatenate([jnp.where(upper_half, zero, qp), jnp.where(upper_half, qp, zero)], axis=0)
            s = lax.dot_general(q2, kp, (((1,), (1,)), ((), ())), preferred_element_type=F32)
            s = s + bias_s[pair]
            if masked_keys:
                key = lax.broadcasted_iota(jnp.int32, (1, ATTN_BAND_TOKENS), 1)
                s = jnp.where(key < masked_keys, -jnp.inf, s)
            m = jnp.max(s, axis=-1, keepdims=True)
            p = jnp.exp2(s - m)
            l = jnp.sum(p, axis=-1, keepdims=True)
            o = _dot(p.astype(BF16), vp) * (1.0 / l)
            outs.append(jnp.where(upper_half, o[tq:, :], o[:tq, :]))
        ss = sum(jnp.sum(o * o, axis=-1, keepdims=True) for o in outs)
        r = lax.rsqrt(ss / ATTN_WIDTH + NORM_EPS)
        for pair, o in enumerate(outs):
            sl = slice(pair * LANES, (pair + 1) * LANES)
            o_ref[rows, sl] = (o * r * g_ref[layer:layer + 1, sl]).astype(BF16)

    @pl.when(i == 0)
    def _():
        for blk in range(ATTN_BLOCKS_PER_STEP):
            heads(blk, max(ATTN_PREV_TOKENS - blk * tq, 0))

    @pl.when(i > 0)
    def _():
        for blk in range(ATTN_BLOCKS_PER_STEP):
            heads(blk, 0)


def _attention(q, k, v, bias_rows, gain, layer, batch):
    t = q.shape[0]
    ts = ATTN_BLOCKS_PER_STEP * ATTN_Q_TOKENS
    steps = t // batch // ts
    prev_per_step = ts // ATTN_PREV_TOKENS

    cur = pl.BlockSpec((ts, ATTN_WIDTH), lambda b, i: (b * steps + i, 0))
    prev = pl.BlockSpec((ATTN_PREV_TOKENS, ATTN_WIDTH),
                        lambda b, i: (jnp.maximum((b * steps + i) * prev_per_step - 1, 0), 0))

    return pl.pallas_call(
        functools.partial(_attn_body, layer=layer),
        grid=(batch, steps),
        in_specs=[cur, prev, cur, prev, cur,
                  _layer_block(bias_rows, layer), _whole(gain)],
        out_specs=cur,
        out_shape=jax.ShapeDtypeStruct((t, ATTN_WIDTH), BF16),
        scratch_shapes=[pltpu.VMEM((ATTN_HEADS // 2, 2 * ATTN_Q_TOKENS, ATTN_BAND_TOKENS), F32)],
        compiler_params=_cparams("arbitrary", "arbitrary"),
        name="attention",
    )(q, k, k, v, v, bias_rows, gain)


def _s5_position_of_time(g, time):
    return SSM_SLOTS * (time // SSM_SLOTS) + (time % SSM_SLOTS + g) % SSM_SLOTS


def _s5_tables(a_re, a_im, log_dt, b_re, b_im, c_re, c_im):
    hi = lax.Precision.HIGHEST
    tc = SSM_CHUNK
    g, p_dim = a_re.shape
    dt = jnp.exp(log_dt.astype(F32))[:, None]
    ar = a_re.astype(F32)
    ai = a_im.astype(F32)
    mag = jnp.exp(ar * dt)
    abar_re = mag * jnp.cos(ai * dt)
    abar_im = mag * jnp.sin(ai * dt)
    den = ar * ar + ai * ai
    nr = abar_re - 1.0
    ni = abar_im
    coef_re = ((nr * ar + ni * ai) / den)[..., None]
    coef_im = ((ni * ar - nr * ai) / den)[..., None]
    br = b_re.astype(F32)
    bi = b_im.astype(F32)
    bbar_re = coef_re * br - coef_im * bi
    bbar_im = coef_re * bi + coef_im * br
    n = jnp.arange(tc + 1, dtype=F32)
    pmag = jnp.exp((ar * dt)[..., None] * n)
    pw_re = pmag * jnp.cos((ai * dt)[..., None] * n)
    pw_im = pmag * jnp.sin((ai * dt)[..., None] * n)
    cw = tc * SSM_GROUP_DIM
    lag_rep = jnp.asarray(np.kron(np.eye(tc), np.ones((1, SSM_GROUP_DIM))), F32)
    ch_rep = jnp.asarray(np.kron(np.ones((1, tc)), np.eye(SSM_GROUP_DIM)), F32)
    expand = lambda a, rep: jnp.einsum('gpn,nx->gpx', a, rep, precision=hi)
    c_re_rep = expand(c_re.astype(F32).transpose(0, 2, 1), ch_rep)
    c_im_rep = expand(c_im.astype(F32).transpose(0, 2, 1), ch_rep)

    def output_coefficients(first_power, rep, subscripts):
        p_re = jnp.einsum(subscripts, pw_re[..., first_power:first_power + tc], rep, precision=hi)
        p_im = jnp.einsum(subscripts, pw_im[..., first_power:first_power + tc], rep, precision=hi)
        return c_re_rep * p_re - c_im_rep * p_im, -(c_re_rep * p_im + c_im_rep * p_re)

    on_re, on_im = output_coefficients(0, lag_rep, 'gpn,nx->gpx')
    kern = (jnp.einsum('gpk,gpx->gkx', bbar_re, on_re, precision=hi)
            + jnp.einsum('gpk,gpx->gkx', bbar_im, on_im, precision=hi))
    position = np.arange(tc)
    time_at = (SSM_SLOTS * (position // SSM_SLOTS)
               + (position % SSM_SLOTS - np.arange(g)[:, None]) % SSM_SLOTS)
    slot_rep = np.repeat(time_at[:, None, :] == np.arange(tc)[None, :, None], SSM_GROUP_DIM, axis=2)
    inter = jnp.concatenate(output_coefficients(1, jnp.asarray(slot_rep, F32), 'gpn,gnx->gpx'), axis=1)
    twice = lambda a: jnp.concatenate([a, a], -1)
    pt_re = twice(pw_re[..., :tc].transpose(0, 2, 1))[:, :, None, :]
    pt_im = twice(pw_im[..., :tc].transpose(0, 2, 1))[:, :, None, :]
    bt_re = bbar_re.transpose(0, 2, 1)[:, None]
    bt_im = bbar_im.transpose(0, 2, 1)[:, None]
    side = lambda a, b: jnp.concatenate([a, b], -1)
    est = (pt_re * side(bt_re, bt_im) + pt_im * side(-bt_im, bt_re)).reshape(g, cw, 2 * p_dim)
    est_swapped = (pt_re * side(bt_im, bt_re) + pt_im * side(bt_re, -bt_im)).reshape(g, cw, 2 * p_dim)
    return dict(
        kern=kern,
        est=est,
        est_swapped=est_swapped,
        inter=inter,
        apow_re=pw_re[..., tc].reshape(1, g * p_dim), apow_im=pw_im[..., tc].reshape(1, g * p_dim))


def _s5_prepare(kern_ref, est_ref, estsw_ref, int_ref, toep_s, est_s, int_s):
    lane = lax.broadcasted_iota(jnp.int32, (1, LANES), 1)
    slot = lane // SSM_GROUP_DIM
    zero = jnp.zeros((SSM_GROUP_DIM, LANES), F32)
    zero_rows = jnp.zeros((SSM_STATE, SSM_CHUNK_WIDTH), BF16)
    for g in range(SSM_GROUPS):
        mine = (lane >= SSM_STATE) if g % 2 else (lane < SSM_STATE)
        turn = g % SSM_SLOTS
        time_slot = (slot - turn) % SSM_SLOTS
        k0 = [kern_ref[g, :, 0:LANES]]
        k1 = [kern_ref[g, :, LANES:2 * LANES]]
        for r in range(1, SSM_SLOTS):
            k0.append(pltpu.roll(k0[0], SSM_GROUP_DIM * r, 1))
            k1.append(pltpu.roll(k1[0], SSM_GROUP_DIM * r, 1))
        for time in range(SSM_CHUNK):
            r = (time + turn) % SSM_SLOTS
            later = time_slot >= time % SSM_SLOTS
            if time < SSM_SLOTS:
                h0 = jnp.where(later, k0[r], zero)
                h1 = jnp.where(later, k1[r], k0[r])
            else:
                h0 = zero
                h1 = jnp.where(later, k0[r], zero)
            rows = pl.ds(SSM_GROUP_DIM * _s5_position_of_time(g, time), SSM_GROUP_DIM)
            toep_s[g, rows, 0:LANES] = h0.astype(BF16)
            toep_s[g, rows, LANES:2 * LANES] = h1.astype(BF16)
            src = pl.ds(SSM_GROUP_DIM * (SSM_CHUNK - 1 - time), SSM_GROUP_DIM)
            e, e_swapped = est_ref[g, src, :], estsw_ref[g, src, :]
            e_re, e_im = (e_swapped, e) if g % 2 else (e, e_swapped)
            est_s[g, rows, 0:LANES] = jnp.where(mine, e_re, zero).astype(BF16)
            est_s[g, rows, LANES:2 * LANES] = jnp.where(mine, e_im, zero).astype(BF16)
        for part in range(2):
            src = pl.ds(part * SSM_STATE, SSM_STATE)
            base = part * SSM_PAIR_WIDTH
            int_s[g, pl.ds(base + (1 - g % 2) * SSM_STATE, SSM_STATE), :] = zero_rows
            int_s[g, pl.ds(base + (g % 2) * SSM_STATE, SSM_STATE), :] = int_ref[g, src, :].astype(BF16)


def _s5_body(u_ref, kern_ref, estin_ref, estswin_ref, intin_ref, apre_ref, apim_ref, d_ref, y_ref,
             toep_ref, est_ref, int_ref, ub_s, ere_s, eim_s, spre_s, spim_s, sre_s, sim_s, *, layer):
    rows = u_ref.shape[0]

    @pl.when((pl.program_id(0) == 0) & (pl.program_id(1) == 0))
    def _():
        _s5_prepare(kern_ref, estin_ref, estswin_ref, intin_ref, toep_ref, est_ref, int_ref)

    @pl.when(pl.program_id(1) == 0)
    def _():
        sre_s[...] = jnp.zeros_like(sre_s)
        sim_s[...] = jnp.zeros_like(sim_s)

    slot = lax.broadcasted_iota(jnp.int32, (1, LANES), 1) // SSM_GROUP_DIM
    slot_bits = [(b, (slot & b) != 0) for b in (1, 2, 4)]
    piece = _ssm_piece

    for v in range(SSM_LANE_BLOCKS):
        for m in range(SSM_TIME_BLOCKS):
            rot = []
            for j in range(SSM_SLOTS):
                a = u_ref[:, piece(SSM_SLOTS * m + j, v)]
                rot.append(a if j == 0 else pltpu.roll(a, SSM_GROUP_DIM * j, 1))
            for bit in slot_bits:
                rot = [jnp.where(bit[1], rot[(i + bit[0]) % SSM_SLOTS], rot[i]) for i in range(SSM_SLOTS)]
            for gam in range(SSM_SLOTS):
                ub_s[SSM_SLOTS * v + gam, :, m * LANES:(m + 1) * LANES] = rot[-gam % SSM_SLOTS].astype(BF16)

    for q in range(SSM_GROUPS // 2):
        e = _dot(ub_s[2 * q], est_ref[2 * q]) + _dot(ub_s[2 * q + 1], est_ref[2 * q + 1])
        ere_s[:, q * LANES:(q + 1) * LANES] = e[:, :SSM_PAIR_WIDTH]
        eim_s[:, q * LANES:(q + 1) * LANES] = e[:, SSM_PAIR_WIDTH:]

    a_re = apre_ref[...]
    a_im = apim_ref[...]

    def carry_step(r, carry):
        s_re, s_im = carry
        spre_s[pl.ds(r, 1), :] = s_re
        spim_s[pl.ds(r, 1), :] = s_im
        e_re = ere_s[pl.ds(r, 1), :]
        e_im = eim_s[pl.ds(r, 1), :]
        return (a_re * s_re - a_im * s_im + e_re, a_re * s_im + a_im * s_re + e_im)

    s_re, s_im = lax.fori_loop(0, rows, carry_step, (sre_s[...], sim_s[...]))
    sre_s[...] = s_re
    sim_s[...] = s_im

    for v in range(SSM_LANE_BLOCKS):
        yg = []
        for gam in range(SSM_SLOTS):
            g = SSM_SLOTS * v + gam
            q = g // 2
            sp = jnp.concatenate([spre_s[:, q * LANES:(q + 1) * LANES], spim_s[:, q * LANES:(q + 1) * LANES]],
                                 axis=1).astype(BF16)
            yg.append(_dot(ub_s[g], toep_ref[g]) + _dot(sp, int_ref[g]))
        d = d_ref[layer:layer + 1, v * LANES:(v + 1) * LANES]
        for m in range(SSM_TIME_BLOCKS):
            back = [yg[-i % SSM_SLOTS][:, m * LANES:(m + 1) * LANES] for i in range(SSM_SLOTS)]
            for bit in slot_bits:
                back = [jnp.where(bit[1], back[(i - bit[0]) % SSM_SLOTS], back[i]) for i in range(SSM_SLOTS)]
            for j in range(SSM_SLOTS):
                o = back[j]
                if j:
                    o = pltpu.roll(o, LANES - SSM_GROUP_DIM * j, 1)
                sl = piece(SSM_SLOTS * m + j, v)
                y_ref[:, sl] = jax.nn.gelu(o + d * u_ref[:, sl])


def _s5(u_rows, tab, d_skip, layer, batch):
    nch, width = u_rows.shape
    rows = SSM_ROWS
    steps = nch // batch // rows
    once = lambda a: _layer_block(a, layer, pipeline_mode=pl.Buffered(1))
    blk = pl.BlockSpec((rows, width), lambda b, i: (b * steps + i, 0))
    table = pltpu.VMEM((SSM_GROUPS, SSM_CHUNK_WIDTH, SSM_CHUNK_WIDTH), BF16)
    state = pltpu.VMEM((rows, SSM_STATE_LANES), F32)
    carry = pltpu.VMEM((1, SSM_STATE_LANES), F32)
    return pl.pallas_call(
        functools.partial(_s5_body, layer=layer),
        grid=(batch, steps),
        in_specs=[blk, once(tab['kern']), once(tab['est']), once(tab['est_swapped']), once(tab['inter']),
                  once(tab['apow_re']), once(tab['apow_im']), _whole(d_skip)],
        out_specs=blk,
        out_shape=jax.ShapeDtypeStruct((nch, width), F32),
        scratch_shapes=[table, table, table,
                        pltpu.VMEM((SSM_GROUPS, rows, SSM_CHUNK_WIDTH), BF16), state, state, state, state,
                        carry, carry],
        compiler_params=_cparams("arbitrary", "arbitrary"),
        name="s5",
    )(u_rows, tab['kern'], tab['est'], tab['est_swapped'], tab['inter'], tab['apow_re'], tab['apow_im'], d_skip)


def _route_rows(scores, biased):
    ng = N_EXPERTS // EXPERTS_PER_GROUP
    group_score = []
    for gi in range(ng):
        a, b, c, d = biased[gi * EXPERTS_PER_GROUP:(gi + 1) * EXPERTS_PER_GROUP]
        hi1, lo1 = jnp.maximum(a, b), jnp.minimum(a, b)
        hi2, lo2 = jnp.maximum(c, d), jnp.minimum(c, d)
        top1 = jnp.maximum(hi1, hi2)
        top2 = jnp.maximum(jnp.minimum(hi1, hi2), jnp.maximum(lo1, lo2))
        group_score.append(top1 + top2)
    best = group_score[0]
    best_idx = jnp.zeros_like(best, dtype=jnp.int32)
    for gi in range(1, ng):
        better = group_score[gi] > best
        best = jnp.where(better, group_score[gi], best)
        best_idx = jnp.where(better, gi, best_idx)
    picked = []
    for e in range(N_EXPERTS):
        gi = e // EXPERTS_PER_GROUP
        rank = jnp.zeros_like(best_idx)
        for o in range(gi * EXPERTS_PER_GROUP, (gi + 1) * EXPERTS_PER_GROUP):
            if o == e:
                continue
            ahead = (biased[o] > biased[e]) | ((biased[o] == biased[e]) & (o < e))
            rank = rank + ahead.astype(jnp.int32)
        picked.append((best_idx == gi) & (rank < 2))
    wsum = sum(jnp.where(picked[e], scores[e], 0.0) for e in range(N_EXPERTS))
    return [jnp.where(picked[e], scores[e] / wsum, 0.0) for e in range(N_EXPERTS)], best_idx


def _group_sort_positions(best_idx, before_ref):
    ng = N_EXPERTS // EXPERTS_PER_GROUP
    tokens = best_idx.shape[1]
    member = [(best_idx == gi).astype(F32) for gi in range(ng)]
    stacked = jnp.concatenate(member + [jnp.zeros((SUBLANES - ng, tokens), F32)], axis=0)
    parts = []
    run = jnp.zeros((SUBLANES, 1), F32)
    for blk in range(tokens // LANES):
        piece = stacked[:, blk * LANES:(blk + 1) * LANES]
        parts.append(_dot(piece.astype(BF16), before_ref[...]) + run)
        run = run + jnp.sum(piece, axis=1, keepdims=True)
    earlier = jnp.concatenate(parts, axis=1)
    counts = [run[gi:gi + 1, :] for gi in range(ng)]
    pos = jnp.zeros_like(member[0])
    start = jnp.zeros_like(counts[0])
    for gi in range(ng):
        pos = pos + member[gi] * (start + earlier[gi:gi + 1, :])
        start = start + counts[gi]
    return pos, counts


def _outproj_body(x_ref, yp_ref, ya_ref, ys_ref, wgluf_ref, bglu_ref, gssm_ref, woutf_ref, g_ref, b_ref,
                  wr_ref, rb_ref, before_ref, h_ref, comb_ref, pos_ref, cnt_ref, ys_s, wglu_ref, wout_ref,
                  *, layer):
    @pl.when(pl.program_id(0) == 0)
    def _():
        wglu_ref[...] = wgluf_ref[...].astype(BF16)
        wout_ref[...] = woutf_ref[...].astype(BF16)

    chunks = ys_ref.shape[0]
    for t in range(SSM_CHUNK):
        for v in range(SSM_LANE_BLOCKS):
            ys_s[v, pl.ds(t, chunks, stride=SSM_CHUNK), :] = ys_ref[:, _ssm_piece(t, v)]
    def rows_block(sl):
        ys = jnp.concatenate([ys_s[v, sl, :] for v in range(SSM_LANE_BLOCKS)], axis=1)
        gate = jax.nn.sigmoid(_dot(ys.astype(BF16), wglu_ref[...]) + _layer_row(bglu_ref, layer))
        ys = ys * gate
        r = lax.rsqrt(jnp.mean(ys * ys, axis=-1, keepdims=True) + NORM_EPS)
        ysn = (ys * r * _layer_row(gssm_ref, layer)).astype(BF16)
        mix = _dot(jnp.concatenate([yp_ref[sl, :], ya_ref[sl, :], ysn], axis=1), wout_ref[...])
        h = _layer_norm(DN_ALPHA * x_ref[sl, :] + mix, _layer_row(g_ref, layer), _layer_row(b_ref, layer))
        h_ref[sl, :] = h
        return _dot(h.astype(BF16), wr_ref[...])

    tokens = x_ref.shape[0]
    parts = jnp.concatenate([rows_block(pl.ds(r0, OUT_ROWS)) for r0 in range(0, tokens, OUT_ROWS)], axis=0)
    parts_t = parts.T
    sc = jax.nn.sigmoid(parts_t[:N_EXPERTS, :] + parts_t[N_EXPERTS:2 * N_EXPERTS, :])
    bs = sc + rb_ref[...]
    scores = [sc[e:e + 1, :] for e in range(N_EXPERTS)]
    biased = [bs[e:e + 1, :] for e in range(N_EXPERTS)]
    comb_rows, best_idx = _route_rows(scores, biased)
    pos, counts = _group_sort_positions(best_idx, before_ref)
    comb_t = jnp.concatenate(comb_rows + [pos, jnp.zeros((LANES - N_EXPERTS - 1, tokens), F32)], axis=0)
    comb_ref[...] = comb_t.T
    pos_ref[...] = jnp.concatenate([pos, jnp.zeros((SUBLANES - 1, tokens), F32)], axis=0)
    cnt_ref[...] = jnp.concatenate(
        [jnp.broadcast_to(c, (1, LANES)) for c in counts]
        + [jnp.zeros((SUBLANES - len(counts), LANES), F32)], axis=0).astype(jnp.int32)


def _outproj(x, y_pool, y_attn, y_ssm_rows, w_glu, b_glu, g_ssm, w_out, ln_g, ln_b,
             w_router_split, router_bias, layer):
    t = x.shape[0]
    tm = MOE_TOKENS
    nt = t // tm
    row = lambda width: pl.BlockSpec((tm, width), lambda i: (i, 0))
    full = _whole
    once = lambda a: _layer_block(a, layer, pipeline_mode=pl.Buffered(1))
    token = np.arange(LANES)
    before = jnp.asarray(token[:, None] < token[None, :], BF16)
    return pl.pallas_call(
        functools.partial(_outproj_body, layer=layer),
        grid=(nt,),
        in_specs=[row(D_MODEL), row(POOL_WIDTH), row(ATTN_WIDTH),
                  pl.BlockSpec((tm // SSM_CHUNK, SSM_CHUNK * SSM_WIDTH), lambda i: (i, 0)),
                  once(w_glu), _whole(b_glu), _whole(g_ssm), once(w_out), _whole(ln_g), _whole(ln_b), full(w_router_split), full(router_bias), full(before)],
        out_specs=[row(D_MODEL), row(LANES), pl.BlockSpec((SUBLANES, tm), lambda i: (0, i)),
                   pl.BlockSpec((SUBLANES, LANES), lambda i: (i, 0))],
        out_shape=[jax.ShapeDtypeStruct((t, D_MODEL), F32), jax.ShapeDtypeStruct((t, LANES), F32),
                   jax.ShapeDtypeStruct((SUBLANES, t), F32),
                   jax.ShapeDtypeStruct((SUBLANES * nt, LANES), jnp.int32)],
        scratch_shapes=[pltpu.VMEM((SSM_LANE_BLOCKS, tm, LANES), F32),
                        pltpu.VMEM(w_glu.shape[1:], BF16), pltpu.VMEM(w_out.shape[1:], BF16)],
        compiler_params=_cparams("arbitrary"),
        name="outproj",
    )(x, y_pool, y_attn, y_ssm_rows, w_glu, b_glu, g_ssm, w_out, ln_g, ln_b,
      w_router_split, router_bias, before)


def _moe_body(cnt_ref, h_ref, comb_ref, pos_ref, p_ref, wg_ref, wu_ref, wd_ref, wpgf_ref, wppf_ref, g_ref, b_ref,
              o_ref, hs_s, cs_s, acc_s, ple_s, wpg_ref, wpp_ref, *, layer):
    i = pl.program_id(0)
    group = pl.program_id(1)
    ng = pl.num_programs(1)
    tm = h_ref.shape[0]

    @pl.when((i == 0) & (group == 0))
    def _():
        wpg_ref[...] = wpgf_ref[...].astype(BF16)
        wpp_ref[...] = wppf_ref[...].astype(BF16)

    @pl.when(group == 0)
    def _():
        hb = h_ref[...].astype(BF16)
        comb = comb_ref[...]
        comb_lo = comb - comb.astype(BF16).astype(F32)
        low_lanes = lax.broadcasted_iota(jnp.int32, (1, LANES), 1) < COMB_LO_LANE
        comb_b = jnp.where(low_lanes, comb, pltpu.roll(comb_lo, COMB_LO_LANE, 1)).astype(BF16)
        for r0 in range(0, tm, MOE_SIDE_ROWS):
            sl = pl.ds(r0, MOE_SIDE_ROWS)
            row = r0 + lax.broadcasted_iota(jnp.int32, (MOE_SIDE_ROWS, tm), 0)
            perm = jnp.where(pos_ref[0:1, :] == row.astype(F32), 1.0, 0.0).astype(BF16)
            hs_s[sl, :] = _dot(perm, hb).astype(BF16)
            both = _dot(perm, comb_b)
            cs_s[sl, :] = both + pltpu.roll(both, LANES - COMB_LO_LANE, 1)
        acc_s[...] = jnp.zeros_like(acc_s)

    count = cnt_ref[i * ng + group]
    start = jnp.int32(0)
    for gi in range(N_EXPERTS // EXPERTS_PER_GROUP - 1):
        start = start + jnp.where(group > gi, cnt_ref[i * ng + gi], 0)
    lane = lax.broadcasted_iota(jnp.int32, (1, LANES), 1)
    first = (start // BF16_ROW_PACK) * BF16_ROW_PACK
    need = start - first + count

    def window(w, size):
        wanted = first + w * size
        lo = pl.multiple_of(jnp.minimum(wanted, tm - size), BF16_ROW_PACK)
        rows = pl.ds(lo, size)
        x = hs_s[rows, :]
        fresh = lo + lax.broadcasted_iota(jnp.int32, (size, 1), 0) >= wanted
        cs = jnp.where(fresh, cs_s[rows, :], 0.0)
        total = None
        for e in range(EXPERTS_PER_GROUP):
            gate = _dot(x, wg_ref[e])
            up = _dot(x, wu_ref[e])
            c = jnp.sum(jnp.where(lane == group * EXPERTS_PER_GROUP + e, cs, 0.0), axis=1, keepdims=True)
            a = (jax.nn.silu(gate) * up * c).astype(BF16)
            d = _dot(a, wd_ref[e])
            total = d if total is None else total + d
        acc_s[rows, :] += total

    below = 0
    for size in MOE_WINDOW_SIZES[:-1]:
        @pl.when((need > below) & (need <= size))
        def _():
            window(0, size)
        below = size
    largest = MOE_WINDOW_SIZES[-1]

    @pl.when(need > below)
    def _():
        def step(w, carry):
            window(w, largest)
            return carry
        lax.fori_loop(0, (need + largest - 1) // largest, step, 0)

    half = tm // 2
    for mid in (1, 2):
        @pl.when(group == mid)
        def _():
            for r0 in range((mid - 1) * half, mid * half, MOE_SIDE_ROWS):
                sl = pl.ds(r0, MOE_SIDE_ROWS)
                hb = h_ref[sl, :].astype(BF16)
                ple_s[sl, :] = (jax.nn.sigmoid(_dot(hb, wpg_ref[...]))
                                * _dot(p_ref[sl, :].astype(BF16), wpp_ref[...]))

    @pl.when(group == ng - 1)
    def _():
        sorted_out = acc_s[...].astype(BF16)
        col = lax.broadcasted_iota(jnp.int32, (MOE_SIDE_ROWS, tm), 1).astype(F32)
        for r0 in range(0, tm, MOE_SIDE_ROWS):
            sl = pl.ds(r0, MOE_SIDE_ROWS)
            unperm = jnp.where(comb_ref[sl, SORT_POS_LANE:SORT_POS_LANE + 1] == col, 1.0, 0.0).astype(BF16)
            ffn = _dot(unperm, sorted_out)
            o_ref[sl, :] = _layer_norm(DN_ALPHA * h_ref[sl, :] + ffn + ple_s[sl, :], _layer_row(g_ref, layer),
                                      _layer_row(b_ref, layer))


def _moe(h, comb, pos_rows, counts, p_all, layer, wg_bf16, wu_bf16, wd_bf16, w_ple_gate, w_ple_proj, ln_g, ln_b):
    t = h.shape[0]
    tm = MOE_TOKENS
    nt = t // tm
    ng = N_EXPERTS // EXPERTS_PER_GROUP
    once = lambda a: _layer_block(a, layer, pipeline_mode=pl.Buffered(1))
    experts = lambda rows, cols: pl.BlockSpec((EXPERTS_PER_GROUP, rows, cols),
                                              lambda i, g, cnt: (layer * ng + g, 0, 0))
    grid_spec = pltpu.PrefetchScalarGridSpec(
        num_scalar_prefetch=1,
        grid=(nt, ng),
        in_specs=[pl.BlockSpec((tm, D_MODEL), lambda i, g, cnt: (i, 0)),
                  pl.BlockSpec((tm, LANES), lambda i, g, cnt: (i, 0)),
                  pl.BlockSpec((SUBLANES, tm), lambda i, g, cnt: (0, i)),
                  pl.BlockSpec((tm, PLE_DIM), lambda i, g, cnt: (layer * nt + i, 0)),
                  experts(D_MODEL, D_EXPERT), experts(D_MODEL, D_EXPERT), experts(D_EXPERT, D_MODEL),
                  once(w_ple_gate), once(w_ple_proj), _whole(ln_g), _whole(ln_b)],
        out_specs=pl.BlockSpec((tm, D_MODEL), lambda i, g, cnt: (i, 0)),
        scratch_shapes=[pltpu.VMEM((tm, D_MODEL), BF16), pltpu.VMEM((tm, LANES), F32),
                        pltpu.VMEM((tm, D_MODEL), F32), pltpu.VMEM((tm, D_MODEL), F32),
                        pltpu.VMEM(w_ple_gate.shape[1:], BF16), pltpu.VMEM(w_ple_proj.shape[1:], BF16)])
    return pl.pallas_call(
        functools.partial(_moe_body, layer=layer),
        grid_spec=grid_spec,
        out_shape=jax.ShapeDtypeStruct((t, D_MODEL), F32),
        compiler_params=_cparams("arbitrary", "arbitrary"),
        name="moe",
    )(counts, h, comb, pos_rows, p_all, wg_bf16, wu_bf16, wd_bf16, w_ple_gate, w_ple_proj, ln_g, ln_b)


def _block_diag(w):
    g, n, m = w.shape
    eye = jnp.eye(g, dtype=w.dtype)
    return (eye[:, None, :, None] * w[:, :, None, :]).reshape(g * n, g * m)


def kernel(x, p, w_in, w_out, w_pool, pool_scale, rel_bias, ssm_a_re, ssm_a_im, ssm_log_dt, ssm_b_re, ssm_b_im,
           ssm_c_re, ssm_c_im, ssm_d, w_glu, b_glu, g_pool, g_attn, g_ssm, ln1_g, ln1_b, ln2_g, ln2_b,
           w_router, router_bias, w_exp_gate, w_exp_up, w_exp_down, w_ple_gate, w_ple_proj):
    batch, seq, d = x.shape
    t = batch * seq
    xt = x.reshape(t, d)
    p_all = p.reshape(DEPTH * t, PLE_DIM)

    f32 = vec = lambda a: a.astype(F32)
    w_pool_b = jax.vmap(_block_diag)(w_pool).astype(BF16)
    bias_rows = _attn_bias_rows(rel_bias)
    tables = jax.vmap(_s5_tables)(ssm_a_re, ssm_a_im, ssm_log_dt, ssm_b_re, ssm_b_im, ssm_c_re, ssm_c_im)
    wr = w_router.astype(F32)
    wr_hi = wr.astype(BF16)
    wr_lo = (wr - wr_hi.astype(F32)).astype(BF16)
    wr_split = jnp.pad(jnp.concatenate([wr_hi, wr_lo], axis=1), ((0, 0), (0, LANES - 2 * N_EXPERTS)))
    r_bias = router_bias.astype(F32).reshape(N_EXPERTS, 1)
    stack_experts = lambda w: w.astype(BF16).reshape((DEPTH * N_EXPERTS,) + w.shape[2:])
    wg_b, wu_b, wd_b = stack_experts(w_exp_gate), stack_experts(w_exp_up), stack_experts(w_exp_down)
    ng = N_EXPERTS // EXPERTS_PER_GROUP

    for layer in range(DEPTH):
        y_pool, q, k, v, u_ssm = _inproj(xt, f32(w_in), w_pool_b, vec(pool_scale), vec(g_pool), layer, batch)
        y_attn = _attention(q, k, v, bias_rows, vec(g_attn), layer, batch)
        y_ssm = _s5(u_ssm, tables, vec(ssm_d), layer, batch)
        h, comb, pos_rows, cnt = _outproj(xt, y_pool, y_attn, y_ssm, f32(w_glu), vec(b_glu), vec(g_ssm),
                                          f32(w_out), vec(ln1_g), vec(ln1_b), wr_split, r_bias, layer)
        counts = cnt[:, 0].reshape(-1, SUBLANES)[:, :ng].reshape(-1)
        xt = _moe(h, comb, pos_rows, counts, p_all, layer, wg_b, wu_b, wd_b, f32(w_ple_gate), f32(w_ple_proj),
                  vec(ln2_g), vec(ln2_b))
    return xt.reshape(batch, seq, d)
```
